```python
import math
import jax, jax.numpy as jnp
from jax import lax
import numpy as np

D_MODEL = 2048
BATCH = 2
SEQ = 4096
DEPTH = 1

A_HEADS = 16
A_KV_HEADS = 4
A_HEAD_DIM = 64
WINDOW = 128
A_BLOCK = 128
B_HEADS = 8
B_Q_RANK = 512
B_KV_RANK = 256
B_NOPE_DIM = 128
B_ROPE_DIM = 64
B_V_DIM = 128
B_BLOCK = 128
ROPE_THETA = 10000.0
A_WIDTH = A_HEADS * A_HEAD_DIM
B_WIDTH = B_HEADS * B_V_DIM
MIX_WIDTH = A_WIDTH + B_WIDTH
A_KV_COLS = A_KV_HEADS * A_HEAD_DIM
IN_COL_SIZES = (A_WIDTH, A_KV_COLS, A_KV_COLS, B_Q_RANK, B_KV_RANK, B_ROPE_DIM)
IN_COLS = sum(IN_COL_SIZES)
IN_SPLITS = [sum(IN_COL_SIZES[:i + 1]) for i in range(len(IN_COL_SIZES) - 1)]
N_EXPERTS = 32
TOP_K = 4
D_FF = 2048
SWIGLU_ALPHA = 1.702
SWIGLU_LIMIT = 7.0
MOE_BLOCK = 128
EPS = 1e-5

kernel_name = "hybrid_swa_mla_moe_encoder"


def rmsnorm(x, g):
    xf = x.astype(jnp.float32)
    y = xf * lax.rsqrt(jnp.mean(xf * xf, axis=-1, keepdims=True) + EPS)
    return (y * g.astype(jnp.float32)).astype(x.dtype)


def alibi_slopes(n):
    return jnp.power(2.0, -8.0 * jnp.arange(1, n + 1, dtype=jnp.float32) / n)


def window_gqa_sink(q, k, v, sink):
    b, s, hq, dh = q.shape
    hkv = k.shape[2]
    g = hq // hkv
    nb = s // A_BLOCK
    qb = q.reshape(b, nb, A_BLOCK, hkv, g, dh)

    def band(t):
        tb = t.reshape(b, nb, A_BLOCK, hkv, dh)
        tp = jnp.pad(tb, ((0, 0), (1, 1), (0, 0), (0, 0), (0, 0)))
        return jnp.concatenate([tp[:, :-2], tp[:, 1:-1], tp[:, 2:]], axis=2)

    kb, vb = band(k), band(v)
    scores = jnp.einsum('bnqhgd,bnkhd->bnhgqk', qb, kb).astype(jnp.float32) * (dh ** -0.5)
    qi = jnp.arange(A_BLOCK)[:, None]
    kj = jnp.arange(3 * A_BLOCK)[None, :]
    dist = qi + A_BLOCK - kj
    kpos = jnp.arange(nb)[:, None, None] * A_BLOCK - A_BLOCK + kj[None]
    valid = (jnp.abs(dist)[None] <= WINDOW) & (kpos >= 0) & (kpos < s)
    slopes = alibi_slopes(hq).reshape(hkv, g)
    bias = -slopes[:, :, None, None] * jnp.abs(dist).astype(jnp.float32)
    scores = jnp.where(valid[None, :, None, None], scores + bias, -jnp.inf)
    sink_l = sink.astype(jnp.float32).reshape(hkv, g)[None, None, :, :, None, None]
    m = jnp.maximum(jnp.max(scores, axis=-1, keepdims=True), sink_l)
    p = jnp.exp(scores - m)
    p = p / (jnp.sum(p, axis=-1, keepdims=True) + jnp.exp(sink_l - m))
    out = jnp.einsum('bnhgqk,bnkhd->bnqhgd', p.astype(v.dtype), vb)
    return out.reshape(b, s, hq * dh)


def rotate(x, cos, sin):
    x1, x2 = jnp.split(x, 2, axis=-1)
    return jnp.concatenate([x1 * cos - x2 * sin, x1 * sin + x2 * cos], axis=-1)


def mla(c_q, c_kv, k_pe, q_norm, w_uq, kv_norm, w_ukv):
    b, s, _ = c_q.shape
    q = (rmsnorm(c_q, q_norm) @ w_uq).reshape(b, s, B_HEADS, B_NOPE_DIM + B_ROPE_DIM)
    q_nope, q_pe = q[..., :B_NOPE_DIM], q[..., B_NOPE_DIM:]
    kv = (rmsnorm(c_kv, kv_norm) @ w_ukv).reshape(b, s, B_HEADS, B_NOPE_DIM + B_V_DIM)
    k_nope, v = kv[..., :B_NOPE_DIM], kv[..., B_NOPE_DIM:]
    pos = jnp.arange(s, dtype=jnp.float32)
    inv_freq = jnp.power(ROPE_THETA, -jnp.arange(0, B_ROPE_DIM, 2, dtype=jnp.float32) / B_ROPE_DIM)
    ang = pos[:, None] * inv_freq[None, :]
    cos, sin = jnp.cos(ang).astype(q.dtype), jnp.sin(ang).astype(q.dtype)
    q_pe = rotate(q_pe, cos[None, :, None], sin[None, :, None])
    k_pe = rotate(k_pe, cos[None], sin[None])
    scale = (B_NOPE_DIM + B_ROPE_DIM) ** -0.5
    nb = s // B_BLOCK
    qn_b = q_nope.reshape(b, nb, B_BLOCK, B_HEADS, B_NOPE_DIM).transpose(1, 0, 2, 3, 4)
    qp_b = q_pe.reshape(b, nb, B_BLOCK, B_HEADS, B_ROPE_DIM).transpose(1, 0, 2, 3, 4)

    def attend(args):
        qn, qp = args
        sc = (jnp.einsum('bqhd,bkhd->bhqk', qn, k_nope)
              + jnp.einsum('bqhd,bkd->bhqk', qp, k_pe)).astype(jnp.float32) * scale
        p = jax.nn.softmax(sc, axis=-1)
        return jnp.einsum('bhqk,bkhd->bqhd', p.astype(v.dtype), v)

    out = lax.map(attend, (qn_b, qp_b))
    return out.transpose(1, 0, 2, 3, 4).reshape(b, s, B_HEADS * B_V_DIM)


def moe(x, w_router, b_router, w_up, b_up, w_down, b_down):
    b, s, d = x.shape
    t = b * s
    xt = x.reshape(t, d)
    logits = (xt @ w_router + b_router).astype(jnp.float32)
    top_vals, top_idx = lax.top_k(logits, TOP_K)
    gates = jax.nn.softmax(top_vals, axis=-1)
    a = t * TOP_K
    e_flat = top_idx.reshape(a)
    tok_flat = jnp.arange(a, dtype=jnp.int32) // TOP_K
    g_flat = gates.reshape(a)
    order = jnp.argsort(e_flat)
    e_sorted = e_flat[order]
    counts = jnp.bincount(e_flat, length=N_EXPERTS)
    starts = jnp.cumsum(counts) - counts
    padded = (counts + MOE_BLOCK - 1) // MOE_BLOCK * MOE_BLOCK
    pad_ends = jnp.cumsum(padded)
    pad_starts = pad_ends - padded
    rank = jnp.arange(a) - starts[e_sorted]
    dest = pad_starts[e_sorted] + rank
    n_blocks = (a + N_EXPERTS * (MOE_BLOCK - 1) + MOE_BLOCK - 1) // MOE_BLOCK
    n_slots = n_blocks * MOE_BLOCK
    slot_tok = jnp.zeros((n_slots,), jnp.int32).at[dest].set(tok_flat[order])
    slot_gate = jnp.zeros((n_slots,), jnp.float32).at[dest].set(g_flat[order])
    block_expert = jnp.minimum(
        jnp.searchsorted(pad_ends, jnp.arange(n_blocks) * MOE_BLOCK, side='right'), N_EXPERTS - 1)
    tok_blocks = slot_tok.reshape(n_blocks, MOE_BLOCK)

    def expert_block(args):
        e, tok = args
        h = xt[tok] @ w_up[e] + b_up[e]
        gate = jnp.minimum(h[:, :D_FF], SWIGLU_LIMIT)
        up = jnp.clip(h[:, D_FF:], -SWIGLU_LIMIT, SWIGLU_LIMIT)
        glu = gate * jax.nn.sigmoid(gate * SWIGLU_ALPHA)
        return ((up + 1.0) * glu) @ w_down[e] + b_down[e]

    y = lax.map(expert_block, (block_expert, tok_blocks)).reshape(n_slots, d)
    y = y * slot_gate[:, None].astype(y.dtype)
    out = jax.ops.segment_sum(y, slot_tok, num_segments=t)
    return out.reshape(b, s, d)


def _dense(key, shape, fan_in):
    return jax.random.normal(key, shape, jnp.float32) * (fan_in ** -0.5)


def _gain(key, shape):
    return 1.0 + 0.02 * jax.random.normal(key, shape, jnp.float32)


def setup_inputs(seed: int = 0) -> dict:
    key = jax.random.key(seed)
    ks = jax.random.split(key, 20)
    L = DEPTH
    return {
        "x": jax.random.normal(ks[0], (BATCH, SEQ, D_MODEL), jnp.float32),
        "attn_norm": _gain(ks[1], (L, D_MODEL)),
        "w_in": _dense(ks[2], (L, D_MODEL, IN_COLS), D_MODEL),
        "a_sink": 0.5 * jax.random.normal(ks[3], (L, A_HEADS), jnp.float32),
        "b_q_norm": _gain(ks[4], (L, B_Q_RANK)),
        "b_w_uq": _dense(ks[5], (L, B_Q_RANK, B_HEADS * (B_NOPE_DIM + B_ROPE_DIM)), B_Q_RANK),
        "b_kv_norm": _gain(ks[6], (L, B_KV_RANK)),
        "b_w_ukv": _dense(ks[7], (L, B_KV_RANK, B_HEADS * (B_NOPE_DIM + B_V_DIM)), B_KV_RANK),
        "out_norm_a": _gain(ks[8], (L, A_WIDTH)),
        "out_norm_b": _gain(ks[9], (L, B_WIDTH)),
        "w_o": _dense(ks[10], (L, MIX_WIDTH, D_MODEL), MIX_WIDTH),
        "mlp_norm": _gain(ks[11], (L, D_MODEL)),
        "w_router": _dense(ks[12], (L, D_MODEL, N_EXPERTS), D_MODEL),
        "b_router": 0.01 * jax.random.normal(ks[13], (L, N_EXPERTS), jnp.float32),
        "w_up": _dense(ks[14], (L, N_EXPERTS, D_MODEL, 2 * D_FF), D_MODEL),
        "b_up": 0.01 * jax.random.normal(ks[15], (L, N_EXPERTS, 2 * D_FF), jnp.float32),
        "w_down": _dense(ks[16], (L, N_EXPERTS, D_FF, D_MODEL), D_FF),
        "b_down": 0.01 * jax.random.normal(ks[17], (L, N_EXPERTS, D_MODEL), jnp.float32),
        "final_norm": _gain(ks[18], (D_MODEL,)),
    }


def reference(x, attn_norm, w_in, a_sink, b_q_norm, b_w_uq, b_kv_norm, b_w_ukv, out_norm_a,
              out_norm_b, w_o, mlp_norm, w_router, b_router, w_up, b_up, w_down, b_down,
              final_norm):
    b, s, _ = x.shape
    for l in range(DEPTH):
        xn = rmsnorm(x, attn_norm[l])
        proj = xn @ w_in[l]
        a_q, a_k, a_v, b_cq, b_ckv, b_kpe = jnp.split(proj, IN_SPLITS, axis=-1)
        y_a = window_gqa_sink(a_q.reshape(b, s, A_HEADS, A_HEAD_DIM),
                              a_k.reshape(b, s, A_KV_HEADS, A_HEAD_DIM),
                              a_v.reshape(b, s, A_KV_HEADS, A_HEAD_DIM),
                              a_sink[l])
        y_b = mla(b_cq, b_ckv, b_kpe, b_q_norm[l], b_w_uq[l], b_kv_norm[l], b_w_ukv[l])
        mixed = jnp.concatenate([rmsnorm(y_a, out_norm_a[l]), rmsnorm(y_b, out_norm_b[l])], axis=-1)
        x = x + mixed @ w_o[l]
        x = x + moe(rmsnorm(x, mlp_norm[l]), w_router[l], b_router[l], w_up[l], b_up[l],
                    w_down[l], b_down[l])
    return rmsnorm(x, final_norm)
```

```python
import functools
import math

import jax
import jax.numpy as jnp
from jax import lax
from jax.experimental import pallas as pl
from jax.experimental.pallas import tpu as pltpu

D_MODEL = 2048
A_HEADS, A_KV_HEADS, A_HEAD_DIM = 16, 4, 64
A_GROUP = A_HEADS // A_KV_HEADS
WINDOW = 128
A_BLOCK = 128
B_HEADS, B_Q_RANK, B_KV_RANK = 8, 512, 256
B_NOPE_DIM, B_ROPE_DIM, B_V_DIM = 128, 64, 128
ROPE_THETA = 10000.0
A_WIDTH = A_HEADS * A_HEAD_DIM
B_WIDTH = B_HEADS * B_V_DIM
A_KV_COLS = A_KV_HEADS * A_HEAD_DIM
N_EXPERTS, TOP_K, D_FF = 32, 4, 2048
SWIGLU_ALPHA, SWIGLU_LIMIT = 1.702, 7.0
EPS = 1e-5

LANES = 128
SUBLANES = 8
B_QK_PAD = 2 * LANES
IN_COLS_PAD = A_WIDTH + 2 * A_KV_COLS + B_Q_RANK + B_KV_RANK + LANES
VMEM_LIMIT = 56 * 1024 * 1024

TM_PRO = 256
TQ_MLA = 512
TM_OUT = 256
TM_DISPATCH = 512
TM_COMBINE = 128
MOE_BM = 512
MOE_TF = 256
WAIT_UNROLL = 16

BF16 = jnp.bfloat16
F32 = jnp.float32


def _rms(x, g):
    return x * lax.rsqrt(jnp.mean(x * x, axis=-1, keepdims=True) + EPS) * g


def _const_spec(shape):
    nd = len(shape)
    return pl.BlockSpec(shape, lambda *_: (0,) * nd, pipeline_mode=pl.Buffered(1))


def _rope(x, cos, sin):
    lane = lax.broadcasted_iota(jnp.int32, x.shape, 1)
    up = pltpu.roll(x, LANES - B_ROPE_DIM // 2, 1)
    dn = pltpu.roll(x, B_ROPE_DIM // 2, 1)
    sw = jnp.where(lane < B_ROPE_DIM // 2, -up, jnp.where(lane < B_ROPE_DIM, dn, 0.0))
    return x * cos + sw * sin


def _prologue_body(x_ref, g_ref, win_ref, qn_ref, wuq_ref, kvn_ref, wukv_ref, cos_ref, sin_ref,
                   qa_ref, ka_ref, va_ref, qb_ref, kb_ref, vb_ref):
    x = x_ref[0]
    xn = _rms(x, g_ref[...]).astype(BF16)
    proj = jnp.dot(xn, win_ref[...], preferred_element_type=F32)
    c0 = A_WIDTH
    c1 = c0 + A_KV_COLS
    c2 = c1 + A_KV_COLS
    c3 = c2 + B_Q_RANK
    c4 = c3 + B_KV_RANK
    a_scale = A_HEAD_DIM ** -0.5
    for h in range(A_HEADS):
        qa_ref[0, h] = (proj[:, h * A_HEAD_DIM:(h + 1) * A_HEAD_DIM] * a_scale).astype(BF16)
    for h in range(A_KV_HEADS):
        ka_ref[0, h] = proj[:, c0 + h * A_HEAD_DIM:c0 + (h + 1) * A_HEAD_DIM].astype(BF16)
        va_ref[0, h] = proj[:, c1 + h * A_HEAD_DIM:c1 + (h + 1) * A_HEAD_DIM].astype(BF16)
    cq = _rms(proj[:, c2:c3], qn_ref[...]).astype(BF16)
    q = jnp.dot(cq, wuq_ref[...], preferred_element_type=F32)
    ckv = _rms(proj[:, c3:c4], kvn_ref[...]).astype(BF16)
    kv = jnp.dot(ckv, wukv_ref[...], preferred_element_type=F32)
    cos = cos_ref[...]
    sin = sin_ref[...]
    kpe = _rope(proj[:, c4:c4 + LANES], cos, sin).astype(BF16)
    b_scale = (B_NOPE_DIM + B_ROPE_DIM) ** -0.5
    hw = B_HEADS * LANES
    for h in range(B_HEADS):
        sl = slice(h * LANES, (h + 1) * LANES)
        qb_ref[0, h, :, :LANES] = (q[:, sl] * b_scale).astype(BF16)
        qpe = _rope(q[:, hw + h * LANES:hw + (h + 1) * LANES], cos, sin)
        qb_ref[0, h, :, LANES:] = (qpe * b_scale).astype(BF16)
        kb_ref[0, h, :, :LANES] = kv[:, sl].astype(BF16)
        kb_ref[0, h, :, LANES:] = kpe
        vb_ref[0, h] = kv[:, hw + h * LANES:hw + (h + 1) * LANES].astype(BF16)


def _prologue(x, g, win, qn, wuq, kvn, wukv, cos, sin):
    b, s, d = x.shape
    tm = TM_PRO
    grid = (b, s // tm)
    sds = jax.ShapeDtypeStruct
    out_shape = (
        sds((b, A_HEADS, s, A_HEAD_DIM), BF16),
        sds((b, A_KV_HEADS, s, A_HEAD_DIM), BF16),
        sds((b, A_KV_HEADS, s, A_HEAD_DIM), BF16),
        sds((b, B_HEADS, s, B_QK_PAD), BF16),
        sds((b, B_HEADS, s, B_QK_PAD), BF16),
        sds((b, B_HEADS, s, B_V_DIM), BF16),
    )

    def hspec(nh, w):
        return pl.BlockSpec((1, nh, tm, w), lambda bi, i: (bi, 0, i, 0))

    return pl.pallas_call(
        _prologue_body,
        grid=grid,
        in_specs=[
            pl.BlockSpec((1, tm, d), lambda bi, i: (bi, i, 0)),
            _const_spec(g.shape), _const_spec(win.shape), _const_spec(qn.shape), _const_spec(wuq.shape),
            _const_spec(kvn.shape), _const_spec(wukv.shape),
            pl.BlockSpec((tm, LANES), lambda bi, i: (i, 0)),
            pl.BlockSpec((tm, LANES), lambda bi, i: (i, 0)),
        ],
        out_specs=(hspec(A_HEADS, A_HEAD_DIM), hspec(A_KV_HEADS, A_HEAD_DIM), hspec(A_KV_HEADS, A_HEAD_DIM),
                   hspec(B_HEADS, B_QK_PAD), hspec(B_HEADS, B_QK_PAD), hspec(B_HEADS, B_V_DIM)),
        out_shape=out_shape,
        compiler_params=pltpu.CompilerParams(dimension_semantics=("parallel", "parallel"),
                                             vmem_limit_bytes=VMEM_LIMIT),
        name="prologue",
    )(x, g, win, qn, wuq, kvn, wukv, cos, sin)


def _window_body(sink_ref, q_ref, kp_ref, kc_ref, kn_ref, vp_ref, vc_ref, vn_ref, bias_ref, o_ref):
    n = pl.program_id(1)
    nb = pl.num_programs(1)
    rows = A_GROUP * A_BLOCK
    col = lax.broadcasted_iota(jnp.int32, (1, 3 * A_BLOCK), 1)
    outside = ((col < A_BLOCK) & (n == 0)) | ((col >= 2 * A_BLOCK) & (n == nb - 1))
    for hk in range(A_KV_HEADS):
        q4 = q_ref[0, hk * A_GROUP:(hk + 1) * A_GROUP].reshape(rows, A_HEAD_DIM)
        kband = jnp.concatenate([kp_ref[0, hk], kc_ref[0, hk], kn_ref[0, hk]], axis=0)
        vband = jnp.concatenate([vp_ref[0, hk], vc_ref[0, hk], vn_ref[0, hk]], axis=0)
        s = lax.dot_general(q4, kband, (((1,), (1,)), ((), ())), preferred_element_type=F32)
        s = s + bias_ref[hk * A_GROUP:(hk + 1) * A_GROUP].reshape(rows, 3 * A_BLOCK)
        s = jnp.where(outside, -jnp.inf, s)
        sink = jnp.concatenate(
            [jnp.full((A_BLOCK, 1), sink_ref[hk * A_GROUP + g], F32) for g in range(A_GROUP)], axis=0)
        m = jnp.maximum(jnp.max(s, axis=-1, keepdims=True), sink)
        p = jnp.exp(s - m)
        den = jnp.sum(p, axis=-1, keepdims=True) + jnp.exp(sink - m)
        o = jnp.dot(p.astype(BF16), vband, preferred_element_type=F32) / den
        for g in range(A_GROUP):
            h = hk * A_GROUP + g
            o_ref[0, :, h * A_HEAD_DIM:(h + 1) * A_HEAD_DIM] = o[g * A_BLOCK:(g + 1) * A_BLOCK].astype(BF16)


def _window_attn(qa, ka, va, sink, bias):
    b, _, s, _ = qa.shape
    nb = s // A_BLOCK
    kv_blk = (1, A_KV_HEADS, A_BLOCK, A_HEAD_DIM)
    prev = pl.BlockSpec(kv_blk, lambda bi, n: (bi, 0, jnp.maximum(n - 1, 0), 0))
    cur = pl.BlockSpec(kv_blk, lambda bi, n: (bi, 0, n, 0))
    nxt = pl.BlockSpec(kv_blk, lambda bi, n: (bi, 0, jnp.minimum(n + 1, nb - 1), 0))
    return pl.pallas_call(
        _window_body,
        grid=(b, nb),
        in_specs=[
            pl.BlockSpec(memory_space=pltpu.SMEM),
            pl.BlockSpec((1, A_HEADS, A_BLOCK, A_HEAD_DIM), lambda bi, n: (bi, 0, n, 0)),
            prev, cur, nxt, prev, cur, nxt,
            _const_spec(bias.shape),
        ],
        out_specs=pl.BlockSpec((1, A_BLOCK, A_WIDTH), lambda bi, n: (bi, n, 0)),
        out_shape=jax.ShapeDtypeStruct((b, s, A_WIDTH), BF16),
        compiler_params=pltpu.CompilerParams(dimension_semantics=("parallel", "parallel"),
                                             vmem_limit_bytes=VMEM_LIMIT),
        name="window_attn",
    )(sink, qa, ka, ka, ka, va, va, va, bias)


def _mla_body(q_ref, k_ref, v_ref, o_ref):
    s = lax.dot_general(q_ref[0, 0], k_ref[0, 0], (((1,), (1,)), ((), ())), preferred_element_type=F32)
    m = jnp.max(s, axis=-1, keepdims=True)
    p = jnp.exp(s - m)
    den = jnp.sum(p, axis=-1, keepdims=True)
    o = jnp.dot(p.astype(BF16), v_ref[0, 0], preferred_element_type=F32) / den
    o_ref[0] = o.astype(BF16)


def _mla_attn(qb, kb, vb):
    b, nh, s, _ = qb.shape
    tq = min(TQ_MLA, s)
    return pl.pallas_call(
        _mla_body,
        grid=(b, nh, s // tq),
        in_specs=[
            pl.BlockSpec((1, 1, tq, B_QK_PAD), lambda bi, h, i: (bi, h, i, 0)),
            pl.BlockSpec((1, 1, s, B_QK_PAD), lambda bi, h, i: (bi, h, 0, 0)),
            pl.BlockSpec((1, 1, s, B_V_DIM), lambda bi, h, i: (bi, h, 0, 0)),
        ],
        out_specs=pl.BlockSpec((1, tq, B_V_DIM), lambda bi, h, i: (bi, i, h)),
        out_shape=jax.ShapeDtypeStruct((b, s, B_WIDTH), BF16),
        compiler_params=pltpu.CompilerParams(dimension_semantics=("parallel", "parallel", "parallel"),
                                             vmem_limit_bytes=VMEM_LIMIT),
        name="mla_attn",
    )(qb, kb, vb)


def _out_router_body(ya_ref, yb_ref, x_ref, ga_ref, gb_ref, wo_ref, gm_ref, wr_ref, br_ref,
                     x1_ref, xm_ref, eidx_ref, gate_ref, rank_ref, cnt_ref, run_ref):
    i = pl.program_id(0)

    @pl.when(i == 0)
    def _():
        run_ref[...] = jnp.zeros_like(run_ref)

    na = _rms(ya_ref[...].astype(F32), ga_ref[...]).astype(BF16)
    nb = _rms(yb_ref[...].astype(F32), gb_ref[...]).astype(BF16)
    att = jnp.dot(na, wo_ref[:A_WIDTH, :], preferred_element_type=F32)
    att = att + jnp.dot(nb, wo_ref[A_WIDTH:, :], preferred_element_type=F32)
    x1 = x_ref[...] + att
    x1_ref[...] = x1
    hn = _rms(x1, gm_ref[...])
    xm_ref[...] = hn
    logits = jnp.dot(hn, wr_ref[...], preferred_element_type=F32,
                     precision=lax.Precision.HIGHEST) + br_ref[...]
    tm = logits.shape[0]
    lane = lax.broadcasted_iota(jnp.int32, (tm, N_EXPERTS), 1)
    work = logits
    sel = jnp.zeros((tm, N_EXPERTS), F32)
    hots, vals, idxs = [], [], []
    for _k in range(TOP_K):
        mx = jnp.max(work, axis=-1, keepdims=True)
        idx = jnp.min(jnp.where(work == mx, lane, N_EXPERTS), axis=-1, keepdims=True)
        hot = lane == idx
        hots.append(hot)
        vals.append(mx)
        idxs.append(idx)
        sel = sel + hot.astype(F32)
        work = jnp.where(hot, -jnp.inf, work)
    exps = [jnp.exp(v - vals[0]) for v in vals]
    den = exps[0] + exps[1] + exps[2] + exps[3]
    r_i = lax.broadcasted_iota(jnp.int32, (tm, tm), 0)
    c_i = lax.broadcasted_iota(jnp.int32, (tm, tm), 1)
    tri = (c_i < r_i).astype(BF16)
    before = jnp.dot(tri, sel.astype(BF16), preferred_element_type=F32) + run_ref[...]
    lane4 = lax.broadcasted_iota(jnp.int32, (tm, TOP_K), 1)
    eidx = jnp.zeros((tm, TOP_K), jnp.int32)
    gate = jnp.zeros((tm, TOP_K), F32)
    rank = jnp.zeros((tm, TOP_K), jnp.int32)
    for k in range(TOP_K):
        rk = jnp.sum(jnp.where(hots[k], before, 0.0), axis=-1, keepdims=True).astype(jnp.int32)
        eidx = jnp.where(lane4 == k, idxs[k], eidx)
        gate = jnp.where(lane4 == k, exps[k] / den, gate)
        rank = jnp.where(lane4 == k, rk, rank)
    eidx_ref[...] = eidx
    gate_ref[...] = gate
    rank_ref[...] = rank
    run = run_ref[...] + jnp.sum(sel, axis=0, keepdims=True)
    run_ref[...] = run
    cnt_ref[...] = run.astype(jnp.int32)


def _out_router(ya, yb, x2d, ga, gb, wo, gm, wr, br):
    t, d = x2d.shape
    tm = min(TM_OUT, t)
    sds = jax.ShapeDtypeStruct

    def row(w):
        return pl.BlockSpec((tm, w), lambda i: (i, 0))

    return pl.pallas_call(
        _out_router_body,
        grid=(t // tm,),
        in_specs=[row(A_WIDTH), row(B_WIDTH), row(d), _const_spec(ga.shape), _const_spec(gb.shape),
                  _const_spec(wo.shape), _const_spec(gm.shape), _const_spec(wr.shape), _const_spec(br.shape)],
        out_specs=(row(d), row(d), row(TOP_K), row(TOP_K), row(TOP_K),
                   pl.BlockSpec((1, N_EXPERTS), lambda i: (0, 0))),
        out_shape=(sds((t, d), F32), sds((t, d), F32), sds((t, TOP_K), jnp.int32), sds((t, TOP_K), F32),
                   sds((t, TOP_K), jnp.int32), sds((1, N_EXPERTS), jnp.int32)),
        scratch_shapes=[pltpu.VMEM((1, N_EXPERTS), F32)],
        compiler_params=pltpu.CompilerParams(dimension_semantics=("arbitrary",),
                                             vmem_limit_bytes=VMEM_LIMIT),
        name="out_router",
    )(ya, yb, x2d, ga, gb, wo, gm, wr, br)


def _row_copy(src_ref, src_row, dst_ref, dst_row, sem):
    return pltpu.make_async_copy(src_ref.at[pl.ds(src_row, 1)], dst_ref.at[pl.ds(dst_row, 1)], sem)


def _zero_fill_pads(pad_start_ref, pad_len_ref, nact_ref, xs_ref, zero_ref, zsem):
    bm = zero_ref.shape[0]
    nblk = xs_ref.shape[0] // bm
    zero_ref[...] = jnp.zeros_like(zero_ref)

    def per_expert(e, c):
        start = pad_start_ref[e]
        n = pad_len_ref[e]
        head = jnp.minimum((-start) & (SUBLANES - 1), n)

        def one(r, c2):
            cp = _row_copy(zero_ref, 0, xs_ref, start + r, zsem)
            cp.start()
            cp.wait()
            return c2

        lax.fori_loop(0, head, one, 0)
        off = start + head
        rest = n - head
        p = bm // 2
        while p >= SUBLANES:
            take = (rest & p) != 0
            dst = pl.multiple_of(off, SUBLANES)

            @pl.when(take)
            def _(p=p, dst=dst):
                cp = pltpu.make_async_copy(zero_ref.at[pl.ds(0, p)], xs_ref.at[pl.ds(dst, p)], zsem)
                cp.start()
                cp.wait()

            off = off + jnp.where(take, p, 0)
            p //= 2
        return c

    lax.fori_loop(0, N_EXPERTS, per_expert, 0)

    def tail(j, c):
        cp = pltpu.make_async_copy(zero_ref, xs_ref.at[pl.ds(pl.multiple_of(j * bm, bm), bm)], zsem)
        cp.start()
        cp.wait()
        return c

    lax.fori_loop(nact_ref[0], nblk, tail, 0)


def _dispatch_body(pad_start_ref, pad_len_ref, nact_ref, pos_ref, xm_ref, xs_ref, zero_ref, sem, zsem):
    tm = pos_ref.shape[0] // TOP_K
    base = pl.program_id(0) * tm

    @pl.when(pl.program_id(0) == 0)
    def _():
        _zero_fill_pads(pad_start_ref, pad_len_ref, nact_ref, xs_ref, zero_ref, zsem)

    def issue(i, c):
        for k in range(TOP_K):
            _row_copy(xm_ref, base + i, xs_ref, pos_ref[i * TOP_K + k], sem).start()
        return c

    lax.fori_loop(0, tm, issue, 0)

    def drain(i, c):
        for _ in range(WAIT_UNROLL):
            _row_copy(xm_ref, 0, xs_ref, 0, sem).wait()
        return c

    lax.fori_loop(0, tm * TOP_K // WAIT_UNROLL, drain, 0)


def _dispatch(pad_start, pad_len, nact, pos_flat, xm, n_slots):
    t, d = xm.shape
    tm = min(TM_DISPATCH, t)
    grid_spec = pltpu.PrefetchScalarGridSpec(
        num_scalar_prefetch=3,
        grid=(t // tm,),
        in_specs=[pl.BlockSpec((tm * TOP_K,), lambda i, *_: (i,), memory_space=pltpu.SMEM),
                  pl.BlockSpec(memory_space=pl.ANY)],
        out_specs=pl.BlockSpec(memory_space=pl.ANY),
        scratch_shapes=[pltpu.VMEM((MOE_BM, d), F32), pltpu.SemaphoreType.DMA(()), pltpu.SemaphoreType.DMA(())],
    )
    return pl.pallas_call(
        _dispatch_body,
        grid_spec=grid_spec,
        out_shape=jax.ShapeDtypeStruct((n_slots, d), F32),
        compiler_params=pltpu.CompilerParams(dimension_semantics=("arbitrary",), has_side_effects=True,
                                             vmem_limit_bytes=VMEM_LIMIT),
        name="dispatch",
    )(pad_start, pad_len, nact, pos_flat, xm)


def _experts_body(bexp_ref, nact_ref, x_ref, wg_ref, wu_ref, bg_ref, bu_ref, wd_ref, bd_ref,
                  y_ref, xb_ref, acc_ref):
    i = pl.program_id(0)
    f = pl.program_id(1)
    nf = pl.num_programs(1)
    active = i < nact_ref[0]

    @pl.when(active)
    def _():
        @pl.when(f == 0)
        def _():
            xb_ref[...] = x_ref[...].astype(BF16)
            acc_ref[...] = jnp.zeros_like(acc_ref)

        xb = xb_ref[...]
        hg = jnp.dot(xb, wg_ref[0].astype(BF16), preferred_element_type=F32) + bg_ref[0]
        hu = jnp.dot(xb, wu_ref[0].astype(BF16), preferred_element_type=F32) + bu_ref[0]
        gate = jnp.minimum(hg, SWIGLU_LIMIT)
        up = jnp.clip(hu, -SWIGLU_LIMIT, SWIGLU_LIMIT)
        glu = gate / (1.0 + jnp.exp(-SWIGLU_ALPHA * gate))
        act = ((up + 1.0) * glu).astype(BF16)
        acc_ref[...] += jnp.dot(act, wd_ref[0].astype(BF16), preferred_element_type=F32)

        @pl.when(f == nf - 1)
        def _():
            y_ref[...] = acc_ref[...] + bd_ref[0]

    @pl.when(jnp.logical_not(active) & (f == nf - 1))
    def _():
        y_ref[...] = jnp.zeros_like(y_ref)


def _experts(bexp, nact, xs, w_up, b_up, w_down, b_down):
    n_slots, d = xs.shape
    bm = MOE_BM
    nblk = n_slots // bm
    nf = D_FF // MOE_TF

    def blk(i, na):
        return jnp.minimum(i, na[0] - 1)

    def ftile(i, f, na):
        return jnp.where(i < na[0], f, nf - 1)

    grid_spec = pltpu.PrefetchScalarGridSpec(
        num_scalar_prefetch=2,
        grid=(nblk, nf),
        in_specs=[
            pl.BlockSpec((bm, d), lambda i, f, be, na: (blk(i, na), 0)),
            pl.BlockSpec((1, d, MOE_TF), lambda i, f, be, na: (be[blk(i, na)], 0, ftile(i, f, na))),
            pl.BlockSpec((1, d, MOE_TF), lambda i, f, be, na: (be[blk(i, na)], 0, nf + ftile(i, f, na))),
            pl.BlockSpec((1, 1, MOE_TF), lambda i, f, be, na: (be[blk(i, na)], 0, ftile(i, f, na))),
            pl.BlockSpec((1, 1, MOE_TF), lambda i, f, be, na: (be[blk(i, na)], 0, nf + ftile(i, f, na))),
            pl.BlockSpec((1, MOE_TF, d), lambda i, f, be, na: (be[blk(i, na)], ftile(i, f, na), 0)),
            pl.BlockSpec((1, 1, d), lambda i, f, be, na: (be[blk(i, na)], 0, 0)),
        ],
        out_specs=pl.BlockSpec((bm, d), lambda i, f, be, na: (i, 0)),
        scratch_shapes=[pltpu.VMEM((bm, d), BF16), pltpu.VMEM((bm, d), F32)],
    )
    return pl.pallas_call(
        _experts_body,
        grid_spec=grid_spec,
        out_shape=jax.ShapeDtypeStruct((n_slots, d), F32),
        compiler_params=pltpu.CompilerParams(dimension_semantics=("arbitrary", "arbitrary"),
                                             vmem_limit_bytes=VMEM_LIMIT),
        name="experts",
    )(bexp, nact, xs, w_up, w_up, b_up, b_up, w_down, b_down)


def _combine_body(pos_ref, gate_ref, x1_ref, gf_ref, ys_ref, o_ref, buf_ref, sem):
    tm = x1_ref.shape[0]

    def issue(i, c):
        for k in range(TOP_K):
            _row_copy(ys_ref, pos_ref[i * TOP_K + k], buf_ref.at[k], i, sem).start()
        return c

    lax.fori_loop(0, tm, issue, 0)

    def drain(i, c):
        for _ in range(WAIT_UNROLL):
            _row_copy(ys_ref, 0, buf_ref.at[0], 0, sem).wait()
        return c

    lax.fori_loop(0, tm * TOP_K // WAIT_UNROLL, drain, 0)
    gate = gate_ref[...]
    acc = x1_ref[...]
    for k in range(TOP_K):
        acc = acc + gate[:, k:k + 1] * buf_ref[k]
    o_ref[...] = _rms(acc, gf_ref[...])


def _combine(pos_flat, gates, x1, gf, ys):
    t, d = x1.shape
    tm = min(TM_COMBINE, t)
    return pl.pallas_call(
        _combine_body,
        grid=(t // tm,),
        in_specs=[pl.BlockSpec((tm * TOP_K,), lambda i: (i,), memory_space=pltpu.SMEM),
                  pl.BlockSpec((tm, TOP_K), lambda i: (i, 0)),
                  pl.BlockSpec((tm, d), lambda i: (i, 0)),
                  _const_spec(gf.shape),
                  pl.BlockSpec(memory_space=pl.ANY)],
        out_specs=pl.BlockSpec((tm, d), lambda i: (i, 0)),
        out_shape=jax.ShapeDtypeStruct((t, d), F32),
        scratch_shapes=[pltpu.VMEM((TOP_K, tm, d), F32), pltpu.SemaphoreType.DMA(())],
        compiler_params=pltpu.CompilerParams(dimension_semantics=("arbitrary",),
                                             vmem_limit_bytes=VMEM_LIMIT),
        name="combine",
    )(pos_flat, gates, x1, gf, ys)


def _rope_tables(s):
    pos = jnp.arange(s, dtype=F32)
    inv_freq = jnp.power(ROPE_THETA, -jnp.arange(0, B_ROPE_DIM, 2, dtype=F32) / B_ROPE_DIM)
    ang = pos[:, None] * inv_freq[None, :]
    cos, sin = jnp.cos(ang), jnp.sin(ang)
    pad = LANES - B_ROPE_DIM
    cos = jnp.concatenate([cos, cos, jnp.ones((s, pad), F32)], axis=1)
    sin = jnp.concatenate([sin, sin, jnp.zeros((s, pad), F32)], axis=1)
    return cos, sin


def _window_bias():
    qi = jnp.arange(A_BLOCK)[:, None]
    kj = jnp.arange(3 * A_BLOCK)[None, :]
    dist = jnp.abs(qi + A_BLOCK - kj)
    slopes = jnp.power(2.0, -8.0 * jnp.arange(1, A_HEADS + 1, dtype=F32) / A_HEADS)
    bias = -slopes[:, None, None] * dist.astype(F32)[None]
    return jnp.where((dist <= WINDOW)[None], bias, -jnp.inf)


def _layer(x, attn_norm, w_in, a_sink, b_q_norm, b_w_uq, b_kv_norm, b_w_ukv, out_norm_a, out_norm_b, w_o,
           mlp_norm, w_router, b_router, w_up, b_up, w_down, b_down):
    b, s, d = x.shape
    t = b * s
    win = jnp.pad(w_in, ((0, 0), (0, IN_COLS_PAD - w_in.shape[1]))).astype(BF16)
    wq = b_w_uq.reshape(B_Q_RANK, B_HEADS, B_NOPE_DIM + B_ROPE_DIM)
    wq_pe = jnp.pad(wq[:, :, B_NOPE_DIM:], ((0, 0), (0, 0), (0, LANES - B_ROPE_DIM)))
    wuq = jnp.concatenate([wq[:, :, :B_NOPE_DIM].reshape(B_Q_RANK, -1), wq_pe.reshape(B_Q_RANK, -1)],
                          axis=1).astype(BF16)
    wkv = b_w_ukv.reshape(B_KV_RANK, B_HEADS, B_NOPE_DIM + B_V_DIM)
    wukv = jnp.concatenate([wkv[:, :, :B_NOPE_DIM].reshape(B_KV_RANK, -1),
                            wkv[:, :, B_NOPE_DIM:].reshape(B_KV_RANK, -1)], axis=1).astype(BF16)
    cos, sin = _rope_tables(s)

    qa, ka, va, qb, kb, vb = _prologue(x, attn_norm[None], win, b_q_norm[None], wuq, b_kv_norm[None], wukv,
                                       cos, sin)
    ya = _window_attn(qa, ka, va, a_sink, _window_bias())
    yb = _mla_attn(qb, kb, vb)
    x1, xm, eidx, gates, rank, counts = _out_router(
        ya.reshape(t, A_WIDTH), yb.reshape(t, B_WIDTH), x.reshape(t, d), out_norm_a[None], out_norm_b[None],
        w_o.astype(BF16), mlp_norm[None], w_router, b_router[None])

    bm = MOE_BM
    counts = counts[0]
    padded = (counts + bm - 1) // bm * bm
    ends = jnp.cumsum(padded)
    starts = ends - padded
    pos = (starts[eidx] + rank).reshape(t * TOP_K)
    nblk = (t * TOP_K + N_EXPERTS * (bm - 1) + bm - 1) // bm
    nact = (ends[-1] // bm).astype(jnp.int32)
    blk_row = jnp.minimum(jnp.arange(nblk, dtype=jnp.int32), nact - 1) * bm
    bexp = jnp.minimum(jnp.searchsorted(ends, blk_row, side="right"), N_EXPERTS - 1).astype(jnp.int32)
    nact = nact.reshape(1)

    xs = _dispatch((starts + counts).astype(jnp.int32), (padded - counts).astype(jnp.int32), nact, pos, xm,
                   nblk * bm)
    ys = _experts(bexp, nact, xs, w_up, b_up[:, None, :], w_down, b_down[:, None, :])
    return x1, pos, gates, ys


def kernel(x, attn_norm, w_in, a_sink, b_q_norm, b_w_uq, b_kv_norm, b_w_ukv, out_norm_a, out_norm_b, w_o,
           mlp_norm, w_router, b_router, w_up, b_up, w_down, b_down, final_norm):
    b, s, d = x.shape
    assert d == D_MODEL and s % TQ_MLA == 0 and s % A_BLOCK == 0 and attn_norm.shape[0] == 1
    x1, pos, gates, ys = _layer(x, attn_norm[0], w_in[0], a_sink[0], b_q_norm[0], b_w_uq[0], b_kv_norm[0],
                                b_w_ukv[0], out_norm_a[0], out_norm_b[0], w_o[0], mlp_norm[0], w_router[0],
                                b_router[0], w_up[0], b_up[0], w_down[0], b_down[0])
    out = _combine(pos, gates, x1, final_norm[None], ys)
    return out.reshape(b, s, d)
```

```python
import functools
import math

import jax
import jax.numpy as jnp
from jax import lax
from jax.experimental import pallas as pl
from jax.experimental.pallas import tpu as pltpu

D_MODEL = 2048
A_HEADS, A_KV_HEADS, A_HEAD_DIM = 16, 4, 64
A_GROUP = A_HEADS // A_KV_HEADS
WINDOW = 128
A_BLOCK = 128
B_HEADS, B_Q_RANK, B_KV_RANK = 8, 512, 256
B_NOPE_DIM, B_ROPE_DIM, B_V_DIM = 128, 64, 128
ROPE_THETA = 10000.0
A_WIDTH = A_HEADS * A_HEAD_DIM
B_WIDTH = B_HEADS * B_V_DIM
A_KV_COLS = A_KV_HEADS * A_HEAD_DIM
N_EXPERTS, TOP_K, D_FF = 32, 4, 2048
SWIGLU_ALPHA, SWIGLU_LIMIT = 1.702, 7.0
EPS = 1e-5

LANES = 128
SUBLANES = 8
B_QK_PAD = 2 * LANES
IN_COLS_PAD = A_WIDTH + 2 * A_KV_COLS + B_Q_RANK + B_KV_RANK + LANES
VMEM_LIMIT = 56 * 1024 * 1024

TM_PRO = 256
TQ_MLA = 512
TM_OUT = 256
TM_DISPATCH = 512
TM_COMBINE = 128
MOE_BM = 512
MOE_TF = 256
WAIT_UNROLL = 16

BF16 = jnp.bfloat16
F32 = jnp.float32


def _rms(x, g):
    return x * lax.rsqrt(jnp.mean(x * x, axis=-1, keepdims=True) + EPS) * g


def _const_spec(shape):
    nd = len(shape)
    return pl.BlockSpec(shape, lambda *_: (0,) * nd, pipeline_mode=pl.Buffered(1))


def _rope(x, cos, sin):
    lane = lax.broadcasted_iota(jnp.int32, x.shape, 1)
    up = pltpu.roll(x, LANES - B_ROPE_DIM // 2, 1)
    dn = pltpu.roll(x, B_ROPE_DIM // 2, 1)
    sw = jnp.where(lane < B_ROPE_DIM // 2, -up, jnp.where(lane < B_ROPE_DIM, dn, 0.0))
    return x * cos + sw * sin


def _prologue_body(x_ref, g_ref, win_ref, qn_ref, wuq_ref, kvn_ref, wukv_ref, cos_ref, sin_ref,
                   qa_ref, ka_ref, va_ref, qb_ref, kb_ref, vb_ref):
    x = x_ref[0]
    xn = _rms(x, g_ref[...]).astype(BF16)
    proj = jnp.dot(xn, win_ref[...], preferred_element_type=F32)
    c0 = A_WIDTH
    c1 = c0 + A_KV_COLS
    c2 = c1 + A_KV_COLS
    c3 = c2 + B_Q_RANK
    c4 = c3 + B_KV_RANK
    a_scale = A_HEAD_DIM ** -0.5
    for h in range(A_HEADS):
        qa_ref[0, h] = (proj[:, h * A_HEAD_DIM:(h + 1) * A_HEAD_DIM] * a_scale).astype(BF16)
    for h in range(A_KV_HEADS):
        ka_ref[0, h] = proj[:, c0 + h * A_HEAD_DIM:c0 + (h + 1) * A_HEAD_DIM].astype(BF16)
        va_ref[0, h] = proj[:, c1 + h * A_HEAD_DIM:c1 + (h + 1) * A_HEAD_DIM].astype(BF16)
    cq = _rms(proj[:, c2:c3], qn_ref[...]).astype(BF16)
    q = jnp.dot(cq, wuq_ref[...], preferred_element_type=F32)
    ckv = _rms(proj[:, c3:c4], kvn_ref[...]).astype(BF16)
    kv = jnp.dot(ckv, wukv_ref[...], preferred_element_type=F32)
    cos = cos_ref[...]
    sin = sin_ref[...]
    kpe = _rope(proj[:, c4:c4 + LANES], cos, sin).astype(BF16)
    b_scale = (B_NOPE_DIM + B_ROPE_DIM) ** -0.5
    hw = B_HEADS * LANES
    for h in range(B_HEADS):
        sl = slice(h * LANES, (h + 1) * LANES)
        qb_ref[0, h, :, :LANES] = (q[:, sl] * b_scale).astype(BF16)
        qpe = _rope(q[:, hw + h * LANES:hw + (h + 1) * LANES], cos, sin)
        qb_ref[0, h, :, LANES:] = (qpe * b_scale).astype(BF16)
        kb_ref[0, h, :, :LANES] = kv[:, sl].astype(BF16)
        kb_ref[0, h, :, LANES:] = kpe
        vb_ref[0, h] = kv[:, hw + h * LANES:hw + (h + 1) * LANES].astype(BF16)


def _prologue(x, g, win, qn, wuq, kvn, wukv, cos, sin):
    b, s, d = x.shape
    tm = TM_PRO
    grid = (b, s // tm)
    sds = jax.ShapeDtypeStruct
    out_shape = (
        sds((b, A_HEADS, s, A_HEAD_DIM), BF16),
        sds((b, A_KV_HEADS, s, A_HEAD_DIM), BF16),
        sds((b, A_KV_HEADS, s, A_HEAD_DIM), BF16),
        sds((b, B_HEADS, s, B_QK_PAD), BF16),
        sds((b, B_HEADS, s, B_QK_PAD), BF16),
        sds((b, B_HEADS, s, B_V_DIM), BF16),
    )

    def hspec(nh, w):
        return pl.BlockSpec((1, nh, tm, w), lambda bi, i: (bi, 0, i, 0))

    return pl.pallas_call(
        _prologue_body,
        grid=grid,
        in_specs=[
            pl.BlockSpec((1, tm, d), lambda bi, i: (bi, i, 0)),
            _const_spec(g.shape), _const_spec(win.shape), _const_spec(qn.shape), _const_spec(wuq.shape),
            _const_spec(kvn.shape), _const_spec(wukv.shape),
            pl.BlockSpec((tm, LANES), lambda bi, i: (i, 0)),
            pl.BlockSpec((tm, LANES), lambda bi, i: (i, 0)),
        ],
        out_specs=(hspec(A_HEADS, A_HEAD_DIM), hspec(A_KV_HEADS, A_HEAD_DIM), hspec(A_KV_HEADS, A_HEAD_DIM),
                   hspec(B_HEADS, B_QK_PAD), hspec(B_HEADS, B_QK_PAD), hspec(B_HEADS, B_V_DIM)),
        out_shape=out_shape,
        compiler_params=pltpu.CompilerParams(dimension_semantics=("parallel", "parallel"),
                                             vmem_limit_bytes=VMEM_LIMIT),
        name="prologue",
    )(x, g, win, qn, wuq, kvn, wukv, cos, sin)


def _window_body(sink_ref, q_ref, kp_ref, kc_ref, kn_ref, vp_ref, vc_ref, vn_ref, bias_ref, o_ref):
    n = pl.program_id(1)
    nb = pl.num_programs(1)
    rows = A_GROUP * A_BLOCK
    col = lax.broadcasted_iota(jnp.int32, (1, 3 * A_BLOCK), 1)
    outside = ((col < A_BLOCK) & (n == 0)) | ((col >= 2 * A_BLOCK) & (n == nb - 1))
    for hk in range(A_KV_HEADS):
        q4 = q_ref[0, hk * A_GROUP:(hk + 1) * A_GROUP].reshape(rows, A_HEAD_DIM)
        kband = jnp.concatenate([kp_ref[0, hk], kc_ref[0, hk], kn_ref[0, hk]], axis=0)
        vband = jnp.concatenate([vp_ref[0, hk], vc_ref[0, hk], vn_ref[0, hk]], axis=0)
        s = lax.dot_general(q4, kband, (((1,), (1,)), ((), ())), preferred_element_type=F32)
        s = s + bias_ref[hk * A_GROUP:(hk + 1) * A_GROUP].reshape(rows, 3 * A_BLOCK)
        s = jnp.where(outside, -jnp.inf, s)
        sink = jnp.concatenate(
            [jnp.full((A_BLOCK, 1), sink_ref[hk * A_GROUP + g], F32) for g in range(A_GROUP)], axis=0)
        m = jnp.maximum(jnp.max(s, axis=-1, keepdims=True), sink)
        p = jnp.exp(s - m)
        den = jnp.sum(p, axis=-1, keepdims=True) + jnp.exp(sink - m)
        o = jnp.dot(p.astype(BF16), vband, preferred_element_type=F32) / den
        for g in range(A_GROUP):
            h = hk * A_GROUP + g
            o_ref[0, :, h * A_HEAD_DIM:(h + 1) * A_HEAD_DIM] = o[g * A_BLOCK:(g + 1) * A_BLOCK].astype(BF16)


def _window_attn(qa, ka, va, sink, bias):
    b, _, s, _ = qa.shape
    nb = s // A_BLOCK
    kv_blk = (1, A_KV_HEADS, A_BLOCK, A_HEAD_DIM)
    prev = pl.BlockSpec(kv_blk, lambda bi, n: (bi, 0, jnp.maximum(n - 1, 0), 0))
    cur = pl.BlockSpec(kv_blk, lambda bi, n: (bi, 0, n, 0))
    nxt = pl.BlockSpec(kv_blk, lambda bi, n: (bi, 0, jnp.minimum(n + 1, nb - 1), 0))
    return pl.pallas_call(
        _window_body,
        grid=(b, nb),
        in_specs=[
            pl.BlockSpec(memory_space=pltpu.SMEM),
            pl.BlockSpec((1, A_HEADS, A_BLOCK, A_HEAD_DIM), lambda bi, n: (bi, 0, n, 0)),
            prev, cur, nxt, prev, cur, nxt,
            _const_spec(bias.shape),
        ],
        out_specs=pl.BlockSpec((1, A_BLOCK, A_WIDTH), lambda bi, n: (bi, n, 0)),
        out_shape=jax.ShapeDtypeStruct((b, s, A_WIDTH), BF16),
        compiler_params=pltpu.CompilerParams(dimension_semantics=("parallel", "parallel"),
                                             vmem_limit_bytes=VMEM_LIMIT),
        name="window_attn",
    )(sink, qa, ka, ka, ka, va, va, va, bias)


def _mla_body(q_ref, k_ref, v_ref, o_ref):
    s = lax.dot_general(q_ref[0, 0], k_ref[0, 0], (((1,), (1,)), ((), ())), preferred_element_type=F32)
    m = jnp.max(s, axis=-1, keepdims=True)
    p = jnp.exp(s - m)
    den = jnp.sum(p, axis=-1, keepdims=True)
    o = jnp.dot(p.astype(BF16), v_ref[0, 0], preferred_element_type=F32) / den
    o_ref[0] = o.astype(BF16)


def _mla_attn(qb, kb, vb):
    b, nh, s, _ = qb.shape
    tq = min(TQ_MLA, s)
    return pl.pallas_call(
        _mla_body,
        grid=(b, nh, s // tq),
        in_specs=[
            pl.BlockSpec((1, 1, tq, B_QK_PAD), lambda bi, h, i: (bi, h, i, 0)),
            pl.BlockSpec((1, 1, s, B_QK_PAD), lambda bi, h, i: (bi, h, 0, 0)),
            pl.BlockSpec((1, 1, s, B_V_DIM), lambda bi, h, i: (bi, h, 0, 0)),
        ],
        out_specs=pl.BlockSpec((1, tq, B_V_DIM), lambda bi, h, i: (bi, i, h)),
        out_shape=jax.ShapeDtypeStruct((b, s, B_WIDTH), BF16),
        compiler_params=pltpu.CompilerParams(dimension_semantics=("parallel", "parallel", "parallel"),
                                             vmem_limit_bytes=VMEM_LIMIT),
        name="mla_attn",
    )(qb, kb, vb)


def _out_router_body(ya_ref, yb_ref, x_ref, ga_ref, gb_ref, wo_ref, gm_ref, wr_ref, br_ref,
                     x1_ref, xm_ref, eidx_ref, gate_ref, rank_ref, cnt_ref, run_ref):
    i = pl.program_id(0)

    @pl.when(i == 0)
    def _():
        run_ref[...] = jnp.zeros_like(run_ref)

    na = _rms(ya_ref[...].astype(F32), ga_ref[...]).astype(BF16)
    nb = _rms(yb_ref[...].astype(F32), gb_ref[...]).astype(BF16)
    att = jnp.dot(na, wo_ref[:A_WIDTH, :], preferred_element_type=F32)
    att = att + jnp.dot(nb, wo_ref[A_WIDTH:, :], preferred_element_type=F32)
    x1 = x_ref[...] + att
    x1_ref[...] = x1
    hn = _rms(x1, gm_ref[...])
    xm_ref[...] = hn
    logits = jnp.dot(hn, wr_ref[...], preferred_element_type=F32,
                     precision=lax.Precision.HIGHEST) + br_ref[...]
    tm = logits.shape[0]
    lane = lax.broadcasted_iota(jnp.int32, (tm, N_EXPERTS), 1)
    work = logits
    sel = jnp.zeros((tm, N_EXPERTS), F32)
    hots, vals, idxs = [], [], []
    for _k in range(TOP_K):
        mx = jnp.max(work, axis=-1, keepdims=True)
        idx = jnp.min(jnp.where(work == mx, lane, N_EXPERTS), axis=-1, keepdims=True)
        hot = lane == idx
        hots.append(hot)
        vals.append(mx)
        idxs.append(idx)
        sel = sel + hot.astype(F32)
        work = jnp.where(hot, -jnp.inf, work)
    exps = [jnp.exp(v - vals[0]) for v in vals]
    den = exps[0] + exps[1] + exps[2] + exps[3]
    r_i = lax.broadcasted_iota(jnp.int32, (tm, tm), 0)
    c_i = lax.broadcasted_iota(jnp.int32, (tm, tm), 1)
    tri = (c_i < r_i).astype(BF16)
    before = jnp.dot(tri, sel.astype(BF16), preferred_element_type=F32) + run_ref[...]
    lane4 = lax.broadcasted_iota(jnp.int32, (tm, TOP_K), 1)
    eidx = jnp.zeros((tm, TOP_K), jnp.int32)
    gate = jnp.zeros((tm, TOP_K), F32)
    rank = jnp.zeros((tm, TOP_K), jnp.int32)
    for k in range(TOP_K):
        rk = jnp.sum(jnp.where(hots[k], before, 0.0), axis=-1, keepdims=True).astype(jnp.int32)
        eidx = jnp.where(lane4 == k, idxs[k], eidx)
        gate = jnp.where(lane4 == k, exps[k] / den, gate)
        rank = jnp.where(lane4 == k, rk, rank)
    eidx_ref[...] = eidx
    gate_ref[...] = gate
    rank_ref[...] = rank
    run = run_ref[...] + jnp.sum(sel, axis=0, keepdims=True)
    run_ref[...] = run
    cnt_ref[...] = run.astype(jnp.int32)


def _out_router(ya, yb, x2d, ga, gb, wo, gm, wr, br):
    t, d = x2d.shape
    tm = min(TM_OUT, t)
    sds = jax.ShapeDtypeStruct

    def row(w):
        return pl.BlockSpec((tm, w), lambda i: (i, 0))

    return pl.pallas_call(
        _out_router_body,
        grid=(t // tm,),
        in_specs=[row(A_WIDTH), row(B_WIDTH), row(d), _const_spec(ga.shape), _const_spec(gb.shape),
                  _const_spec(wo.shape), _const_spec(gm.shape), _const_spec(wr.shape), _const_spec(br.shape)],
        out_specs=(row(d), row(d), row(TOP_K), row(TOP_K), row(TOP_K),
                   pl.BlockSpec((1, N_EXPERTS), lambda i: (0, 0))),
        out_shape=(sds((t, d), F32), sds((t, d), F32), sds((t, TOP_K), jnp.int32), sds((t, TOP_K), F32),
                   sds((t, TOP_K), jnp.int32), sds((1, N_EXPERTS), jnp.int32)),
        scratch_shapes=[pltpu.VMEM((1, N_EXPERTS), F32)],
        compiler_params=pltpu.CompilerParams(dimension_semantics=("arbitrary",),
                                             vmem_limit_bytes=VMEM_LIMIT),
        name="out_router",
    )(ya, yb, x2d, ga, gb, wo, gm, wr, br)


def _row_copy(src_ref, src_row, dst_ref, dst_row, sem):
    return pltpu.make_async_copy(src_ref.at[pl.ds(src_row, 1)], dst_ref.at[pl.ds(dst_row, 1)], sem)


def _zero_fill_pads(pad_start_ref, pad_len_ref, nact_ref, xs_ref, zero_ref, zsem):
    bm = zero_ref.shape[0]
    nblk = xs_ref.shape[0] // bm
    zero_ref[...] = jnp.zeros_like(zero_ref)

    def per_expert(e, c):
        start = pad_start_ref[e]
        n = pad_len_ref[e]
        head = jnp.minimum((-start) & (SUBLANES - 1), n)

        def one(r, c2):
            cp = _row_copy(zero_ref, 0, xs_ref, start + r, zsem)
            cp.start()
            cp.wait()
            return c2

        lax.fori_loop(0, head, one, 0)
        off = start + head
        rest = n - head
        p = bm // 2
        while p >= SUBLANES:
            take = (rest & p) != 0
            dst = pl.multiple_of(off, SUBLANES)

            @pl.when(take)
            def _(p=p, dst=dst):
                cp = pltpu.make_async_copy(zero_ref.at[pl.ds(0, p)], xs_ref.at[pl.ds(dst, p)], zsem)
                cp.start()
                cp.wait()

            off = off + jnp.where(take, p, 0)
            p //= 2
        return c

    lax.fori_loop(0, N_EXPERTS, per_expert, 0)

    def tail(j, c):
        cp = pltpu.make_async_copy(zero_ref, xs_ref.at[pl.ds(pl.multiple_of(j * bm, bm), bm)], zsem)
        cp.start()
        cp.wait()
        return c

    lax.fori_loop(nact_ref[0], nblk, tail, 0)


def _dispatch_body(pad_start_ref, pad_len_ref, nact_ref, pos_ref, xm_ref, xs_ref, zero_ref, sem, zsem):
    tm = pos_ref.shape[0] // TOP_K

    @pl.when(pl.program_id(0) == 0)
    def _():
        _zero_fill_pads(pad_start_ref, pad_len_ref, nact_ref, xs_ref, zero_ref, zsem)

    def issue(i, c):
        for k in range(TOP_K):
            _row_copy(xm_ref, i, xs_ref, pos_ref[i * TOP_K + k], sem).start()
        return c

    lax.fori_loop(0, tm, issue, 0)

    def drain(i, c):
        for _ in range(WAIT_UNROLL):
            _row_copy(xm_ref, 0, xs_ref, 0, sem).wait()
        return c

    lax.fori_loop(0, tm * TOP_K // WAIT_UNROLL, drain, 0)


def _dispatch(pad_start, pad_len, nact, pos_flat, xm, n_slots):
    t, d = xm.shape
    tm = min(TM_DISPATCH, t)
    grid_spec = pltpu.PrefetchScalarGridSpec(
        num_scalar_prefetch=3,
        grid=(t // tm,),
        in_specs=[pl.BlockSpec((tm * TOP_K,), lambda i, *_: (i,), memory_space=pltpu.SMEM),
                  pl.BlockSpec((tm, d), lambda i, *_: (i, 0))],
        out_specs=pl.BlockSpec(memory_space=pl.ANY),
        scratch_shapes=[pltpu.VMEM((MOE_BM, d), F32), pltpu.SemaphoreType.DMA(()), pltpu.SemaphoreType.DMA(())],
    )
    return pl.pallas_call(
        _dispatch_body,
        grid_spec=grid_spec,
        out_shape=jax.ShapeDtypeStruct((n_slots, d), F32),
        compiler_params=pltpu.CompilerParams(dimension_semantics=("arbitrary",), has_side_effects=True,
                                             vmem_limit_bytes=VMEM_LIMIT),
        name="dispatch",
    )(pad_start, pad_len, nact, pos_flat, xm)


def _experts_body(bexp_ref, nact_ref, x_ref, wg_ref, wu_ref, bg_ref, bu_ref, wd_ref, bd_ref,
                  y_ref, xb_ref, acc_ref):
    i = pl.program_id(0)
    f = pl.program_id(1)
    nf = pl.num_programs(1)
    active = i < nact_ref[0]

    @pl.when(active)
    def _():
        @pl.when(f == 0)
        def _():
            xb_ref[...] = x_ref[...].astype(BF16)
            acc_ref[...] = jnp.zeros_like(acc_ref)

        xb = xb_ref[...]
        hg = jnp.dot(xb, wg_ref[0].astype(BF16), preferred_element_type=F32) + bg_ref[0]
        hu = jnp.dot(xb, wu_ref[0].astype(BF16), preferred_element_type=F32) + bu_ref[0]
        gate = jnp.minimum(hg, SWIGLU_LIMIT)
        up = jnp.clip(hu, -SWIGLU_LIMIT, SWIGLU_LIMIT)
        glu = gate / (1.0 + jnp.exp(-SWIGLU_ALPHA * gate))
        act = ((up + 1.0) * glu).astype(BF16)
        acc_ref[...] += jnp.dot(act, wd_ref[0].astype(BF16), preferred_element_type=F32)

        @pl.when(f == nf - 1)
        def _():
            y_ref[...] = acc_ref[...] + bd_ref[0]

    @pl.when(jnp.logical_not(active) & (f == nf - 1))
    def _():
        y_ref[...] = jnp.zeros_like(y_ref)


def _experts(bexp, nact, xs, w_up, b_up, w_down, b_down):
    n_slots, d = xs.shape
    bm = MOE_BM
    nblk = n_slots // bm
    nf = D_FF // MOE_TF

    def blk(i, na):
        return jnp.minimum(i, na[0] - 1)

    def ftile(i, f, na):
        return jnp.where(i < na[0], f, nf - 1)

    grid_spec = pltpu.PrefetchScalarGridSpec(
        num_scalar_prefetch=2,
        grid=(nblk, nf),
        in_specs=[
            pl.BlockSpec((bm, d), lambda i, f, be, na: (blk(i, na), 0)),
            pl.BlockSpec((1, d, MOE_TF), lambda i, f, be, na: (be[blk(i, na)], 0, ftile(i, f, na))),
            pl.BlockSpec((1, d, MOE_TF), lambda i, f, be, na: (be[blk(i, na)], 0, nf + ftile(i, f, na))),
            pl.BlockSpec((1, 1, MOE_TF), lambda i, f, be, na: (be[blk(i, na)], 0, ftile(i, f, na))),
            pl.BlockSpec((1, 1, MOE_TF), lambda i, f, be, na: (be[blk(i, na)], 0, nf + ftile(i, f, na))),
            pl.BlockSpec((1, MOE_TF, d), lambda i, f, be, na: (be[blk(i, na)], ftile(i, f, na), 0)),
            pl.BlockSpec((1, 1, d), lambda i, f, be, na: (be[blk(i, na)], 0, 0)),
        ],
        out_specs=pl.BlockSpec((bm, d), lambda i, f, be, na: (i, 0)),
        scratch_shapes=[pltpu.VMEM((bm, d), BF16), pltpu.VMEM((bm, d), F32)],
    )
    return pl.pallas_call(
        _experts_body,
        grid_spec=grid_spec,
        out_shape=jax.ShapeDtypeStruct((n_slots, d), F32),
        compiler_params=pltpu.CompilerParams(dimension_semantics=("arbitrary", "arbitrary"),
                                             vmem_limit_bytes=VMEM_LIMIT),
        name="experts",
    )(bexp, nact, xs, w_up, w_up, b_up, b_up, w_down, b_down)


def _combine_body(pos_ref, gate_ref, x1_ref, gf_ref, ys_ref, o_ref, buf_ref, sem):
    tm = x1_ref.shape[0]

    def issue(i, c):
        for k in range(TOP_K):
            _row_copy(ys_ref, pos_ref[i * TOP_K + k], buf_ref.at[k], i, sem).start()
        return c

    lax.fori_loop(0, tm, issue, 0)

    def drain(i, c):
        for _ in range(WAIT_UNROLL):
            _row_copy(ys_ref, 0, buf_ref.at[0], 0, sem).wait()
        return c

    lax.fori_loop(0, tm * TOP_K // WAIT_UNROLL, drain, 0)
    gate = gate_ref[...]
    acc = x1_ref[...]
    for k in range(TOP_K):
        acc = acc + gate[:, k:k + 1] * buf_ref[k]
    o_ref[...] = _rms(acc, gf_ref[...])


def _combine(pos_flat, gates, x1, gf, ys):
    t, d = x1.shape
    tm = min(TM_COMBINE, t)
    return pl.pallas_call(
        _combine_body,
        grid=(t // tm,),
        in_specs=[pl.BlockSpec((tm * TOP_K,), lambda i: (i,), memory_space=pltpu.SMEM),
                  pl.BlockSpec((tm, TOP_K), lambda i: (i, 0)),
                  pl.BlockSpec((tm, d), lambda i: (i, 0)),
                  _const_spec(gf.shape),
                  pl.BlockSpec(memory_space=pl.ANY)],
        out_specs=pl.BlockSpec((tm, d), lambda i: (i, 0)),
        out_shape=jax.ShapeDtypeStruct((t, d), F32),
        scratch_shapes=[pltpu.VMEM((TOP_K, tm, d), F32), pltpu.SemaphoreType.DMA(())],
        compiler_params=pltpu.CompilerParams(dimension_semantics=("arbitrary",),
                                             vmem_limit_bytes=VMEM_LIMIT),
        name="combine",
    )(pos_flat, gates, x1, gf, ys)


def _rope_tables(s):
    pos = jnp.arange(s, dtype=F32)
    inv_freq = jnp.power(ROPE_THETA, -jnp.arange(0, B_ROPE_DIM, 2, dtype=F32) / B_ROPE_DIM)
    ang = pos[:, None] * inv_freq[None, :]
    cos, sin = jnp.cos(ang), jnp.sin(ang)
    pad = LANES - B_ROPE_DIM
    cos = jnp.concatenate([cos, cos, jnp.ones((s, pad), F32)], axis=1)
    sin = jnp.concatenate([sin, sin, jnp.zeros((s, pad), F32)], axis=1)
    return cos, sin


def _window_bias():
    qi = jnp.arange(A_BLOCK)[:, None]
    kj = jnp.arange(3 * A_BLOCK)[None, :]
    dist = jnp.abs(qi + A_BLOCK - kj)
    slopes = jnp.power(2.0, -8.0 * jnp.arange(1, A_HEADS + 1, dtype=F32) / A_HEADS)
    bias = -slopes[:, None, None] * dist.astype(F32)[None]
    return jnp.where((dist <= WINDOW)[None], bias, -jnp.inf)


def _layer(x, attn_norm, w_in, a_sink, b_q_norm, b_w_uq, b_kv_norm, b_w_ukv, out_norm_a, out_norm_b, w_o,
           mlp_norm, w_router, b_router, w_up, b_up, w_down, b_down):
    b, s, d = x.shape
    t = b * s
    win = jnp.pad(w_in, ((0, 0), (0, IN_COLS_PAD - w_in.shape[1]))).astype(BF16)
    wq = b_w_uq.reshape(B_Q_RANK, B_HEADS, B_NOPE_DIM + B_ROPE_DIM)
    wq_pe = jnp.pad(wq[:, :, B_NOPE_DIM:], ((0, 0), (0, 0), (0, LANES - B_ROPE_DIM)))
    wuq = jnp.concatenate([wq[:, :, :B_NOPE_DIM].reshape(B_Q_RANK, -1), wq_pe.reshape(B_Q_RANK, -1)],
                          axis=1).astype(BF16)
    wkv = b_w_ukv.reshape(B_KV_RANK, B_HEADS, B_NOPE_DIM + B_V_DIM)
    wukv = jnp.concatenate([wkv[:, :, :B_NOPE_DIM].reshape(B_KV_RANK, -1),
                            wkv[:, :, B_NOPE_DIM:].reshape(B_KV_RANK, -1)], axis=1).astype(BF16)
    cos, sin = _rope_tables(s)

    qa, ka, va, qb, kb, vb = _prologue(x, attn_norm[None], win, b_q_norm[None], wuq, b_kv_norm[None], wukv,
                                       cos, sin)
    ya = _window_attn(qa, ka, va, a_sink, _window_bias())
    yb = _mla_attn(qb, kb, vb)
    x1, xm, eidx, gates, rank, counts = _out_router(
        ya.reshape(t, A_WIDTH), yb.reshape(t, B_WIDTH), x.reshape(t, d), out_norm_a[None], out_norm_b[None],
        w_o.astype(BF16), mlp_norm[None], w_router, b_router[None])

    bm = MOE_BM
    counts = counts[0]
    padded = (counts + bm - 1) // bm * bm
    ends = jnp.cumsum(padded)
    starts = ends - padded
    pos = (starts[eidx] + rank).reshape(t * TOP_K)
    nblk = (t * TOP_K + N_EXPERTS * (bm - 1) + bm - 1) // bm
    nact = (ends[-1] // bm).astype(jnp.int32)
    blk_row = jnp.minimum(jnp.arange(nblk, dtype=jnp.int32), nact - 1) * bm
    bexp = jnp.minimum(jnp.sum(blk_row[:, None] >= ends[None, :], axis=1), N_EXPERTS - 1).astype(jnp.int32)
    nact = nact.reshape(1)

    xs = _dispatch((starts + counts).astype(jnp.int32), (padded - counts).astype(jnp.int32), nact, pos, xm,
                   nblk * bm)
    ys = _experts(bexp, nact, xs, w_up, b_up[:, None, :], w_down, b_down[:, None, :])
    return x1, pos, gates, ys


def kernel(x, attn_norm, w_in, a_sink, b_q_norm, b_w_uq, b_kv_norm, b_w_ukv, out_norm_a, out_norm_b, w_o,
           mlp_norm, w_router, b_router, w_up, b_up, w_down, b_down, final_norm):
    b, s, d = x.shape
    assert d == D_MODEL and s % TQ_MLA == 0 and s % A_BLOCK == 0 and attn_norm.shape[0] == 1
    x1, pos, gates, ys = _layer(x, attn_norm[0], w_in[0], a_sink[0], b_q_norm[0], b_w_uq[0], b_kv_norm[0],
                                b_w_ukv[0], out_norm_a[0], out_norm_b[0], w_o[0], mlp_norm[0], w_router[0],
                                b_router[0], w_up[0], b_up[0], w_down[0], b_down[0])
    out = _combine(pos, gates, x1, final_norm[None], ys)
    return out.reshape(b, s, d)
```

```python
import functools
import math

import jax
import jax.numpy as jnp
from jax import lax
from jax.experimental import pallas as pl
from jax.experimental.pallas import tpu as pltpu

D_MODEL = 2048
A_HEADS, A_KV_HEADS, A_HEAD_DIM = 16, 4, 64
A_GROUP = A_HEADS // A_KV_HEADS
WINDOW = 128
A_BLOCK = 128
B_HEADS, B_Q_RANK, B_KV_RANK = 8, 512, 256
B_NOPE_DIM, B_ROPE_DIM, B_V_DIM = 128, 64, 128
ROPE_THETA = 10000.0
A_WIDTH = A_HEADS * A_HEAD_DIM
B_WIDTH = B_HEADS * B_V_DIM
A_KV_COLS = A_KV_HEADS * A_HEAD_DIM
N_EXPERTS, TOP_K, D_FF = 32, 4, 2048
SWIGLU_ALPHA, SWIGLU_LIMIT = 1.702, 7.0
EPS = 1e-5

LANES = 128
SUBLANES = 8
B_QK_PAD = 2 * LANES
IN_COLS_PAD = A_WIDTH + 2 * A_KV_COLS + B_Q_RANK + B_KV_RANK + LANES
VMEM_LIMIT = 56 * 1024 * 1024

TM_PRO = 256
TQ_MLA = 512
TM_OUT = 256
TM_DISPATCH = 512
TM_COMBINE = 128
MOE_BM = 512
MOE_SUB = 256
MOE_RM = 2048
MOE_TF = 256
WAIT_UNROLL = 16

BF16 = jnp.bfloat16
F32 = jnp.float32


def _rms(x, g):
    return x * lax.rsqrt(jnp.mean(x * x, axis=-1, keepdims=True) + EPS) * g


def _const_spec(shape):
    nd = len(shape)
    return pl.BlockSpec(shape, lambda *_: (0,) * nd, pipeline_mode=pl.Buffered(1))


def _rope(x, cos, sin):
    lane = lax.broadcasted_iota(jnp.int32, x.shape, 1)
    up = pltpu.roll(x, LANES - B_ROPE_DIM // 2, 1)
    dn = pltpu.roll(x, B_ROPE_DIM // 2, 1)
    sw = jnp.where(lane < B_ROPE_DIM // 2, -up, jnp.where(lane < B_ROPE_DIM, dn, 0.0))
    return x * cos + sw * sin


def _prologue_body(x_ref, g_ref, win_ref, qn_ref, wuq_ref, kvn_ref, wukv_ref, cos_ref, sin_ref,
                   qa_ref, ka_ref, va_ref, qb_ref, kb_ref, vb_ref):
    x = x_ref[0]
    xn = _rms(x, g_ref[...]).astype(BF16)
    proj = jnp.dot(xn, win_ref[...], preferred_element_type=F32)
    c0 = A_WIDTH
    c1 = c0 + A_KV_COLS
    c2 = c1 + A_KV_COLS
    c3 = c2 + B_Q_RANK
    c4 = c3 + B_KV_RANK
    a_scale = A_HEAD_DIM ** -0.5
    for h in range(A_HEADS):
        qa_ref[0, h] = (proj[:, h * A_HEAD_DIM:(h + 1) * A_HEAD_DIM] * a_scale).astype(BF16)
    for h in range(A_KV_HEADS):
        ka_ref[0, h] = proj[:, c0 + h * A_HEAD_DIM:c0 + (h + 1) * A_HEAD_DIM].astype(BF16)
        va_ref[0, h] = proj[:, c1 + h * A_HEAD_DIM:c1 + (h + 1) * A_HEAD_DIM].astype(BF16)
    cq = _rms(proj[:, c2:c3], qn_ref[...]).astype(BF16)
    q = jnp.dot(cq, wuq_ref[...], preferred_element_type=F32)
    ckv = _rms(proj[:, c3:c4], kvn_ref[...]).astype(BF16)
    kv = jnp.dot(ckv, wukv_ref[...], preferred_element_type=F32)
    cos = cos_ref[...]
    sin = sin_ref[...]
    kpe = _rope(proj[:, c4:c4 + LANES], cos, sin).astype(BF16)
    b_scale = (B_NOPE_DIM + B_ROPE_DIM) ** -0.5
    hw = B_HEADS * LANES
    for h in range(B_HEADS):
        sl = slice(h * LANES, (h + 1) * LANES)
        qb_ref[0, h, :, :LANES] = (q[:, sl] * b_scale).astype(BF16)
        qpe = _rope(q[:, hw + h * LANES:hw + (h + 1) * LANES], cos, sin)
        qb_ref[0, h, :, LANES:] = (qpe * b_scale).astype(BF16)
        kb_ref[0, h, :, :LANES] = kv[:, sl].astype(BF16)
        kb_ref[0, h, :, LANES:] = kpe
        vb_ref[0, h] = kv[:, hw + h * LANES:hw + (h + 1) * LANES].astype(BF16)


def _prologue(x, g, win, qn, wuq, kvn, wukv, cos, sin):
    b, s, d = x.shape
    tm = TM_PRO
    grid = (b, s // tm)
    sds = jax.ShapeDtypeStruct
    out_shape = (
        sds((b, A_HEADS, s, A_HEAD_DIM), BF16),
        sds((b, A_KV_HEADS, s, A_HEAD_DIM), BF16),
        sds((b, A_KV_HEADS, s, A_HEAD_DIM), BF16),
        sds((b, B_HEADS, s, B_QK_PAD), BF16),
        sds((b, B_HEADS, s, B_QK_PAD), BF16),
        sds((b, B_HEADS, s, B_V_DIM), BF16),
    )

    def hspec(nh, w):
        return pl.BlockSpec((1, nh, tm, w), lambda bi, i: (bi, 0, i, 0))

    return pl.pallas_call(
        _prologue_body,
        grid=grid,
        in_specs=[
            pl.BlockSpec((1, tm, d), lambda bi, i: (bi, i, 0)),
            _const_spec(g.shape), _const_spec(win.shape), _const_spec(qn.shape), _const_spec(wuq.shape),
            _const_spec(kvn.shape), _const_spec(wukv.shape),
            pl.BlockSpec((tm, LANES), lambda bi, i: (i, 0)),
            pl.BlockSpec((tm, LANES), lambda bi, i: (i, 0)),
        ],
        out_specs=(hspec(A_HEADS, A_HEAD_DIM), hspec(A_KV_HEADS, A_HEAD_DIM), hspec(A_KV_HEADS, A_HEAD_DIM),
                   hspec(B_HEADS, B_QK_PAD), hspec(B_HEADS, B_QK_PAD), hspec(B_HEADS, B_V_DIM)),
        out_shape=out_shape,
        compiler_params=pltpu.CompilerParams(dimension_semantics=("parallel", "parallel"),
                                             vmem_limit_bytes=VMEM_LIMIT),
        name="prologue",
    )(x, g, win, qn, wuq, kvn, wukv, cos, sin)


def _window_body(sink_ref, q_ref, kp_ref, kc_ref, kn_ref, vp_ref, vc_ref, vn_ref, bias_ref, o_ref):
    n = pl.program_id(1)
    nb = pl.num_programs(1)
    rows = A_GROUP * A_BLOCK
    col = lax.broadcasted_iota(jnp.int32, (1, 3 * A_BLOCK), 1)
    outside = ((col < A_BLOCK) & (n == 0)) | ((col >= 2 * A_BLOCK) & (n == nb - 1))
    for hk in range(A_KV_HEADS):
        q4 = q_ref[0, hk * A_GROUP:(hk + 1) * A_GROUP].reshape(rows, A_HEAD_DIM)
        kband = jnp.concatenate([kp_ref[0, hk], kc_ref[0, hk], kn_ref[0, hk]], axis=0)
        vband = jnp.concatenate([vp_ref[0, hk], vc_ref[0, hk], vn_ref[0, hk]], axis=0)
        s = lax.dot_general(q4, kband, (((1,), (1,)), ((), ())), preferred_element_type=F32)
        s = s + bias_ref[hk * A_GROUP:(hk + 1) * A_GROUP].reshape(rows, 3 * A_BLOCK)
        s = jnp.where(outside, -jnp.inf, s)
        sink = jnp.concatenate(
            [jnp.full((A_BLOCK, 1), sink_ref[hk * A_GROUP + g], F32) for g in range(A_GROUP)], axis=0)
        m = jnp.maximum(jnp.max(s, axis=-1, keepdims=True), sink)
        p = jnp.exp(s - m)
        den = jnp.sum(p, axis=-1, keepdims=True) + jnp.exp(sink - m)
        o = jnp.dot(p.astype(BF16), vband, preferred_element_type=F32) / den
        for g in range(A_GROUP):
            h = hk * A_GROUP + g
            o_ref[0, :, h * A_HEAD_DIM:(h + 1) * A_HEAD_DIM] = o[g * A_BLOCK:(g + 1) * A_BLOCK].astype(BF16)


def _window_attn(qa, ka, va, sink, bias):
    b, _, s, _ = qa.shape
    nb = s // A_BLOCK
    kv_blk = (1, A_KV_HEADS, A_BLOCK, A_HEAD_DIM)
    prev = pl.BlockSpec(kv_blk, lambda bi, n: (bi, 0, jnp.maximum(n - 1, 0), 0))
    cur = pl.BlockSpec(kv_blk, lambda bi, n: (bi, 0, n, 0))
    nxt = pl.BlockSpec(kv_blk, lambda bi, n: (bi, 0, jnp.minimum(n + 1, nb - 1), 0))
    return pl.pallas_call(
        _window_body,
        grid=(b, nb),
        in_specs=[
            pl.BlockSpec(memory_space=pltpu.SMEM),
            pl.BlockSpec((1, A_HEADS, A_BLOCK, A_HEAD_DIM), lambda bi, n: (bi, 0, n, 0)),
            prev, cur, nxt, prev, cur, nxt,
            _const_spec(bias.shape),
        ],
        out_specs=pl.BlockSpec((1, A_BLOCK, A_WIDTH), lambda bi, n: (bi, n, 0)),
        out_shape=jax.ShapeDtypeStruct((b, s, A_WIDTH), BF16),
        compiler_params=pltpu.CompilerParams(dimension_semantics=("parallel", "parallel"),
                                             vmem_limit_bytes=VMEM_LIMIT),
        name="window_attn",
    )(sink, qa, ka, ka, ka, va, va, va, bias)


def _mla_body(q_ref, k_ref, v_ref, o_ref):
    s = lax.dot_general(q_ref[0, 0], k_ref[0, 0], (((1,), (1,)), ((), ())), preferred_element_type=F32)
    m = jnp.max(s, axis=-1, keepdims=True)
    p = jnp.exp(s - m)
    den = jnp.sum(p, axis=-1, keepdims=True)
    o = jnp.dot(p.astype(BF16), v_ref[0, 0], preferred_element_type=F32) / den
    o_ref[0] = o.astype(BF16)


def _mla_attn(qb, kb, vb):
    b, nh, s, _ = qb.shape
    tq = min(TQ_MLA, s)
    return pl.pallas_call(
        _mla_body,
        grid=(b, nh, s // tq),
        in_specs=[
            pl.BlockSpec((1, 1, tq, B_QK_PAD), lambda bi, h, i: (bi, h, i, 0)),
            pl.BlockSpec((1, 1, s, B_QK_PAD), lambda bi, h, i: (bi, h, 0, 0)),
            pl.BlockSpec((1, 1, s, B_V_DIM), lambda bi, h, i: (bi, h, 0, 0)),
        ],
        out_specs=pl.BlockSpec((1, tq, B_V_DIM), lambda bi, h, i: (bi, i, h)),
        out_shape=jax.ShapeDtypeStruct((b, s, B_WIDTH), BF16),
        compiler_params=pltpu.CompilerParams(dimension_semantics=("parallel", "parallel", "parallel"),
                                             vmem_limit_bytes=VMEM_LIMIT),
        name="mla_attn",
    )(qb, kb, vb)


def _out_router_body(ya_ref, yb_ref, x_ref, ga_ref, gb_ref, wo_ref, gm_ref, wr_ref, br_ref,
                     x1_ref, xm_ref, eidx_ref, gate_ref, rank_ref, cnt_ref, run_ref):
    i = pl.program_id(0)

    @pl.when(i == 0)
    def _():
        run_ref[...] = jnp.zeros_like(run_ref)

    na = _rms(ya_ref[...].astype(F32), ga_ref[...]).astype(BF16)
    nb = _rms(yb_ref[...].astype(F32), gb_ref[...]).astype(BF16)
    att = jnp.dot(na, wo_ref[:A_WIDTH, :], preferred_element_type=F32)
    att = att + jnp.dot(nb, wo_ref[A_WIDTH:, :], preferred_element_type=F32)
    x1 = x_ref[...] + att
    x1_ref[...] = x1
    hn = _rms(x1, gm_ref[...])
    xm_ref[...] = hn
    logits = jnp.dot(hn, wr_ref[...], preferred_element_type=F32,
                     precision=lax.Precision.HIGHEST) + br_ref[...]
    tm = logits.shape[0]
    lane = lax.broadcasted_iota(jnp.int32, (tm, N_EXPERTS), 1)
    work = logits
    sel = jnp.zeros((tm, N_EXPERTS), F32)
    hots, vals, idxs = [], [], []
    for _k in range(TOP_K):
        mx = jnp.max(work, axis=-1, keepdims=True)
        idx = jnp.min(jnp.where(work == mx, lane, N_EXPERTS), axis=-1, keepdims=True)
        hot = lane == idx
        hots.append(hot)
        vals.append(mx)
        idxs.append(idx)
        sel = sel + hot.astype(F32)
        work = jnp.where(hot, -jnp.inf, work)
    exps = [jnp.exp(v - vals[0]) for v in vals]
    den = exps[0] + exps[1] + exps[2] + exps[3]
    r_i = lax.broadcasted_iota(jnp.int32, (tm, tm), 0)
    c_i = lax.broadcasted_iota(jnp.int32, (tm, tm), 1)
    tri = (c_i < r_i).astype(BF16)
    before = jnp.dot(tri, sel.astype(BF16), preferred_element_type=F32) + run_ref[...]
    lane4 = lax.broadcasted_iota(jnp.int32, (tm, TOP_K), 1)
    eidx = jnp.zeros((tm, TOP_K), jnp.int32)
    gate = jnp.zeros((tm, TOP_K), F32)
    rank = jnp.zeros((tm, TOP_K), jnp.int32)
    for k in range(TOP_K):
        rk = jnp.sum(jnp.where(hots[k], before, 0.0), axis=-1, keepdims=True).astype(jnp.int32)
        eidx = jnp.where(lane4 == k, idxs[k], eidx)
        gate = jnp.where(lane4 == k, exps[k] / den, gate)
        rank = jnp.where(lane4 == k, rk, rank)
    eidx_ref[...] = eidx
    gate_ref[...] = gate
    rank_ref[...] = rank
    run = run_ref[...] + jnp.sum(sel, axis=0, keepdims=True)
    run_ref[...] = run
    cnt_ref[...] = run.astype(jnp.int32)


def _out_router(ya, yb, x2d, ga, gb, wo, gm, wr, br):
    t, d = x2d.shape
    tm = min(TM_OUT, t)
    sds = jax.ShapeDtypeStruct

    def row(w):
        return pl.BlockSpec((tm, w), lambda i: (i, 0))

    return pl.pallas_call(
        _out_router_body,
        grid=(t // tm,),
        in_specs=[row(A_WIDTH), row(B_WIDTH), row(d), _const_spec(ga.shape), _const_spec(gb.shape),
                  _const_spec(wo.shape), _const_spec(gm.shape), _const_spec(wr.shape), _const_spec(br.shape)],
        out_specs=(row(d), row(d), row(TOP_K), row(TOP_K), row(TOP_K),
                   pl.BlockSpec((1, N_EXPERTS), lambda i: (0, 0))),
        out_shape=(sds((t, d), F32), sds((t, d), F32), sds((t, TOP_K), jnp.int32), sds((t, TOP_K), F32),
                   sds((t, TOP_K), jnp.int32), sds((1, N_EXPERTS), jnp.int32)),
        scratch_shapes=[pltpu.VMEM((1, N_EXPERTS), F32)],
        compiler_params=pltpu.CompilerParams(dimension_semantics=("arbitrary",),
                                             vmem_limit_bytes=VMEM_LIMIT),
        name="out_router",
    )(ya, yb, x2d, ga, gb, wo, gm, wr, br)


def _row_copy(src_ref, src_row, dst_ref, dst_row, sem):
    return pltpu.make_async_copy(src_ref.at[pl.ds(src_row, 1)], dst_ref.at[pl.ds(dst_row, 1)], sem)


def _zero_fill_pads(pad_start_ref, pad_len_ref, nact_ref, xs_ref, zero_ref, zsem):
    bm = zero_ref.shape[0]
    nblk = xs_ref.shape[0] // bm
    zero_ref[...] = jnp.zeros_like(zero_ref)

    def per_expert(e, c):
        start = pad_start_ref[e]
        n = pad_len_ref[e]
        head = jnp.minimum((-start) & (SUBLANES - 1), n)

        def one(r, c2):
            cp = _row_copy(zero_ref, 0, xs_ref, start + r, zsem)
            cp.start()
            cp.wait()
            return c2

        lax.fori_loop(0, head, one, 0)
        off = start + head
        rest = n - head
        p = bm // 2
        while p >= SUBLANES:
            take = (rest & p) != 0
            dst = pl.multiple_of(off, SUBLANES)

            @pl.when(take)
            def _(p=p, dst=dst):
                cp = pltpu.make_async_copy(zero_ref.at[pl.ds(0, p)], xs_ref.at[pl.ds(dst, p)], zsem)
                cp.start()
                cp.wait()

            off = off + jnp.where(take, p, 0)
            p //= 2
        return c

    lax.fori_loop(0, N_EXPERTS, per_expert, 0)

    def tail(j, c):
        cp = pltpu.make_async_copy(zero_ref, xs_ref.at[pl.ds(pl.multiple_of(j * bm, bm), bm)], zsem)
        cp.start()
        cp.wait()
        return c

    lax.fori_loop(nact_ref[0], nblk, tail, 0)


def _dispatch_body(pad_start_ref, pad_len_ref, nact_ref, pos_ref, xm_ref, xs_ref, zero_ref, sem, zsem):
    tm = pos_ref.shape[0] // TOP_K

    @pl.when(pl.program_id(0) == 0)
    def _():
        _zero_fill_pads(pad_start_ref, pad_len_ref, nact_ref, xs_ref, zero_ref, zsem)

    def issue(i, c):
        for k in range(TOP_K):
            _row_copy(xm_ref, i, xs_ref, pos_ref[i * TOP_K + k], sem).start()
        return c

    lax.fori_loop(0, tm, issue, 0)

    def drain(i, c):
        for _ in range(WAIT_UNROLL):
            _row_copy(xm_ref, 0, xs_ref, 0, sem).wait()
        return c

    lax.fori_loop(0, tm * TOP_K // WAIT_UNROLL, drain, 0)


def _dispatch(pad_start, pad_len, nact, pos_flat, xm, n_slots):
    t, d = xm.shape
    tm = min(TM_DISPATCH, t)
    grid_spec = pltpu.PrefetchScalarGridSpec(
        num_scalar_prefetch=3,
        grid=(t // tm,),
        in_specs=[pl.BlockSpec((tm * TOP_K,), lambda i, *_: (i,), memory_space=pltpu.SMEM),
                  pl.BlockSpec((tm, d), lambda i, *_: (i, 0))],
        out_specs=pl.BlockSpec(memory_space=pl.ANY),
        scratch_shapes=[pltpu.VMEM((MOE_BM, d), F32), pltpu.SemaphoreType.DMA(()), pltpu.SemaphoreType.DMA(())],
    )
    return pl.pallas_call(
        _dispatch_body,
        grid_spec=grid_spec,
        out_shape=jax.ShapeDtypeStruct((n_slots, d), F32),
        compiler_params=pltpu.CompilerParams(dimension_semantics=("arbitrary",), has_side_effects=True,
                                             vmem_limit_bytes=VMEM_LIMIT),
        name="dispatch",
    )(pad_start, pad_len, nact, pos_flat, xm)


STEP_UP, STEP_DOWN, STEP_TAIL, STEP_IDLE = 0, 1, 2, 3
NF = D_FF // MOE_TF
PASS_BLOCKS = MOE_RM // MOE_BM


def _swiglu(hg, hu):
    gate = jnp.minimum(hg, SWIGLU_LIMIT)
    up = jnp.clip(hu, -SWIGLU_LIMIT, SWIGLU_LIMIT)
    glu = gate / (1.0 + jnp.exp(-SWIGLU_ALPHA * gate))
    return (up + 1.0) * glu


def _experts_body(e_ref, f_ref, kind_ref, blk_ref, blk0_ref, nsub_ref,
                  xs_ref, wg_ref, wu_ref, bg_ref, bu_ref, wd_ref, bd_ref,
                  y_ref, xb_ref, h_ref, wgb_ref, wub_ref, wdb_ref, stg_ref, sem):
    t = pl.program_id(0)
    kind = kind_ref[t]
    f = f_ref[t]
    nsub = nsub_ref[t]

    def x_copy(j, slot):
        row = pl.multiple_of(blk0_ref[t] * MOE_BM + j * MOE_SUB, MOE_SUB)
        return pltpu.make_async_copy(xs_ref.at[pl.ds(row, MOE_SUB)], stg_ref.at[slot], sem.at[slot])

    @pl.when(kind == STEP_UP)
    def _up():
        @pl.when(f == 0)
        def _load_rows():
            x_copy(0, 0).start()

            def body(j, c):
                slot = j % 2

                @pl.when(j + 1 < nsub)
                def _():
                    x_copy(j + 1, 1 - slot).start()

                x_copy(j, slot).wait()
                xb_ref[pl.ds(pl.multiple_of(j * MOE_SUB, MOE_SUB), MOE_SUB), :] = stg_ref[slot].astype(BF16)
                return c

            lax.fori_loop(0, nsub, body, 0)

        wgb_ref[...] = wg_ref[0].astype(BF16)
        wub_ref[...] = wu_ref[0].astype(BF16)
        wdb_ref[pl.ds(pl.multiple_of(f * MOE_TF, MOE_TF), MOE_TF), :] = wd_ref[0].astype(BF16)

        def up_rows(row0, nrows):
            x = xb_ref[pl.ds(row0, nrows), :]
            hg = jnp.dot(x, wgb_ref[...], preferred_element_type=F32) + bg_ref[0]
            hu = jnp.dot(x, wub_ref[...], preferred_element_type=F32) + bu_ref[0]
            act = _swiglu(hg, hu).astype(BF16)
            for fs in range(NF):
                @pl.when(f == fs)
                def _(fs=fs):
                    h_ref[pl.ds(row0, nrows), fs * MOE_TF:(fs + 1) * MOE_TF] = act

        npair = nsub // 2

        def pair(i, c):
            up_rows(pl.multiple_of(i * MOE_BM, MOE_BM), MOE_BM)
            return c

        lax.fori_loop(0, npair, pair, 0)

        @pl.when(nsub % 2 == 1)
        def _():
            up_rows(pl.multiple_of(npair * MOE_BM, MOE_BM), MOE_SUB)

    @pl.when(kind == STEP_DOWN)
    def _down():
        row0 = pl.multiple_of((blk_ref[t] - blk0_ref[t]) * MOE_BM, MOE_BM)

        def down_rows(nrows):
            hrows = h_ref[pl.ds(row0, nrows), :]
            return jnp.dot(hrows, wdb_ref[...], preferred_element_type=F32) + bd_ref[0]

        @pl.when(nsub == 2)
        def _():
            y_ref[...] = down_rows(MOE_BM)

        @pl.when(nsub == 1)
        def _():
            y_ref[:MOE_SUB, :] = down_rows(MOE_SUB)
            y_ref[MOE_SUB:, :] = jnp.zeros((MOE_BM - MOE_SUB, y_ref.shape[1]), F32)

    @pl.when(kind == STEP_TAIL)
    def _tail():
        y_ref[...] = jnp.zeros_like(y_ref)


def _expert_schedule(counts, n_blocks, n_assign):
    i32 = jnp.int32
    counts = counts.astype(i32)
    nblk_e = (counts + MOE_BM - 1) // MOE_BM
    blk_end = jnp.cumsum(nblk_e)
    blk_start = blk_end - nblk_e
    npass_e = (nblk_e + PASS_BLOCKS - 1) // PASS_BLOCKS
    pass_end = jnp.cumsum(npass_e)
    pass_start = pass_end - npass_e
    n_pass_max = N_EXPERTS + n_assign // MOE_RM
    n_steps = n_pass_max * NF + n_blocks
    p = jnp.arange(n_pass_max, dtype=i32)
    p_valid = p < pass_end[-1]
    pe = jnp.minimum(jnp.sum(p[:, None] >= pass_end[None, :], axis=1), N_EXPERTS - 1)
    p_local = p - pass_start[pe]
    p_blk0 = blk_start[pe] + p_local * PASS_BLOCKS
    p_nblk = jnp.where(p_valid, jnp.clip(nblk_e[pe] - p_local * PASS_BLOCKS, 0, PASS_BLOCKS), 0)
    p_rows = jnp.clip(counts[pe] - p_local * MOE_RM, 0, MOE_RM)
    p_nsub = (p_rows + MOE_SUB - 1) // MOE_SUB
    p_steps = jnp.where(p_valid, NF + p_nblk, 0)
    s_end = jnp.cumsum(p_steps)
    s_start = s_end - p_steps
    total = s_end[-1]
    nact = blk_end[-1]
    last_e = pe[jnp.maximum(pass_end[-1] - 1, 0)]
    t = jnp.arange(n_steps, dtype=i32)
    sp = jnp.minimum(jnp.sum(t[:, None] >= s_end[None, :], axis=1), n_pass_max - 1)
    loc = t - s_start[sp]
    real = t < total
    is_up = real & (loc < NF)
    is_down = real & (loc >= NF)
    jb = loc - NF
    k = t - total
    is_tail = (~real) & (k < n_blocks - nact)
    kind = jnp.where(is_up, STEP_UP, jnp.where(is_down, STEP_DOWN, jnp.where(is_tail, STEP_TAIL, STEP_IDLE)))
    e_t = jnp.where(real, pe[sp], last_e)
    f_t = jnp.where(is_up, loc, NF - 1)
    blk0_t = p_blk0[sp]
    blk_t = jnp.where(is_up, blk0_t, jnp.where(is_down, blk0_t + jb,
                                               jnp.where(is_tail, nact + k, n_blocks - 1)))
    nsub_t = jnp.where(is_up, p_nsub[sp], jnp.clip(p_nsub[sp] - jb * (MOE_BM // MOE_SUB), 1, MOE_BM // MOE_SUB))
    tabs = (e_t, f_t, kind, blk_t, blk0_t, nsub_t)
    return tuple(a.astype(i32) for a in tabs)


def _experts(tabs, xs, w_up, b_up, w_down, b_down):
    n_slots, d = xs.shape
    n_steps = tabs[0].shape[0]
    grid_spec = pltpu.PrefetchScalarGridSpec(
        num_scalar_prefetch=len(tabs),
        grid=(n_steps,),
        in_specs=[
            pl.BlockSpec(memory_space=pl.ANY),
            pl.BlockSpec((1, d, MOE_TF), lambda t, e, f, *_: (e[t], 0, f[t])),
            pl.BlockSpec((1, d, MOE_TF), lambda t, e, f, *_: (e[t], 0, NF + f[t])),
            pl.BlockSpec((1, 1, MOE_TF), lambda t, e, f, *_: (e[t], 0, f[t])),
            pl.BlockSpec((1, 1, MOE_TF), lambda t, e, f, *_: (e[t], 0, NF + f[t])),
            pl.BlockSpec((1, MOE_TF, d), lambda t, e, f, *_: (e[t], f[t], 0)),
            pl.BlockSpec((1, 1, d), lambda t, e, f, *_: (e[t], 0, 0)),
        ],
        out_specs=pl.BlockSpec((MOE_BM, d), lambda t, e, f, kind, blk, *_: (blk[t], 0)),
        scratch_shapes=[
            pltpu.VMEM((MOE_RM, d), BF16),
            pltpu.VMEM((MOE_RM, D_FF), BF16),
            pltpu.VMEM((d, MOE_TF), BF16),
            pltpu.VMEM((d, MOE_TF), BF16),
            pltpu.VMEM((D_FF, d), BF16),
            pltpu.VMEM((2, MOE_SUB, d), F32),
            pltpu.SemaphoreType.DMA((2,)),
        ],
    )
    return pl.pallas_call(
        _experts_body,
        grid_spec=grid_spec,
        out_shape=jax.ShapeDtypeStruct((n_slots, d), F32),
        compiler_params=pltpu.CompilerParams(dimension_semantics=("arbitrary",),
                                             vmem_limit_bytes=VMEM_LIMIT),
        name="experts",
    )(*tabs, xs, w_up, w_up, b_up, b_up, w_down, b_down)


def _combine_body(pos_ref, gate_ref, x1_ref, gf_ref, ys_ref, o_ref, buf_ref, sem):
    tm = x1_ref.shape[0]

    def issue(i, c):
        for k in range(TOP_K):
            _row_copy(ys_ref, pos_ref[i * TOP_K + k], buf_ref.at[k], i, sem).start()
        return c

    lax.fori_loop(0, tm, issue, 0)

    def drain(i, c):
        for _ in range(WAIT_UNROLL):
            _row_copy(ys_ref, 0, buf_ref.at[0], 0, sem).wait()
        return c

    lax.fori_loop(0, tm * TOP_K // WAIT_UNROLL, drain, 0)
    gate = gate_ref[...]
    acc = x1_ref[...]
    for k in range(TOP_K):
        acc = acc + gate[:, k:k + 1] * buf_ref[k]
    o_ref[...] = _rms(acc, gf_ref[...])


def _combine(pos_flat, gates, x1, gf, ys):
    t, d = x1.shape
    tm = min(TM_COMBINE, t)
    return pl.pallas_call(
        _combine_body,
        grid=(t // tm,),
        in_specs=[pl.BlockSpec((tm * TOP_K,), lambda i: (i,), memory_space=pltpu.SMEM),
                  pl.BlockSpec((tm, TOP_K), lambda i: (i, 0)),
                  pl.BlockSpec((tm, d), lambda i: (i, 0)),
                  _const_spec(gf.shape),
                  pl.BlockSpec(memory_space=pl.ANY)],
        out_specs=pl.BlockSpec((tm, d), lambda i: (i, 0)),
        out_shape=jax.ShapeDtypeStruct((t, d), F32),
        scratch_shapes=[pltpu.VMEM((TOP_K, tm, d), F32), pltpu.SemaphoreType.DMA(())],
        compiler_params=pltpu.CompilerParams(dimension_semantics=("arbitrary",),
                                             vmem_limit_bytes=VMEM_LIMIT),
        name="combine",
    )(pos_flat, gates, x1, gf, ys)


def _rope_tables(s):
    pos = jnp.arange(s, dtype=F32)
    inv_freq = jnp.power(ROPE_THETA, -jnp.arange(0, B_ROPE_DIM, 2, dtype=F32) / B_ROPE_DIM)
    ang = pos[:, None] * inv_freq[None, :]
    cos, sin = jnp.cos(ang), jnp.sin(ang)
    pad = LANES - B_ROPE_DIM
    cos = jnp.concatenate([cos, cos, jnp.ones((s, pad), F32)], axis=1)
    sin = jnp.concatenate([sin, sin, jnp.zeros((s, pad), F32)], axis=1)
    return cos, sin


def _window_bias():
    qi = jnp.arange(A_BLOCK)[:, None]
    kj = jnp.arange(3 * A_BLOCK)[None, :]
    dist = jnp.abs(qi + A_BLOCK - kj)
    slopes = jnp.power(2.0, -8.0 * jnp.arange(1, A_HEADS + 1, dtype=F32) / A_HEADS)
    bias = -slopes[:, None, None] * dist.astype(F32)[None]
    return jnp.where((dist <= WINDOW)[None], bias, -jnp.inf)


def _layer(x, attn_norm, w_in, a_sink, b_q_norm, b_w_uq, b_kv_norm, b_w_ukv, out_norm_a, out_norm_b, w_o,
           mlp_norm, w_router, b_router, w_up, b_up, w_down, b_down):
    b, s, d = x.shape
    t = b * s
    win = jnp.pad(w_in, ((0, 0), (0, IN_COLS_PAD - w_in.shape[1]))).astype(BF16)
    wq = b_w_uq.reshape(B_Q_RANK, B_HEADS, B_NOPE_DIM + B_ROPE_DIM)
    wq_pe = jnp.pad(wq[:, :, B_NOPE_DIM:], ((0, 0), (0, 0), (0, LANES - B_ROPE_DIM)))
    wuq = jnp.concatenate([wq[:, :, :B_NOPE_DIM].reshape(B_Q_RANK, -1), wq_pe.reshape(B_Q_RANK, -1)],
                          axis=1).astype(BF16)
    wkv = b_w_ukv.reshape(B_KV_RANK, B_HEADS, B_NOPE_DIM + B_V_DIM)
    wukv = jnp.concatenate([wkv[:, :, :B_NOPE_DIM].reshape(B_KV_RANK, -1),
                            wkv[:, :, B_NOPE_DIM:].reshape(B_KV_RANK, -1)], axis=1).astype(BF16)
    cos, sin = _rope_tables(s)

    qa, ka, va, qb, kb, vb = _prologue(x, attn_norm[None], win, b_q_norm[None], wuq, b_kv_norm[None], wukv,
                                       cos, sin)
    ya = _window_attn(qa, ka, va, a_sink, _window_bias())
    yb = _mla_attn(qb, kb, vb)
    x1, xm, eidx, gates, rank, counts = _out_router(
        ya.reshape(t, A_WIDTH), yb.reshape(t, B_WIDTH), x.reshape(t, d), out_norm_a[None], out_norm_b[None],
        w_o.astype(BF16), mlp_norm[None], w_router, b_router[None])

    bm = MOE_BM
    counts = counts[0]
    padded = (counts + bm - 1) // bm * bm
    ends = jnp.cumsum(padded)
    starts = ends - padded
    pos = (starts[eidx] + rank).reshape(t * TOP_K)
    nblk = (t * TOP_K + N_EXPERTS * (bm - 1) + bm - 1) // bm
    nact = (ends[-1] // bm).astype(jnp.int32).reshape(1)

    xs = _dispatch((starts + counts).astype(jnp.int32), (padded - counts).astype(jnp.int32), nact, pos, xm,
                   nblk * bm)
    ys = _experts(_expert_schedule(counts, nblk, t * TOP_K), xs, w_up, b_up[:, None, :], w_down,
                  b_down[:, None, :])
    return x1, pos, gates, ys


def kernel(x, attn_norm, w_in, a_sink, b_q_norm, b_w_uq, b_kv_norm, b_w_ukv, out_norm_a, out_norm_b, w_o,
           mlp_norm, w_router, b_router, w_up, b_up, w_down, b_down, final_norm):
    b, s, d = x.shape
    assert d == D_MODEL and s % TQ_MLA == 0 and s % A_BLOCK == 0 and attn_norm.shape[0] == 1
    x1, pos, gates, ys = _layer(x, attn_norm[0], w_in[0], a_sink[0], b_q_norm[0], b_w_uq[0], b_kv_norm[0],
                                b_w_ukv[0], out_norm_a[0], out_norm_b[0], w_o[0], mlp_norm[0], w_router[0],
                                b_router[0], w_up[0], b_up[0], w_down[0], b_down[0])
    out = _combine(pos, gates, x1, final_norm[None], ys)
    return out.reshape(b, s, d)
```

```python
import functools
import math

import jax
import jax.numpy as jnp
from jax import lax
from jax.experimental import pallas as pl
from jax.experimental.pallas import tpu as pltpu

D_MODEL = 2048
A_HEADS, A_KV_HEADS, A_HEAD_DIM = 16, 4, 64
A_GROUP = A_HEADS // A_KV_HEADS
WINDOW = 128
A_BLOCK = 128
B_HEADS, B_Q_RANK, B_KV_RANK = 8, 512, 256
B_NOPE_DIM, B_ROPE_DIM, B_V_DIM = 128, 64, 128
ROPE_THETA = 10000.0
A_WIDTH = A_HEADS * A_HEAD_DIM
B_WIDTH = B_HEADS * B_V_DIM
A_KV_COLS = A_KV_HEADS * A_HEAD_DIM
N_EXPERTS, TOP_K, D_FF = 32, 4, 2048
SWIGLU_ALPHA, SWIGLU_LIMIT = 1.702, 7.0
EPS = 1e-5

LANES = 128
SUBLANES = 8
B_QK_PAD = 2 * LANES
IN_COLS_PAD = A_WIDTH + 2 * A_KV_COLS + B_Q_RANK + B_KV_RANK + LANES
VMEM_LIMIT = 56 * 1024 * 1024

TM_PRO = 256
TQ_MLA = 1024
CK_MLA = 1024
TM_OUT = 256
TM_DISPATCH = 512
TM_COMBINE = 128
MOE_BM = 512
MOE_SUB = 256
MOE_RM = 2048
MOE_TF = 256
WAIT_UNROLL = 16

BF16 = jnp.bfloat16
F32 = jnp.float32


def _rms(x, g):
    return x * lax.rsqrt(jnp.mean(x * x, axis=-1, keepdims=True) + EPS) * g


def _const_spec(shape):
    nd = len(shape)
    return pl.BlockSpec(shape, lambda *_: (0,) * nd, pipeline_mode=pl.Buffered(1))


def _rope(x, cos, sin):
    lane = lax.broadcasted_iota(jnp.int32, x.shape, 1)
    up = pltpu.roll(x, LANES - B_ROPE_DIM // 2, 1)
    dn = pltpu.roll(x, B_ROPE_DIM // 2, 1)
    sw = jnp.where(lane < B_ROPE_DIM // 2, -up, jnp.where(lane < B_ROPE_DIM, dn, 0.0))
    return x * cos + sw * sin


def _prologue_body(x_ref, g_ref, win_ref, qn_ref, wuq_ref, kvn_ref, wukv_ref, cos_ref, sin_ref,
                   qa_ref, ka_ref, va_ref, qb_ref, kb_ref, vb_ref):
    x = x_ref[0]
    xn = _rms(x, g_ref[...]).astype(BF16)
    proj = jnp.dot(xn, win_ref[...], preferred_element_type=F32)
    c0 = A_WIDTH
    c1 = c0 + A_KV_COLS
    c2 = c1 + A_KV_COLS
    c3 = c2 + B_Q_RANK
    c4 = c3 + B_KV_RANK
    qa_ref[0] = (proj[:, :A_WIDTH] * A_HEAD_DIM ** -0.5).astype(BF16)
    low = lax.broadcasted_iota(jnp.int32, (proj.shape[0], LANES), 1) < A_HEAD_DIM
    for src, dst in ((c0, ka_ref), (c1, va_ref)):
        for pair in range(A_KV_HEADS // 2):
            two = proj[:, src + pair * LANES:src + (pair + 1) * LANES]
            swapped = pltpu.roll(two, A_HEAD_DIM, 1)
            dst[0, 4 * pair + 0] = jnp.where(low, two, 0.0).astype(BF16)
            dst[0, 4 * pair + 1] = jnp.where(low, 0.0, swapped).astype(BF16)
            dst[0, 4 * pair + 2] = jnp.where(low, swapped, 0.0).astype(BF16)
            dst[0, 4 * pair + 3] = jnp.where(low, 0.0, two).astype(BF16)
    cq = _rms(proj[:, c2:c3], qn_ref[...]).astype(BF16)
    q = jnp.dot(cq, wuq_ref[...], preferred_element_type=F32)
    ckv = _rms(proj[:, c3:c4], kvn_ref[...]).astype(BF16)
    kv = jnp.dot(ckv, wukv_ref[...], preferred_element_type=F32)
    cos = cos_ref[...]
    sin = sin_ref[...]
    kpe = _rope(proj[:, c4:c4 + LANES], cos, sin).astype(BF16)
    b_scale = (B_NOPE_DIM + B_ROPE_DIM) ** -0.5
    hw = B_HEADS * LANES
    for h in range(B_HEADS):
        sl = slice(h * LANES, (h + 1) * LANES)
        qb_ref[0, h, :, :LANES] = (q[:, sl] * b_scale).astype(BF16)
        qpe = _rope(q[:, hw + h * LANES:hw + (h + 1) * LANES], cos, sin)
        qb_ref[0, h, :, LANES:] = (qpe * b_scale).astype(BF16)
        kb_ref[0, h, :, :LANES] = kv[:, sl].astype(BF16)
        kb_ref[0, h, :, LANES:] = kpe
        vb_ref[0, h] = kv[:, hw + h * LANES:hw + (h + 1) * LANES].astype(BF16)


def _prologue(x, g, win, qn, wuq, kvn, wukv, cos, sin):
    b, s, d = x.shape
    tm = TM_PRO
    grid = (b, s // tm)
    sds = jax.ShapeDtypeStruct
    out_shape = (
        sds((b, s, A_WIDTH), BF16),
        sds((b, 2 * A_KV_HEADS, s, LANES), BF16),
        sds((b, 2 * A_KV_HEADS, s, LANES), BF16),
        sds((b, B_HEADS, s, B_QK_PAD), BF16),
        sds((b, B_HEADS, s, B_QK_PAD), BF16),
        sds((b, B_HEADS, s, B_V_DIM), BF16),
    )

    def hspec(nh, w):
        return pl.BlockSpec((1, nh, tm, w), lambda bi, i: (bi, 0, i, 0))

    return pl.pallas_call(
        _prologue_body,
        grid=grid,
        in_specs=[
            pl.BlockSpec((1, tm, d), lambda bi, i: (bi, i, 0)),
            _const_spec(g.shape), _const_spec(win.shape), _const_spec(qn.shape), _const_spec(wuq.shape),
            _const_spec(kvn.shape), _const_spec(wukv.shape),
            pl.BlockSpec((tm, LANES), lambda bi, i: (i, 0)),
            pl.BlockSpec((tm, LANES), lambda bi, i: (i, 0)),
        ],
        out_specs=(pl.BlockSpec((1, tm, A_WIDTH), lambda bi, i: (bi, i, 0)),
                   hspec(2 * A_KV_HEADS, LANES), hspec(2 * A_KV_HEADS, LANES),
                   hspec(B_HEADS, B_QK_PAD), hspec(B_HEADS, B_QK_PAD), hspec(B_HEADS, B_V_DIM)),
        out_shape=out_shape,
        compiler_params=pltpu.CompilerParams(dimension_semantics=("parallel", "parallel"),
                                             vmem_limit_bytes=VMEM_LIMIT),
        name="prologue",
    )(x, g, win, qn, wuq, kvn, wukv, cos, sin)


def _window_body(sink_ref, q_ref, kp_ref, kc_ref, kn_ref, vp_ref, vc_ref, vn_ref, bias_ref, o_ref):
    two = 2 * A_BLOCK
    low = lax.broadcasted_iota(jnp.int32, (two, LANES), 1) < A_HEAD_DIM
    for hk in range(A_KV_HEADS):
        q2 = jnp.concatenate([q_ref[0, :, (2 * hk) * LANES:(2 * hk + 1) * LANES],
                              q_ref[0, :, (2 * hk + 1) * LANES:(2 * hk + 2) * LANES]], axis=0)
        acc = jnp.zeros((two, LANES), F32)
        inv = []
        for half in range(2):
            z = 2 * hk + half
            kband = jnp.concatenate([kp_ref[0, z], kc_ref[0, z], kn_ref[0, z]], axis=0)
            vband = jnp.concatenate([vp_ref[0, z], vc_ref[0, z], vn_ref[0, z]], axis=0)
            s = lax.dot_general(q2, kband, (((1,), (1,)), ((), ())), preferred_element_type=F32)
            s = s + bias_ref[0, z]
            sink = jnp.concatenate(
                [jnp.full((A_BLOCK, 1), sink_ref[hk * A_GROUP + 2 * jj + half], F32) for jj in range(2)], axis=0)
            m = jnp.maximum(jnp.max(s, axis=-1, keepdims=True), sink)
            p = jnp.exp(s - m)
            den = jnp.sum(p, axis=-1, keepdims=True) + jnp.exp(sink - m)
            inv.append(1.0 / den)
            acc = acc + jnp.dot(p.astype(BF16), vband, preferred_element_type=F32)
        o = (acc * jnp.where(low, inv[0], inv[1])).astype(BF16)
        o_ref[0, :, (2 * hk) * LANES:(2 * hk + 1) * LANES] = o[:A_BLOCK]
        o_ref[0, :, (2 * hk + 1) * LANES:(2 * hk + 2) * LANES] = o[A_BLOCK:]


def _window_attn(qa, ka, va, sink, bias):
    b, s, _ = qa.shape
    nb = s // A_BLOCK
    kv_blk = (1, 2 * A_KV_HEADS, A_BLOCK, LANES)
    prev = pl.BlockSpec(kv_blk, lambda bi, n: (bi, 0, jnp.maximum(n - 1, 0), 0))
    cur = pl.BlockSpec(kv_blk, lambda bi, n: (bi, 0, n, 0))
    nxt = pl.BlockSpec(kv_blk, lambda bi, n: (bi, 0, jnp.minimum(n + 1, nb - 1), 0))

    def edge(bi, n):
        return ((n == 0).astype(jnp.int32) + 2 * (n == nb - 1).astype(jnp.int32), 0, 0, 0)

    return pl.pallas_call(
        _window_body,
        grid=(b, nb),
        in_specs=[
            pl.BlockSpec(memory_space=pltpu.SMEM),
            pl.BlockSpec((1, A_BLOCK, A_WIDTH), lambda bi, n: (bi, n, 0)),
            prev, cur, nxt, prev, cur, nxt,
            pl.BlockSpec((1,) + bias.shape[1:], edge),
        ],
        out_specs=pl.BlockSpec((1, A_BLOCK, A_WIDTH), lambda bi, n: (bi, n, 0)),
        out_shape=jax.ShapeDtypeStruct((b, s, A_WIDTH), BF16),
        compiler_params=pltpu.CompilerParams(dimension_semantics=("parallel", "parallel"),
                                             vmem_limit_bytes=VMEM_LIMIT),
        name="window_attn",
    )(sink, qa, ka, ka, ka, va, va, va, bias)


def _mla_body(q_ref, k_ref, v_ref, o_ref):
    q = q_ref[0, 0]
    tq = q.shape[0]
    s_len = k_ref.shape[2]
    ck = min(CK_MLA, s_len)
    m = jnp.full((tq, 1), -jnp.inf, F32)
    l = jnp.zeros((tq, 1), F32)
    acc = jnp.zeros((tq, B_V_DIM), F32)
    for c in range(s_len // ck):
        k_c = k_ref[0, 0, c * ck:(c + 1) * ck, :]
        v_c = v_ref[0, 0, c * ck:(c + 1) * ck, :]
        s = lax.dot_general(q, k_c, (((1,), (1,)), ((), ())), preferred_element_type=F32)
        m_new = jnp.maximum(m, jnp.max(s, axis=-1, keepdims=True))
        alpha = jnp.exp(m - m_new)
        p = jnp.exp(s - m_new)
        l = alpha * l + jnp.sum(p, axis=-1, keepdims=True)
        acc = alpha * acc + jnp.dot(p.astype(BF16), v_c, preferred_element_type=F32)
        m = m_new
    o_ref[0] = (acc / l).astype(BF16)


def _mla_attn(qb, kb, vb):
    b, nh, s, _ = qb.shape
    tq = min(TQ_MLA, s)
    return pl.pallas_call(
        _mla_body,
        grid=(b, nh, s // tq),
        in_specs=[
            pl.BlockSpec((1, 1, tq, B_QK_PAD), lambda bi, h, i: (bi, h, i, 0)),
            pl.BlockSpec((1, 1, s, B_QK_PAD), lambda bi, h, i: (bi, h, 0, 0)),
            pl.BlockSpec((1, 1, s, B_V_DIM), lambda bi, h, i: (bi, h, 0, 0)),
        ],
        out_specs=pl.BlockSpec((1, tq, B_V_DIM), lambda bi, h, i: (bi, i, h)),
        out_shape=jax.ShapeDtypeStruct((b, s, B_WIDTH), BF16),
        compiler_params=pltpu.CompilerParams(dimension_semantics=("parallel", "parallel", "parallel"),
                                             vmem_limit_bytes=VMEM_LIMIT),
        name="mla_attn",
    )(qb, kb, vb)


def _out_router_body(ya_ref, yb_ref, x_ref, ga_ref, gb_ref, wo_ref, gm_ref, wr_ref, br_ref,
                     x1_ref, xm_ref, eidx_ref, gate_ref, rank_ref, cnt_ref, run_ref):
    i = pl.program_id(0)

    @pl.when(i == 0)
    def _():
        run_ref[...] = jnp.zeros_like(run_ref)

    na = _rms(ya_ref[...].astype(F32), ga_ref[...]).astype(BF16)
    nb = _rms(yb_ref[...].astype(F32), gb_ref[...]).astype(BF16)
    att = jnp.dot(na, wo_ref[:A_WIDTH, :], preferred_element_type=F32)
    att = att + jnp.dot(nb, wo_ref[A_WIDTH:, :], preferred_element_type=F32)
    x1 = x_ref[...] + att
    x1_ref[...] = x1
    hn = _rms(x1, gm_ref[...])
    xm_ref[...] = hn
    logits = jnp.dot(hn, wr_ref[...], preferred_element_type=F32,
                     precision=lax.Precision.HIGHEST) + br_ref[...]
    tm = logits.shape[0]
    lane = lax.broadcasted_iota(jnp.int32, (tm, N_EXPERTS), 1)
    work = logits
    sel = jnp.zeros((tm, N_EXPERTS), F32)
    hots, vals, idxs = [], [], []
    for _k in range(TOP_K):
        mx = jnp.max(work, axis=-1, keepdims=True)
        idx = jnp.min(jnp.where(work == mx, lane, N_EXPERTS), axis=-1, keepdims=True)
        hot = lane == idx
        hots.append(hot)
        vals.append(mx)
        idxs.append(idx)
        sel = sel + hot.astype(F32)
        work = jnp.where(hot, -jnp.inf, work)
    exps = [jnp.exp(v - vals[0]) for v in vals]
    den = exps[0] + exps[1] + exps[2] + exps[3]
    r_i = lax.broadcasted_iota(jnp.int32, (tm, tm), 0)
    c_i = lax.broadcasted_iota(jnp.int32, (tm, tm), 1)
    tri = (c_i < r_i).astype(BF16)
    before = jnp.dot(tri, sel.astype(BF16), preferred_element_type=F32) + run_ref[...]
    lane4 = lax.broadcasted_iota(jnp.int32, (tm, TOP_K), 1)
    eidx = jnp.zeros((tm, TOP_K), jnp.int32)
    gate = jnp.zeros((tm, TOP_K), F32)
    rank = jnp.zeros((tm, TOP_K), jnp.int32)
    for k in range(TOP_K):
        rk = jnp.sum(jnp.where(hots[k], before, 0.0), axis=-1, keepdims=True).astype(jnp.int32)
        eidx = jnp.where(lane4 == k, idxs[k], eidx)
        gate = jnp.where(lane4 == k, exps[k] / den, gate)
        rank = jnp.where(lane4 == k, rk, rank)
    eidx_ref[...] = eidx
    gate_ref[...] = gate
    rank_ref[...] = rank
    run = run_ref[...] + jnp.sum(sel, axis=0, keepdims=True)
    run_ref[...] = run
    cnt_ref[...] = run.astype(jnp.int32)


def _out_router(ya, yb, x2d, ga, gb, wo, gm, wr, br):
    t, d = x2d.shape
    tm = min(TM_OUT, t)
    sds = jax.ShapeDtypeStruct

    def row(w):
        return pl.BlockSpec((tm, w), lambda i: (i, 0))

    return pl.pallas_call(
        _out_router_body,
        grid=(t // tm,),
        in_specs=[row(A_WIDTH), row(B_WIDTH), row(d), _const_spec(ga.shape), _const_spec(gb.shape),
                  _const_spec(wo.shape), _const_spec(gm.shape), _const_spec(wr.shape), _const_spec(br.shape)],
        out_specs=(row(d), row(d), row(TOP_K), row(TOP_K), row(TOP_K),
                   pl.BlockSpec((1, N_EXPERTS), lambda i: (0, 0))),
        out_shape=(sds((t, d), F32), sds((t, d), F32), sds((t, TOP_K), jnp.int32), sds((t, TOP_K), F32),
                   sds((t, TOP_K), jnp.int32), sds((1, N_EXPERTS), jnp.int32)),
        scratch_shapes=[pltpu.VMEM((1, N_EXPERTS), F32)],
        compiler_params=pltpu.CompilerParams(dimension_semantics=("arbitrary",),
                                             vmem_limit_bytes=VMEM_LIMIT),
        name="out_router",
    )(ya, yb, x2d, ga, gb, wo, gm, wr, br)


def _row_copy(src_ref, src_row, dst_ref, dst_row, sem):
    return pltpu.make_async_copy(src_ref.at[pl.ds(src_row, 1)], dst_ref.at[pl.ds(dst_row, 1)], sem)


def _zero_fill_pads(pad_start_ref, pad_len_ref, nact_ref, xs_ref, zero_ref, zsem):
    bm = zero_ref.shape[0]
    nblk = xs_ref.shape[0] // bm
    zero_ref[...] = jnp.zeros_like(zero_ref)

    def per_expert(e, c):
        start = pad_start_ref[e]
        n = pad_len_ref[e]
        head = jnp.minimum((-start) & (SUBLANES - 1), n)

        def one(r, c2):
            cp = _row_copy(zero_ref, 0, xs_ref, start + r, zsem)
            cp.start()
            cp.wait()
            return c2

        lax.fori_loop(0, head, one, 0)
        off = start + head
        rest = n - head
        p = bm // 2
        while p >= SUBLANES:
            take = (rest & p) != 0
            dst = pl.multiple_of(off, SUBLANES)

            @pl.when(take)
            def _(p=p, dst=dst):
                cp = pltpu.make_async_copy(zero_ref.at[pl.ds(0, p)], xs_ref.at[pl.ds(dst, p)], zsem)
                cp.start()
                cp.wait()

            off = off + jnp.where(take, p, 0)
            p //= 2
        return c

    lax.fori_loop(0, N_EXPERTS, per_expert, 0)

    def tail(j, c):
        cp = pltpu.make_async_copy(zero_ref, xs_ref.at[pl.ds(pl.multiple_of(j * bm, bm), bm)], zsem)
        cp.start()
        cp.wait()
        return c

    lax.fori_loop(nact_ref[0], nblk, tail, 0)


def _dispatch_body(pad_start_ref, pad_len_ref, nact_ref, pos_ref, xm_ref, xs_ref, zero_ref, sem, zsem):
    tm = pos_ref.shape[0] // TOP_K

    @pl.when(pl.program_id(0) == 0)
    def _():
        _zero_fill_pads(pad_start_ref, pad_len_ref, nact_ref, xs_ref, zero_ref, zsem)

    def issue(i, c):
        for k in range(TOP_K):
            _row_copy(xm_ref, i, xs_ref, pos_ref[i * TOP_K + k], sem).start()
        return c

    lax.fori_loop(0, tm, issue, 0)

    def drain(i, c):
        for _ in range(WAIT_UNROLL):
            _row_copy(xm_ref, 0, xs_ref, 0, sem).wait()
        return c

    lax.fori_loop(0, tm * TOP_K // WAIT_UNROLL, drain, 0)


def _dispatch(pad_start, pad_len, nact, pos_flat, xm, n_slots):
    t, d = xm.shape
    tm = min(TM_DISPATCH, t)
    grid_spec = pltpu.PrefetchScalarGridSpec(
        num_scalar_prefetch=3,
        grid=(t // tm,),
        in_specs=[pl.BlockSpec((tm * TOP_K,), lambda i, *_: (i,), memory_space=pltpu.SMEM),
                  pl.BlockSpec((tm, d), lambda i, *_: (i, 0))],
        out_specs=pl.BlockSpec(memory_space=pl.ANY),
        scratch_shapes=[pltpu.VMEM((MOE_BM, d), F32), pltpu.SemaphoreType.DMA(()), pltpu.SemaphoreType.DMA(())],
    )
    return pl.pallas_call(
        _dispatch_body,
        grid_spec=grid_spec,
        out_shape=jax.ShapeDtypeStruct((n_slots, d), F32),
        compiler_params=pltpu.CompilerParams(dimension_semantics=("arbitrary",), has_side_effects=True,
                                             vmem_limit_bytes=VMEM_LIMIT),
        name="dispatch",
    )(pad_start, pad_len, nact, pos_flat, xm)


STEP_UP, STEP_DOWN, STEP_TAIL, STEP_IDLE = 0, 1, 2, 3
NF = D_FF // MOE_TF
PASS_BLOCKS = MOE_RM // MOE_BM


def _swiglu(hg, hu):
    gate = jnp.minimum(hg, SWIGLU_LIMIT)
    up = jnp.clip(hu, -SWIGLU_LIMIT, SWIGLU_LIMIT)
    glu = gate / (1.0 + jnp.exp(-SWIGLU_ALPHA * gate))
    return (up + 1.0) * glu


def _experts_body(e_ref, f_ref, kind_ref, blk_ref, blk0_ref, nsub_ref,
                  xs_ref, wg_ref, wu_ref, bg_ref, bu_ref, wd_ref, bd_ref,
                  y_ref, xb_ref, h_ref, wgb_ref, wub_ref, wdb_ref, stg_ref, sem):
    t = pl.program_id(0)
    kind = kind_ref[t]
    f = f_ref[t]
    nsub = nsub_ref[t]

    def x_copy(j, slot):
        row = pl.multiple_of(blk0_ref[t] * MOE_BM + j * MOE_SUB, MOE_SUB)
        return pltpu.make_async_copy(xs_ref.at[pl.ds(row, MOE_SUB)], stg_ref.at[slot], sem.at[slot])

    @pl.when(kind == STEP_UP)
    def _up():
        @pl.when(f == 0)
        def _load_rows():
            x_copy(0, 0).start()

            def body(j, c):
                slot = j % 2

                @pl.when(j + 1 < nsub)
                def _():
                    x_copy(j + 1, 1 - slot).start()

                x_copy(j, slot).wait()
                xb_ref[pl.ds(pl.multiple_of(j * MOE_SUB, MOE_SUB), MOE_SUB), :] = stg_ref[slot].astype(BF16)
                return c

            lax.fori_loop(0, nsub, body, 0)

        wgb_ref[...] = wg_ref[0].astype(BF16)
        wub_ref[...] = wu_ref[0].astype(BF16)
        wdb_ref[pl.ds(pl.multiple_of(f * MOE_TF, MOE_TF), MOE_TF), :] = wd_ref[0].astype(BF16)

        def up_rows(row0, nrows):
            x = xb_ref[pl.ds(row0, nrows), :]
            hg = jnp.dot(x, wgb_ref[...], preferred_element_type=F32) + bg_ref[0]
            hu = jnp.dot(x, wub_ref[...], preferred_element_type=F32) + bu_ref[0]
            act = _swiglu(hg, hu).astype(BF16)
            for fs in range(NF):
                @pl.when(f == fs)
                def _(fs=fs):
                    h_ref[pl.ds(row0, nrows), fs * MOE_TF:(fs + 1) * MOE_TF] = act

        npair = nsub // 2

        def pair(i, c):
            up_rows(pl.multiple_of(i * MOE_BM, MOE_BM), MOE_BM)
            return c

        lax.fori_loop(0, npair, pair, 0)

        @pl.when(nsub % 2 == 1)
        def _():
            up_rows(pl.multiple_of(npair * MOE_BM, MOE_BM), MOE_SUB)

    @pl.when(kind == STEP_DOWN)
    def _down():
        row0 = pl.multiple_of((blk_ref[t] - blk0_ref[t]) * MOE_BM, MOE_BM)

        def down_rows(nrows):
            hrows = h_ref[pl.ds(row0, nrows), :]
            return jnp.dot(hrows, wdb_ref[...], preferred_element_type=F32) + bd_ref[0]

        @pl.when(nsub == 2)
        def _():
            y_ref[...] = down_rows(MOE_BM)

        @pl.when(nsub == 1)
        def _():
            y_ref[:MOE_SUB, :] = down_rows(MOE_SUB)
            y_ref[MOE_SUB:, :] = jnp.zeros((MOE_BM - MOE_SUB, y_ref.shape[1]), F32)

    @pl.when(kind == STEP_TAIL)
    def _tail():
        y_ref[...] = jnp.zeros_like(y_ref)


def _expert_schedule(counts, n_blocks, n_assign):
    i32 = jnp.int32
    counts = counts.astype(i32)
    nblk_e = (counts + MOE_BM - 1) // MOE_BM
    blk_end = jnp.cumsum(nblk_e)
    blk_start = blk_end - nblk_e
    npass_e = (nblk_e + PASS_BLOCKS - 1) // PASS_BLOCKS
    pass_end = jnp.cumsum(npass_e)
    pass_start = pass_end - npass_e
    n_pass_max = N_EXPERTS + n_assign // MOE_RM
    n_steps = n_pass_max * NF + n_blocks
    p = jnp.arange(n_pass_max, dtype=i32)
    p_valid = p < pass_end[-1]
    pe = jnp.minimum(jnp.sum(p[:, None] >= pass_end[None, :], axis=1), N_EXPERTS - 1)
    p_local = p - pass_start[pe]
    p_blk0 = blk_start[pe] + p_local * PASS_BLOCKS
    p_nblk = jnp.where(p_valid, jnp.clip(nblk_e[pe] - p_local * PASS_BLOCKS, 0, PASS_BLOCKS), 0)
    p_rows = jnp.clip(counts[pe] - p_local * MOE_RM, 0, MOE_RM)
    p_nsub = (p_rows + MOE_SUB - 1) // MOE_SUB
    p_steps = jnp.where(p_valid, NF + p_nblk, 0)
    s_end = jnp.cumsum(p_steps)
    s_start = s_end - p_steps
    total = s_end[-1]
    nact = blk_end[-1]
    last_e = pe[jnp.maximum(pass_end[-1] - 1, 0)]
    t = jnp.arange(n_steps, dtype=i32)
    sp = jnp.minimum(jnp.sum(t[:, None] >= s_end[None, :], axis=1), n_pass_max - 1)
    loc = t - s_start[sp]
    real = t < total
    is_up = real & (loc < NF)
    is_down = real & (loc >= NF)
    jb = loc - NF
    k = t - total
    is_tail = (~real) & (k < n_blocks - nact)
    kind = jnp.where(is_up, STEP_UP, jnp.where(is_down, STEP_DOWN, jnp.where(is_tail, STEP_TAIL, STEP_IDLE)))
    e_t = jnp.where(real, pe[sp], last_e)
    f_t = jnp.where(is_up, loc, NF - 1)
    blk0_t = p_blk0[sp]
    blk_t = jnp.where(is_up, blk0_t, jnp.where(is_down, blk0_t + jb,
                                               jnp.where(is_tail, nact + k, n_blocks - 1)))
    nsub_t = jnp.where(is_up, p_nsub[sp], jnp.clip(p_nsub[sp] - jb * (MOE_BM // MOE_SUB), 1, MOE_BM // MOE_SUB))
    tabs = (e_t, f_t, kind, blk_t, blk0_t, nsub_t)
    return tuple(a.astype(i32) for a in tabs)


def _experts(tabs, xs, w_up, b_up, w_down, b_down):
    n_slots, d = xs.shape
    n_steps = tabs[0].shape[0]
    grid_spec = pltpu.PrefetchScalarGridSpec(
        num_scalar_prefetch=len(tabs),
        grid=(n_steps,),
        in_specs=[
            pl.BlockSpec(memory_space=pl.ANY),
            pl.BlockSpec((1, d, MOE_TF), lambda t, e, f, *_: (e[t], 0, f[t])),
            pl.BlockSpec((1, d, MOE_TF), lambda t, e, f, *_: (e[t], 0, NF + f[t])),
            pl.BlockSpec((1, 1, MOE_TF), lambda t, e, f, *_: (e[t], 0, f[t])),
            pl.BlockSpec((1, 1, MOE_TF), lambda t, e, f, *_: (e[t], 0, NF + f[t])),
            pl.BlockSpec((1, MOE_TF, d), lambda t, e, f, *_: (e[t], f[t], 0)),
            pl.BlockSpec((1, 1, d), lambda t, e, f, *_: (e[t], 0, 0)),
        ],
        out_specs=pl.BlockSpec((MOE_BM, d), lambda t, e, f, kind, blk, *_: (blk[t], 0)),
        scratch_shapes=[
            pltpu.VMEM((MOE_RM, d), BF16),
            pltpu.VMEM((MOE_RM, D_FF), BF16),
            pltpu.VMEM((d, MOE_TF), BF16),
            pltpu.VMEM((d, MOE_TF), BF16),
            pltpu.VMEM((D_FF, d), BF16),
            pltpu.VMEM((2, MOE_SUB, d), F32),
            pltpu.SemaphoreType.DMA((2,)),
        ],
    )
    return pl.pallas_call(
        _experts_body,
        grid_spec=grid_spec,
        out_shape=jax.ShapeDtypeStruct((n_slots, d), F32),
        compiler_params=pltpu.CompilerParams(dimension_semantics=("arbitrary",),
                                             vmem_limit_bytes=VMEM_LIMIT),
        name="experts",
    )(*tabs, xs, w_up, w_up, b_up, b_up, w_down, b_down)


def _combine_body(pos_ref, gate_ref, x1_ref, gf_ref, ys_ref, o_ref, buf_ref, sem):
    tm = x1_ref.shape[0]

    def issue(i, c):
        for k in range(TOP_K):
            _row_copy(ys_ref, pos_ref[i * TOP_K + k], buf_ref.at[k], i, sem).start()
        return c

    lax.fori_loop(0, tm, issue, 0)

    def drain(i, c):
        for _ in range(WAIT_UNROLL):
            _row_copy(ys_ref, 0, buf_ref.at[0], 0, sem).wait()
        return c

    lax.fori_loop(0, tm * TOP_K // WAIT_UNROLL, drain, 0)
    gate = gate_ref[...]
    acc = x1_ref[...]
    for k in range(TOP_K):
        acc = acc + gate[:, k:k + 1] * buf_ref[k]
    o_ref[...] = _rms(acc, gf_ref[...])


def _combine(pos_flat, gates, x1, gf, ys):
    t, d = x1.shape
    tm = min(TM_COMBINE, t)
    return pl.pallas_call(
        _combine_body,
        grid=(t // tm,),
        in_specs=[pl.BlockSpec((tm * TOP_K,), lambda i: (i,), memory_space=pltpu.SMEM),
                  pl.BlockSpec((tm, TOP_K), lambda i: (i, 0)),
                  pl.BlockSpec((tm, d), lambda i: (i, 0)),
                  _const_spec(gf.shape),
                  pl.BlockSpec(memory_space=pl.ANY)],
        out_specs=pl.BlockSpec((tm, d), lambda i: (i, 0)),
        out_shape=jax.ShapeDtypeStruct((t, d), F32),
        scratch_shapes=[pltpu.VMEM((TOP_K, tm, d), F32), pltpu.SemaphoreType.DMA(())],
        compiler_params=pltpu.CompilerParams(dimension_semantics=("arbitrary",),
                                             vmem_limit_bytes=VMEM_LIMIT),
        name="combine",
    )(pos_flat, gates, x1, gf, ys)


def _rope_tables(s):
    pos = jnp.arange(s, dtype=F32)
    inv_freq = jnp.power(ROPE_THETA, -jnp.arange(0, B_ROPE_DIM, 2, dtype=F32) / B_ROPE_DIM)
    ang = pos[:, None] * inv_freq[None, :]
    cos, sin = jnp.cos(ang), jnp.sin(ang)
    pad = LANES - B_ROPE_DIM
    cos = jnp.concatenate([cos, cos, jnp.ones((s, pad), F32)], axis=1)
    sin = jnp.concatenate([sin, sin, jnp.zeros((s, pad), F32)], axis=1)
    return cos, sin


def _window_bias():
    qi = jnp.arange(A_BLOCK)[:, None]
    kj = jnp.arange(3 * A_BLOCK)[None, :]
    dist = jnp.abs(qi + A_BLOCK - kj)
    slopes = jnp.power(2.0, -8.0 * jnp.arange(1, A_HEADS + 1, dtype=F32) / A_HEADS)
    bias = -slopes[:, None, None] * dist.astype(F32)[None]
    bias = jnp.where((dist <= WINDOW)[None], bias, -jnp.inf)
    bias = bias.reshape(A_KV_HEADS, 2, 2, A_BLOCK, 3 * A_BLOCK).transpose(0, 2, 1, 3, 4)
    bias = bias.reshape(2 * A_KV_HEADS, 2 * A_BLOCK, 3 * A_BLOCK)
    no_prev = (kj < A_BLOCK)[None]
    no_next = (kj >= 2 * A_BLOCK)[None]
    ninf = -jnp.inf
    return jnp.stack([bias, jnp.where(no_prev, ninf, bias), jnp.where(no_next, ninf, bias),
                      jnp.where(no_prev | no_next, ninf, bias)])


def _layer(x, attn_norm, w_in, a_sink, b_q_norm, b_w_uq, b_kv_norm, b_w_ukv, out_norm_a, out_norm_b, w_o,
           mlp_norm, w_router, b_router, w_up, b_up, w_down, b_down):
    b, s, d = x.shape
    t = b * s
    win = jnp.pad(w_in, ((0, 0), (0, IN_COLS_PAD - w_in.shape[1]))).astype(BF16)
    wq = b_w_uq.reshape(B_Q_RANK, B_HEADS, B_NOPE_DIM + B_ROPE_DIM)
    wq_pe = jnp.pad(wq[:, :, B_NOPE_DIM:], ((0, 0), (0, 0), (0, LANES - B_ROPE_DIM)))
    wuq = jnp.concatenate([wq[:, :, :B_NOPE_DIM].reshape(B_Q_RANK, -1), wq_pe.reshape(B_Q_RANK, -1)],
                          axis=1).astype(BF16)
    wkv = b_w_ukv.reshape(B_KV_RANK, B_HEADS, B_NOPE_DIM + B_V_DIM)
    wukv = jnp.concatenate([wkv[:, :, :B_NOPE_DIM].reshape(B_KV_RANK, -1),
                            wkv[:, :, B_NOPE_DIM:].reshape(B_KV_RANK, -1)], axis=1).astype(BF16)
    cos, sin = _rope_tables(s)

    qa, ka, va, qb, kb, vb = _prologue(x, attn_norm[None], win, b_q_norm[None], wuq, b_kv_norm[None], wukv,
                                       cos, sin)
    ya = _window_attn(qa, ka, va, a_sink, _window_bias())
    yb = _mla_attn(qb, kb, vb)
    x1, xm, eidx, gates, rank, counts = _out_router(
        ya.reshape(t, A_WIDTH), yb.reshape(t, B_WIDTH), x.reshape(t, d), out_norm_a[None], out_norm_b[None],
        w_o.astype(BF16), mlp_norm[None], w_router, b_router[None])

    bm = MOE_BM
    counts = counts[0]
    padded = (counts + bm - 1) // bm * bm
    ends = jnp.cumsum(padded)
    starts = ends - padded
    pos = (starts[eidx] + rank).reshape(t * TOP_K)
    nblk = (t * TOP_K + N_EXPERTS * (bm - 1) + bm - 1) // bm
    nact = (ends[-1] // bm).astype(jnp.int32).reshape(1)

    xs = _dispatch((starts + counts).astype(jnp.int32), (padded - counts).astype(jnp.int32), nact, pos, xm,
                   nblk * bm)
    ys = _experts(_expert_schedule(counts, nblk, t * TOP_K), xs, w_up, b_up[:, None, :], w_down,
                  b_down[:, None, :])
    return x1, pos, gates, ys


def kernel(x, attn_norm, w_in, a_sink, b_q_norm, b_w_uq, b_kv_norm, b_w_ukv, out_norm_a, out_norm_b, w_o,
           mlp_norm, w_router, b_router, w_up, b_up, w_down, b_down, final_norm):
    b, s, d = x.shape
    assert d == D_MODEL and s % min(TQ_MLA, s) == 0 and s % min(CK_MLA, s) == 0 and s % TM_PRO == 0 and attn_norm.shape[0] == 1
    x1, pos, gates, ys = _layer(x, attn_norm[0], w_in[0], a_sink[0], b_q_norm[0], b_w_uq[0], b_kv_norm[0],
                                b_w_ukv[0], out_norm_a[0], out_norm_b[0], w_o[0], mlp_norm[0], w_router[0],
                                b_router[0], w_up[0], b_up[0], w_down[0], b_down[0])
    out = _combine(pos, gates, x1, final_norm[None], ys)
    return out.reshape(b, s, d)
```

```python
import functools

import jax
import numpy as np
import jax.numpy as jnp
from jax import lax
from jax.experimental import pallas as pl
from jax.experimental.pallas import tpu as pltpu

D_MODEL = 2048
A_HEADS, A_KV_HEADS, A_HEAD_DIM = 16, 4, 64
A_GROUP = A_HEADS // A_KV_HEADS
WINDOW = 128
A_BLOCK = 128
B_HEADS, B_Q_RANK, B_KV_RANK = 8, 512, 256
B_NOPE_DIM, B_ROPE_DIM, B_V_DIM = 128, 64, 128
ROPE_THETA = 10000.0
A_WIDTH = A_HEADS * A_HEAD_DIM
B_WIDTH = B_HEADS * B_V_DIM
A_KV_COLS = A_KV_HEADS * A_HEAD_DIM
N_EXPERTS, TOP_K, D_FF = 32, 4, 2048
SWIGLU_ALPHA, SWIGLU_LIMIT = 1.702, 7.0
EPS = 1e-5

LANES = 128
SUBLANES = 8
B_QK_PAD = 2 * LANES
IN_COLS_PAD = A_WIDTH + 2 * A_KV_COLS + B_Q_RANK + B_KV_RANK + LANES
VMEM_LIMIT = 56 * 1024 * 1024

TM_PRO = 256
TQ_MLA = 1024
CK_MLA = 1024
TM_OUT = 512
TM_DISPATCH = 512
TM_COMBINE = 128
MOE_BM = 512
MOE_SUB = 256
MOE_RM = 2048
MOE_TF = 256
WAIT_UNROLL = 16

BF16 = jnp.bfloat16
F32 = jnp.float32


def _rms(x, g):
    return x * lax.rsqrt(jnp.mean(x * x, axis=-1, keepdims=True) + EPS) * g


def _const_spec(shape):
    nd = len(shape)
    return pl.BlockSpec(shape, lambda *_: (0,) * nd, pipeline_mode=pl.Buffered(1))


def _rope(x, cos, sin):
    lane = lax.broadcasted_iota(jnp.int32, x.shape, 1)
    up = pltpu.roll(x, LANES - B_ROPE_DIM // 2, 1)
    dn = pltpu.roll(x, B_ROPE_DIM // 2, 1)
    sw = jnp.where(lane < B_ROPE_DIM // 2, -up, jnp.where(lane < B_ROPE_DIM, dn, 0.0))
    return x * cos + sw * sin


def _prologue_body(x_ref, g_ref, win_ref, qn_ref, wuq_ref, kvn_ref, wukv_ref, cos_ref, sin_ref,
                   qa_ref, ka_ref, va_ref, qb_ref, kb_ref, vb_ref):
    x = x_ref[0]
    xn = _rms(x, g_ref[...]).astype(BF16)
    proj = jnp.dot(xn, win_ref[...], preferred_element_type=F32)
    c0 = A_WIDTH
    c1 = c0 + A_KV_COLS
    c2 = c1 + A_KV_COLS
    c3 = c2 + B_Q_RANK
    c4 = c3 + B_KV_RANK
    qa_ref[0] = (proj[:, :A_WIDTH] * A_HEAD_DIM ** -0.5).astype(BF16)
    low = lax.broadcasted_iota(jnp.int32, (proj.shape[0], LANES), 1) < A_HEAD_DIM
    for src, dst in ((c0, ka_ref), (c1, va_ref)):
        for pair in range(A_KV_HEADS // 2):
            two = proj[:, src + pair * LANES:src + (pair + 1) * LANES]
            swapped = pltpu.roll(two, A_HEAD_DIM, 1)
            dst[0, 4 * pair + 0] = jnp.where(low, two, 0.0).astype(BF16)
            dst[0, 4 * pair + 1] = jnp.where(low, 0.0, swapped).astype(BF16)
            dst[0, 4 * pair + 2] = jnp.where(low, swapped, 0.0).astype(BF16)
            dst[0, 4 * pair + 3] = jnp.where(low, 0.0, two).astype(BF16)
    cq = _rms(proj[:, c2:c3], qn_ref[...]).astype(BF16)
    q = jnp.dot(cq, wuq_ref[...], preferred_element_type=F32)
    ckv = _rms(proj[:, c3:c4], kvn_ref[...]).astype(BF16)
    kv = jnp.dot(ckv, wukv_ref[...], preferred_element_type=F32)
    cos = cos_ref[...]
    sin = sin_ref[...]
    kpe = _rope(proj[:, c4:c4 + LANES], cos, sin).astype(BF16)
    b_scale = (B_NOPE_DIM + B_ROPE_DIM) ** -0.5
    hw = B_HEADS * LANES
    for h in range(B_HEADS):
        sl = slice(h * LANES, (h + 1) * LANES)
        qb_ref[0, h, :, :LANES] = (q[:, sl] * b_scale).astype(BF16)
        qpe = _rope(q[:, hw + h * LANES:hw + (h + 1) * LANES], cos, sin)
        qb_ref[0, h, :, LANES:] = (qpe * b_scale).astype(BF16)
        kb_ref[0, h, :, :LANES] = kv[:, sl].astype(BF16)
        kb_ref[0, h, :, LANES:] = kpe
        vb_ref[0, h] = kv[:, hw + h * LANES:hw + (h + 1) * LANES].astype(BF16)


def _prologue(x, g, win, qn, wuq, kvn, wukv, cos, sin):
    b, s, d = x.shape
    tm = TM_PRO
    grid = (b, s // tm)
    sds = jax.ShapeDtypeStruct
    out_shape = (
        sds((b, s, A_WIDTH), BF16),
        sds((b, 2 * A_KV_HEADS, s, LANES), BF16),
        sds((b, 2 * A_KV_HEADS, s, LANES), BF16),
        sds((b, B_HEADS, s, B_QK_PAD), BF16),
        sds((b, B_HEADS, s, B_QK_PAD), BF16),
        sds((b, B_HEADS, s, B_V_DIM), BF16),
    )

    def hspec(nh, w):
        return pl.BlockSpec((1, nh, tm, w), lambda bi, i: (bi, 0, i, 0))

    return pl.pallas_call(
        _prologue_body,
        grid=grid,
        in_specs=[
            pl.BlockSpec((1, tm, d), lambda bi, i: (bi, i, 0)),
            _const_spec(g.shape), _const_spec(win.shape), _const_spec(qn.shape), _const_spec(wuq.shape),
            _const_spec(kvn.shape), _const_spec(wukv.shape),
            pl.BlockSpec((tm, LANES), lambda bi, i: (i, 0)),
            pl.BlockSpec((tm, LANES), lambda bi, i: (i, 0)),
        ],
        out_specs=(pl.BlockSpec((1, tm, A_WIDTH), lambda bi, i: (bi, i, 0)),
                   hspec(2 * A_KV_HEADS, LANES), hspec(2 * A_KV_HEADS, LANES),
                   hspec(B_HEADS, B_QK_PAD), hspec(B_HEADS, B_QK_PAD), hspec(B_HEADS, B_V_DIM)),
        out_shape=out_shape,
        compiler_params=pltpu.CompilerParams(dimension_semantics=("parallel", "parallel"),
                                             vmem_limit_bytes=VMEM_LIMIT),
        name="prologue",
    )(x, g, win, qn, wuq, kvn, wukv, cos, sin)


def _window_body(sink_ref, q_ref, kp_ref, kc_ref, kn_ref, vp_ref, vc_ref, vn_ref, bias_ref, o_ref):
    two = 2 * A_BLOCK
    low = lax.broadcasted_iota(jnp.int32, (two, LANES), 1) < A_HEAD_DIM
    for hk in range(A_KV_HEADS):
        q2 = jnp.concatenate([q_ref[0, :, (2 * hk) * LANES:(2 * hk + 1) * LANES],
                              q_ref[0, :, (2 * hk + 1) * LANES:(2 * hk + 2) * LANES]], axis=0)
        acc = jnp.zeros((two, LANES), F32)
        inv = []
        for half in range(2):
            z = 2 * hk + half
            kband = jnp.concatenate([kp_ref[0, z], kc_ref[0, z], kn_ref[0, z]], axis=0)
            vband = jnp.concatenate([vp_ref[0, z], vc_ref[0, z], vn_ref[0, z]], axis=0)
            s = lax.dot_general(q2, kband, (((1,), (1,)), ((), ())), preferred_element_type=F32)
            s = s + bias_ref[0, z]
            sink = jnp.concatenate(
                [jnp.full((A_BLOCK, 1), sink_ref[hk * A_GROUP + 2 * jj + half], F32) for jj in range(2)], axis=0)
            m = jnp.maximum(jnp.max(s, axis=-1, keepdims=True), sink)
            p = jnp.exp(s - m)
            den = jnp.sum(p, axis=-1, keepdims=True) + jnp.exp(sink - m)
            inv.append(1.0 / den)
            acc = acc + jnp.dot(p.astype(BF16), vband, preferred_element_type=F32)
        o = (acc * jnp.where(low, inv[0], inv[1])).astype(BF16)
        o_ref[0, :, (2 * hk) * LANES:(2 * hk + 1) * LANES] = o[:A_BLOCK]
        o_ref[0, :, (2 * hk + 1) * LANES:(2 * hk + 2) * LANES] = o[A_BLOCK:]


def _window_attn(qa, ka, va, sink, bias):
    b, s, _ = qa.shape
    nb = s // A_BLOCK
    kv_blk = (1, 2 * A_KV_HEADS, A_BLOCK, LANES)
    prev = pl.BlockSpec(kv_blk, lambda bi, n: (bi, 0, jnp.maximum(n - 1, 0), 0))
    cur = pl.BlockSpec(kv_blk, lambda bi, n: (bi, 0, n, 0))
    nxt = pl.BlockSpec(kv_blk, lambda bi, n: (bi, 0, jnp.minimum(n + 1, nb - 1), 0))

    def edge(bi, n):
        return ((n == 0).astype(jnp.int32) + 2 * (n == nb - 1).astype(jnp.int32), 0, 0, 0)

    return pl.pallas_call(
        _window_body,
        grid=(b, nb),
        in_specs=[
            pl.BlockSpec(memory_space=pltpu.SMEM),
            pl.BlockSpec((1, A_BLOCK, A_WIDTH), lambda bi, n: (bi, n, 0)),
            prev, cur, nxt, prev, cur, nxt,
            pl.BlockSpec((1,) + bias.shape[1:], edge),
        ],
        out_specs=pl.BlockSpec((1, A_BLOCK, A_WIDTH), lambda bi, n: (bi, n, 0)),
        out_shape=jax.ShapeDtypeStruct((b, s, A_WIDTH), BF16),
        compiler_params=pltpu.CompilerParams(dimension_semantics=("parallel", "parallel"),
                                             vmem_limit_bytes=VMEM_LIMIT),
        name="window_attn",
    )(sink, qa, ka, ka, ka, va, va, va, bias)


def _mla_body(q_ref, k_ref, v_ref, o_ref):
    q = q_ref[0, 0]
    tq = q.shape[0]
    s_len = k_ref.shape[2]
    ck = min(CK_MLA, s_len)
    m = jnp.full((tq, 1), -jnp.inf, F32)
    l = jnp.zeros((tq, 1), F32)
    acc = jnp.zeros((tq, B_V_DIM), F32)
    for c in range(s_len // ck):
        k_c = k_ref[0, 0, c * ck:(c + 1) * ck, :]
        v_c = v_ref[0, 0, c * ck:(c + 1) * ck, :]
        s = lax.dot_general(q, k_c, (((1,), (1,)), ((), ())), preferred_element_type=F32)
        m_new = jnp.maximum(m, jnp.max(s, axis=-1, keepdims=True))
        alpha = jnp.exp(m - m_new)
        p = jnp.exp(s - m_new)
        l = alpha * l + jnp.sum(p, axis=-1, keepdims=True)
        acc = alpha * acc + jnp.dot(p.astype(BF16), v_c, preferred_element_type=F32)
        m = m_new
    o_ref[0] = (acc / l).astype(BF16)


def _mla_attn(qb, kb, vb):
    b, nh, s, _ = qb.shape
    tq = min(TQ_MLA, s)
    return pl.pallas_call(
        _mla_body,
        grid=(b, nh, s // tq),
        in_specs=[
            pl.BlockSpec((1, 1, tq, B_QK_PAD), lambda bi, h, i: (bi, h, i, 0)),
            pl.BlockSpec((1, 1, s, B_QK_PAD), lambda bi, h, i: (bi, h, 0, 0)),
            pl.BlockSpec((1, 1, s, B_V_DIM), lambda bi, h, i: (bi, h, 0, 0)),
        ],
        out_specs=pl.BlockSpec((1, tq, B_V_DIM), lambda bi, h, i: (bi, i, h)),
        out_shape=jax.ShapeDtypeStruct((b, s, B_WIDTH), BF16),
        compiler_params=pltpu.CompilerParams(dimension_semantics=("parallel", "parallel", "parallel"),
                                             vmem_limit_bytes=VMEM_LIMIT),
        name="mla_attn",
    )(qb, kb, vb)


def _out_router_body(ya_ref, yb_ref, x_ref, ga_ref, gb_ref, wo_ref, gm_ref, wr_ref, br_ref,
                     x1_ref, xm_ref, eidx_ref, gate_ref, rank_ref, cnt_ref, run_ref):
    i = pl.program_id(0)

    @pl.when(i == 0)
    def _():
        run_ref[...] = jnp.zeros_like(run_ref)

    na = _rms(ya_ref[...].astype(F32), ga_ref[...]).astype(BF16)
    nb = _rms(yb_ref[...].astype(F32), gb_ref[...]).astype(BF16)
    att = jnp.dot(na, wo_ref[:A_WIDTH, :], preferred_element_type=F32)
    att = att + jnp.dot(nb, wo_ref[A_WIDTH:, :], preferred_element_type=F32)
    x1 = x_ref[...] + att
    x1_ref[...] = x1
    hn = _rms(x1, gm_ref[...])
    xm_ref[...] = hn
    tm = hn.shape[0]
    hi = hn.astype(BF16)
    lo = (hn - hi.astype(F32)).astype(BF16)
    prod = jnp.dot(jnp.concatenate([hi, lo], axis=0), wr_ref[...], preferred_element_type=F32)
    logits = (prod[:tm, :N_EXPERTS] + prod[:tm, N_EXPERTS:] + prod[tm:, :N_EXPERTS] + prod[tm:, N_EXPERTS:]
              + br_ref[...])
    lane = lax.broadcasted_iota(jnp.int32, (tm, N_EXPERTS), 1)
    work = logits
    sel = jnp.zeros((tm, N_EXPERTS), F32)
    hots, vals, idxs = [], [], []
    for _k in range(TOP_K):
        mx = jnp.max(work, axis=-1, keepdims=True)
        idx = jnp.min(jnp.where(work == mx, lane, N_EXPERTS), axis=-1, keepdims=True)
        hot = lane == idx
        hots.append(hot)
        vals.append(mx)
        idxs.append(idx)
        sel = sel + hot.astype(F32)
        work = jnp.where(hot, -jnp.inf, work)
    exps = [jnp.exp(v - vals[0]) for v in vals]
    den = exps[0] + exps[1] + exps[2] + exps[3]
    r_i = lax.broadcasted_iota(jnp.int32, (tm, tm), 0)
    c_i = lax.broadcasted_iota(jnp.int32, (tm, tm), 1)
    tri = (c_i < r_i).astype(BF16)
    before = jnp.dot(tri, sel.astype(BF16), preferred_element_type=F32) + run_ref[...]
    lane4 = lax.broadcasted_iota(jnp.int32, (tm, TOP_K), 1)
    eidx = jnp.zeros((tm, TOP_K), jnp.int32)
    gate = jnp.zeros((tm, TOP_K), F32)
    rank = jnp.zeros((tm, TOP_K), jnp.int32)
    for k in range(TOP_K):
        rk = jnp.sum(jnp.where(hots[k], before, 0.0), axis=-1, keepdims=True).astype(jnp.int32)
        eidx = jnp.where(lane4 == k, idxs[k], eidx)
        gate = jnp.where(lane4 == k, exps[k] / den, gate)
        rank = jnp.where(lane4 == k, rk, rank)
    eidx_ref[...] = eidx
    gate_ref[...] = gate
    rank_ref[...] = rank
    run = run_ref[...] + jnp.sum(sel, axis=0, keepdims=True)
    run_ref[...] = run
    cnt_ref[...] = run.astype(jnp.int32)


def _out_router(ya, yb, x2d, ga, gb, wo, gm, wr, br):
    t, d = x2d.shape
    tm = min(TM_OUT, t)
    sds = jax.ShapeDtypeStruct

    def row(w):
        return pl.BlockSpec((tm, w), lambda i: (i, 0))

    return pl.pallas_call(
        _out_router_body,
        grid=(t // tm,),
        in_specs=[row(A_WIDTH), row(B_WIDTH), row(d), _const_spec(ga.shape), _const_spec(gb.shape),
                  _const_spec(wo.shape), _const_spec(gm.shape), _const_spec(wr.shape), _const_spec(br.shape)],
        out_specs=(row(d), row(d), row(TOP_K), row(TOP_K), row(TOP_K),
                   pl.BlockSpec((1, N_EXPERTS), lambda i: (0, 0))),
        out_shape=(sds((t, d), F32), sds((t, d), F32), sds((t, TOP_K), jnp.int32), sds((t, TOP_K), F32),
                   sds((t, TOP_K), jnp.int32), sds((1, N_EXPERTS), jnp.int32)),
        scratch_shapes=[pltpu.VMEM((1, N_EXPERTS), F32)],
        compiler_params=pltpu.CompilerParams(dimension_semantics=("arbitrary",),
                                             vmem_limit_bytes=VMEM_LIMIT),
        name="out_router",
    )(ya, yb, x2d, ga, gb, wo, gm, wr, br)


def _row_copy(src_ref, src_row, dst_ref, dst_row, sem):
    return pltpu.make_async_copy(src_ref.at[pl.ds(src_row, 1)], dst_ref.at[pl.ds(dst_row, 1)], sem)


def _zero_fill_pads(pad_start_ref, pad_len_ref, nact_ref, xs_ref, zero_ref, zsem):
    bm = zero_ref.shape[0]
    nblk = xs_ref.shape[0] // bm
    zero_ref[...] = jnp.zeros_like(zero_ref)

    def per_expert(e, c):
        start = pad_start_ref[e]
        n = pad_len_ref[e]
        head = jnp.minimum((-start) & (SUBLANES - 1), n)

        def one(r, c2):
            cp = _row_copy(zero_ref, 0, xs_ref, start + r, zsem)
            cp.start()
            cp.wait()
            return c2

        lax.fori_loop(0, head, one, 0)
        off = start + head
        rest = n - head
        p = bm // 2
        while p >= SUBLANES:
            take = (rest & p) != 0
            dst = pl.multiple_of(off, SUBLANES)

            @pl.when(take)
            def _(p=p, dst=dst):
                cp = pltpu.make_async_copy(zero_ref.at[pl.ds(0, p)], xs_ref.at[pl.ds(dst, p)], zsem)
                cp.start()
                cp.wait()

            off = off + jnp.where(take, p, 0)
            p //= 2
        return c

    lax.fori_loop(0, N_EXPERTS, per_expert, 0)

    def tail(j, c):
        cp = pltpu.make_async_copy(zero_ref, xs_ref.at[pl.ds(pl.multiple_of(j * bm, bm), bm)], zsem)
        cp.start()
        cp.wait()
        return c

    lax.fori_loop(nact_ref[0], nblk, tail, 0)


def _dispatch_body(start_ref, pad_start_ref, pad_len_ref, nact_ref, eidx_ref, rank_ref, xm_ref, xs_ref,
                   zero_ref, sem, zsem):
    tm = eidx_ref.shape[0] // TOP_K

    @pl.when(pl.program_id(0) == 0)
    def _():
        _zero_fill_pads(pad_start_ref, pad_len_ref, nact_ref, xs_ref, zero_ref, zsem)

    def issue(i, c):
        for k in range(TOP_K):
            a = i * TOP_K + k
            _row_copy(xm_ref, i, xs_ref, start_ref[eidx_ref[a]] + rank_ref[a], sem).start()
        return c

    lax.fori_loop(0, tm, issue, 0)

    def drain(i, c):
        for _ in range(WAIT_UNROLL):
            _row_copy(xm_ref, 0, xs_ref, 0, sem).wait()
        return c

    lax.fori_loop(0, tm * TOP_K // WAIT_UNROLL, drain, 0)


def _dispatch(start, pad_start, pad_len, nact, eidx_flat, rank_flat, xm, n_slots):
    t, d = xm.shape
    tm = min(TM_DISPATCH, t)
    grid_spec = pltpu.PrefetchScalarGridSpec(
        num_scalar_prefetch=4,
        grid=(t // tm,),
        in_specs=[pl.BlockSpec((tm * TOP_K,), lambda i, *_: (i,), memory_space=pltpu.SMEM),
                  pl.BlockSpec((tm * TOP_K,), lambda i, *_: (i,), memory_space=pltpu.SMEM),
                  pl.BlockSpec((tm, d), lambda i, *_: (i, 0))],
        out_specs=pl.BlockSpec(memory_space=pl.ANY),
        scratch_shapes=[pltpu.VMEM((MOE_BM, d), F32), pltpu.SemaphoreType.DMA(()), pltpu.SemaphoreType.DMA(())],
    )
    return pl.pallas_call(
        _dispatch_body,
        grid_spec=grid_spec,
        out_shape=jax.ShapeDtypeStruct((n_slots, d), F32),
        compiler_params=pltpu.CompilerParams(dimension_semantics=("arbitrary",), has_side_effects=True,
                                             vmem_limit_bytes=VMEM_LIMIT),
        name="dispatch",
    )(start, pad_start, pad_len, nact, eidx_flat, rank_flat, xm)


STEP_UP, STEP_DOWN, STEP_TAIL, STEP_IDLE = 0, 1, 2, 3
NF = D_FF // MOE_TF
PASS_BLOCKS = MOE_RM // MOE_BM


def _swiglu(hg, hu):
    gate = jnp.minimum(hg, SWIGLU_LIMIT)
    up = jnp.clip(hu, -SWIGLU_LIMIT, SWIGLU_LIMIT)
    glu = gate / (1.0 + jnp.exp(-SWIGLU_ALPHA * gate))
    return (up + 1.0) * glu


def _experts_body(e_ref, f_ref, kind_ref, blk_ref, blk0_ref, nsub_ref,
                  xs_ref, wg_ref, wu_ref, bg_ref, bu_ref, wd_ref, bd_ref,
                  y_ref, xb_ref, h_ref, wgb_ref, wub_ref, wdb_ref, stg_ref, sem):
    t = pl.program_id(0)
    kind = kind_ref[t]
    f = f_ref[t]
    nsub = nsub_ref[t]

    def x_copy(j, slot):
        row = pl.multiple_of(blk0_ref[t] * MOE_BM + j * MOE_SUB, MOE_SUB)
        return pltpu.make_async_copy(xs_ref.at[pl.ds(row, MOE_SUB)], stg_ref.at[slot], sem.at[slot])

    @pl.when(kind == STEP_UP)
    def _up():
        @pl.when(f == 0)
        def _load_rows():
            x_copy(0, 0).start()

            def body(j, c):
                slot = j % 2

                @pl.when(j + 1 < nsub)
                def _():
                    x_copy(j + 1, 1 - slot).start()

                x_copy(j, slot).wait()
                xb_ref[pl.ds(pl.multiple_of(j * MOE_SUB, MOE_SUB), MOE_SUB), :] = stg_ref[slot].astype(BF16)
                return c

            lax.fori_loop(0, nsub, body, 0)

        wgb_ref[...] = wg_ref[0].astype(BF16)
        wub_ref[...] = wu_ref[0].astype(BF16)
        wdb_ref[pl.ds(pl.multiple_of(f * MOE_TF, MOE_TF), MOE_TF), :] = wd_ref[0].astype(BF16)

        def up_rows(row0, nrows):
            x = xb_ref[pl.ds(row0, nrows), :]
            hg = jnp.dot(x, wgb_ref[...], preferred_element_type=F32) + bg_ref[0]
            hu = jnp.dot(x, wub_ref[...], preferred_element_type=F32) + bu_ref[0]
            act = _swiglu(hg, hu).astype(BF16)
            for fs in range(NF):
                @pl.when(f == fs)
                def _(fs=fs):
                    h_ref[pl.ds(row0, nrows), fs * MOE_TF:(fs + 1) * MOE_TF] = act

        npair = nsub // 2

        def pair(i, c):
            up_rows(pl.multiple_of(i * MOE_BM, MOE_BM), MOE_BM)
            return c

        lax.fori_loop(0, npair, pair, 0)

        @pl.when(nsub % 2 == 1)
        def _():
            up_rows(pl.multiple_of(npair * MOE_BM, MOE_BM), MOE_SUB)

    @pl.when(kind == STEP_DOWN)
    def _down():
        row0 = pl.multiple_of((blk_ref[t] - blk0_ref[t]) * MOE_BM, MOE_BM)

        def down_rows(nrows):
            hrows = h_ref[pl.ds(row0, nrows), :]
            return jnp.dot(hrows, wdb_ref[...], preferred_element_type=F32) + bd_ref[0]

        @pl.when(nsub == 2)
        def _():
            y_ref[...] = down_rows(MOE_BM)

        @pl.when(nsub == 1)
        def _():
            y_ref[:MOE_SUB, :] = down_rows(MOE_SUB)
            y_ref[MOE_SUB:, :] = jnp.zeros((MOE_BM - MOE_SUB, y_ref.shape[1]), F32)

    @pl.when(kind == STEP_TAIL)
    def _tail():
        y_ref[...] = jnp.zeros_like(y_ref)


def _schedule_body(n_blocks, cnt_ref, e_ref, f_ref, kind_ref, blk_ref, blk0_ref, nsub_ref,
                   start_ref, pad_start_ref, pad_len_ref, nact_ref):
    n_steps = e_ref.shape[0]
    sub_per_blk = MOE_BM // MOE_SUB
    zero = jnp.int32(0)

    def put(idx, e, f, kind, blk, blk0, nsub):
        e_ref[idx] = e
        f_ref[idx] = f
        kind_ref[idx] = kind
        blk_ref[idx] = blk
        blk0_ref[idx] = blk0
        nsub_ref[idx] = nsub

    def per_expert(e, carry):
        t, blk, last_e = carry
        n = cnt_ref[e]
        nblk = (n + (MOE_BM - 1)) // MOE_BM
        start_ref[e] = blk * MOE_BM
        pad_start_ref[e] = blk * MOE_BM + n
        pad_len_ref[e] = nblk * MOE_BM - n

        def per_pass(p, t2):
            b0 = blk + p * PASS_BLOCKS
            nb = jnp.minimum(nblk - p * PASS_BLOCKS, PASS_BLOCKS)
            nsub = (jnp.minimum(n - p * MOE_RM, MOE_RM) + (MOE_SUB - 1)) // MOE_SUB
            for f in range(NF):
                put(t2 + f, e, f, STEP_UP, b0, b0, nsub)

            def per_blk(j, c):
                put(t2 + NF + j, e, NF - 1, STEP_DOWN, b0 + j, b0,
                    jnp.clip(nsub - j * sub_per_blk, 1, sub_per_blk))
                return c

            lax.fori_loop(0, nb, per_blk, 0)
            return t2 + NF + nb

        t = lax.fori_loop(0, (nblk + (PASS_BLOCKS - 1)) // PASS_BLOCKS, per_pass, t)
        return t, blk + nblk, jnp.where(n > 0, e, last_e)

    t, nact, last_e = lax.fori_loop(0, N_EXPERTS, per_expert, (zero, zero, zero))
    nact_ref[0] = nact

    def spare(i, c):
        is_tail = i < n_blocks - nact
        blk = jnp.where(is_tail, nact + i, n_blocks - 1)
        put(t + i, last_e, NF - 1, jnp.where(is_tail, STEP_TAIL, STEP_IDLE), blk, blk, 1)
        return c

    lax.fori_loop(0, n_steps - t, spare, 0)


def _expert_schedule(counts, n_blocks, n_assign):
    n_pass_max = N_EXPERTS + n_assign // MOE_RM
    n_steps = n_pass_max * NF + n_blocks
    smem = pl.BlockSpec(memory_space=pltpu.SMEM)
    i32 = jnp.int32
    out_shape = tuple(jax.ShapeDtypeStruct((n,), i32) for n in (n_steps,) * 6 + (N_EXPERTS,) * 3 + (1,))
    outs = pl.pallas_call(
        functools.partial(_schedule_body, n_blocks),
        in_specs=[smem],
        out_specs=tuple(smem for _ in out_shape),
        out_shape=out_shape,
        name="schedule",
    )(counts)
    return outs[:6], outs[6], outs[7], outs[8], outs[9]


def _experts(tabs, xs, w_up, b_up, w_down, b_down):
    n_slots, d = xs.shape
    n_steps = tabs[0].shape[0]
    grid_spec = pltpu.PrefetchScalarGridSpec(
        num_scalar_prefetch=len(tabs),
        grid=(n_steps,),
        in_specs=[
            pl.BlockSpec(memory_space=pl.ANY),
            pl.BlockSpec((1, d, MOE_TF), lambda t, e, f, *_: (e[t], 0, f[t])),
            pl.BlockSpec((1, d, MOE_TF), lambda t, e, f, *_: (e[t], 0, NF + f[t])),
            pl.BlockSpec((1, 1, MOE_TF), lambda t, e, f, *_: (e[t], 0, f[t])),
            pl.BlockSpec((1, 1, MOE_TF), lambda t, e, f, *_: (e[t], 0, NF + f[t])),
            pl.BlockSpec((1, MOE_TF, d), lambda t, e, f, *_: (e[t], f[t], 0)),
            pl.BlockSpec((1, 1, d), lambda t, e, f, *_: (e[t], 0, 0)),
        ],
        out_specs=pl.BlockSpec((MOE_BM, d), lambda t, e, f, kind, blk, *_: (blk[t], 0)),
        scratch_shapes=[
            pltpu.VMEM((MOE_RM, d), BF16),
            pltpu.VMEM((MOE_RM, D_FF), BF16),
            pltpu.VMEM((d, MOE_TF), BF16),
            pltpu.VMEM((d, MOE_TF), BF16),
            pltpu.VMEM((D_FF, d), BF16),
            pltpu.VMEM((2, MOE_SUB, d), F32),
            pltpu.SemaphoreType.DMA((2,)),
        ],
    )
    return pl.pallas_call(
        _experts_body,
        grid_spec=grid_spec,
        out_shape=jax.ShapeDtypeStruct((n_slots, d), F32),
        compiler_params=pltpu.CompilerParams(dimension_semantics=("arbitrary",),
                                             vmem_limit_bytes=VMEM_LIMIT),
        name="experts",
    )(*tabs, xs, w_up, w_up, b_up, b_up, w_down, b_down)


def _combine_body(start_ref, eidx_ref, rank_ref, gate_ref, x1_ref, gf_ref, ys_ref, o_ref, buf_ref, sem):
    tm = x1_ref.shape[0]

    def issue(i, c):
        for k in range(TOP_K):
            a = i * TOP_K + k
            _row_copy(ys_ref, start_ref[eidx_ref[a]] + rank_ref[a], buf_ref.at[k], i, sem).start()
        return c

    lax.fori_loop(0, tm, issue, 0)

    def drain(i, c):
        for _ in range(WAIT_UNROLL):
            _row_copy(ys_ref, 0, buf_ref.at[0], 0, sem).wait()
        return c

    lax.fori_loop(0, tm * TOP_K // WAIT_UNROLL, drain, 0)
    gate = gate_ref[...]
    acc = x1_ref[...]
    for k in range(TOP_K):
        acc = acc + gate[:, k:k + 1] * buf_ref[k]
    o_ref[...] = _rms(acc, gf_ref[...])


def _combine(start, eidx_flat, rank_flat, gates, x1, gf, ys):
    t, d = x1.shape
    tm = min(TM_COMBINE, t)
    grid_spec = pltpu.PrefetchScalarGridSpec(
        num_scalar_prefetch=1,
        grid=(t // tm,),
        in_specs=[pl.BlockSpec((tm * TOP_K,), lambda i, *_: (i,), memory_space=pltpu.SMEM),
                  pl.BlockSpec((tm * TOP_K,), lambda i, *_: (i,), memory_space=pltpu.SMEM),
                  pl.BlockSpec((tm, TOP_K), lambda i, *_: (i, 0)),
                  pl.BlockSpec((tm, d), lambda i, *_: (i, 0)),
                  _const_spec(gf.shape),
                  pl.BlockSpec(memory_space=pl.ANY)],
        out_specs=pl.BlockSpec((tm, d), lambda i, *_: (i, 0)),
        scratch_shapes=[pltpu.VMEM((TOP_K, tm, d), F32), pltpu.SemaphoreType.DMA(())],
    )
    return pl.pallas_call(
        _combine_body,
        grid_spec=grid_spec,
        out_shape=jax.ShapeDtypeStruct((t, d), F32),
        compiler_params=pltpu.CompilerParams(dimension_semantics=("arbitrary",),
                                             vmem_limit_bytes=VMEM_LIMIT),
        name="combine",
    )(start, eidx_flat, rank_flat, gates, x1, gf, ys)


def _rope_tables(s):
    pos = np.arange(s, dtype=np.float32)
    inv_freq = np.power(np.float32(ROPE_THETA), -np.arange(0, B_ROPE_DIM, 2, dtype=np.float32) / B_ROPE_DIM)
    ang = (pos[:, None] * inv_freq[None, :]).astype(np.float32)
    cos, sin = np.cos(ang), np.sin(ang)
    pad = LANES - B_ROPE_DIM
    cos = np.concatenate([cos, cos, np.ones((s, pad), np.float32)], axis=1)
    sin = np.concatenate([sin, sin, np.zeros((s, pad), np.float32)], axis=1)
    return cos.astype(np.float32), sin.astype(np.float32)


def _window_bias():
    qi = np.arange(A_BLOCK)[:, None]
    kj = np.arange(3 * A_BLOCK)[None, :]
    dist = np.abs(qi + A_BLOCK - kj)
    slopes = np.power(np.float32(2.0), -8.0 * np.arange(1, A_HEADS + 1, dtype=np.float32) / A_HEADS)
    bias = -slopes[:, None, None] * dist.astype(np.float32)[None]
    bias = np.where((dist <= WINDOW)[None], bias, -np.inf)
    bias = bias.reshape(A_KV_HEADS, 2, 2, A_BLOCK, 3 * A_BLOCK).transpose(0, 2, 1, 3, 4)
    bias = bias.reshape(2 * A_KV_HEADS, 2 * A_BLOCK, 3 * A_BLOCK)
    no_prev = (kj < A_BLOCK)[None]
    no_next = (kj >= 2 * A_BLOCK)[None]
    ninf = -np.inf
    return np.stack([bias, np.where(no_prev, ninf, bias), np.where(no_next, ninf, bias),
                     np.where(no_prev | no_next, ninf, bias)]).astype(np.float32)


def _layer(x, attn_norm, w_in, a_sink, b_q_norm, b_w_uq, b_kv_norm, b_w_ukv, out_norm_a, out_norm_b, w_o,
           mlp_norm, w_router, b_router, w_up, b_up, w_down, b_down):
    b, s, d = x.shape
    t = b * s
    win = jnp.pad(w_in, ((0, 0), (0, IN_COLS_PAD - w_in.shape[1]))).astype(BF16)
    wq = b_w_uq.reshape(B_Q_RANK, B_HEADS, B_NOPE_DIM + B_ROPE_DIM)
    wq_pe = jnp.pad(wq[:, :, B_NOPE_DIM:], ((0, 0), (0, 0), (0, LANES - B_ROPE_DIM)))
    wuq = jnp.concatenate([wq[:, :, :B_NOPE_DIM].reshape(B_Q_RANK, -1), wq_pe.reshape(B_Q_RANK, -1)],
                          axis=1).astype(BF16)
    wkv = b_w_ukv.reshape(B_KV_RANK, B_HEADS, B_NOPE_DIM + B_V_DIM)
    wukv = jnp.concatenate([wkv[:, :, :B_NOPE_DIM].reshape(B_KV_RANK, -1),
                            wkv[:, :, B_NOPE_DIM:].reshape(B_KV_RANK, -1)], axis=1).astype(BF16)
    cos, sin = _rope_tables(s)

    qa, ka, va, qb, kb, vb = _prologue(x, attn_norm[None], win, b_q_norm[None], wuq, b_kv_norm[None], wukv,
                                       cos, sin)
    ya = _window_attn(qa, ka, va, a_sink, _window_bias())
    yb = _mla_attn(qb, kb, vb)
    wr_hi = w_router.astype(BF16)
    wr_lo = (w_router - wr_hi.astype(F32)).astype(BF16)
    x1, xm, eidx, gates, rank, counts = _out_router(
        ya.reshape(t, A_WIDTH), yb.reshape(t, B_WIDTH), x.reshape(t, d), out_norm_a[None], out_norm_b[None],
        w_o.astype(BF16), mlp_norm[None], jnp.concatenate([wr_hi, wr_lo], axis=1), b_router[None])

    nblk = (t * TOP_K + N_EXPERTS * (MOE_BM - 1) + MOE_BM - 1) // MOE_BM
    tabs, start, pad_start, pad_len, nact = _expert_schedule(counts.reshape(N_EXPERTS), nblk, t * TOP_K)
    eidx = eidx.reshape(t * TOP_K)
    rank = rank.reshape(t * TOP_K)
    xs = _dispatch(start, pad_start, pad_len, nact, eidx, rank, xm, nblk * MOE_BM)
    ys = _experts(tabs, xs, w_up, b_up[:, None, :], w_down, b_down[:, None, :])
    return x1, start, eidx, rank, gates, ys


def kernel(x, attn_norm, w_in, a_sink, b_q_norm, b_w_uq, b_kv_norm, b_w_ukv, out_norm_a, out_norm_b, w_o,
           mlp_norm, w_router, b_router, w_up, b_up, w_down, b_down, final_norm):
    b, s, d = x.shape
    assert d == D_MODEL and s % min(TQ_MLA, s) == 0 and s % min(CK_MLA, s) == 0 and s % TM_PRO == 0 and attn_norm.shape[0] == 1
    x1, start, eidx, rank, gates, ys = _layer(
        x, attn_norm[0], w_in[0], a_sink[0], b_q_norm[0], b_w_uq[0], b_kv_norm[0], b_w_ukv[0], out_norm_a[0],
        out_norm_b[0], w_o[0], mlp_norm[0], w_router[0], b_router[0], w_up[0], b_up[0], w_down[0], b_down[0])
    out = _combine(start, eidx, rank, gates, x1, final_norm[None], ys)
    return out.reshape(b, s, d)
```

```python
import functools

import jax
import numpy as np
import jax.numpy as jnp
from jax import lax
from jax.experimental import pallas as pl
from jax.experimental.pallas import tpu as pltpu

D_MODEL = 2048
A_HEADS, A_KV_HEADS, A_HEAD_DIM = 16, 4, 64
A_GROUP = A_HEADS // A_KV_HEADS
WINDOW = 128
A_BLOCK = 128
B_HEADS, B_Q_RANK, B_KV_RANK = 8, 512, 256
B_NOPE_DIM, B_ROPE_DIM, B_V_DIM = 128, 64, 128
ROPE_THETA = 10000.0
A_WIDTH = A_HEADS * A_HEAD_DIM
B_WIDTH = B_HEADS * B_V_DIM
A_KV_COLS = A_KV_HEADS * A_HEAD_DIM
N_EXPERTS, TOP_K, D_FF = 32, 4, 2048
SWIGLU_ALPHA, SWIGLU_LIMIT = 1.702, 7.0
EPS = 1e-5

LANES = 128
SUBLANES = 8
B_QK_PAD = 2 * LANES
IN_COLS_PAD = A_WIDTH + 2 * A_KV_COLS + B_Q_RANK + B_KV_RANK + LANES
VMEM_LIMIT = 56 * 1024 * 1024
ROW_TILES = D_MODEL // LANES
ROW_PITCH = ROW_TILES + 1

TM_PRO = 256
TQ_MLA = 1024
CK_MLA = 1024
TM_OUT = 512
TM_DISPATCH = 512
TM_COMBINE = 128
MOE_BM = 512
MOE_SUB = 256
MOE_RM = 2048
MOE_TF = 256
WAIT_UNROLL = 16

BF16 = jnp.bfloat16
F32 = jnp.float32


def _rms(x, g):
    return x * lax.rsqrt(jnp.mean(x * x, axis=-1, keepdims=True) + EPS) * g


def _const_spec(shape):
    nd = len(shape)
    return pl.BlockSpec(shape, lambda *_: (0,) * nd, pipeline_mode=pl.Buffered(1))


def _store_token_rows(ref, first, val):
    n = val.shape[0]
    for j in range(ROW_TILES):
        ref[pl.ds(first * ROW_PITCH + j, n, stride=ROW_PITCH), :] = val[:, j * LANES:(j + 1) * LANES]
    ref[pl.ds(first * ROW_PITCH + ROW_TILES, n, stride=ROW_PITCH), :] = jnp.zeros((n, LANES), val.dtype)


def _load_token_rows(ref, n):
    return [ref[pl.ds(j, n, stride=ROW_PITCH), :] for j in range(ROW_TILES)]


def _rope(x, cos, sin):
    lane = lax.broadcasted_iota(jnp.int32, x.shape, 1)
    up = pltpu.roll(x, LANES - B_ROPE_DIM // 2, 1)
    dn = pltpu.roll(x, B_ROPE_DIM // 2, 1)
    sw = jnp.where(lane < B_ROPE_DIM // 2, -up, jnp.where(lane < B_ROPE_DIM, dn, 0.0))
    return x * cos + sw * sin


def _prologue_body(x_ref, g_ref, win_ref, qn_ref, wuq_ref, kvn_ref, wukv_ref, cos_ref, sin_ref,
                   qa_ref, ka_ref, va_ref, qb_ref, kb_ref, vb_ref):
    x = x_ref[0]
    xn = _rms(x, g_ref[...]).astype(BF16)
    proj = jnp.dot(xn, win_ref[...], preferred_element_type=F32)
    c0 = A_WIDTH
    c1 = c0 + A_KV_COLS
    c2 = c1 + A_KV_COLS
    c3 = c2 + B_Q_RANK
    c4 = c3 + B_KV_RANK
    qa_ref[0] = (proj[:, :A_WIDTH] * A_HEAD_DIM ** -0.5).astype(BF16)
    low = lax.broadcasted_iota(jnp.int32, (proj.shape[0], LANES), 1) < A_HEAD_DIM
    for src, dst in ((c0, ka_ref), (c1, va_ref)):
        for pair in range(A_KV_HEADS // 2):
            two = proj[:, src + pair * LANES:src + (pair + 1) * LANES]
            swapped = pltpu.roll(two, A_HEAD_DIM, 1)
            dst[0, 4 * pair + 0] = jnp.where(low, two, 0.0).astype(BF16)
            dst[0, 4 * pair + 1] = jnp.where(low, 0.0, swapped).astype(BF16)
            dst[0, 4 * pair + 2] = jnp.where(low, swapped, 0.0).astype(BF16)
            dst[0, 4 * pair + 3] = jnp.where(low, 0.0, two).astype(BF16)
    cq = _rms(proj[:, c2:c3], qn_ref[...]).astype(BF16)
    q = jnp.dot(cq, wuq_ref[...], preferred_element_type=F32)
    ckv = _rms(proj[:, c3:c4], kvn_ref[...]).astype(BF16)
    kv = jnp.dot(ckv, wukv_ref[...], preferred_element_type=F32)
    cos = cos_ref[...]
    sin = sin_ref[...]
    kpe = _rope(proj[:, c4:c4 + LANES], cos, sin).astype(BF16)
    b_scale = (B_NOPE_DIM + B_ROPE_DIM) ** -0.5
    hw = B_HEADS * LANES
    for h in range(B_HEADS):
        sl = slice(h * LANES, (h + 1) * LANES)
        qb_ref[0, h, :, :LANES] = (q[:, sl] * b_scale).astype(BF16)
        qpe = _rope(q[:, hw + h * LANES:hw + (h + 1) * LANES], cos, sin)
        qb_ref[0, h, :, LANES:] = (qpe * b_scale).astype(BF16)
        kb_ref[0, h, :, :LANES] = kv[:, sl].astype(BF16)
        kb_ref[0, h, :, LANES:] = kpe
        vb_ref[0, h] = kv[:, hw + h * LANES:hw + (h + 1) * LANES].astype(BF16)


def _prologue(x, g, win, qn, wuq, kvn, wukv, cos, sin):
    b, s, d = x.shape
    tm = TM_PRO
    grid = (b, s // tm)
    sds = jax.ShapeDtypeStruct
    out_shape = (
        sds((b, s, A_WIDTH), BF16),
        sds((b, 2 * A_KV_HEADS, s, LANES), BF16),
        sds((b, 2 * A_KV_HEADS, s, LANES), BF16),
        sds((b, B_HEADS, s, B_QK_PAD), BF16),
        sds((b, B_HEADS, s, B_QK_PAD), BF16),
        sds((b, B_HEADS, s, B_V_DIM), BF16),
    )

    def hspec(nh, w):
        return pl.BlockSpec((1, nh, tm, w), lambda bi, i: (bi, 0, i, 0))

    return pl.pallas_call(
        _prologue_body,
        grid=grid,
        in_specs=[
            pl.BlockSpec((1, tm, d), lambda bi, i: (bi, i, 0)),
            _const_spec(g.shape), _const_spec(win.shape), _const_spec(qn.shape), _const_spec(wuq.shape),
            _const_spec(kvn.shape), _const_spec(wukv.shape),
            pl.BlockSpec((tm, LANES), lambda bi, i: (i, 0)),
            pl.BlockSpec((tm, LANES), lambda bi, i: (i, 0)),
        ],
        out_specs=(pl.BlockSpec((1, tm, A_WIDTH), lambda bi, i: (bi, i, 0)),
                   hspec(2 * A_KV_HEADS, LANES), hspec(2 * A_KV_HEADS, LANES),
                   hspec(B_HEADS, B_QK_PAD), hspec(B_HEADS, B_QK_PAD), hspec(B_HEADS, B_V_DIM)),
        out_shape=out_shape,
        compiler_params=pltpu.CompilerParams(dimension_semantics=("parallel", "parallel"),
                                             vmem_limit_bytes=VMEM_LIMIT),
        name="prologue",
    )(x, g, win, qn, wuq, kvn, wukv, cos, sin)


def _window_body(sink_ref, q_ref, kp_ref, kc_ref, kn_ref, vp_ref, vc_ref, vn_ref, bias_ref, o_ref):
    two = 2 * A_BLOCK
    low = lax.broadcasted_iota(jnp.int32, (two, LANES), 1) < A_HEAD_DIM
    for hk in range(A_KV_HEADS):
        q2 = jnp.concatenate([q_ref[0, :, (2 * hk) * LANES:(2 * hk + 1) * LANES],
                              q_ref[0, :, (2 * hk + 1) * LANES:(2 * hk + 2) * LANES]], axis=0)
        acc = jnp.zeros((two, LANES), F32)
        inv = []
        for half in range(2):
            z = 2 * hk + half
            kband = jnp.concatenate([kp_ref[0, z], kc_ref[0, z], kn_ref[0, z]], axis=0)
            vband = jnp.concatenate([vp_ref[0, z], vc_ref[0, z], vn_ref[0, z]], axis=0)
            s = lax.dot_general(q2, kband, (((1,), (1,)), ((), ())), preferred_element_type=F32)
            s = s + bias_ref[0, z]
            sink = jnp.concatenate(
                [jnp.full((A_BLOCK, 1), sink_ref[hk * A_GROUP + 2 * jj + half], F32) for jj in range(2)], axis=0)
            m = jnp.maximum(jnp.max(s, axis=-1, keepdims=True), sink)
            p = jnp.exp(s - m)
            den = jnp.sum(p, axis=-1, keepdims=True) + jnp.exp(sink - m)
            inv.append(1.0 / den)
            acc = acc + jnp.dot(p.astype(BF16), vband, preferred_element_type=F32)
        o = (acc * jnp.where(low, inv[0], inv[1])).astype(BF16)
        o_ref[0, :, (2 * hk) * LANES:(2 * hk + 1) * LANES] = o[:A_BLOCK]
        o_ref[0, :, (2 * hk + 1) * LANES:(2 * hk + 2) * LANES] = o[A_BLOCK:]


def _window_attn(qa, ka, va, sink, bias):
    b, s, _ = qa.shape
    nb = s // A_BLOCK
    kv_blk = (1, 2 * A_KV_HEADS, A_BLOCK, LANES)
    prev = pl.BlockSpec(kv_blk, lambda bi, n: (bi, 0, jnp.maximum(n - 1, 0), 0))
    cur = pl.BlockSpec(kv_blk, lambda bi, n: (bi, 0, n, 0))
    nxt = pl.BlockSpec(kv_blk, lambda bi, n: (bi, 0, jnp.minimum(n + 1, nb - 1), 0))

    def edge(bi, n):
        return ((n == 0).astype(jnp.int32) + 2 * (n == nb - 1).astype(jnp.int32), 0, 0, 0)

    return pl.pallas_call(
        _window_body,
        grid=(b, nb),
        in_specs=[
            pl.BlockSpec(memory_space=pltpu.SMEM),
            pl.BlockSpec((1, A_BLOCK, A_WIDTH), lambda bi, n: (bi, n, 0)),
            prev, cur, nxt, prev, cur, nxt,
            pl.BlockSpec((1,) + bias.shape[1:], edge),
        ],
        out_specs=pl.BlockSpec((1, A_BLOCK, A_WIDTH), lambda bi, n: (bi, n, 0)),
        out_shape=jax.ShapeDtypeStruct((b, s, A_WIDTH), BF16),
        compiler_params=pltpu.CompilerParams(dimension_semantics=("parallel", "parallel"),
                                             vmem_limit_bytes=VMEM_LIMIT),
        name="window_attn",
    )(sink, qa, ka, ka, ka, va, va, va, bias)


def _mla_body(q_ref, k_ref, v_ref, o_ref):
    q = q_ref[0, 0]
    tq = q.shape[0]
    s_len = k_ref.shape[2]
    ck = min(CK_MLA, s_len)
    m = jnp.full((tq, 1), -jnp.inf, F32)
    l = jnp.zeros((tq, 1), F32)
    acc = jnp.zeros((tq, B_V_DIM), F32)
    for c in range(s_len // ck):
        k_c = k_ref[0, 0, c * ck:(c + 1) * ck, :]
        v_c = v_ref[0, 0, c * ck:(c + 1) * ck, :]
        s = lax.dot_general(q, k_c, (((1,), (1,)), ((), ())), preferred_element_type=F32)
        m_new = jnp.maximum(m, jnp.max(s, axis=-1, keepdims=True))
        alpha = jnp.exp(m - m_new)
        p = jnp.exp(s - m_new)
        l = alpha * l + jnp.sum(p, axis=-1, keepdims=True)
        acc = alpha * acc + jnp.dot(p.astype(BF16), v_c, preferred_element_type=F32)
        m = m_new
    o_ref[0] = (acc / l).astype(BF16)


def _mla_attn(qb, kb, vb):
    b, nh, s, _ = qb.shape
    tq = min(TQ_MLA, s)
    return pl.pallas_call(
        _mla_body,
        grid=(b, nh, s // tq),
        in_specs=[
            pl.BlockSpec((1, 1, tq, B_QK_PAD), lambda bi, h, i: (bi, h, i, 0)),
            pl.BlockSpec((1, 1, s, B_QK_PAD), lambda bi, h, i: (bi, h, 0, 0)),
            pl.BlockSpec((1, 1, s, B_V_DIM), lambda bi, h, i: (bi, h, 0, 0)),
        ],
        out_specs=pl.BlockSpec((1, tq, B_V_DIM), lambda bi, h, i: (bi, i, h)),
        out_shape=jax.ShapeDtypeStruct((b, s, B_WIDTH), BF16),
        compiler_params=pltpu.CompilerParams(dimension_semantics=("parallel", "parallel", "parallel"),
                                             vmem_limit_bytes=VMEM_LIMIT),
        name="mla_attn",
    )(qb, kb, vb)


def _out_router_body(ya_ref, yb_ref, x_ref, ga_ref, gb_ref, wo_ref, gm_ref, wr_ref, br_ref,
                     x1_ref, xm_ref, eidx_ref, gate_ref, rank_ref, cnt_ref, run_ref):
    i = pl.program_id(0)

    @pl.when(i == 0)
    def _():
        run_ref[...] = jnp.zeros_like(run_ref)

    na = _rms(ya_ref[...].astype(F32), ga_ref[...]).astype(BF16)
    nb = _rms(yb_ref[...].astype(F32), gb_ref[...]).astype(BF16)
    att = jnp.dot(na, wo_ref[:A_WIDTH, :], preferred_element_type=F32)
    att = att + jnp.dot(nb, wo_ref[A_WIDTH:, :], preferred_element_type=F32)
    x1 = x_ref[...] + att
    x1_ref[...] = x1
    hn = _rms(x1, gm_ref[...])
    _store_token_rows(xm_ref, 0, hn)
    tm = hn.shape[0]
    hi = hn.astype(BF16)
    lo = (hn - hi.astype(F32)).astype(BF16)
    prod = jnp.dot(jnp.concatenate([hi, lo], axis=0), wr_ref[...], preferred_element_type=F32)
    logits = (prod[:tm, :N_EXPERTS] + prod[:tm, N_EXPERTS:] + prod[tm:, :N_EXPERTS] + prod[tm:, N_EXPERTS:]
              + br_ref[...])
    lane = lax.broadcasted_iota(jnp.int32, (tm, N_EXPERTS), 1)
    work = logits
    sel = jnp.zeros((tm, N_EXPERTS), F32)
    hots, vals, idxs = [], [], []
    for _k in range(TOP_K):
        mx = jnp.max(work, axis=-1, keepdims=True)
        idx = jnp.min(jnp.where(work == mx, lane, N_EXPERTS), axis=-1, keepdims=True)
        hot = lane == idx
        hots.append(hot)
        vals.append(mx)
        idxs.append(idx)
        sel = sel + hot.astype(F32)
        work = jnp.where(hot, -jnp.inf, work)
    exps = [jnp.exp(v - vals[0]) for v in vals]
    den = exps[0] + exps[1] + exps[2] + exps[3]
    r_i = lax.broadcasted_iota(jnp.int32, (tm, tm), 0)
    c_i = lax.broadcasted_iota(jnp.int32, (tm, tm), 1)
    tri = (c_i < r_i).astype(BF16)
    before = jnp.dot(tri, sel.astype(BF16), preferred_element_type=F32) + run_ref[...]
    lane4 = lax.broadcasted_iota(jnp.int32, (tm, TOP_K), 1)
    eidx = jnp.zeros((tm, TOP_K), jnp.int32)
    gate = jnp.zeros((tm, TOP_K), F32)
    rank = jnp.zeros((tm, TOP_K), jnp.int32)
    for k in range(TOP_K):
        rk = jnp.sum(jnp.where(hots[k], before, 0.0), axis=-1, keepdims=True).astype(jnp.int32)
        eidx = jnp.where(lane4 == k, idxs[k], eidx)
        gate = jnp.where(lane4 == k, exps[k] / den, gate)
        rank = jnp.where(lane4 == k, rk, rank)
    eidx_ref[...] = eidx
    gate_ref[...] = gate
    rank_ref[...] = rank
    run = run_ref[...] + jnp.sum(sel, axis=0, keepdims=True)
    run_ref[...] = run
    cnt_ref[...] = run.astype(jnp.int32)


def _out_router(ya, yb, x2d, ga, gb, wo, gm, wr, br):
    t, d = x2d.shape
    tm = min(TM_OUT, t)
    sds = jax.ShapeDtypeStruct

    def row(w):
        return pl.BlockSpec((tm, w), lambda i: (i, 0))

    return pl.pallas_call(
        _out_router_body,
        grid=(t // tm,),
        in_specs=[row(A_WIDTH), row(B_WIDTH), row(d), _const_spec(ga.shape), _const_spec(gb.shape),
                  _const_spec(wo.shape), _const_spec(gm.shape), _const_spec(wr.shape), _const_spec(br.shape)],
        out_specs=(row(d), pl.BlockSpec((tm * ROW_PITCH, LANES), lambda i: (i, 0)),
                   row(TOP_K), row(TOP_K), row(TOP_K),
                   pl.BlockSpec((1, N_EXPERTS), lambda i: (0, 0))),
        out_shape=(sds((t, d), F32), sds((t * ROW_PITCH, LANES), F32), sds((t, TOP_K), jnp.int32),
                   sds((t, TOP_K), F32),
                   sds((t, TOP_K), jnp.int32), sds((1, N_EXPERTS), jnp.int32)),
        scratch_shapes=[pltpu.VMEM((1, N_EXPERTS), F32)],
        compiler_params=pltpu.CompilerParams(dimension_semantics=("arbitrary",),
                                             vmem_limit_bytes=VMEM_LIMIT),
        name="out_router",
    )(ya, yb, x2d, ga, gb, wo, gm, wr, br)


def _token_copy(src_ref, src_tok, dst_ref, dst_tok, sem, ntok=1):
    rows = ntok * ROW_PITCH
    return pltpu.make_async_copy(src_ref.at[pl.ds(src_tok * ROW_PITCH, rows)],
                                 dst_ref.at[pl.ds(dst_tok * ROW_PITCH, rows)], sem)


def _zero_fill_pads(pad_start_ref, pad_len_ref, nact_ref, xs_ref, zero_ref, zsem):
    bm = zero_ref.shape[0] // ROW_PITCH
    nblk = xs_ref.shape[0] // (bm * ROW_PITCH)
    zero_ref[...] = jnp.zeros_like(zero_ref)

    def per_expert(e, c):
        off = pad_start_ref[e]
        n = pad_len_ref[e]
        p = bm // 2
        while p >= 1:
            take = (n & p) != 0

            @pl.when(take)
            def _(p=p, off=off):
                cp = _token_copy(zero_ref, 0, xs_ref, off, zsem, p)
                cp.start()
                cp.wait()

            off = off + jnp.where(take, p, 0)
            p //= 2
        return c

    lax.fori_loop(0, N_EXPERTS, per_expert, 0)

    def tail(j, c):
        cp = _token_copy(zero_ref, 0, xs_ref, j * bm, zsem, bm)
        cp.start()
        cp.wait()
        return c

    lax.fori_loop(nact_ref[0], nblk, tail, 0)


def _dispatch_body(start_ref, pad_start_ref, pad_len_ref, nact_ref, eidx_ref, rank_ref, xm_ref, xs_ref,
                   zero_ref, sem, zsem):
    tm = eidx_ref.shape[0] // TOP_K

    @pl.when(pl.program_id(0) == 0)
    def _():
        _zero_fill_pads(pad_start_ref, pad_len_ref, nact_ref, xs_ref, zero_ref, zsem)

    def issue(i, c):
        for k in range(TOP_K):
            a = i * TOP_K + k
            _token_copy(xm_ref, i, xs_ref, start_ref[eidx_ref[a]] + rank_ref[a], sem).start()
        return c

    lax.fori_loop(0, tm, issue, 0)

    def drain(i, c):
        for _ in range(WAIT_UNROLL):
            _token_copy(xm_ref, 0, xs_ref, 0, sem).wait()
        return c

    lax.fori_loop(0, tm * TOP_K // WAIT_UNROLL, drain, 0)


def _dispatch(start, pad_start, pad_len, nact, eidx_flat, rank_flat, xm, n_slots):
    t = xm.shape[0] // ROW_PITCH
    tm = min(TM_DISPATCH, t)
    grid_spec = pltpu.PrefetchScalarGridSpec(
        num_scalar_prefetch=4,
        grid=(t // tm,),
        in_specs=[pl.BlockSpec((tm * TOP_K,), lambda i, *_: (i,), memory_space=pltpu.SMEM),
                  pl.BlockSpec((tm * TOP_K,), lambda i, *_: (i,), memory_space=pltpu.SMEM),
                  pl.BlockSpec((tm * ROW_PITCH, LANES), lambda i, *_: (i, 0))],
        out_specs=pl.BlockSpec(memory_space=pl.ANY),
        scratch_shapes=[pltpu.VMEM((MOE_BM * ROW_PITCH, LANES), F32), pltpu.SemaphoreType.DMA(()),
                        pltpu.SemaphoreType.DMA(())],
    )
    return pl.pallas_call(
        _dispatch_body,
        grid_spec=grid_spec,
        out_shape=jax.ShapeDtypeStruct((n_slots * ROW_PITCH, LANES), F32),
        compiler_params=pltpu.CompilerParams(dimension_semantics=("arbitrary",), has_side_effects=True,
                                             vmem_limit_bytes=VMEM_LIMIT),
        name="dispatch",
    )(start, pad_start, pad_len, nact, eidx_flat, rank_flat, xm)


STEP_UP, STEP_DOWN, STEP_TAIL, STEP_IDLE = 0, 1, 2, 3
NF = D_FF // MOE_TF
PASS_BLOCKS = MOE_RM // MOE_BM


def _swiglu(hg, hu):
    gate = jnp.minimum(hg, SWIGLU_LIMIT)
    up = jnp.clip(hu, -SWIGLU_LIMIT, SWIGLU_LIMIT)
    glu = gate / (1.0 + jnp.exp(-SWIGLU_ALPHA * gate))
    return (up + 1.0) * glu


def _experts_body(e_ref, f_ref, kind_ref, blk_ref, blk0_ref, nsub_ref,
                  xs_ref, wg_ref, wu_ref, bg_ref, bu_ref, wd_ref, bd_ref,
                  y_ref, xb_ref, h_ref, wgb_ref, wub_ref, wdb_ref, stg_ref, sem):
    t = pl.program_id(0)
    kind = kind_ref[t]
    f = f_ref[t]
    nsub = nsub_ref[t]

    def x_copy(j, slot):
        rows = MOE_SUB * ROW_PITCH
        first = pl.multiple_of((blk0_ref[t] * MOE_BM + j * MOE_SUB) * ROW_PITCH, SUBLANES)
        return pltpu.make_async_copy(xs_ref.at[pl.ds(first, rows)], stg_ref.at[slot], sem.at[slot])

    @pl.when(kind == STEP_UP)
    def _up():
        @pl.when(f == 0)
        def _load_rows():
            x_copy(0, 0).start()

            def body(j, c):
                slot = j % 2

                @pl.when(j + 1 < nsub)
                def _():
                    x_copy(j + 1, 1 - slot).start()

                x_copy(j, slot).wait()
                rows = pl.ds(pl.multiple_of(j * MOE_SUB, MOE_SUB), MOE_SUB)
                for jt, tile in enumerate(_load_token_rows(stg_ref.at[slot], MOE_SUB)):
                    xb_ref[rows, jt * LANES:(jt + 1) * LANES] = tile.astype(BF16)
                return c

            lax.fori_loop(0, nsub, body, 0)

        wgb_ref[...] = wg_ref[0].astype(BF16)
        wub_ref[...] = wu_ref[0].astype(BF16)
        wdb_ref[pl.ds(pl.multiple_of(f * MOE_TF, MOE_TF), MOE_TF), :] = wd_ref[0].astype(BF16)

        def up_rows(row0, nrows):
            x = xb_ref[pl.ds(row0, nrows), :]
            hg = jnp.dot(x, wgb_ref[...], preferred_element_type=F32) + bg_ref[0]
            hu = jnp.dot(x, wub_ref[...], preferred_element_type=F32) + bu_ref[0]
            act = _swiglu(hg, hu).astype(BF16)
            for fs in range(NF):
                @pl.when(f == fs)
                def _(fs=fs):
                    h_ref[pl.ds(row0, nrows), fs * MOE_TF:(fs + 1) * MOE_TF] = act

        npair = nsub // 2

        def pair(i, c):
            up_rows(pl.multiple_of(i * MOE_BM, MOE_BM), MOE_BM)
            return c

        lax.fori_loop(0, npair, pair, 0)

        @pl.when(nsub % 2 == 1)
        def _():
            up_rows(pl.multiple_of(npair * MOE_BM, MOE_BM), MOE_SUB)

    @pl.when(kind == STEP_DOWN)
    def _down():
        row0 = pl.multiple_of((blk_ref[t] - blk0_ref[t]) * MOE_BM, MOE_BM)

        def down_rows(nrows):
            hrows = h_ref[pl.ds(row0, nrows), :]
            return jnp.dot(hrows, wdb_ref[...], preferred_element_type=F32) + bd_ref[0]

        @pl.when(nsub == 2)
        def _():
            _store_token_rows(y_ref, 0, down_rows(MOE_BM))

        @pl.when(nsub == 1)
        def _():
            _store_token_rows(y_ref, 0, down_rows(MOE_SUB))
            y_ref[MOE_SUB * ROW_PITCH:, :] = jnp.zeros(((MOE_BM - MOE_SUB) * ROW_PITCH, LANES), F32)

    @pl.when(kind == STEP_TAIL)
    def _tail():
        y_ref[...] = jnp.zeros_like(y_ref)


def _schedule_body(n_blocks, cnt_ref, e_ref, f_ref, kind_ref, blk_ref, blk0_ref, nsub_ref,
                   start_ref, pad_start_ref, pad_len_ref, nact_ref):
    n_steps = e_ref.shape[0]
    sub_per_blk = MOE_BM // MOE_SUB
    zero = jnp.int32(0)

    def put(idx, e, f, kind, blk, blk0, nsub):
        e_ref[idx] = e
        f_ref[idx] = f
        kind_ref[idx] = kind
        blk_ref[idx] = blk
        blk0_ref[idx] = blk0
        nsub_ref[idx] = nsub

    def per_expert(e, carry):
        t, blk, last_e = carry
        n = cnt_ref[e]
        nblk = (n + (MOE_BM - 1)) // MOE_BM
        start_ref[e] = blk * MOE_BM
        pad_start_ref[e] = blk * MOE_BM + n
        pad_len_ref[e] = nblk * MOE_BM - n

        def per_pass(p, t2):
            b0 = blk + p * PASS_BLOCKS
            nb = jnp.minimum(nblk - p * PASS_BLOCKS, PASS_BLOCKS)
            nsub = (jnp.minimum(n - p * MOE_RM, MOE_RM) + (MOE_SUB - 1)) // MOE_SUB
            for f in range(NF):
                put(t2 + f, e, f, STEP_UP, b0, b0, nsub)

            def per_blk(j, c):
                put(t2 + NF + j, e, NF - 1, STEP_DOWN, b0 + j, b0,
                    jnp.clip(nsub - j * sub_per_blk, 1, sub_per_blk))
                return c

            lax.fori_loop(0, nb, per_blk, 0)
            return t2 + NF + nb

        t = lax.fori_loop(0, (nblk + (PASS_BLOCKS - 1)) // PASS_BLOCKS, per_pass, t)
        return t, blk + nblk, jnp.where(n > 0, e, last_e)

    t, nact, last_e = lax.fori_loop(0, N_EXPERTS, per_expert, (zero, zero, zero))
    nact_ref[0] = nact

    def spare(i, c):
        is_tail = i < n_blocks - nact
        blk = jnp.where(is_tail, nact + i, n_blocks - 1)
        put(t + i, last_e, NF - 1, jnp.where(is_tail, STEP_TAIL, STEP_IDLE), blk, blk, 1)
        return c

    lax.fori_loop(0, n_steps - t, spare, 0)


def _expert_schedule(counts, n_blocks, n_assign):
    n_pass_max = N_EXPERTS + n_assign // MOE_RM
    n_steps = n_pass_max * NF + n_blocks
    smem = pl.BlockSpec(memory_space=pltpu.SMEM)
    i32 = jnp.int32
    out_shape = tuple(jax.ShapeDtypeStruct((n,), i32) for n in (n_steps,) * 6 + (N_EXPERTS,) * 3 + (1,))
    outs = pl.pallas_call(
        functools.partial(_schedule_body, n_blocks),
        in_specs=[smem],
        out_specs=tuple(smem for _ in out_shape),
        out_shape=out_shape,
        name="schedule",
    )(counts)
    return outs[:6], outs[6], outs[7], outs[8], outs[9]


def _experts(tabs, xs, w_up, b_up, w_down, b_down):
    d = D_MODEL
    n_steps = tabs[0].shape[0]
    grid_spec = pltpu.PrefetchScalarGridSpec(
        num_scalar_prefetch=len(tabs),
        grid=(n_steps,),
        in_specs=[
            pl.BlockSpec(memory_space=pl.ANY),
            pl.BlockSpec((1, d, MOE_TF), lambda t, e, f, *_: (e[t], 0, f[t])),
            pl.BlockSpec((1, d, MOE_TF), lambda t, e, f, *_: (e[t], 0, NF + f[t])),
            pl.BlockSpec((1, 1, MOE_TF), lambda t, e, f, *_: (e[t], 0, f[t])),
            pl.BlockSpec((1, 1, MOE_TF), lambda t, e, f, *_: (e[t], 0, NF + f[t])),
            pl.BlockSpec((1, MOE_TF, d), lambda t, e, f, *_: (e[t], f[t], 0)),
            pl.BlockSpec((1, 1, d), lambda t, e, f, *_: (e[t], 0, 0)),
        ],
        out_specs=pl.BlockSpec((MOE_BM * ROW_PITCH, LANES), lambda t, e, f, kind, blk, *_: (blk[t], 0)),
        scratch_shapes=[
            pltpu.VMEM((MOE_RM, d), BF16),
            pltpu.VMEM((MOE_RM, D_FF), BF16),
            pltpu.VMEM((d, MOE_TF), BF16),
            pltpu.VMEM((d, MOE_TF), BF16),
            pltpu.VMEM((D_FF, d), BF16),
            pltpu.VMEM((2, MOE_SUB * ROW_PITCH, LANES), F32),
            pltpu.SemaphoreType.DMA((2,)),
        ],
    )
    return pl.pallas_call(
        _experts_body,
        grid_spec=grid_spec,
        out_shape=jax.ShapeDtypeStruct(xs.shape, F32),
        compiler_params=pltpu.CompilerParams(dimension_semantics=("arbitrary",),
                                             vmem_limit_bytes=VMEM_LIMIT),
        name="experts",
    )(*tabs, xs, w_up, w_up, b_up, b_up, w_down, b_down)


def _combine_body(start_ref, eidx_ref, rank_ref, gate_ref, x1_ref, gf_ref, ys_ref, o_ref, buf_ref, sem):
    tm = x1_ref.shape[0]

    def issue(i, c):
        for k in range(TOP_K):
            a = i * TOP_K + k
            _token_copy(ys_ref, start_ref[eidx_ref[a]] + rank_ref[a], buf_ref.at[k], i, sem).start()
        return c

    lax.fori_loop(0, tm, issue, 0)

    def drain(i, c):
        for _ in range(WAIT_UNROLL):
            _token_copy(ys_ref, 0, buf_ref.at[0], 0, sem).wait()
        return c

    lax.fori_loop(0, tm * TOP_K // WAIT_UNROLL, drain, 0)
    gate = gate_ref[...]
    tiles = [x1_ref[:, j * LANES:(j + 1) * LANES] for j in range(ROW_TILES)]
    for k in range(TOP_K):
        g = gate[:, k:k + 1]
        for j, tile in enumerate(_load_token_rows(buf_ref.at[k], tm)):
            tiles[j] = tiles[j] + g * tile
    o_ref[...] = _rms(jnp.concatenate(tiles, axis=1), gf_ref[...])


def _combine(start, eidx_flat, rank_flat, gates, x1, gf, ys):
    t, d = x1.shape
    tm = min(TM_COMBINE, t)
    grid_spec = pltpu.PrefetchScalarGridSpec(
        num_scalar_prefetch=1,
        grid=(t // tm,),
        in_specs=[pl.BlockSpec((tm * TOP_K,), lambda i, *_: (i,), memory_space=pltpu.SMEM),
                  pl.BlockSpec((tm * TOP_K,), lambda i, *_: (i,), memory_space=pltpu.SMEM),
                  pl.BlockSpec((tm, TOP_K), lambda i, *_: (i, 0)),
                  pl.BlockSpec((tm, d), lambda i, *_: (i, 0)),
                  _const_spec(gf.shape),
                  pl.BlockSpec(memory_space=pl.ANY)],
        out_specs=pl.BlockSpec((tm, d), lambda i, *_: (i, 0)),
        scratch_shapes=[pltpu.VMEM((TOP_K, tm * ROW_PITCH, LANES), F32), pltpu.SemaphoreType.DMA(())],
    )
    return pl.pallas_call(
        _combine_body,
        grid_spec=grid_spec,
        out_shape=jax.ShapeDtypeStruct((t, d), F32),
        compiler_params=pltpu.CompilerParams(dimension_semantics=("arbitrary",),
                                             vmem_limit_bytes=VMEM_LIMIT),
        name="combine",
    )(start, eidx_flat, rank_flat, gates, x1, gf, ys)


def _rope_tables(s):
    pos = np.arange(s, dtype=np.float32)
    inv_freq = np.power(np.float32(ROPE_THETA), -np.arange(0, B_ROPE_DIM, 2, dtype=np.float32) / B_ROPE_DIM)
    ang = (pos[:, None] * inv_freq[None, :]).astype(np.float32)
    cos, sin = np.cos(ang), np.sin(ang)
    pad = LANES - B_ROPE_DIM
    cos = np.concatenate([cos, cos, np.ones((s, pad), np.float32)], axis=1)
    sin = np.concatenate([sin, sin, np.zeros((s, pad), np.float32)], axis=1)
    return cos.astype(np.float32), sin.astype(np.float32)


def _window_bias():
    qi = np.arange(A_BLOCK)[:, None]
    kj = np.arange(3 * A_BLOCK)[None, :]
    dist = np.abs(qi + A_BLOCK - kj)
    slopes = np.power(np.float32(2.0), -8.0 * np.arange(1, A_HEADS + 1, dtype=np.float32) / A_HEADS)
    bias = -slopes[:, None, None] * dist.astype(np.float32)[None]
    bias = np.where((dist <= WINDOW)[None], bias, -np.inf)
    bias = bias.reshape(A_KV_HEADS, 2, 2, A_BLOCK, 3 * A_BLOCK).transpose(0, 2, 1, 3, 4)
    bias = bias.reshape(2 * A_KV_HEADS, 2 * A_BLOCK, 3 * A_BLOCK)
    no_prev = (kj < A_BLOCK)[None]
    no_next = (kj >= 2 * A_BLOCK)[None]
    ninf = -np.inf
    return np.stack([bias, np.where(no_prev, ninf, bias), np.where(no_next, ninf, bias),
                     np.where(no_prev | no_next, ninf, bias)]).astype(np.float32)


def _layer(x, attn_norm, w_in, a_sink, b_q_norm, b_w_uq, b_kv_norm, b_w_ukv, out_norm_a, out_norm_b, w_o,
           mlp_norm, w_router, b_router, w_up, b_up, w_down, b_down):
    b, s, d = x.shape
    t = b * s
    win = jnp.pad(w_in, ((0, 0), (0, IN_COLS_PAD - w_in.shape[1]))).astype(BF16)
    wq = b_w_uq.reshape(B_Q_RANK, B_HEADS, B_NOPE_DIM + B_ROPE_DIM)
    wq_pe = jnp.pad(wq[:, :, B_NOPE_DIM:], ((0, 0), (0, 0), (0, LANES - B_ROPE_DIM)))
    wuq = jnp.concatenate([wq[:, :, :B_NOPE_DIM].reshape(B_Q_RANK, -1), wq_pe.reshape(B_Q_RANK, -1)],
                          axis=1).astype(BF16)
    wkv = b_w_ukv.reshape(B_KV_RANK, B_HEADS, B_NOPE_DIM + B_V_DIM)
    wukv = jnp.concatenate([wkv[:, :, :B_NOPE_DIM].reshape(B_KV_RANK, -1),
                            wkv[:, :, B_NOPE_DIM:].reshape(B_KV_RANK, -1)], axis=1).astype(BF16)
    cos, sin = _rope_tables(s)

    qa, ka, va, qb, kb, vb = _prologue(x, attn_norm[None], win, b_q_norm[None], wuq, b_kv_norm[None], wukv,
                                       cos, sin)
    ya = _window_attn(qa, ka, va, a_sink, _window_bias())
    yb = _mla_attn(qb, kb, vb)
    wr_hi = w_router.astype(BF16)
    wr_lo = (w_router - wr_hi.astype(F32)).astype(BF16)
    x1, xm, eidx, gates, rank, counts = _out_router(
        ya.reshape(t, A_WIDTH), yb.reshape(t, B_WIDTH), x.reshape(t, d), out_norm_a[None], out_norm_b[None],
        w_o.astype(BF16), mlp_norm[None], jnp.concatenate([wr_hi, wr_lo], axis=1), b_router[None])

    nblk = (t * TOP_K + N_EXPERTS * (MOE_BM - 1) + MOE_BM - 1) // MOE_BM
    tabs, start, pad_start, pad_len, nact = _expert_schedule(counts.reshape(N_EXPERTS), nblk, t * TOP_K)
    eidx = eidx.reshape(t * TOP_K)
    rank = rank.reshape(t * TOP_K)
    xs = _dispatch(start, pad_start, pad_len, nact, eidx, rank, xm, nblk * MOE_BM)
    ys = _experts(tabs, xs, w_up, b_up[:, None, :], w_down, b_down[:, None, :])
    return x1, start, eidx, rank, gates, ys


def kernel(x, attn_norm, w_in, a_sink, b_q_norm, b_w_uq, b_kv_norm, b_w_ukv, out_norm_a, out_norm_b, w_o,
           mlp_norm, w_router, b_router, w_up, b_up, w_down, b_down, final_norm):
    b, s, d = x.shape
    assert d == D_MODEL and s % min(TQ_MLA, s) == 0 and s % min(CK_MLA, s) == 0 and s % TM_PRO == 0 and attn_norm.shape[0] == 1
    x1, start, eidx, rank, gates, ys = _layer(
        x, attn_norm[0], w_in[0], a_sink[0], b_q_norm[0], b_w_uq[0], b_kv_norm[0], b_w_ukv[0], out_norm_a[0],
        out_norm_b[0], w_o[0], mlp_norm[0], w_router[0], b_router[0], w_up[0], b_up[0], w_down[0], b_down[0])
    out = _combine(start, eidx, rank, gates, x1, final_norm[None], ys)
    return out.reshape(b, s, d)
```

```python
import functools

import jax
import numpy as np
import jax.numpy as jnp
from jax import lax
from jax.experimental import pallas as pl
from jax.experimental.pallas import tpu as pltpu

D_MODEL = 2048
A_HEADS, A_KV_HEADS, A_HEAD_DIM = 16, 4, 64
A_GROUP = A_HEADS // A_KV_HEADS
WINDOW = 128
A_BLOCK = 128
B_HEADS, B_Q_RANK, B_KV_RANK = 8, 512, 256
B_NOPE_DIM, B_ROPE_DIM, B_V_DIM = 128, 64, 128
ROPE_THETA = 10000.0
A_WIDTH = A_HEADS * A_HEAD_DIM
B_WIDTH = B_HEADS * B_V_DIM
A_KV_COLS = A_KV_HEADS * A_HEAD_DIM
N_EXPERTS, TOP_K, D_FF = 32, 4, 2048
SWIGLU_ALPHA, SWIGLU_LIMIT = 1.702, 7.0
EPS = 1e-5

LANES = 128
SUBLANES = 8
B_QK_PAD = 2 * LANES
IN_COLS_PAD = A_WIDTH + 2 * A_KV_COLS + B_Q_RANK + B_KV_RANK + LANES
VMEM_LIMIT = 56 * 1024 * 1024
ROW_TILES = D_MODEL // LANES
ROW_PITCH = ROW_TILES + 1

TM_PRO = 256
TQ_MLA = 1024
CK_MLA = 1024
TM_OUT = 512
TM_DISPATCH = 512
TM_COMBINE = 128
MOE_BM = 512
MOE_SUB = 256
MOE_RM = 2048
MOE_TF = 256
WAIT_UNROLL = 16

BF16 = jnp.bfloat16
F32 = jnp.float32


def _rms(x, g):
    return x * lax.rsqrt(jnp.mean(x * x, axis=-1, keepdims=True) + EPS) * g


def _const_spec(shape):
    nd = len(shape)
    return pl.BlockSpec(shape, lambda *_: (0,) * nd, pipeline_mode=pl.Buffered(1))


def _store_token_rows(ref, first, val):
    n = val.shape[0]
    for j in range(ROW_TILES):
        ref[pl.ds(first * ROW_PITCH + j, n, stride=ROW_PITCH), :] = val[:, j * LANES:(j + 1) * LANES]
    ref[pl.ds(first * ROW_PITCH + ROW_TILES, n, stride=ROW_PITCH), :] = jnp.zeros((n, LANES), val.dtype)


def _load_token_rows(ref, n):
    return [ref[pl.ds(j, n, stride=ROW_PITCH), :] for j in range(ROW_TILES)]


def _rope(x, cos, sin):
    lane = lax.broadcasted_iota(jnp.int32, x.shape, 1)
    up = pltpu.roll(x, LANES - B_ROPE_DIM // 2, 1)
    dn = pltpu.roll(x, B_ROPE_DIM // 2, 1)
    sw = jnp.where(lane < B_ROPE_DIM // 2, -up, jnp.where(lane < B_ROPE_DIM, dn, 0.0))
    return x * cos + sw * sin


def _prologue_body(x_ref, g_ref, win_ref, qn_ref, wuq_ref, kvn_ref, wukv_ref, cos_ref, sin_ref,
                   qa_ref, ka_ref, va_ref, qb_ref, kb_ref, vb_ref):
    x = x_ref[0]
    xn = _rms(x, g_ref[...]).astype(BF16)
    proj = jnp.dot(xn, win_ref[...], preferred_element_type=F32)
    c0 = A_WIDTH
    c1 = c0 + A_KV_COLS
    c2 = c1 + A_KV_COLS
    c3 = c2 + B_Q_RANK
    c4 = c3 + B_KV_RANK
    qa_ref[0] = (proj[:, :A_WIDTH] * A_HEAD_DIM ** -0.5).astype(BF16)
    low = lax.broadcasted_iota(jnp.int32, (proj.shape[0], LANES), 1) < A_HEAD_DIM
    for src, dst in ((c0, ka_ref), (c1, va_ref)):
        for pair in range(A_KV_HEADS // 2):
            two = proj[:, src + pair * LANES:src + (pair + 1) * LANES]
            swapped = pltpu.roll(two, A_HEAD_DIM, 1)
            dst[0, 4 * pair + 0] = jnp.where(low, two, 0.0).astype(BF16)
            dst[0, 4 * pair + 1] = jnp.where(low, 0.0, swapped).astype(BF16)
            dst[0, 4 * pair + 2] = jnp.where(low, swapped, 0.0).astype(BF16)
            dst[0, 4 * pair + 3] = jnp.where(low, 0.0, two).astype(BF16)
    cq = _rms(proj[:, c2:c3], qn_ref[...]).astype(BF16)
    q = jnp.dot(cq, wuq_ref[...], preferred_element_type=F32)
    ckv = _rms(proj[:, c3:c4], kvn_ref[...]).astype(BF16)
    kv = jnp.dot(ckv, wukv_ref[...], preferred_element_type=F32)
    cos = cos_ref[...]
    sin = sin_ref[...]
    kpe = _rope(proj[:, c4:c4 + LANES], cos, sin).astype(BF16)
    b_scale = (B_NOPE_DIM + B_ROPE_DIM) ** -0.5
    hw = B_HEADS * LANES
    for h in range(B_HEADS):
        sl = slice(h * LANES, (h + 1) * LANES)
        qb_ref[0, h, :, :LANES] = (q[:, sl] * b_scale).astype(BF16)
        qpe = _rope(q[:, hw + h * LANES:hw + (h + 1) * LANES], cos, sin)
        qb_ref[0, h, :, LANES:] = (qpe * b_scale).astype(BF16)
        kb_ref[0, h, :, :LANES] = kv[:, sl].astype(BF16)
        kb_ref[0, h, :, LANES:] = kpe
        vb_ref[0, h] = kv[:, hw + h * LANES:hw + (h + 1) * LANES].astype(BF16)


def _prologue(x, g, win, qn, wuq, kvn, wukv, cos, sin):
    b, s, d = x.shape
    tm = TM_PRO
    grid = (b, s // tm)
    sds = jax.ShapeDtypeStruct
    out_shape = (
        sds((b, s, A_WIDTH), BF16),
        sds((b, 2 * A_KV_HEADS, s, LANES), BF16),
        sds((b, 2 * A_KV_HEADS, s, LANES), BF16),
        sds((b, B_HEADS, s, B_QK_PAD), BF16),
        sds((b, B_HEADS, s, B_QK_PAD), BF16),
        sds((b, B_HEADS, s, B_V_DIM), BF16),
    )

    def hspec(nh, w):
        return pl.BlockSpec((1, nh, tm, w), lambda bi, i: (bi, 0, i, 0))

    return pl.pallas_call(
        _prologue_body,
        grid=grid,
        in_specs=[
            pl.BlockSpec((1, tm, d), lambda bi, i: (bi, i, 0)),
            _const_spec(g.shape), _const_spec(win.shape), _const_spec(qn.shape), _const_spec(wuq.shape),
            _const_spec(kvn.shape), _const_spec(wukv.shape),
            pl.BlockSpec((tm, LANES), lambda bi, i: (i, 0)),
            pl.BlockSpec((tm, LANES), lambda bi, i: (i, 0)),
        ],
        out_specs=(pl.BlockSpec((1, tm, A_WIDTH), lambda bi, i: (bi, i, 0)),
                   hspec(2 * A_KV_HEADS, LANES), hspec(2 * A_KV_HEADS, LANES),
                   hspec(B_HEADS, B_QK_PAD), hspec(B_HEADS, B_QK_PAD), hspec(B_HEADS, B_V_DIM)),
        out_shape=out_shape,
        compiler_params=pltpu.CompilerParams(dimension_semantics=("parallel", "parallel"),
                                             vmem_limit_bytes=VMEM_LIMIT),
        name="prologue",
    )(x, g, win, qn, wuq, kvn, wukv, cos, sin)


def _window_body(sink_ref, q_ref, kp_ref, kc_ref, kn_ref, vp_ref, vc_ref, vn_ref, bias_ref, o_ref):
    two = 2 * A_BLOCK
    low = lax.broadcasted_iota(jnp.int32, (two, LANES), 1) < A_HEAD_DIM
    for hk in range(A_KV_HEADS):
        q2 = jnp.concatenate([q_ref[0, :, (2 * hk) * LANES:(2 * hk + 1) * LANES],
                              q_ref[0, :, (2 * hk + 1) * LANES:(2 * hk + 2) * LANES]], axis=0)
        acc = jnp.zeros((two, LANES), F32)
        inv = []
        for half in range(2):
            z = 2 * hk + half
            kband = jnp.concatenate([kp_ref[0, z], kc_ref[0, z], kn_ref[0, z]], axis=0)
            vband = jnp.concatenate([vp_ref[0, z], vc_ref[0, z], vn_ref[0, z]], axis=0)
            s = lax.dot_general(q2, kband, (((1,), (1,)), ((), ())), preferred_element_type=F32)
            s = s + bias_ref[0, z]
            sink = jnp.concatenate(
                [jnp.full((A_BLOCK, 1), sink_ref[hk * A_GROUP + 2 * jj + half], F32) for jj in range(2)], axis=0)
            m = jnp.maximum(jnp.max(s, axis=-1, keepdims=True), sink)
            p = jnp.exp(s - m)
            den = jnp.sum(p, axis=-1, keepdims=True) + jnp.exp(sink - m)
            inv.append(1.0 / den)
            acc = acc + jnp.dot(p.astype(BF16), vband, preferred_element_type=F32)
        o = (acc * jnp.where(low, inv[0], inv[1])).astype(BF16)
        o_ref[0, :, (2 * hk) * LANES:(2 * hk + 1) * LANES] = o[:A_BLOCK]
        o_ref[0, :, (2 * hk + 1) * LANES:(2 * hk + 2) * LANES] = o[A_BLOCK:]


def _window_attn(qa, ka, va, sink, bias):
    b, s, _ = qa.shape
    nb = s // A_BLOCK
    kv_blk = (1, 2 * A_KV_HEADS, A_BLOCK, LANES)
    prev = pl.BlockSpec(kv_blk, lambda bi, n: (bi, 0, jnp.maximum(n - 1, 0), 0))
    cur = pl.BlockSpec(kv_blk, lambda bi, n: (bi, 0, n, 0))
    nxt = pl.BlockSpec(kv_blk, lambda bi, n: (bi, 0, jnp.minimum(n + 1, nb - 1), 0))

    def edge(bi, n):
        return ((n == 0).astype(jnp.int32) + 2 * (n == nb - 1).astype(jnp.int32), 0, 0, 0)

    return pl.pallas_call(
        _window_body,
        grid=(b, nb),
        in_specs=[
            pl.BlockSpec(memory_space=pltpu.SMEM),
            pl.BlockSpec((1, A_BLOCK, A_WIDTH), lambda bi, n: (bi, n, 0)),
            prev, cur, nxt, prev, cur, nxt,
            pl.BlockSpec((1,) + bias.shape[1:], edge),
        ],
        out_specs=pl.BlockSpec((1, A_BLOCK, A_WIDTH), lambda bi, n: (bi, n, 0)),
        out_shape=jax.ShapeDtypeStruct((b, s, A_WIDTH), BF16),
        compiler_params=pltpu.CompilerParams(dimension_semantics=("parallel", "parallel"),
                                             vmem_limit_bytes=VMEM_LIMIT),
        name="window_attn",
    )(sink, qa, ka, ka, ka, va, va, va, bias)


def _mla_body(q_ref, k_ref, v_ref, o_ref):
    q = q_ref[0, 0]
    tq = q.shape[0]
    s_len = k_ref.shape[2]
    ck = min(CK_MLA, s_len)
    m = jnp.full((tq, 1), -jnp.inf, F32)
    l = jnp.zeros((tq, 1), F32)
    acc = jnp.zeros((tq, B_V_DIM), F32)
    for c in range(s_len // ck):
        k_c = k_ref[0, 0, c * ck:(c + 1) * ck, :]
        v_c = v_ref[0, 0, c * ck:(c + 1) * ck, :]
        s = lax.dot_general(q, k_c, (((1,), (1,)), ((), ())), preferred_element_type=F32)
        m_new = jnp.maximum(m, jnp.max(s, axis=-1, keepdims=True))
        alpha = jnp.exp(m - m_new)
        p = jnp.exp(s - m_new)
        l = alpha * l + jnp.sum(p, axis=-1, keepdims=True)
        acc = alpha * acc + jnp.dot(p.astype(BF16), v_c, preferred_element_type=F32)
        m = m_new
    o_ref[0] = (acc / l).astype(BF16)


def _mla_attn(qb, kb, vb):
    b, nh, s, _ = qb.shape
    tq = min(TQ_MLA, s)
    return pl.pallas_call(
        _mla_body,
        grid=(b, nh, s // tq),
        in_specs=[
            pl.BlockSpec((1, 1, tq, B_QK_PAD), lambda bi, h, i: (bi, h, i, 0)),
            pl.BlockSpec((1, 1, s, B_QK_PAD), lambda bi, h, i: (bi, h, 0, 0)),
            pl.BlockSpec((1, 1, s, B_V_DIM), lambda bi, h, i: (bi, h, 0, 0)),
        ],
        out_specs=pl.BlockSpec((1, tq, B_V_DIM), lambda bi, h, i: (bi, i, h)),
        out_shape=jax.ShapeDtypeStruct((b, s, B_WIDTH), BF16),
        compiler_params=pltpu.CompilerParams(dimension_semantics=("parallel", "parallel", "parallel"),
                                             vmem_limit_bytes=VMEM_LIMIT),
        name="mla_attn",
    )(qb, kb, vb)


def _out_router_body(ya_ref, yb_ref, x_ref, ga_ref, gb_ref, wo_ref, gm_ref, wr_ref, br_ref,
                     x1_ref, xm_ref, eidx_ref, gate_ref, rank_ref, cnt_ref, run_ref):
    i = pl.program_id(0)

    @pl.when(i == 0)
    def _():
        run_ref[...] = jnp.zeros_like(run_ref)

    na = _rms(ya_ref[...].astype(F32), ga_ref[...]).astype(BF16)
    nb = _rms(yb_ref[...].astype(F32), gb_ref[...]).astype(BF16)
    att = jnp.dot(na, wo_ref[:A_WIDTH, :], preferred_element_type=F32)
    att = att + jnp.dot(nb, wo_ref[A_WIDTH:, :], preferred_element_type=F32)
    x1 = x_ref[...] + att
    x1_ref[...] = x1
    hn = _rms(x1, gm_ref[...])
    _store_token_rows(xm_ref, 0, hn)
    tm = hn.shape[0]
    hi = hn.astype(BF16)
    lo = (hn - hi.astype(F32)).astype(BF16)
    prod = jnp.dot(jnp.concatenate([hi, lo], axis=0), wr_ref[...], preferred_element_type=F32)
    logits = (prod[:tm, :N_EXPERTS] + prod[:tm, N_EXPERTS:] + prod[tm:, :N_EXPERTS] + prod[tm:, N_EXPERTS:]
              + br_ref[...])
    lane = lax.broadcasted_iota(jnp.int32, (tm, N_EXPERTS), 1)
    work = logits
    sel = jnp.zeros((tm, N_EXPERTS), F32)
    hots, vals, idxs = [], [], []
    for _k in range(TOP_K):
        mx = jnp.max(work, axis=-1, keepdims=True)
        idx = jnp.min(jnp.where(work == mx, lane, N_EXPERTS), axis=-1, keepdims=True)
        hot = lane == idx
        hots.append(hot)
        vals.append(mx)
        idxs.append(idx)
        sel = sel + hot.astype(F32)
        work = jnp.where(hot, -jnp.inf, work)
    exps = [jnp.exp(v - vals[0]) for v in vals]
    den = exps[0] + exps[1] + exps[2] + exps[3]
    r_i = lax.broadcasted_iota(jnp.int32, (tm, tm), 0)
    c_i = lax.broadcasted_iota(jnp.int32, (tm, tm), 1)
    tri = (c_i < r_i).astype(BF16)
    before = jnp.dot(tri, sel.astype(BF16), preferred_element_type=F32) + run_ref[...]
    lane4 = lax.broadcasted_iota(jnp.int32, (tm, TOP_K), 1)
    eidx = jnp.zeros((tm, TOP_K), jnp.int32)
    gate = jnp.zeros((tm, TOP_K), F32)
    rank = jnp.zeros((tm, TOP_K), jnp.int32)
    for k in range(TOP_K):
        rk = jnp.sum(jnp.where(hots[k], before, 0.0), axis=-1, keepdims=True).astype(jnp.int32)
        eidx = jnp.where(lane4 == k, idxs[k], eidx)
        gate = jnp.where(lane4 == k, exps[k] / den, gate)
        rank = jnp.where(lane4 == k, rk, rank)
    eidx_ref[...] = eidx
    gate_ref[...] = gate
    rank_ref[...] = rank
    run = run_ref[...] + jnp.sum(sel, axis=0, keepdims=True)
    run_ref[...] = run
    cnt_ref[...] = run.astype(jnp.int32)


def _out_router(ya, yb, x2d, ga, gb, wo, gm, wr, br):
    t, d = x2d.shape
    tm = min(TM_OUT, t)
    sds = jax.ShapeDtypeStruct

    def row(w):
        return pl.BlockSpec((tm, w), lambda i: (i, 0))

    return pl.pallas_call(
        _out_router_body,
        grid=(t // tm,),
        in_specs=[row(A_WIDTH), row(B_WIDTH), row(d), _const_spec(ga.shape), _const_spec(gb.shape),
                  _const_spec(wo.shape), _const_spec(gm.shape), _const_spec(wr.shape), _const_spec(br.shape)],
        out_specs=(row(d), pl.BlockSpec((tm * ROW_PITCH, LANES), lambda i: (i, 0)),
                   row(TOP_K), row(TOP_K), row(TOP_K),
                   pl.BlockSpec((1, N_EXPERTS), lambda i: (0, 0))),
        out_shape=(sds((t, d), F32), sds((t * ROW_PITCH, LANES), F32), sds((t, TOP_K), jnp.int32),
                   sds((t, TOP_K), F32),
                   sds((t, TOP_K), jnp.int32), sds((1, N_EXPERTS), jnp.int32)),
        scratch_shapes=[pltpu.VMEM((1, N_EXPERTS), F32)],
        compiler_params=pltpu.CompilerParams(dimension_semantics=("arbitrary",),
                                             vmem_limit_bytes=VMEM_LIMIT),
        name="out_router",
    )(ya, yb, x2d, ga, gb, wo, gm, wr, br)


def _token_copy(src_ref, src_tok, dst_ref, dst_tok, sem, ntok=1):
    rows = ntok * ROW_PITCH
    return pltpu.make_async_copy(src_ref.at[pl.ds(src_tok * ROW_PITCH, rows)],
                                 dst_ref.at[pl.ds(dst_tok * ROW_PITCH, rows)], sem)


def _zero_fill_pads(pad_start_ref, pad_len_ref, nact_ref, xs_ref, zero_ref, zsem):
    bm = zero_ref.shape[0] // ROW_PITCH
    nblk = xs_ref.shape[0] // (bm * ROW_PITCH)
    zero_ref[...] = jnp.zeros_like(zero_ref)

    def per_expert(e, c):
        off = pad_start_ref[e]
        n = pad_len_ref[e]
        p = bm // 2
        while p >= 1:
            take = (n & p) != 0

            @pl.when(take)
            def _(p=p, off=off):
                cp = _token_copy(zero_ref, 0, xs_ref, off, zsem, p)
                cp.start()
                cp.wait()

            off = off + jnp.where(take, p, 0)
            p //= 2
        return c

    lax.fori_loop(0, N_EXPERTS, per_expert, 0)

    def tail(j, c):
        cp = _token_copy(zero_ref, 0, xs_ref, j * bm, zsem, bm)
        cp.start()
        cp.wait()
        return c

    lax.fori_loop(nact_ref[0], nblk, tail, 0)


def _dispatch_body(start_ref, pad_start_ref, pad_len_ref, nact_ref, eidx_ref, rank_ref, xm_ref, xs_ref,
                   zero_ref, sem, zsem):
    tm = eidx_ref.shape[0] // TOP_K

    @pl.when(pl.program_id(0) == 0)
    def _():
        _zero_fill_pads(pad_start_ref, pad_len_ref, nact_ref, xs_ref, zero_ref, zsem)

    def issue(i, c):
        for k in range(TOP_K):
            a = i * TOP_K + k
            _token_copy(xm_ref, i, xs_ref, start_ref[eidx_ref[a]] + rank_ref[a], sem).start(priority=k % 2)
        return c

    lax.fori_loop(0, tm, issue, 0)

    def drain(i, c):
        for _ in range(WAIT_UNROLL):
            _token_copy(xm_ref, 0, xs_ref, 0, sem).wait()
        return c

    lax.fori_loop(0, tm * TOP_K // WAIT_UNROLL, drain, 0)


def _dispatch(start, pad_start, pad_len, nact, eidx_flat, rank_flat, xm, n_slots):
    t = xm.shape[0] // ROW_PITCH
    tm = min(TM_DISPATCH, t)
    grid_spec = pltpu.PrefetchScalarGridSpec(
        num_scalar_prefetch=4,
        grid=(t // tm,),
        in_specs=[pl.BlockSpec((tm * TOP_K,), lambda i, *_: (i,), memory_space=pltpu.SMEM),
                  pl.BlockSpec((tm * TOP_K,), lambda i, *_: (i,), memory_space=pltpu.SMEM),
                  pl.BlockSpec((tm * ROW_PITCH, LANES), lambda i, *_: (i, 0))],
        out_specs=pl.BlockSpec(memory_space=pl.ANY),
        scratch_shapes=[pltpu.VMEM((MOE_BM * ROW_PITCH, LANES), F32), pltpu.SemaphoreType.DMA(()),
                        pltpu.SemaphoreType.DMA(())],
    )
    return pl.pallas_call(
        _dispatch_body,
        grid_spec=grid_spec,
        out_shape=jax.ShapeDtypeStruct((n_slots * ROW_PITCH, LANES), F32),
        compiler_params=pltpu.CompilerParams(dimension_semantics=("arbitrary",), has_side_effects=True,
                                             vmem_limit_bytes=VMEM_LIMIT),
        name="dispatch",
    )(start, pad_start, pad_len, nact, eidx_flat, rank_flat, xm)


STEP_UP, STEP_DOWN, STEP_TAIL, STEP_IDLE = 0, 1, 2, 3
NF = D_FF // MOE_TF
PASS_BLOCKS = MOE_RM // MOE_BM


def _swiglu(hg, hu):
    gate = jnp.minimum(hg, SWIGLU_LIMIT)
    up = jnp.clip(hu, -SWIGLU_LIMIT, SWIGLU_LIMIT)
    glu = gate / (1.0 + jnp.exp(-SWIGLU_ALPHA * gate))
    return (up + 1.0) * glu


def _experts_body(e_ref, f_ref, kind_ref, blk_ref, blk0_ref, nsub_ref,
                  xs_ref, wg_ref, wu_ref, bg_ref, bu_ref, wd_ref, bd_ref,
                  y_ref, xb_ref, h_ref, wgb_ref, wub_ref, wdb_ref, stg_ref, sem):
    t = pl.program_id(0)
    kind = kind_ref[t]
    f = f_ref[t]
    nsub = nsub_ref[t]

    def x_copy(j, slot):
        rows = MOE_SUB * ROW_PITCH
        first = pl.multiple_of((blk0_ref[t] * MOE_BM + j * MOE_SUB) * ROW_PITCH, SUBLANES)
        return pltpu.make_async_copy(xs_ref.at[pl.ds(first, rows)], stg_ref.at[slot], sem.at[slot])

    @pl.when(kind == STEP_UP)
    def _up():
        @pl.when(f == 0)
        def _load_rows():
            x_copy(0, 0).start()

            def body(j, c):
                slot = j % 2

                @pl.when(j + 1 < nsub)
                def _():
                    x_copy(j + 1, 1 - slot).start()

                x_copy(j, slot).wait()
                rows = pl.ds(pl.multiple_of(j * MOE_SUB, MOE_SUB), MOE_SUB)
                for jt, tile in enumerate(_load_token_rows(stg_ref.at[slot], MOE_SUB)):
                    xb_ref[rows, jt * LANES:(jt + 1) * LANES] = tile.astype(BF16)
                return c

            lax.fori_loop(0, nsub, body, 0)

        wgb_ref[...] = wg_ref[0].astype(BF16)
        wub_ref[...] = wu_ref[0].astype(BF16)
        wdb_ref[pl.ds(pl.multiple_of(f * MOE_TF, MOE_TF), MOE_TF), :] = wd_ref[0].astype(BF16)

        def up_rows(row0, nrows):
            x = xb_ref[pl.ds(row0, nrows), :]
            hg = jnp.dot(x, wgb_ref[...], preferred_element_type=F32) + bg_ref[0]
            hu = jnp.dot(x, wub_ref[...], preferred_element_type=F32) + bu_ref[0]
            act = _swiglu(hg, hu).astype(BF16)
            for fs in range(NF):
                @pl.when(f == fs)
                def _(fs=fs):
                    h_ref[pl.ds(row0, nrows), fs * MOE_TF:(fs + 1) * MOE_TF] = act

        npair = nsub // 2

        def pair(i, c):
            up_rows(pl.multiple_of(i * MOE_BM, MOE_BM), MOE_BM)
            return c

        lax.fori_loop(0, npair, pair, 0)

        @pl.when(nsub % 2 == 1)
        def _():
            up_rows(pl.multiple_of(npair * MOE_BM, MOE_BM), MOE_SUB)

    @pl.when(kind == STEP_DOWN)
    def _down():
        row0 = pl.multiple_of((blk_ref[t] - blk0_ref[t]) * MOE_BM, MOE_BM)

        def down_rows(nrows):
            hrows = h_ref[pl.ds(row0, nrows), :]
            return jnp.dot(hrows, wdb_ref[...], preferred_element_type=F32) + bd_ref[0]

        @pl.when(nsub == 2)
        def _():
            _store_token_rows(y_ref, 0, down_rows(MOE_BM))

        @pl.when(nsub == 1)
        def _():
            _store_token_rows(y_ref, 0, down_rows(MOE_SUB))
            y_ref[MOE_SUB * ROW_PITCH:, :] = jnp.zeros(((MOE_BM - MOE_SUB) * ROW_PITCH, LANES), F32)

    @pl.when(kind == STEP_TAIL)
    def _tail():
        y_ref[...] = jnp.zeros_like(y_ref)


def _schedule_body(n_blocks, cnt_ref, e_ref, f_ref, kind_ref, blk_ref, blk0_ref, nsub_ref,
                   start_ref, pad_start_ref, pad_len_ref, nact_ref):
    n_steps = e_ref.shape[0]
    sub_per_blk = MOE_BM // MOE_SUB
    zero = jnp.int32(0)

    def put(idx, e, f, kind, blk, blk0, nsub):
        e_ref[idx] = e
        f_ref[idx] = f
        kind_ref[idx] = kind
        blk_ref[idx] = blk
        blk0_ref[idx] = blk0
        nsub_ref[idx] = nsub

    def per_expert(e, carry):
        t, blk, last_e = carry
        n = cnt_ref[e]
        nblk = (n + (MOE_BM - 1)) // MOE_BM
        start_ref[e] = blk * MOE_BM
        pad_start_ref[e] = blk * MOE_BM + n
        pad_len_ref[e] = nblk * MOE_BM - n

        def per_pass(p, t2):
            b0 = blk + p * PASS_BLOCKS
            nb = jnp.minimum(nblk - p * PASS_BLOCKS, PASS_BLOCKS)
            nsub = (jnp.minimum(n - p * MOE_RM, MOE_RM) + (MOE_SUB - 1)) // MOE_SUB
            for f in range(NF):
                put(t2 + f, e, f, STEP_UP, b0, b0, nsub)

            def per_blk(j, c):
                put(t2 + NF + j, e, NF - 1, STEP_DOWN, b0 + j, b0,
                    jnp.clip(nsub - j * sub_per_blk, 1, sub_per_blk))
                return c

            lax.fori_loop(0, nb, per_blk, 0)
            return t2 + NF + nb

        t = lax.fori_loop(0, (nblk + (PASS_BLOCKS - 1)) // PASS_BLOCKS, per_pass, t)
        return t, blk + nblk, jnp.where(n > 0, e, last_e)

    t, nact, last_e = lax.fori_loop(0, N_EXPERTS, per_expert, (zero, zero, zero))
    nact_ref[0] = nact

    def spare(i, c):
        is_tail = i < n_blocks - nact
        blk = jnp.where(is_tail, nact + i, n_blocks - 1)
        put(t + i, last_e, NF - 1, jnp.where(is_tail, STEP_TAIL, STEP_IDLE), blk, blk, 1)
        return c

    lax.fori_loop(0, n_steps - t, spare, 0)


def _expert_schedule(counts, n_blocks, n_assign):
    n_pass_max = N_EXPERTS + n_assign // MOE_RM
    n_steps = n_pass_max * NF + n_blocks
    smem = pl.BlockSpec(memory_space=pltpu.SMEM)
    i32 = jnp.int32
    out_shape = tuple(jax.ShapeDtypeStruct((n,), i32) for n in (n_steps,) * 6 + (N_EXPERTS,) * 3 + (1,))
    outs = pl.pallas_call(
        functools.partial(_schedule_body, n_blocks),
        in_specs=[smem],
        out_specs=tuple(smem for _ in out_shape),
        out_shape=out_shape,
        name="schedule",
    )(counts)
    return outs[:6], outs[6], outs[7], outs[8], outs[9]


def _experts(tabs, xs, w_up, b_up, w_down, b_down):
    d = D_MODEL
    n_steps = tabs[0].shape[0]
    grid_spec = pltpu.PrefetchScalarGridSpec(
        num_scalar_prefetch=len(tabs),
        grid=(n_steps,),
        in_specs=[
            pl.BlockSpec(memory_space=pl.ANY),
            pl.BlockSpec((1, d, MOE_TF), lambda t, e, f, *_: (e[t], 0, f[t])),
            pl.BlockSpec((1, d, MOE_TF), lambda t, e, f, *_: (e[t], 0, NF + f[t])),
            pl.BlockSpec((1, 1, MOE_TF), lambda t, e, f, *_: (e[t], 0, f[t])),
            pl.BlockSpec((1, 1, MOE_TF), lambda t, e, f, *_: (e[t], 0, NF + f[t])),
            pl.BlockSpec((1, MOE_TF, d), lambda t, e, f, *_: (e[t], f[t], 0)),
            pl.BlockSpec((1, 1, d), lambda t, e, f, *_: (e[t], 0, 0)),
        ],
        out_specs=pl.BlockSpec((MOE_BM * ROW_PITCH, LANES), lambda t, e, f, kind, blk, *_: (blk[t], 0)),
        scratch_shapes=[
            pltpu.VMEM((MOE_RM, d), BF16),
            pltpu.VMEM((MOE_RM, D_FF), BF16),
            pltpu.VMEM((d, MOE_TF), BF16),
            pltpu.VMEM((d, MOE_TF), BF16),
            pltpu.VMEM((D_FF, d), BF16),
            pltpu.VMEM((2, MOE_SUB * ROW_PITCH, LANES), F32),
            pltpu.SemaphoreType.DMA((2,)),
        ],
    )
    return pl.pallas_call(
        _experts_body,
        grid_spec=grid_spec,
        out_shape=jax.ShapeDtypeStruct(xs.shape, F32),
        compiler_params=pltpu.CompilerParams(dimension_semantics=("arbitrary",),
                                             vmem_limit_bytes=VMEM_LIMIT),
        name="experts",
    )(*tabs, xs, w_up, w_up, b_up, b_up, w_down, b_down)


def _combine_body(start_ref, eidx_ref, rank_ref, eidx_nx_ref, rank_nx_ref, gate_ref, x1_ref, gf_ref, ys_ref,
                  o_ref, buf_ref, sem):
    step = pl.program_id(0)
    slot = step % 2
    tm = x1_ref.shape[0]

    def gather(e_ref, r_ref, s):
        def issue(i, c):
            for k in range(TOP_K):
                a = i * TOP_K + k
                _token_copy(ys_ref, start_ref[e_ref[a]] + r_ref[a], buf_ref.at[s, k], i,
                            sem.at[s]).start(priority=k % 2)
            return c

        lax.fori_loop(0, tm, issue, 0)

    @pl.when(step == 0)
    def _():
        gather(eidx_ref, rank_ref, 0)

    @pl.when(step + 1 < pl.num_programs(0))
    def _():
        gather(eidx_nx_ref, rank_nx_ref, 1 - slot)

    def drain(i, c):
        for _ in range(WAIT_UNROLL):
            _token_copy(ys_ref, 0, buf_ref.at[slot, 0], 0, sem.at[slot]).wait()
        return c

    lax.fori_loop(0, tm * TOP_K // WAIT_UNROLL, drain, 0)
    gate = gate_ref[...]
    tiles = [x1_ref[:, j * LANES:(j + 1) * LANES] for j in range(ROW_TILES)]
    for k in range(TOP_K):
        g = gate[:, k:k + 1]
        for j, tile in enumerate(_load_token_rows(buf_ref.at[slot, k], tm)):
            tiles[j] = tiles[j] + g * tile
    o_ref[...] = _rms(jnp.concatenate(tiles, axis=1), gf_ref[...])


def _combine(start, eidx_flat, rank_flat, gates, x1, gf, ys):
    t, d = x1.shape
    tm = min(TM_COMBINE, t)
    n = t // tm
    idx_blk = (tm * TOP_K,)
    grid_spec = pltpu.PrefetchScalarGridSpec(
        num_scalar_prefetch=1,
        grid=(n,),
        in_specs=[pl.BlockSpec(idx_blk, lambda i, *_: (i,), memory_space=pltpu.SMEM),
                  pl.BlockSpec(idx_blk, lambda i, *_: (i,), memory_space=pltpu.SMEM),
                  pl.BlockSpec(idx_blk, lambda i, *_: (jnp.minimum(i + 1, n - 1),), memory_space=pltpu.SMEM),
                  pl.BlockSpec(idx_blk, lambda i, *_: (jnp.minimum(i + 1, n - 1),), memory_space=pltpu.SMEM),
                  pl.BlockSpec((tm, TOP_K), lambda i, *_: (i, 0)),
                  pl.BlockSpec((tm, d), lambda i, *_: (i, 0)),
                  _const_spec(gf.shape),
                  pl.BlockSpec(memory_space=pl.ANY)],
        out_specs=pl.BlockSpec((tm, d), lambda i, *_: (i, 0)),
        scratch_shapes=[pltpu.VMEM((2, TOP_K, tm * ROW_PITCH, LANES), F32), pltpu.SemaphoreType.DMA((2,))],
    )
    return pl.pallas_call(
        _combine_body,
        grid_spec=grid_spec,
        out_shape=jax.ShapeDtypeStruct((t, d), F32),
        compiler_params=pltpu.CompilerParams(dimension_semantics=("arbitrary",),
                                             vmem_limit_bytes=VMEM_LIMIT),
        name="combine",
    )(start, eidx_flat, rank_flat, eidx_flat, rank_flat, gates, x1, gf, ys)


def _rope_tables(s):
    pos = np.arange(s, dtype=np.float32)
    inv_freq = np.power(np.float32(ROPE_THETA), -np.arange(0, B_ROPE_DIM, 2, dtype=np.float32) / B_ROPE_DIM)
    ang = (pos[:, None] * inv_freq[None, :]).astype(np.float32)
    cos, sin = np.cos(ang), np.sin(ang)
    pad = LANES - B_ROPE_DIM
    cos = np.concatenate([cos, cos, np.ones((s, pad), np.float32)], axis=1)
    sin = np.concatenate([sin, sin, np.zeros((s, pad), np.float32)], axis=1)
    return cos.astype(np.float32), sin.astype(np.float32)


def _window_bias():
    qi = np.arange(A_BLOCK)[:, None]
    kj = np.arange(3 * A_BLOCK)[None, :]
    dist = np.abs(qi + A_BLOCK - kj)
    slopes = np.power(np.float32(2.0), -8.0 * np.arange(1, A_HEADS + 1, dtype=np.float32) / A_HEADS)
    bias = -slopes[:, None, None] * dist.astype(np.float32)[None]
    bias = np.where((dist <= WINDOW)[None], bias, -np.inf)
    bias = bias.reshape(A_KV_HEADS, 2, 2, A_BLOCK, 3 * A_BLOCK).transpose(0, 2, 1, 3, 4)
    bias = bias.reshape(2 * A_KV_HEADS, 2 * A_BLOCK, 3 * A_BLOCK)
    no_prev = (kj < A_BLOCK)[None]
    no_next = (kj >= 2 * A_BLOCK)[None]
    ninf = -np.inf
    return np.stack([bias, np.where(no_prev, ninf, bias), np.where(no_next, ninf, bias),
                     np.where(no_prev | no_next, ninf, bias)]).astype(np.float32)


def _layer(x, attn_norm, w_in, a_sink, b_q_norm, b_w_uq, b_kv_norm, b_w_ukv, out_norm_a, out_norm_b, w_o,
           mlp_norm, w_router, b_router, w_up, b_up, w_down, b_down):
    b, s, d = x.shape
    t = b * s
    win = jnp.pad(w_in, ((0, 0), (0, IN_COLS_PAD - w_in.shape[1]))).astype(BF16)
    wq = b_w_uq.reshape(B_Q_RANK, B_HEADS, B_NOPE_DIM + B_ROPE_DIM)
    wq_pe = jnp.pad(wq[:, :, B_NOPE_DIM:], ((0, 0), (0, 0), (0, LANES - B_ROPE_DIM)))
    wuq = jnp.concatenate([wq[:, :, :B_NOPE_DIM].reshape(B_Q_RANK, -1), wq_pe.reshape(B_Q_RANK, -1)],
                          axis=1).astype(BF16)
    wkv = b_w_ukv.reshape(B_KV_RANK, B_HEADS, B_NOPE_DIM + B_V_DIM)
    wukv = jnp.concatenate([wkv[:, :, :B_NOPE_DIM].reshape(B_KV_RANK, -1),
                            wkv[:, :, B_NOPE_DIM:].reshape(B_KV_RANK, -1)], axis=1).astype(BF16)
    cos, sin = _rope_tables(s)

    qa, ka, va, qb, kb, vb = _prologue(x, attn_norm[None], win, b_q_norm[None], wuq, b_kv_norm[None], wukv,
                                       cos, sin)
    ya = _window_attn(qa, ka, va, a_sink, _window_bias())
    yb = _mla_attn(qb, kb, vb)
    wr_hi = w_router.astype(BF16)
    wr_lo = (w_router - wr_hi.astype(F32)).astype(BF16)
    x1, xm, eidx, gates, rank, counts = _out_router(
        ya.reshape(t, A_WIDTH), yb.reshape(t, B_WIDTH), x.reshape(t, d), out_norm_a[None], out_norm_b[None],
        w_o.astype(BF16), mlp_norm[None], jnp.concatenate([wr_hi, wr_lo], axis=1), b_router[None])

    nblk = (t * TOP_K + N_EXPERTS * (MOE_BM - 1) + MOE_BM - 1) // MOE_BM
    tabs, start, pad_start, pad_len, nact = _expert_schedule(counts.reshape(N_EXPERTS), nblk, t * TOP_K)
    eidx = eidx.reshape(t * TOP_K)
    rank = rank.reshape(t * TOP_K)
    xs = _dispatch(start, pad_start, pad_len, nact, eidx, rank, xm, nblk * MOE_BM)
    ys = _experts(tabs, xs, w_up, b_up[:, None, :], w_down, b_down[:, None, :])
    return x1, start, eidx, rank, gates, ys


def kernel(x, attn_norm, w_in, a_sink, b_q_norm, b_w_uq, b_kv_norm, b_w_ukv, out_norm_a, out_norm_b, w_o,
           mlp_norm, w_router, b_router, w_up, b_up, w_down, b_down, final_norm):
    b, s, d = x.shape
    assert d == D_MODEL and s % min(TQ_MLA, s) == 0 and s % min(CK_MLA, s) == 0 and s % TM_PRO == 0 and attn_norm.shape[0] == 1
    x1, start, eidx, rank, gates, ys = _layer(
        x, attn_norm[0], w_in[0], a_sink[0], b_q_norm[0], b_w_uq[0], b_kv_norm[0], b_w_ukv[0], out_norm_a[0],
        out_norm_b[0], w_o[0], mlp_norm[0], w_router[0], b_router[0], w_up[0], b_up[0], w_down[0], b_down[0])
    out = _combine(start, eidx, rank, gates, x1, final_norm[None], ys)
    return out.reshape(b, s, d)
```

```python
import functools

import jax
import numpy as np
import jax.numpy as jnp
from jax import lax
from jax.experimental import pallas as pl
from jax.experimental.pallas import tpu as pltpu

D_MODEL = 2048
A_HEADS, A_KV_HEADS, A_HEAD_DIM = 16, 4, 64
A_GROUP = A_HEADS // A_KV_HEADS
WINDOW = 128
A_BLOCK = 128
B_HEADS, B_Q_RANK, B_KV_RANK = 8, 512, 256
B_NOPE_DIM, B_ROPE_DIM, B_V_DIM = 128, 64, 128
ROPE_THETA = 10000.0
A_WIDTH = A_HEADS * A_HEAD_DIM
B_WIDTH = B_HEADS * B_V_DIM
A_KV_COLS = A_KV_HEADS * A_HEAD_DIM
N_EXPERTS, TOP_K, D_FF = 32, 4, 2048
SWIGLU_ALPHA, SWIGLU_LIMIT = 1.702, 7.0
EPS = 1e-5

LANES = 128
SUBLANES = 8
B_QK_PAD = 2 * LANES
IN_COLS_PAD = A_WIDTH + 2 * A_KV_COLS + B_Q_RANK + B_KV_RANK + LANES
VMEM_LIMIT = 56 * 1024 * 1024
ROW_TILES = D_MODEL // LANES
ROW_PITCH = ROW_TILES + 1

TM_PRO = 256
TQ_MLA = 1024
CK_MLA = 1024
TM_OUT = 512
TM_DISPATCH = 512
TM_COMBINE = 128
MOE_BM = 512
MOE_SUB = 256
MOE_RM = 2048
MOE_TF = 256
WAIT_UNROLL = 16

BF16 = jnp.bfloat16
F32 = jnp.float32


def _rms(x, g):
    return x * lax.rsqrt(jnp.mean(x * x, axis=-1, keepdims=True) + EPS) * g


def _const_spec(shape):
    nd = len(shape)
    return pl.BlockSpec(shape, lambda *_: (0,) * nd, pipeline_mode=pl.Buffered(1))


def _store_token_rows(ref, first, val):
    n = val.shape[0]
    for j in range(ROW_TILES):
        ref[pl.ds(first * ROW_PITCH + j, n, stride=ROW_PITCH), :] = val[:, j * LANES:(j + 1) * LANES]
    ref[pl.ds(first * ROW_PITCH + ROW_TILES, n, stride=ROW_PITCH), :] = jnp.zeros((n, LANES), val.dtype)


def _load_token_rows(ref, n):
    return [ref[pl.ds(j, n, stride=ROW_PITCH), :] for j in range(ROW_TILES)]


def _rope(x, cos, sin):
    lane = lax.broadcasted_iota(jnp.int32, x.shape, 1)
    up = pltpu.roll(x, LANES - B_ROPE_DIM // 2, 1)
    dn = pltpu.roll(x, B_ROPE_DIM // 2, 1)
    sw = jnp.where(lane < B_ROPE_DIM // 2, -up, jnp.where(lane < B_ROPE_DIM, dn, 0.0))
    return x * cos + sw * sin


def _prologue_body(x_ref, g_ref, win_ref, qn_ref, wuq_ref, kvn_ref, wukv_ref, cos_ref, sin_ref,
                   qa_ref, ka_ref, va_ref, qb_ref, kb_ref, vb_ref):
    x = x_ref[0]
    xn = _rms(x, g_ref[...]).astype(BF16)
    proj = jnp.dot(xn, win_ref[...], preferred_element_type=F32)
    c0 = A_WIDTH
    c1 = c0 + A_KV_COLS
    c2 = c1 + A_KV_COLS
    c3 = c2 + B_Q_RANK
    c4 = c3 + B_KV_RANK
    qa_ref[0] = (proj[:, :A_WIDTH] * A_HEAD_DIM ** -0.5).astype(BF16)
    low = lax.broadcasted_iota(jnp.int32, (proj.shape[0], LANES), 1) < A_HEAD_DIM
    for src, dst in ((c0, ka_ref), (c1, va_ref)):
        for pair in range(A_KV_HEADS // 2):
            two = proj[:, src + pair * LANES:src + (pair + 1) * LANES]
            swapped = pltpu.roll(two, A_HEAD_DIM, 1)
            dst[0, 4 * pair + 0] = jnp.where(low, two, 0.0).astype(BF16)
            dst[0, 4 * pair + 1] = jnp.where(low, 0.0, swapped).astype(BF16)
            dst[0, 4 * pair + 2] = jnp.where(low, swapped, 0.0).astype(BF16)
            dst[0, 4 * pair + 3] = jnp.where(low, 0.0, two).astype(BF16)
    cq = _rms(proj[:, c2:c3], qn_ref[...]).astype(BF16)
    q = jnp.dot(cq, wuq_ref[...], preferred_element_type=F32)
    ckv = _rms(proj[:, c3:c4], kvn_ref[...]).astype(BF16)
    kv = jnp.dot(ckv, wukv_ref[...], preferred_element_type=F32)
    cos = cos_ref[...]
    sin = sin_ref[...]
    kpe = _rope(proj[:, c4:c4 + LANES], cos, sin).astype(BF16)
    b_scale = (B_NOPE_DIM + B_ROPE_DIM) ** -0.5
    hw = B_HEADS * LANES
    for h in range(B_HEADS):
        sl = slice(h * LANES, (h + 1) * LANES)
        qb_ref[0, h, :, :LANES] = (q[:, sl] * b_scale).astype(BF16)
        qpe = _rope(q[:, hw + h * LANES:hw + (h + 1) * LANES], cos, sin)
        qb_ref[0, h, :, LANES:] = (qpe * b_scale).astype(BF16)
        kb_ref[0, h, :, :LANES] = kv[:, sl].astype(BF16)
        kb_ref[0, h, :, LANES:] = kpe
        vb_ref[0, h] = kv[:, hw + h * LANES:hw + (h + 1) * LANES].astype(BF16)


def _prologue(x, g, win, qn, wuq, kvn, wukv, cos, sin):
    b, s, d = x.shape
    tm = TM_PRO
    grid = (b, s // tm)
    sds = jax.ShapeDtypeStruct
    out_shape = (
        sds((b, s, A_WIDTH), BF16),
        sds((b, 2 * A_KV_HEADS, s, LANES), BF16),
        sds((b, 2 * A_KV_HEADS, s, LANES), BF16),
        sds((b, B_HEADS, s, B_QK_PAD), BF16),
        sds((b, B_HEADS, s, B_QK_PAD), BF16),
        sds((b, B_HEADS, s, B_V_DIM), BF16),
    )

    def hspec(nh, w):
        return pl.BlockSpec((1, nh, tm, w), lambda bi, i: (bi, 0, i, 0))

    return pl.pallas_call(
        _prologue_body,
        grid=grid,
        in_specs=[
            pl.BlockSpec((1, tm, d), lambda bi, i: (bi, i, 0)),
            _const_spec(g.shape), _const_spec(win.shape), _const_spec(qn.shape), _const_spec(wuq.shape),
            _const_spec(kvn.shape), _const_spec(wukv.shape),
            pl.BlockSpec((tm, LANES), lambda bi, i: (i, 0)),
            pl.BlockSpec((tm, LANES), lambda bi, i: (i, 0)),
        ],
        out_specs=(pl.BlockSpec((1, tm, A_WIDTH), lambda bi, i: (bi, i, 0)),
                   hspec(2 * A_KV_HEADS, LANES), hspec(2 * A_KV_HEADS, LANES),
                   hspec(B_HEADS, B_QK_PAD), hspec(B_HEADS, B_QK_PAD), hspec(B_HEADS, B_V_DIM)),
        out_shape=out_shape,
        compiler_params=pltpu.CompilerParams(dimension_semantics=("parallel", "parallel"),
                                             vmem_limit_bytes=VMEM_LIMIT),
        name="prologue",
    )(x, g, win, qn, wuq, kvn, wukv, cos, sin)


def _window_body(sink_ref, q_ref, kp_ref, kc_ref, kn_ref, vp_ref, vc_ref, vn_ref, bias_ref, o_ref):
    two = 2 * A_BLOCK
    low = lax.broadcasted_iota(jnp.int32, (two, LANES), 1) < A_HEAD_DIM
    for hk in range(A_KV_HEADS):
        q2 = jnp.concatenate([q_ref[0, :, (2 * hk) * LANES:(2 * hk + 1) * LANES],
                              q_ref[0, :, (2 * hk + 1) * LANES:(2 * hk + 2) * LANES]], axis=0)
        acc = jnp.zeros((two, LANES), F32)
        inv = []
        for half in range(2):
            z = 2 * hk + half
            kband = jnp.concatenate([kp_ref[0, z], kc_ref[0, z], kn_ref[0, z]], axis=0)
            vband = jnp.concatenate([vp_ref[0, z], vc_ref[0, z], vn_ref[0, z]], axis=0)
            s = lax.dot_general(q2, kband, (((1,), (1,)), ((), ())), preferred_element_type=F32)
            s = s + bias_ref[0, z]
            sink = jnp.concatenate(
                [jnp.full((A_BLOCK, 1), sink_ref[hk * A_GROUP + 2 * jj + half], F32) for jj in range(2)], axis=0)
            m = jnp.maximum(jnp.max(s, axis=-1, keepdims=True), sink)
            p = jnp.exp(s - m)
            den = jnp.sum(p, axis=-1, keepdims=True) + jnp.exp(sink - m)
            inv.append(1.0 / den)
            acc = acc + jnp.dot(p.astype(BF16), vband, preferred_element_type=F32)
        o = (acc * jnp.where(low, inv[0], inv[1])).astype(BF16)
        o_ref[0, :, (2 * hk) * LANES:(2 * hk + 1) * LANES] = o[:A_BLOCK]
        o_ref[0, :, (2 * hk + 1) * LANES:(2 * hk + 2) * LANES] = o[A_BLOCK:]


def _window_attn(qa, ka, va, sink, bias):
    b, s, _ = qa.shape
    nb = s // A_BLOCK
    kv_blk = (1, 2 * A_KV_HEADS, A_BLOCK, LANES)
    prev = pl.BlockSpec(kv_blk, lambda bi, n: (bi, 0, jnp.maximum(n - 1, 0), 0))
    cur = pl.BlockSpec(kv_blk, lambda bi, n: (bi, 0, n, 0))
    nxt = pl.BlockSpec(kv_blk, lambda bi, n: (bi, 0, jnp.minimum(n + 1, nb - 1), 0))

    def edge(bi, n):
        return ((n == 0).astype(jnp.int32) + 2 * (n == nb - 1).astype(jnp.int32), 0, 0, 0)

    return pl.pallas_call(
        _window_body,
        grid=(b, nb),
        in_specs=[
            pl.BlockSpec(memory_space=pltpu.SMEM),
            pl.BlockSpec((1, A_BLOCK, A_WIDTH), lambda bi, n: (bi, n, 0)),
            prev, cur, nxt, prev, cur, nxt,
            pl.BlockSpec((1,) + bias.shape[1:], edge),
        ],
        out_specs=pl.BlockSpec((1, A_BLOCK, A_WIDTH), lambda bi, n: (bi, n, 0)),
        out_shape=jax.ShapeDtypeStruct((b, s, A_WIDTH), BF16),
        compiler_params=pltpu.CompilerParams(dimension_semantics=("parallel", "parallel"),
                                             vmem_limit_bytes=VMEM_LIMIT),
        name="window_attn",
    )(sink, qa, ka, ka, ka, va, va, va, bias)


def _mla_body(q_ref, k_ref, v_ref, o_ref):
    q = q_ref[0, 0]
    tq = q.shape[0]
    s_len = k_ref.shape[2]
    ck = min(CK_MLA, s_len)
    m = jnp.full((tq, 1), -jnp.inf, F32)
    l = jnp.zeros((tq, 1), F32)
    acc = jnp.zeros((tq, B_V_DIM), F32)
    for c in range(s_len // ck):
        k_c = k_ref[0, 0, c * ck:(c + 1) * ck, :]
        v_c = v_ref[0, 0, c * ck:(c + 1) * ck, :]
        s = lax.dot_general(q, k_c, (((1,), (1,)), ((), ())), preferred_element_type=F32)
        m_new = jnp.maximum(m, jnp.max(s, axis=-1, keepdims=True))
        alpha = jnp.exp(m - m_new)
        p = jnp.exp(s - m_new)
        l = alpha * l + jnp.sum(p, axis=-1, keepdims=True)
        acc = alpha * acc + jnp.dot(p.astype(BF16), v_c, preferred_element_type=F32)
        m = m_new
    o_ref[0] = (acc / l).astype(BF16)


def _mla_attn(qb, kb, vb):
    b, nh, s, _ = qb.shape
    tq = min(TQ_MLA, s)
    return pl.pallas_call(
        _mla_body,
        grid=(b, nh, s // tq),
        in_specs=[
            pl.BlockSpec((1, 1, tq, B_QK_PAD), lambda bi, h, i: (bi, h, i, 0)),
            pl.BlockSpec((1, 1, s, B_QK_PAD), lambda bi, h, i: (bi, h, 0, 0)),
            pl.BlockSpec((1, 1, s, B_V_DIM), lambda bi, h, i: (bi, h, 0, 0)),
        ],
        out_specs=pl.BlockSpec((1, tq, B_V_DIM), lambda bi, h, i: (bi, i, h)),
        out_shape=jax.ShapeDtypeStruct((b, s, B_WIDTH), BF16),
        compiler_params=pltpu.CompilerParams(dimension_semantics=("parallel", "parallel", "parallel"),
                                             vmem_limit_bytes=VMEM_LIMIT),
        name="mla_attn",
    )(qb, kb, vb)


def _out_router_body(ya_ref, yb_ref, x_ref, ga_ref, gb_ref, wo_ref, gm_ref, wr_ref, br_ref,
                     x1_ref, xm_ref, eidx_ref, gate_ref, rank_ref, cnt_ref, run_ref):
    i = pl.program_id(0)

    @pl.when(i == 0)
    def _():
        run_ref[...] = jnp.zeros_like(run_ref)

    na = _rms(ya_ref[...].astype(F32), ga_ref[...]).astype(BF16)
    nb = _rms(yb_ref[...].astype(F32), gb_ref[...]).astype(BF16)
    att = jnp.dot(na, wo_ref[:A_WIDTH, :], preferred_element_type=F32)
    att = att + jnp.dot(nb, wo_ref[A_WIDTH:, :], preferred_element_type=F32)
    x1 = x_ref[...] + att
    x1_ref[...] = x1
    hn = _rms(x1, gm_ref[...])
    _store_token_rows(xm_ref, 0, hn)
    tm = hn.shape[0]
    hi = hn.astype(BF16)
    lo = (hn - hi.astype(F32)).astype(BF16)
    prod = jnp.dot(jnp.concatenate([hi, lo], axis=0), wr_ref[...], preferred_element_type=F32)
    logits = (prod[:tm, :N_EXPERTS] + prod[:tm, N_EXPERTS:] + prod[tm:, :N_EXPERTS] + prod[tm:, N_EXPERTS:]
              + br_ref[...])
    lane = lax.broadcasted_iota(jnp.int32, (tm, N_EXPERTS), 1)
    work = logits
    sel = jnp.zeros((tm, N_EXPERTS), F32)
    hots, vals, idxs = [], [], []
    for _k in range(TOP_K):
        mx = jnp.max(work, axis=-1, keepdims=True)
        idx = jnp.min(jnp.where(work == mx, lane, N_EXPERTS), axis=-1, keepdims=True)
        hot = lane == idx
        hots.append(hot)
        vals.append(mx)
        idxs.append(idx)
        sel = sel + hot.astype(F32)
        work = jnp.where(hot, -jnp.inf, work)
    exps = [jnp.exp(v - vals[0]) for v in vals]
    den = exps[0] + exps[1] + exps[2] + exps[3]
    r_i = lax.broadcasted_iota(jnp.int32, (tm, tm), 0)
    c_i = lax.broadcasted_iota(jnp.int32, (tm, tm), 1)
    tri = (c_i < r_i).astype(BF16)
    before = jnp.dot(tri, sel.astype(BF16), preferred_element_type=F32) + run_ref[...]
    lane4 = lax.broadcasted_iota(jnp.int32, (tm, TOP_K), 1)
    eidx = jnp.zeros((tm, TOP_K), jnp.int32)
    gate = jnp.zeros((tm, TOP_K), F32)
    rank = jnp.zeros((tm, TOP_K), jnp.int32)
    for k in range(TOP_K):
        rk = jnp.sum(jnp.where(hots[k], before, 0.0), axis=-1, keepdims=True).astype(jnp.int32)
        eidx = jnp.where(lane4 == k, idxs[k], eidx)
        gate = jnp.where(lane4 == k, exps[k] / den, gate)
        rank = jnp.where(lane4 == k, rk, rank)
    eidx_ref[...] = eidx
    gate_ref[...] = gate
    rank_ref[...] = rank
    run = run_ref[...] + jnp.sum(sel, axis=0, keepdims=True)
    run_ref[...] = run
    cnt_ref[...] = run.astype(jnp.int32)


def _out_router(ya, yb, x2d, ga, gb, wo, gm, wr, br):
    t, d = x2d.shape
    tm = min(TM_OUT, t)
    sds = jax.ShapeDtypeStruct

    def row(w):
        return pl.BlockSpec((tm, w), lambda i: (i, 0))

    return pl.pallas_call(
        _out_router_body,
        grid=(t // tm,),
        in_specs=[row(A_WIDTH), row(B_WIDTH), row(d), _const_spec(ga.shape), _const_spec(gb.shape),
                  _const_spec(wo.shape), _const_spec(gm.shape), _const_spec(wr.shape), _const_spec(br.shape)],
        out_specs=(row(d), pl.BlockSpec((tm * ROW_PITCH, LANES), lambda i: (i, 0)),
                   row(TOP_K), row(TOP_K), row(TOP_K),
                   pl.BlockSpec((1, N_EXPERTS), lambda i: (0, 0))),
        out_shape=(sds((t, d), F32), sds((t * ROW_PITCH, LANES), F32), sds((t, TOP_K), jnp.int32),
                   sds((t, TOP_K), F32),
                   sds((t, TOP_K), jnp.int32), sds((1, N_EXPERTS), jnp.int32)),
        scratch_shapes=[pltpu.VMEM((1, N_EXPERTS), F32)],
        compiler_params=pltpu.CompilerParams(dimension_semantics=("arbitrary",),
                                             vmem_limit_bytes=VMEM_LIMIT),
        name="out_router",
    )(ya, yb, x2d, ga, gb, wo, gm, wr, br)


def _token_copy(src_ref, src_tok, dst_ref, dst_tok, sem, ntok=1):
    return _rows_copy(src_ref, src_tok * ROW_PITCH, dst_ref, dst_tok * ROW_PITCH, sem, ntok)


def _rows_copy(src_ref, src_row, dst_ref, dst_row, sem, ntok=1):
    rows = ntok * ROW_PITCH
    return pltpu.make_async_copy(src_ref.at[pl.ds(src_row, rows)], dst_ref.at[pl.ds(dst_row, rows)], sem)


def _slot_rows_body(start_ref, eidx_ref, rank_ref, o_ref):
    e = eidx_ref[...]
    base = jnp.zeros_like(e)
    for ex in range(N_EXPERTS):
        base = jnp.where(e == ex, start_ref[ex], base)
    o_ref[...] = (base + rank_ref[...]) * ROW_PITCH


def _slot_rows(start, eidx, rank):
    n = eidx.shape[0]
    shape2 = (n // LANES, LANES)
    out = pl.pallas_call(
        _slot_rows_body,
        in_specs=[pl.BlockSpec(memory_space=pltpu.SMEM), pl.BlockSpec(shape2, lambda: (0, 0)),
                  pl.BlockSpec(shape2, lambda: (0, 0))],
        out_specs=pl.BlockSpec(shape2, lambda: (0, 0)),
        out_shape=jax.ShapeDtypeStruct(shape2, jnp.int32),
        name="slot_rows",
    )(start, eidx.reshape(shape2), rank.reshape(shape2))
    return out.reshape(n)


def _zero_fill_pads(pad_start_ref, pad_len_ref, nact_ref, xs_ref, zero_ref, zsem):
    bm = zero_ref.shape[0] // ROW_PITCH
    nblk = xs_ref.shape[0] // (bm * ROW_PITCH)
    zero_ref[...] = jnp.zeros_like(zero_ref)

    def per_expert(e, c):
        off = pad_start_ref[e]
        n = pad_len_ref[e]
        p = bm // 2
        while p >= 1:
            take = (n & p) != 0

            @pl.when(take)
            def _(p=p, off=off):
                cp = _token_copy(zero_ref, 0, xs_ref, off, zsem, p)
                cp.start()
                cp.wait()

            off = off + jnp.where(take, p, 0)
            p //= 2
        return c

    lax.fori_loop(0, N_EXPERTS, per_expert, 0)

    def tail(j, c):
        cp = _token_copy(zero_ref, 0, xs_ref, j * bm, zsem, bm)
        cp.start()
        cp.wait()
        return c

    lax.fori_loop(nact_ref[0], nblk, tail, 0)


def _dispatch_body(pad_start_ref, pad_len_ref, nact_ref, slot_ref, xm_ref, xs_ref, zero_ref, sem, zsem):
    tm = slot_ref.shape[0] // TOP_K

    @pl.when(pl.program_id(0) == 0)
    def _():
        _zero_fill_pads(pad_start_ref, pad_len_ref, nact_ref, xs_ref, zero_ref, zsem)

    def issue(i, c):
        for k in range(TOP_K):
            _rows_copy(xm_ref, i * ROW_PITCH, xs_ref, slot_ref[i * TOP_K + k], sem).start(priority=k % 2)
        return c

    lax.fori_loop(0, tm, issue, 0, unroll=2)

    def drain(i, c):
        for _ in range(WAIT_UNROLL):
            _token_copy(xm_ref, 0, xs_ref, 0, sem).wait()
        return c

    lax.fori_loop(0, tm * TOP_K // WAIT_UNROLL, drain, 0)


def _dispatch(pad_start, pad_len, nact, slot_rows, xm, n_slots):
    t = xm.shape[0] // ROW_PITCH
    tm = min(TM_DISPATCH, t)
    grid_spec = pltpu.PrefetchScalarGridSpec(
        num_scalar_prefetch=3,
        grid=(t // tm,),
        in_specs=[pl.BlockSpec((tm * TOP_K,), lambda i, *_: (i,), memory_space=pltpu.SMEM),
                  pl.BlockSpec((tm * ROW_PITCH, LANES), lambda i, *_: (i, 0))],
        out_specs=pl.BlockSpec(memory_space=pl.ANY),
        scratch_shapes=[pltpu.VMEM((MOE_BM * ROW_PITCH, LANES), F32), pltpu.SemaphoreType.DMA(()),
                        pltpu.SemaphoreType.DMA(())],
    )
    return pl.pallas_call(
        _dispatch_body,
        grid_spec=grid_spec,
        out_shape=jax.ShapeDtypeStruct((n_slots * ROW_PITCH, LANES), F32),
        compiler_params=pltpu.CompilerParams(dimension_semantics=("arbitrary",), has_side_effects=True,
                                             vmem_limit_bytes=VMEM_LIMIT),
        name="dispatch",
    )(pad_start, pad_len, nact, slot_rows, xm)


STEP_UP, STEP_DOWN, STEP_TAIL, STEP_IDLE = 0, 1, 2, 3
NF = D_FF // MOE_TF
PASS_BLOCKS = MOE_RM // MOE_BM


def _swiglu(hg, hu):
    gate = jnp.minimum(hg, SWIGLU_LIMIT)
    up = jnp.clip(hu, -SWIGLU_LIMIT, SWIGLU_LIMIT)
    glu = gate / (1.0 + jnp.exp(-SWIGLU_ALPHA * gate))
    return (up + 1.0) * glu


def _experts_body(e_ref, f_ref, kind_ref, blk_ref, blk0_ref, nsub_ref,
                  xs_ref, wg_ref, wu_ref, bg_ref, bu_ref, wd_ref, bd_ref,
                  y_ref, xb_ref, h_ref, wdb_ref, stg_ref, sem):
    t = pl.program_id(0)
    kind = kind_ref[t]
    f = f_ref[t]
    nsub = nsub_ref[t]

    def x_copy(j, slot):
        rows = MOE_SUB * ROW_PITCH
        first = pl.multiple_of((blk0_ref[t] * MOE_BM + j * MOE_SUB) * ROW_PITCH, SUBLANES)
        return pltpu.make_async_copy(xs_ref.at[pl.ds(first, rows)], stg_ref.at[slot], sem.at[slot])

    @pl.when(kind == STEP_UP)
    def _up():
        @pl.when(f == 0)
        def _load_rows():
            x_copy(0, 0).start()

            def body(j, c):
                slot = j % 2

                @pl.when(j + 1 < nsub)
                def _():
                    x_copy(j + 1, 1 - slot).start()

                x_copy(j, slot).wait()
                rows = pl.ds(pl.multiple_of(j * MOE_SUB, MOE_SUB), MOE_SUB)
                for jt, tile in enumerate(_load_token_rows(stg_ref.at[slot], MOE_SUB)):
                    xb_ref[rows, jt * LANES:(jt + 1) * LANES] = tile.astype(BF16)
                return c

            lax.fori_loop(0, nsub, body, 0)

        wdb_ref[pl.ds(pl.multiple_of(f * MOE_TF, MOE_TF), MOE_TF), :] = wd_ref[0].astype(BF16)

        def up_rows(row0, nrows):
            x = xb_ref[pl.ds(row0, nrows), :]
            hg = jnp.dot(x, wg_ref[0].astype(BF16), preferred_element_type=F32) + bg_ref[0]
            hu = jnp.dot(x, wu_ref[0].astype(BF16), preferred_element_type=F32) + bu_ref[0]
            act = _swiglu(hg, hu).astype(BF16)
            for fs in range(NF):
                @pl.when(f == fs)
                def _(fs=fs):
                    h_ref[pl.ds(row0, nrows), fs * MOE_TF:(fs + 1) * MOE_TF] = act

        quad = 4 * MOE_SUB
        nquad = nsub // 4
        rem = nsub % 4

        def quad_rows(i, c):
            up_rows(pl.multiple_of(i * quad, quad), quad)
            return c

        lax.fori_loop(0, nquad, quad_rows, 0)

        @pl.when(rem >= 2)
        def _():
            up_rows(pl.multiple_of(nquad * quad, quad), 2 * MOE_SUB)

        @pl.when(rem % 2 == 1)
        def _():
            up_rows(pl.multiple_of(nquad * quad + (rem // 2) * 2 * MOE_SUB, MOE_SUB), MOE_SUB)

    @pl.when(kind == STEP_DOWN)
    def _down():
        row0 = pl.multiple_of((blk_ref[t] - blk0_ref[t]) * MOE_BM, MOE_BM)

        def down_rows(nrows):
            hrows = h_ref[pl.ds(row0, nrows), :]
            return jnp.dot(hrows, wdb_ref[...], preferred_element_type=F32) + bd_ref[0]

        @pl.when(nsub == 2)
        def _():
            _store_token_rows(y_ref, 0, down_rows(MOE_BM))

        @pl.when(nsub == 1)
        def _():
            _store_token_rows(y_ref, 0, down_rows(MOE_SUB))
            y_ref[MOE_SUB * ROW_PITCH:, :] = jnp.zeros(((MOE_BM - MOE_SUB) * ROW_PITCH, LANES), F32)

    @pl.when(kind == STEP_TAIL)
    def _tail():
        y_ref[...] = jnp.zeros_like(y_ref)


def _schedule_body(n_blocks, cnt_ref, e_ref, f_ref, kind_ref, blk_ref, blk0_ref, nsub_ref,
                   start_ref, pad_start_ref, pad_len_ref, nact_ref):
    n_steps = e_ref.shape[0]
    sub_per_blk = MOE_BM // MOE_SUB
    zero = jnp.int32(0)

    def put(idx, e, f, kind, blk, blk0, nsub):
        e_ref[idx] = e
        f_ref[idx] = f
        kind_ref[idx] = kind
        blk_ref[idx] = blk
        blk0_ref[idx] = blk0
        nsub_ref[idx] = nsub

    def per_expert(e, carry):
        t, blk, last_e = carry
        n = cnt_ref[e]
        nblk = (n + (MOE_BM - 1)) // MOE_BM
        start_ref[e] = blk * MOE_BM
        pad_start_ref[e] = blk * MOE_BM + n
        pad_len_ref[e] = nblk * MOE_BM - n

        def per_pass(p, t2):
            b0 = blk + p * PASS_BLOCKS
            nb = jnp.minimum(nblk - p * PASS_BLOCKS, PASS_BLOCKS)
            nsub = (jnp.minimum(n - p * MOE_RM, MOE_RM) + (MOE_SUB - 1)) // MOE_SUB
            for f in range(NF):
                put(t2 + f, e, f, STEP_UP, b0, b0, nsub)

            def per_blk(j, c):
                put(t2 + NF + j, e, NF - 1, STEP_DOWN, b0 + j, b0,
                    jnp.clip(nsub - j * sub_per_blk, 1, sub_per_blk))
                return c

            lax.fori_loop(0, nb, per_blk, 0)
            return t2 + NF + nb

        t = lax.fori_loop(0, (nblk + (PASS_BLOCKS - 1)) // PASS_BLOCKS, per_pass, t)
        return t, blk + nblk, jnp.where(n > 0, e, last_e)

    t, nact, last_e = lax.fori_loop(0, N_EXPERTS, per_expert, (zero, zero, zero))
    nact_ref[0] = nact

    def spare(i, c):
        is_tail = i < n_blocks - nact
        blk = jnp.where(is_tail, nact + i, n_blocks - 1)
        put(t + i, last_e, NF - 1, jnp.where(is_tail, STEP_TAIL, STEP_IDLE), blk, blk, 1)
        return c

    lax.fori_loop(0, n_steps - t, spare, 0)


def _expert_schedule(counts, n_blocks, n_assign):
    n_pass_max = N_EXPERTS + n_assign // MOE_RM
    n_steps = n_pass_max * NF + n_blocks
    smem = pl.BlockSpec(memory_space=pltpu.SMEM)
    i32 = jnp.int32
    out_shape = tuple(jax.ShapeDtypeStruct((n,), i32) for n in (n_steps,) * 6 + (N_EXPERTS,) * 3 + (1,))
    outs = pl.pallas_call(
        functools.partial(_schedule_body, n_blocks),
        in_specs=[smem],
        out_specs=tuple(smem for _ in out_shape),
        out_shape=out_shape,
        name="schedule",
    )(counts)
    return outs[:6], outs[6], outs[7], outs[8], outs[9]


def _experts(tabs, xs, w_up, b_up, w_down, b_down):
    d = D_MODEL
    n_steps = tabs[0].shape[0]
    grid_spec = pltpu.PrefetchScalarGridSpec(
        num_scalar_prefetch=len(tabs),
        grid=(n_steps,),
        in_specs=[
            pl.BlockSpec(memory_space=pl.ANY),
            pl.BlockSpec((1, d, MOE_TF), lambda t, e, f, *_: (e[t], 0, f[t])),
            pl.BlockSpec((1, d, MOE_TF), lambda t, e, f, *_: (e[t], 0, NF + f[t])),
            pl.BlockSpec((1, 1, MOE_TF), lambda t, e, f, *_: (e[t], 0, f[t])),
            pl.BlockSpec((1, 1, MOE_TF), lambda t, e, f, *_: (e[t], 0, NF + f[t])),
            pl.BlockSpec((1, MOE_TF, d), lambda t, e, f, *_: (e[t], f[t], 0)),
            pl.BlockSpec((1, 1, d), lambda t, e, f, *_: (e[t], 0, 0)),
        ],
        out_specs=pl.BlockSpec((MOE_BM * ROW_PITCH, LANES), lambda t, e, f, kind, blk, *_: (blk[t], 0)),
        scratch_shapes=[
            pltpu.VMEM((MOE_RM, d), BF16),
            pltpu.VMEM((MOE_RM, D_FF), BF16),
            pltpu.VMEM((D_FF, d), BF16),
            pltpu.VMEM((2, MOE_SUB * ROW_PITCH, LANES), F32),
            pltpu.SemaphoreType.DMA((2,)),
        ],
    )
    return pl.pallas_call(
        _experts_body,
        grid_spec=grid_spec,
        out_shape=jax.ShapeDtypeStruct(xs.shape, F32),
        compiler_params=pltpu.CompilerParams(dimension_semantics=("arbitrary",),
                                             vmem_limit_bytes=VMEM_LIMIT),
        name="experts",
    )(*tabs, xs, w_up, w_up, b_up, b_up, w_down, b_down)


def _combine_body(slot_ref, slot_nx_ref, gate_ref, x1_ref, gf_ref, ys_ref, o_ref, buf_ref, sem):
    step = pl.program_id(0)
    slot = step % 2
    tm = x1_ref.shape[0]

    def gather(rows_ref, s):
        def issue(i, c):
            for k in range(TOP_K):
                _rows_copy(ys_ref, rows_ref[i * TOP_K + k], buf_ref.at[s, k], i * ROW_PITCH,
                           sem.at[s]).start(priority=k % 2)
            return c

        lax.fori_loop(0, tm, issue, 0, unroll=2)

    @pl.when(step == 0)
    def _():
        gather(slot_ref, 0)

    for s in range(2):
        @pl.when((step + 1 < pl.num_programs(0)) & (slot == 1 - s))
        def _(s=s):
            gather(slot_nx_ref, s)

    def drain(i, c):
        for _ in range(WAIT_UNROLL):
            _token_copy(ys_ref, 0, buf_ref.at[slot, 0], 0, sem.at[slot]).wait()
        return c

    lax.fori_loop(0, tm * TOP_K // WAIT_UNROLL, drain, 0)
    gate = gate_ref[...]
    tiles = [x1_ref[:, j * LANES:(j + 1) * LANES] for j in range(ROW_TILES)]
    for k in range(TOP_K):
        g = gate[:, k:k + 1]
        for j, tile in enumerate(_load_token_rows(buf_ref.at[slot, k], tm)):
            tiles[j] = tiles[j] + g * tile
    o_ref[...] = _rms(jnp.concatenate(tiles, axis=1), gf_ref[...])


def _combine(slot_rows, gates, x1, gf, ys):
    t, d = x1.shape
    tm = min(TM_COMBINE, t)
    n = t // tm
    idx_blk = (tm * TOP_K,)
    return pl.pallas_call(
        _combine_body,
        grid=(n,),
        in_specs=[pl.BlockSpec(idx_blk, lambda i: (i,), memory_space=pltpu.SMEM),
                  pl.BlockSpec(idx_blk, lambda i: (jnp.minimum(i + 1, n - 1),), memory_space=pltpu.SMEM),
                  pl.BlockSpec((tm, TOP_K), lambda i: (i, 0)),
                  pl.BlockSpec((tm, d), lambda i: (i, 0)),
                  _const_spec(gf.shape),
                  pl.BlockSpec(memory_space=pl.ANY)],
        out_specs=pl.BlockSpec((tm, d), lambda i: (i, 0)),
        scratch_shapes=[pltpu.VMEM((2, TOP_K, tm * ROW_PITCH, LANES), F32), pltpu.SemaphoreType.DMA((2,))],
        out_shape=jax.ShapeDtypeStruct((t, d), F32),
        compiler_params=pltpu.CompilerParams(dimension_semantics=("arbitrary",),
                                             vmem_limit_bytes=VMEM_LIMIT),
        name="combine",
    )(slot_rows, slot_rows, gates, x1, gf, ys)


def _rope_tables(s):
    pos = np.arange(s, dtype=np.float32)
    inv_freq = np.power(np.float32(ROPE_THETA), -np.arange(0, B_ROPE_DIM, 2, dtype=np.float32) / B_ROPE_DIM)
    ang = (pos[:, None] * inv_freq[None, :]).astype(np.float32)
    cos, sin = np.cos(ang), np.sin(ang)
    pad = LANES - B_ROPE_DIM
    cos = np.concatenate([cos, cos, np.ones((s, pad), np.float32)], axis=1)
    sin = np.concatenate([sin, sin, np.zeros((s, pad), np.float32)], axis=1)
    return cos.astype(np.float32), sin.astype(np.float32)


def _window_bias():
    qi = np.arange(A_BLOCK)[:, None]
    kj = np.arange(3 * A_BLOCK)[None, :]
    dist = np.abs(qi + A_BLOCK - kj)
    slopes = np.power(np.float32(2.0), -8.0 * np.arange(1, A_HEADS + 1, dtype=np.float32) / A_HEADS)
    bias = -slopes[:, None, None] * dist.astype(np.float32)[None]
    bias = np.where((dist <= WINDOW)[None], bias, -np.inf)
    bias = bias.reshape(A_KV_HEADS, 2, 2, A_BLOCK, 3 * A_BLOCK).transpose(0, 2, 1, 3, 4)
    bias = bias.reshape(2 * A_KV_HEADS, 2 * A_BLOCK, 3 * A_BLOCK)
    no_prev = (kj < A_BLOCK)[None]
    no_next = (kj >= 2 * A_BLOCK)[None]
    ninf = -np.inf
    return np.stack([bias, np.where(no_prev, ninf, bias), np.where(no_next, ninf, bias),
                     np.where(no_prev | no_next, ninf, bias)]).astype(np.float32)


def _layer(x, attn_norm, w_in, a_sink, b_q_norm, b_w_uq, b_kv_norm, b_w_ukv, out_norm_a, out_norm_b, w_o,
           mlp_norm, w_router, b_router, w_up, b_up, w_down, b_down):
    b, s, d = x.shape
    t = b * s
    win = jnp.pad(w_in, ((0, 0), (0, IN_COLS_PAD - w_in.shape[1]))).astype(BF16)
    wq = b_w_uq.reshape(B_Q_RANK, B_HEADS, B_NOPE_DIM + B_ROPE_DIM)
    wq_pe = jnp.pad(wq[:, :, B_NOPE_DIM:], ((0, 0), (0, 0), (0, LANES - B_ROPE_DIM)))
    wuq = jnp.concatenate([wq[:, :, :B_NOPE_DIM].reshape(B_Q_RANK, -1), wq_pe.reshape(B_Q_RANK, -1)],
                          axis=1).astype(BF16)
    wkv = b_w_ukv.reshape(B_KV_RANK, B_HEADS, B_NOPE_DIM + B_V_DIM)
    wukv = jnp.concatenate([wkv[:, :, :B_NOPE_DIM].reshape(B_KV_RANK, -1),
                            wkv[:, :, B_NOPE_DIM:].reshape(B_KV_RANK, -1)], axis=1).astype(BF16)
    cos, sin = _rope_tables(s)

    qa, ka, va, qb, kb, vb = _prologue(x, attn_norm[None], win, b_q_norm[None], wuq, b_kv_norm[None], wukv,
                                       cos, sin)
    ya = _window_attn(qa, ka, va, a_sink, _window_bias())
    yb = _mla_attn(qb, kb, vb)
    wr_hi = w_router.astype(BF16)
    wr_lo = (w_router - wr_hi.astype(F32)).astype(BF16)
    x1, xm, eidx, gates, rank, counts = _out_router(
        ya.reshape(t, A_WIDTH), yb.reshape(t, B_WIDTH), x.reshape(t, d), out_norm_a[None], out_norm_b[None],
        w_o.astype(BF16), mlp_norm[None], jnp.concatenate([wr_hi, wr_lo], axis=1), b_router[None])

    nblk = (t * TOP_K + N_EXPERTS * (MOE_BM - 1) + MOE_BM - 1) // MOE_BM
    tabs, start, pad_start, pad_len, nact = _expert_schedule(counts.reshape(N_EXPERTS), nblk, t * TOP_K)
    slot_rows = _slot_rows(start, eidx.reshape(t * TOP_K), rank.reshape(t * TOP_K))
    xs = _dispatch(pad_start, pad_len, nact, slot_rows, xm, nblk * MOE_BM)
    ys = _experts(tabs, xs, w_up, b_up[:, None, :], w_down, b_down[:, None, :])
    return x1, slot_rows, gates, ys


def kernel(x, attn_norm, w_in, a_sink, b_q_norm, b_w_uq, b_kv_norm, b_w_ukv, out_norm_a, out_norm_b, w_o,
           mlp_norm, w_router, b_router, w_up, b_up, w_down, b_down, final_norm):
    b, s, d = x.shape
    assert d == D_MODEL and s % min(TQ_MLA, s) == 0 and s % min(CK_MLA, s) == 0 and s % TM_PRO == 0 and attn_norm.shape[0] == 1
    x1, slot_rows, gates, ys = _layer(
        x, attn_norm[0], w_in[0], a_sink[0], b_q_norm[0], b_w_uq[0], b_kv_norm[0], b_w_ukv[0], out_norm_a[0],
        out_norm_b[0], w_o[0], mlp_norm[0], w_router[0], b_router[0], w_up[0], b_up[0], w_down[0], b_down[0])
    out = _combine(slot_rows, gates, x1, final_norm[None], ys)
    return out.reshape(b, s, d)
```

```python
import functools

import jax
import numpy as np
import jax.numpy as jnp
from jax import lax
from jax.experimental import pallas as pl
from jax.experimental.pallas import tpu as pltpu

D_MODEL = 2048
A_HEADS, A_KV_HEADS, A_HEAD_DIM = 16, 4, 64
A_GROUP = A_HEADS // A_KV_HEADS
WINDOW = 128
A_BLOCK = 128
B_HEADS, B_Q_RANK, B_KV_RANK = 8, 512, 256
B_NOPE_DIM, B_ROPE_DIM, B_V_DIM = 128, 64, 128
ROPE_THETA = 10000.0
A_WIDTH = A_HEADS * A_HEAD_DIM
B_WIDTH = B_HEADS * B_V_DIM
A_KV_COLS = A_KV_HEADS * A_HEAD_DIM
N_EXPERTS, TOP_K, D_FF = 32, 4, 2048
SWIGLU_ALPHA, SWIGLU_LIMIT = 1.702, 7.0
EPS = 1e-5

LANES = 128
SUBLANES = 8
B_QK_PAD = 2 * LANES
IN_COLS_PAD = A_WIDTH + 2 * A_KV_COLS + B_Q_RANK + B_KV_RANK + LANES
VMEM_LIMIT = 56 * 1024 * 1024
ROW_TILES = D_MODEL // LANES
ROW_PITCH = ROW_TILES + 1

TM_PRO = 256
TQ_MLA = 1024
CK_MLA = 1024
TM_OUT = 512
TM_DISPATCH = 512
TM_COMBINE = 128
MOE_BM = 512
MOE_SUB = 256
MOE_RM = 2048
MOE_TF = 256
WAIT_UNROLL = 16

BF16 = jnp.bfloat16
F32 = jnp.float32


def _rms(x, g):
    return x * lax.rsqrt(jnp.mean(x * x, axis=-1, keepdims=True) + EPS) * g


def _const_spec(shape):
    nd = len(shape)
    return pl.BlockSpec(shape, lambda *_: (0,) * nd, pipeline_mode=pl.Buffered(1))


def _store_token_rows(ref, first, val):
    n = val.shape[0]
    for j in range(ROW_TILES):
        ref[pl.ds(first * ROW_PITCH + j, n, stride=ROW_PITCH), :] = val[:, j * LANES:(j + 1) * LANES]
    ref[pl.ds(first * ROW_PITCH + ROW_TILES, n, stride=ROW_PITCH), :] = jnp.zeros((n, LANES), val.dtype)


def _load_token_rows(ref, n):
    return [ref[pl.ds(j, n, stride=ROW_PITCH), :] for j in range(ROW_TILES)]


def _rope(x, cos, sin):
    lane = lax.broadcasted_iota(jnp.int32, x.shape, 1)
    up = pltpu.roll(x, LANES - B_ROPE_DIM // 2, 1)
    dn = pltpu.roll(x, B_ROPE_DIM // 2, 1)
    sw = jnp.where(lane < B_ROPE_DIM // 2, -up, jnp.where(lane < B_ROPE_DIM, dn, 0.0))
    return x * cos + sw * sin


def _prologue_body(x_ref, g_ref, win_ref, qn_ref, wuq_ref, kvn_ref, wukv_ref, cos_ref, sin_ref,
                   qa_ref, ka_ref, va_ref, qb_ref, kb_ref, vb_ref, winb_ref):
    @pl.when((pl.program_id(0) == 0) & (pl.program_id(1) == 0))
    def _():
        n_in = win_ref.shape[1]
        winb_ref[:, :n_in] = win_ref[...].astype(BF16)
        winb_ref[:, n_in:] = jnp.zeros((winb_ref.shape[0], IN_COLS_PAD - n_in), BF16)

    x = x_ref[0]
    xn = _rms(x, g_ref[...]).astype(BF16)
    proj = jnp.dot(xn, winb_ref[...], preferred_element_type=F32)
    c0 = A_WIDTH
    c1 = c0 + A_KV_COLS
    c2 = c1 + A_KV_COLS
    c3 = c2 + B_Q_RANK
    c4 = c3 + B_KV_RANK
    qa_ref[0] = (proj[:, :A_WIDTH] * A_HEAD_DIM ** -0.5).astype(BF16)
    low = lax.broadcasted_iota(jnp.int32, (proj.shape[0], LANES), 1) < A_HEAD_DIM
    for src, dst in ((c0, ka_ref), (c1, va_ref)):
        for pair in range(A_KV_HEADS // 2):
            two = proj[:, src + pair * LANES:src + (pair + 1) * LANES]
            swapped = pltpu.roll(two, A_HEAD_DIM, 1)
            dst[0, 4 * pair + 0] = jnp.where(low, two, 0.0).astype(BF16)
            dst[0, 4 * pair + 1] = jnp.where(low, 0.0, swapped).astype(BF16)
            dst[0, 4 * pair + 2] = jnp.where(low, swapped, 0.0).astype(BF16)
            dst[0, 4 * pair + 3] = jnp.where(low, 0.0, two).astype(BF16)
    cq = _rms(proj[:, c2:c3], qn_ref[...]).astype(BF16)
    q = jnp.dot(cq, wuq_ref[...], preferred_element_type=F32)
    ckv = _rms(proj[:, c3:c4], kvn_ref[...]).astype(BF16)
    kv = jnp.dot(ckv, wukv_ref[...], preferred_element_type=F32)
    cos = cos_ref[...]
    sin = sin_ref[...]
    kpe = _rope(proj[:, c4:c4 + LANES], cos, sin).astype(BF16)
    b_scale = (B_NOPE_DIM + B_ROPE_DIM) ** -0.5
    hw = B_HEADS * LANES
    for h in range(B_HEADS):
        sl = slice(h * LANES, (h + 1) * LANES)
        qb_ref[0, h, :, :LANES] = (q[:, sl] * b_scale).astype(BF16)
        qpe = _rope(q[:, hw + h * LANES:hw + (h + 1) * LANES], cos, sin)
        qb_ref[0, h, :, LANES:] = (qpe * b_scale).astype(BF16)
        kb_ref[0, h, :, :LANES] = kv[:, sl].astype(BF16)
        kb_ref[0, h, :, LANES:] = kpe
        vb_ref[0, h] = kv[:, hw + h * LANES:hw + (h + 1) * LANES].astype(BF16)


def _prologue(x, g, win, qn, wuq, kvn, wukv, cos, sin):
    b, s, d = x.shape
    tm = TM_PRO
    grid = (b, s // tm)
    sds = jax.ShapeDtypeStruct
    out_shape = (
        sds((b, s, A_WIDTH), BF16),
        sds((b, 2 * A_KV_HEADS, s, LANES), BF16),
        sds((b, 2 * A_KV_HEADS, s, LANES), BF16),
        sds((b, B_HEADS, s, B_QK_PAD), BF16),
        sds((b, B_HEADS, s, B_QK_PAD), BF16),
        sds((b, B_HEADS, s, B_V_DIM), BF16),
    )

    def hspec(nh, w):
        return pl.BlockSpec((1, nh, tm, w), lambda bi, i: (bi, 0, i, 0))

    return pl.pallas_call(
        _prologue_body,
        grid=grid,
        in_specs=[
            pl.BlockSpec((1, tm, d), lambda bi, i: (bi, i, 0)),
            _const_spec(g.shape), _const_spec(win.shape), _const_spec(qn.shape), _const_spec(wuq.shape),
            _const_spec(kvn.shape), _const_spec(wukv.shape),
            pl.BlockSpec((tm, LANES), lambda bi, i: (i, 0)),
            pl.BlockSpec((tm, LANES), lambda bi, i: (i, 0)),
        ],
        out_specs=(pl.BlockSpec((1, tm, A_WIDTH), lambda bi, i: (bi, i, 0)),
                   hspec(2 * A_KV_HEADS, LANES), hspec(2 * A_KV_HEADS, LANES),
                   hspec(B_HEADS, B_QK_PAD), hspec(B_HEADS, B_QK_PAD), hspec(B_HEADS, B_V_DIM)),
        out_shape=out_shape,
        scratch_shapes=[pltpu.VMEM((d, IN_COLS_PAD), BF16)],
        compiler_params=pltpu.CompilerParams(dimension_semantics=("arbitrary", "arbitrary"),
                                             vmem_limit_bytes=VMEM_LIMIT),
        name="prologue",
    )(x, g, win, qn, wuq, kvn, wukv, cos, sin)


def _window_body(sink_ref, q_ref, kp_ref, kc_ref, kn_ref, vp_ref, vc_ref, vn_ref, bias_ref, o_ref):
    two = 2 * A_BLOCK
    low = lax.broadcasted_iota(jnp.int32, (two, LANES), 1) < A_HEAD_DIM
    for hk in range(A_KV_HEADS):
        q2 = jnp.concatenate([q_ref[0, :, (2 * hk) * LANES:(2 * hk + 1) * LANES],
                              q_ref[0, :, (2 * hk + 1) * LANES:(2 * hk + 2) * LANES]], axis=0)
        acc = jnp.zeros((two, LANES), F32)
        inv = []
        for half in range(2):
            z = 2 * hk + half
            kband = jnp.concatenate([kp_ref[0, z], kc_ref[0, z], kn_ref[0, z]], axis=0)
            vband = jnp.concatenate([vp_ref[0, z], vc_ref[0, z], vn_ref[0, z]], axis=0)
            s = lax.dot_general(q2, kband, (((1,), (1,)), ((), ())), preferred_element_type=F32)
            s = s + bias_ref[0, z]
            sink = jnp.concatenate(
                [jnp.full((A_BLOCK, 1), sink_ref[hk * A_GROUP + 2 * jj + half], F32) for jj in range(2)], axis=0)
            m = jnp.maximum(jnp.max(s, axis=-1, keepdims=True), sink)
            p = jnp.exp(s - m)
            den = jnp.sum(p, axis=-1, keepdims=True) + jnp.exp(sink - m)
            inv.append(1.0 / den)
            acc = acc + jnp.dot(p.astype(BF16), vband, preferred_element_type=F32)
        o = (acc * jnp.where(low, inv[0], inv[1])).astype(BF16)
        o_ref[0, :, (2 * hk) * LANES:(2 * hk + 1) * LANES] = o[:A_BLOCK]
        o_ref[0, :, (2 * hk + 1) * LANES:(2 * hk + 2) * LANES] = o[A_BLOCK:]


def _window_attn(qa, ka, va, sink, bias):
    b, s, _ = qa.shape
    nb = s // A_BLOCK
    kv_blk = (1, 2 * A_KV_HEADS, A_BLOCK, LANES)
    prev = pl.BlockSpec(kv_blk, lambda bi, n: (bi, 0, jnp.maximum(n - 1, 0), 0))
    cur = pl.BlockSpec(kv_blk, lambda bi, n: (bi, 0, n, 0))
    nxt = pl.BlockSpec(kv_blk, lambda bi, n: (bi, 0, jnp.minimum(n + 1, nb - 1), 0))

    def edge(bi, n):
        return ((n == 0).astype(jnp.int32) + 2 * (n == nb - 1).astype(jnp.int32), 0, 0, 0)

    return pl.pallas_call(
        _window_body,
        grid=(b, nb),
        in_specs=[
            pl.BlockSpec(memory_space=pltpu.SMEM),
            pl.BlockSpec((1, A_BLOCK, A_WIDTH), lambda bi, n: (bi, n, 0)),
            prev, cur, nxt, prev, cur, nxt,
            pl.BlockSpec((1,) + bias.shape[1:], edge),
        ],
        out_specs=pl.BlockSpec((1, A_BLOCK, A_WIDTH), lambda bi, n: (bi, n, 0)),
        out_shape=jax.ShapeDtypeStruct((b, s, A_WIDTH), BF16),
        compiler_params=pltpu.CompilerParams(dimension_semantics=("parallel", "parallel"),
                                             vmem_limit_bytes=VMEM_LIMIT),
        name="window_attn",
    )(sink, qa, ka, ka, ka, va, va, va, bias)


def _mla_body(q_ref, k_ref, v_ref, o_ref):
    q = q_ref[0, 0]
    tq = q.shape[0]
    s_len = k_ref.shape[2]
    ck = min(CK_MLA, s_len)
    m = jnp.full((tq, 1), -jnp.inf, F32)
    l = jnp.zeros((tq, 1), F32)
    acc = jnp.zeros((tq, B_V_DIM), F32)
    for c in range(s_len // ck):
        k_c = k_ref[0, 0, c * ck:(c + 1) * ck, :]
        v_c = v_ref[0, 0, c * ck:(c + 1) * ck, :]
        s = lax.dot_general(q, k_c, (((1,), (1,)), ((), ())), preferred_element_type=F32)
        m_new = jnp.maximum(m, jnp.max(s, axis=-1, keepdims=True))
        alpha = jnp.exp(m - m_new)
        p = jnp.exp(s - m_new)
        l = alpha * l + jnp.sum(p, axis=-1, keepdims=True)
        acc = alpha * acc + jnp.dot(p.astype(BF16), v_c, preferred_element_type=F32)
        m = m_new
    o_ref[0] = (acc / l).astype(BF16)


def _mla_attn(qb, kb, vb):
    b, nh, s, _ = qb.shape
    tq = min(TQ_MLA, s)
    return pl.pallas_call(
        _mla_body,
        grid=(b, nh, s // tq),
        in_specs=[
            pl.BlockSpec((1, 1, tq, B_QK_PAD), lambda bi, h, i: (bi, h, i, 0)),
            pl.BlockSpec((1, 1, s, B_QK_PAD), lambda bi, h, i: (bi, h, 0, 0)),
            pl.BlockSpec((1, 1, s, B_V_DIM), lambda bi, h, i: (bi, h, 0, 0)),
        ],
        out_specs=pl.BlockSpec((1, tq, B_V_DIM), lambda bi, h, i: (bi, i, h)),
        out_shape=jax.ShapeDtypeStruct((b, s, B_WIDTH), BF16),
        compiler_params=pltpu.CompilerParams(dimension_semantics=("parallel", "parallel", "parallel"),
                                             vmem_limit_bytes=VMEM_LIMIT),
        name="mla_attn",
    )(qb, kb, vb)


def _out_router_body(ya_ref, yb_ref, x_ref, ga_ref, gb_ref, wo_ref, gm_ref, wr_ref, br_ref,
                     x1_ref, xm_ref, eidx_ref, gate_ref, rank_ref, cnt_ref, run_ref):
    i = pl.program_id(0)

    @pl.when(i == 0)
    def _():
        run_ref[...] = jnp.zeros_like(run_ref)

    na = _rms(ya_ref[...].astype(F32), ga_ref[...]).astype(BF16)
    nb = _rms(yb_ref[...].astype(F32), gb_ref[...]).astype(BF16)
    att = jnp.dot(na, wo_ref[:A_WIDTH, :], preferred_element_type=F32)
    att = att + jnp.dot(nb, wo_ref[A_WIDTH:, :], preferred_element_type=F32)
    x1 = x_ref[...] + att
    x1_ref[...] = x1
    hn = _rms(x1, gm_ref[...])
    _store_token_rows(xm_ref, 0, hn)
    tm = hn.shape[0]
    hi = hn.astype(BF16)
    lo = (hn - hi.astype(F32)).astype(BF16)
    prod = jnp.dot(jnp.concatenate([hi, lo], axis=0), wr_ref[...], preferred_element_type=F32)
    logits = (prod[:tm, :N_EXPERTS] + prod[:tm, N_EXPERTS:] + prod[tm:, :N_EXPERTS] + prod[tm:, N_EXPERTS:]
              + br_ref[...])
    lane = lax.broadcasted_iota(jnp.int32, (tm, N_EXPERTS), 1)
    work = logits
    sel = jnp.zeros((tm, N_EXPERTS), F32)
    hots, vals, idxs = [], [], []
    for _k in range(TOP_K):
        mx = jnp.max(work, axis=-1, keepdims=True)
        idx = jnp.min(jnp.where(work == mx, lane, N_EXPERTS), axis=-1, keepdims=True)
        hot = lane == idx
        hots.append(hot)
        vals.append(mx)
        idxs.append(idx)
        sel = sel + hot.astype(F32)
        work = jnp.where(hot, -jnp.inf, work)
    exps = [jnp.exp(v - vals[0]) for v in vals]
    den = exps[0] + exps[1] + exps[2] + exps[3]
    r_i = lax.broadcasted_iota(jnp.int32, (tm, tm), 0)
    c_i = lax.broadcasted_iota(jnp.int32, (tm, tm), 1)
    tri = (c_i < r_i).astype(BF16)
    before = jnp.dot(tri, sel.astype(BF16), preferred_element_type=F32) + run_ref[...]
    lane4 = lax.broadcasted_iota(jnp.int32, (tm, TOP_K), 1)
    eidx = jnp.zeros((tm, TOP_K), jnp.int32)
    gate = jnp.zeros((tm, TOP_K), F32)
    rank = jnp.zeros((tm, TOP_K), jnp.int32)
    for k in range(TOP_K):
        rk = jnp.sum(jnp.where(hots[k], before, 0.0), axis=-1, keepdims=True).astype(jnp.int32)
        eidx = jnp.where(lane4 == k, idxs[k], eidx)
        gate = jnp.where(lane4 == k, exps[k] / den, gate)
        rank = jnp.where(lane4 == k, rk, rank)
    eidx_ref[...] = eidx
    gate_ref[...] = gate
    rank_ref[...] = rank
    run = run_ref[...] + jnp.sum(sel, axis=0, keepdims=True)
    run_ref[...] = run
    cnt_ref[...] = run.astype(jnp.int32)


def _out_router(ya, yb, x2d, ga, gb, wo, gm, wr, br):
    t, d = x2d.shape
    tm = min(TM_OUT, t)
    sds = jax.ShapeDtypeStruct

    def row(w):
        return pl.BlockSpec((tm, w), lambda i: (i, 0))

    return pl.pallas_call(
        _out_router_body,
        grid=(t // tm,),
        in_specs=[row(A_WIDTH), row(B_WIDTH), row(d), _const_spec(ga.shape), _const_spec(gb.shape),
                  _const_spec(wo.shape), _const_spec(gm.shape), _const_spec(wr.shape), _const_spec(br.shape)],
        out_specs=(row(d), pl.BlockSpec((tm * ROW_PITCH, LANES), lambda i: (i, 0)),
                   row(TOP_K), row(TOP_K), row(TOP_K),
                   pl.BlockSpec((1, N_EXPERTS), lambda i: (0, 0))),
        out_shape=(sds((t, d), F32), sds((t * ROW_PITCH, LANES), F32), sds((t, TOP_K), jnp.int32),
                   sds((t, TOP_K), F32),
                   sds((t, TOP_K), jnp.int32), sds((1, N_EXPERTS), jnp.int32)),
        scratch_shapes=[pltpu.VMEM((1, N_EXPERTS), F32)],
        compiler_params=pltpu.CompilerParams(dimension_semantics=("arbitrary",),
                                             vmem_limit_bytes=VMEM_LIMIT),
        name="out_router",
    )(ya, yb, x2d, ga, gb, wo, gm, wr, br)


def _token_copy(src_ref, src_tok, dst_ref, dst_tok, sem, ntok=1):
    return _rows_copy(src_ref, src_tok * ROW_PITCH, dst_ref, dst_tok * ROW_PITCH, sem, ntok)


def _rows_copy(src_ref, src_row, dst_ref, dst_row, sem, ntok=1):
    rows = ntok * ROW_PITCH
    return pltpu.make_async_copy(src_ref.at[pl.ds(src_row, rows)], dst_ref.at[pl.ds(dst_row, rows)], sem)


def _slot_rows_body(start_ref, eidx_ref, rank_ref, o_ref):
    e = eidx_ref[...]
    base = jnp.zeros_like(e)
    for ex in range(N_EXPERTS):
        base = jnp.where(e == ex, start_ref[ex], base)
    o_ref[...] = (base + rank_ref[...]) * ROW_PITCH


def _slot_rows(start, eidx, rank):
    n = eidx.shape[0]
    shape2 = (n // LANES, LANES)
    out = pl.pallas_call(
        _slot_rows_body,
        in_specs=[pl.BlockSpec(memory_space=pltpu.SMEM), pl.BlockSpec(shape2, lambda: (0, 0)),
                  pl.BlockSpec(shape2, lambda: (0, 0))],
        out_specs=pl.BlockSpec(shape2, lambda: (0, 0)),
        out_shape=jax.ShapeDtypeStruct(shape2, jnp.int32),
        name="slot_rows",
    )(start, eidx.reshape(shape2), rank.reshape(shape2))
    return out.reshape(n)


def _zero_fill_pads(pad_start_ref, pad_len_ref, nact_ref, xs_ref, zero_ref, zsem):
    bm = zero_ref.shape[0] // ROW_PITCH
    nblk = xs_ref.shape[0] // (bm * ROW_PITCH)
    zero_ref[...] = jnp.zeros_like(zero_ref)

    def sweep(do):
        def per_expert(e, c):
            off = pad_start_ref[e]
            n = pad_len_ref[e]
            p = bm // 2
            while p >= 1:
                take = (n & p) != 0

                @pl.when(take)
                def _(p=p, off=off):
                    do(_token_copy(zero_ref, 0, xs_ref, off, zsem, p))

                off = off + jnp.where(take, p, 0)
                p //= 2
            return c

        lax.fori_loop(0, N_EXPERTS, per_expert, 0)

        def tail(j, c):
            do(_token_copy(zero_ref, 0, xs_ref, j * bm, zsem, bm))
            return c

        lax.fori_loop(nact_ref[0], nblk, tail, 0)

    sweep(lambda cp: cp.start())
    sweep(lambda cp: cp.wait())


def _dispatch_body(pad_start_ref, pad_len_ref, nact_ref, slot_ref, xm_ref, xs_ref, zero_ref, sem, zsem):
    tm = slot_ref.shape[0] // TOP_K

    @pl.when(pl.program_id(0) == 0)
    def _():
        _zero_fill_pads(pad_start_ref, pad_len_ref, nact_ref, xs_ref, zero_ref, zsem)

    def issue(i, c):
        for k in range(TOP_K):
            _rows_copy(xm_ref, i * ROW_PITCH, xs_ref, slot_ref[i * TOP_K + k], sem).start(priority=k % 2)
        return c

    lax.fori_loop(0, tm, issue, 0, unroll=2)

    def drain(i, c):
        for _ in range(WAIT_UNROLL):
            _token_copy(xm_ref, 0, xs_ref, 0, sem).wait()
        return c

    lax.fori_loop(0, tm * TOP_K // WAIT_UNROLL, drain, 0)


def _dispatch(pad_start, pad_len, nact, slot_rows, xm, n_slots):
    t = xm.shape[0] // ROW_PITCH
    tm = min(TM_DISPATCH, t)
    grid_spec = pltpu.PrefetchScalarGridSpec(
        num_scalar_prefetch=3,
        grid=(t // tm,),
        in_specs=[pl.BlockSpec((tm * TOP_K,), lambda i, *_: (i,), memory_space=pltpu.SMEM),
                  pl.BlockSpec((tm * ROW_PITCH, LANES), lambda i, *_: (i, 0))],
        out_specs=pl.BlockSpec(memory_space=pl.ANY),
        scratch_shapes=[pltpu.VMEM((MOE_BM * ROW_PITCH, LANES), F32), pltpu.SemaphoreType.DMA(()),
                        pltpu.SemaphoreType.DMA(())],
    )
    return pl.pallas_call(
        _dispatch_body,
        grid_spec=grid_spec,
        out_shape=jax.ShapeDtypeStruct((n_slots * ROW_PITCH, LANES), F32),
        compiler_params=pltpu.CompilerParams(dimension_semantics=("arbitrary",), has_side_effects=True,
                                             vmem_limit_bytes=VMEM_LIMIT),
        name="dispatch",
    )(pad_start, pad_len, nact, slot_rows, xm)


STEP_UP, STEP_DOWN, STEP_TAIL, STEP_IDLE = 0, 1, 2, 3
NF = D_FF // MOE_TF
PASS_BLOCKS = MOE_RM // MOE_BM


def _swiglu(hg, hu):
    gate = jnp.minimum(hg, SWIGLU_LIMIT)
    up = jnp.clip(hu, -SWIGLU_LIMIT, SWIGLU_LIMIT)
    glu = gate / (1.0 + jnp.exp(-SWIGLU_ALPHA * gate))
    return (up + 1.0) * glu


def _experts_body(e_ref, f_ref, kind_ref, blk_ref, blk0_ref, nsub_ref,
                  xs_ref, wg_ref, wu_ref, bg_ref, bu_ref, wd_ref, bd_ref,
                  y_ref, xb_ref, h_ref, wdb_ref, stg_ref, sem):
    t = pl.program_id(0)
    kind = kind_ref[t]
    f = f_ref[t]
    nsub = nsub_ref[t]

    def x_copy(j, slot):
        rows = MOE_SUB * ROW_PITCH
        first = pl.multiple_of((blk0_ref[t] * MOE_BM + j * MOE_SUB) * ROW_PITCH, SUBLANES)
        return pltpu.make_async_copy(xs_ref.at[pl.ds(first, rows)], stg_ref.at[slot], sem.at[slot])

    @pl.when(kind == STEP_UP)
    def _up():
        @pl.when(f == 0)
        def _load_rows():
            x_copy(0, 0).start()

            def body(j, c):
                slot = j % 2

                @pl.when(j + 1 < nsub)
                def _():
                    x_copy(j + 1, 1 - slot).start()

                x_copy(j, slot).wait()
                rows = pl.ds(pl.multiple_of(j * MOE_SUB, MOE_SUB), MOE_SUB)
                for jt, tile in enumerate(_load_token_rows(stg_ref.at[slot], MOE_SUB)):
                    xb_ref[rows, jt * LANES:(jt + 1) * LANES] = tile.astype(BF16)
                return c

            lax.fori_loop(0, nsub, body, 0)

        wdb_ref[pl.ds(pl.multiple_of(f * MOE_TF, MOE_TF), MOE_TF), :] = wd_ref[0].astype(BF16)

        def up_rows(row0, nrows):
            x = xb_ref[pl.ds(row0, nrows), :]
            hg = jnp.dot(x, wg_ref[0].astype(BF16), preferred_element_type=F32) + bg_ref[0]
            hu = jnp.dot(x, wu_ref[0].astype(BF16), preferred_element_type=F32) + bu_ref[0]
            act = _swiglu(hg, hu).astype(BF16)
            for fs in range(NF):
                @pl.when(f == fs)
                def _(fs=fs):
                    h_ref[pl.ds(row0, nrows), fs * MOE_TF:(fs + 1) * MOE_TF] = act

        quad = 4 * MOE_SUB
        nquad = nsub // 4
        rem = nsub % 4

        def quad_rows(i, c):
            up_rows(pl.multiple_of(i * quad, quad), quad)
            return c

        lax.fori_loop(0, nquad, quad_rows, 0)

        @pl.when(rem >= 2)
        def _():
            up_rows(pl.multiple_of(nquad * quad, quad), 2 * MOE_SUB)

        @pl.when(rem % 2 == 1)
        def _():
            up_rows(pl.multiple_of(nquad * quad + (rem // 2) * 2 * MOE_SUB, MOE_SUB), MOE_SUB)

    @pl.when(kind == STEP_DOWN)
    def _down():
        row0 = pl.multiple_of((blk_ref[t] - blk0_ref[t]) * MOE_BM, MOE_BM)

        def down_rows(nrows):
            hrows = h_ref[pl.ds(row0, nrows), :]
            return jnp.dot(hrows, wdb_ref[...], preferred_element_type=F32) + bd_ref[0]

        @pl.when(nsub == 2)
        def _():
            _store_token_rows(y_ref, 0, down_rows(MOE_BM))

        @pl.when(nsub == 1)
        def _():
            _store_token_rows(y_ref, 0, down_rows(MOE_SUB))
            y_ref[MOE_SUB * ROW_PITCH:, :] = jnp.zeros(((MOE_BM - MOE_SUB) * ROW_PITCH, LANES), F32)

    @pl.when(kind == STEP_TAIL)
    def _tail():
        y_ref[...] = jnp.zeros_like(y_ref)


def _schedule_body(n_blocks, cnt_ref, e_ref, f_ref, kind_ref, blk_ref, blk0_ref, nsub_ref,
                   start_ref, pad_start_ref, pad_len_ref, nact_ref):
    n_steps = e_ref.shape[0]
    sub_per_blk = MOE_BM // MOE_SUB
    zero = jnp.int32(0)

    def put(idx, e, f, kind, blk, blk0, nsub):
        e_ref[idx] = e
        f_ref[idx] = f
        kind_ref[idx] = kind
        blk_ref[idx] = blk
        blk0_ref[idx] = blk0
        nsub_ref[idx] = nsub

    def per_expert(e, carry):
        t, blk, last_e = carry
        n = cnt_ref[e]
        nblk = (n + (MOE_BM - 1)) // MOE_BM
        start_ref[e] = blk * MOE_BM
        pad_start_ref[e] = blk * MOE_BM + n
        pad_len_ref[e] = nblk * MOE_BM - n

        def per_pass(p, t2):
            b0 = blk + p * PASS_BLOCKS
            nb = jnp.minimum(nblk - p * PASS_BLOCKS, PASS_BLOCKS)
            nsub = (jnp.minimum(n - p * MOE_RM, MOE_RM) + (MOE_SUB - 1)) // MOE_SUB
            for f in range(NF):
                put(t2 + f, e, f, STEP_UP, b0, b0, nsub)

            def per_blk(j, c):
                put(t2 + NF + j, e, NF - 1, STEP_DOWN, b0 + j, b0,
                    jnp.clip(nsub - j * sub_per_blk, 1, sub_per_blk))
                return c

            lax.fori_loop(0, nb, per_blk, 0)
            return t2 + NF + nb

        t = lax.fori_loop(0, (nblk + (PASS_BLOCKS - 1)) // PASS_BLOCKS, per_pass, t)
        return t, blk + nblk, jnp.where(n > 0, e, last_e)

    t, nact, last_e = lax.fori_loop(0, N_EXPERTS, per_expert, (zero, zero, zero))
    nact_ref[0] = nact

    def spare(i, c):
        is_tail = i < n_blocks - nact
        blk = jnp.where(is_tail, nact + i, n_blocks - 1)
        put(t + i, last_e, NF - 1, jnp.where(is_tail, STEP_TAIL, STEP_IDLE), blk, blk, 1)
        return c

    lax.fori_loop(0, n_steps - t, spare, 0)


def _expert_schedule(counts, n_blocks, n_assign):
    n_pass_max = N_EXPERTS + n_assign // MOE_RM
    n_steps = n_pass_max * NF + n_blocks
    smem = pl.BlockSpec(memory_space=pltpu.SMEM)
    i32 = jnp.int32
    out_shape = tuple(jax.ShapeDtypeStruct((n,), i32) for n in (n_steps,) * 6 + (N_EXPERTS,) * 3 + (1,))
    outs = pl.pallas_call(
        functools.partial(_schedule_body, n_blocks),
        in_specs=[smem],
        out_specs=tuple(smem for _ in out_shape),
        out_shape=out_shape,
        name="schedule",
    )(counts)
    return outs[:6], outs[6], outs[7], outs[8], outs[9]


def _experts(tabs, xs, w_up, b_up, w_down, b_down):
    d = D_MODEL
    n_steps = tabs[0].shape[0]
    grid_spec = pltpu.PrefetchScalarGridSpec(
        num_scalar_prefetch=len(tabs),
        grid=(n_steps,),
        in_specs=[
            pl.BlockSpec(memory_space=pl.ANY),
            pl.BlockSpec((1, d, MOE_TF), lambda t, e, f, *_: (e[t], 0, f[t])),
            pl.BlockSpec((1, d, MOE_TF), lambda t, e, f, *_: (e[t], 0, NF + f[t])),
            pl.BlockSpec((1, 1, MOE_TF), lambda t, e, f, *_: (e[t], 0, f[t])),
            pl.BlockSpec((1, 1, MOE_TF), lambda t, e, f, *_: (e[t], 0, NF + f[t])),
            pl.BlockSpec((1, MOE_TF, d), lambda t, e, f, *_: (e[t], f[t], 0)),
            pl.BlockSpec((1, 1, d), lambda t, e, f, *_: (e[t], 0, 0)),
        ],
        out_specs=pl.BlockSpec((MOE_BM * ROW_PITCH, LANES), lambda t, e, f, kind, blk, *_: (blk[t], 0)),
        scratch_shapes=[
            pltpu.VMEM((MOE_RM, d), BF16),
            pltpu.VMEM((MOE_RM, D_FF), BF16),
            pltpu.VMEM((D_FF, d), BF16),
            pltpu.VMEM((2, MOE_SUB * ROW_PITCH, LANES), F32),
            pltpu.SemaphoreType.DMA((2,)),
        ],
    )
    return pl.pallas_call(
        _experts_body,
        grid_spec=grid_spec,
        out_shape=jax.ShapeDtypeStruct(xs.shape, F32),
        compiler_params=pltpu.CompilerParams(dimension_semantics=("arbitrary",),
                                             vmem_limit_bytes=VMEM_LIMIT),
        name="experts",
    )(*tabs, xs, w_up, w_up, b_up, b_up, w_down, b_down)


def _combine_body(slot_ref, slot_nx_ref, gate_ref, x1_ref, gf_ref, ys_ref, o_ref, buf_ref, sem):
    step = pl.program_id(0)
    slot = step % 2
    tm = x1_ref.shape[0]

    def gather(rows_ref, s):
        def issue(i, c):
            for k in range(TOP_K):
                _rows_copy(ys_ref, rows_ref[i * TOP_K + k], buf_ref.at[s, k], i * ROW_PITCH,
                           sem.at[s]).start(priority=k % 2)
            return c

        lax.fori_loop(0, tm, issue, 0, unroll=2)

    @pl.when(step == 0)
    def _():
        gather(slot_ref, 0)

    for s in range(2):
        @pl.when((step + 1 < pl.num_programs(0)) & (slot == 1 - s))
        def _(s=s):
            gather(slot_nx_ref, s)

    def drain(i, c):
        for _ in range(WAIT_UNROLL):
            _token_copy(ys_ref, 0, buf_ref.at[slot, 0], 0, sem.at[slot]).wait()
        return c

    lax.fori_loop(0, tm * TOP_K // WAIT_UNROLL, drain, 0)
    gate = gate_ref[...]
    tiles = [x1_ref[:, j * LANES:(j + 1) * LANES] for j in range(ROW_TILES)]
    for k in range(TOP_K):
        g = gate[:, k:k + 1]
        for j, tile in enumerate(_load_token_rows(buf_ref.at[slot, k], tm)):
            tiles[j] = tiles[j] + g * tile
    o_ref[...] = _rms(jnp.concatenate(tiles, axis=1), gf_ref[...])


def _combine(slot_rows, gates, x1, gf, ys):
    t, d = x1.shape
    tm = min(TM_COMBINE, t)
    n = t // tm
    idx_blk = (tm * TOP_K,)
    return pl.pallas_call(
        _combine_body,
        grid=(n,),
        in_specs=[pl.BlockSpec(idx_blk, lambda i: (i,), memory_space=pltpu.SMEM),
                  pl.BlockSpec(idx_blk, lambda i: (jnp.minimum(i + 1, n - 1),), memory_space=pltpu.SMEM),
                  pl.BlockSpec((tm, TOP_K), lambda i: (i, 0)),
                  pl.BlockSpec((tm, d), lambda i: (i, 0)),
                  _const_spec(gf.shape),
                  pl.BlockSpec(memory_space=pl.ANY)],
        out_specs=pl.BlockSpec((tm, d), lambda i: (i, 0)),
        scratch_shapes=[pltpu.VMEM((2, TOP_K, tm * ROW_PITCH, LANES), F32), pltpu.SemaphoreType.DMA((2,))],
        out_shape=jax.ShapeDtypeStruct((t, d), F32),
        compiler_params=pltpu.CompilerParams(dimension_semantics=("arbitrary",),
                                             vmem_limit_bytes=VMEM_LIMIT),
        name="combine",
    )(slot_rows, slot_rows, gates, x1, gf, ys)


def _rope_tables(s):
    pos = np.arange(s, dtype=np.float32)
    inv_freq = np.power(np.float32(ROPE_THETA), -np.arange(0, B_ROPE_DIM, 2, dtype=np.float32) / B_ROPE_DIM)
    ang = (pos[:, None] * inv_freq[None, :]).astype(np.float32)
    cos, sin = np.cos(ang), np.sin(ang)
    pad = LANES - B_ROPE_DIM
    cos = np.concatenate([cos, cos, np.ones((s, pad), np.float32)], axis=1)
    sin = np.concatenate([sin, sin, np.zeros((s, pad), np.float32)], axis=1)
    return cos.astype(np.float32), sin.astype(np.float32)


def _window_bias():
    qi = np.arange(A_BLOCK)[:, None]
    kj = np.arange(3 * A_BLOCK)[None, :]
    dist = np.abs(qi + A_BLOCK - kj)
    slopes = np.power(np.float32(2.0), -8.0 * np.arange(1, A_HEADS + 1, dtype=np.float32) / A_HEADS)
    bias = -slopes[:, None, None] * dist.astype(np.float32)[None]
    bias = np.where((dist <= WINDOW)[None], bias, -np.inf)
    bias = bias.reshape(A_KV_HEADS, 2, 2, A_BLOCK, 3 * A_BLOCK).transpose(0, 2, 1, 3, 4)
    bias = bias.reshape(2 * A_KV_HEADS, 2 * A_BLOCK, 3 * A_BLOCK)
    no_prev = (kj < A_BLOCK)[None]
    no_next = (kj >= 2 * A_BLOCK)[None]
    ninf = -np.inf
    return np.stack([bias, np.where(no_prev, ninf, bias), np.where(no_next, ninf, bias),
                     np.where(no_prev | no_next, ninf, bias)]).astype(np.float32)


def _layer(x, attn_norm, w_in, a_sink, b_q_norm, b_w_uq, b_kv_norm, b_w_ukv, out_norm_a, out_norm_b, w_o,
           mlp_norm, w_router, b_router, w_up, b_up, w_down, b_down):
    b, s, d = x.shape
    t = b * s
    wq = b_w_uq.reshape(B_Q_RANK, B_HEADS, B_NOPE_DIM + B_ROPE_DIM)
    wq_pe = jnp.pad(wq[:, :, B_NOPE_DIM:], ((0, 0), (0, 0), (0, LANES - B_ROPE_DIM)))
    wuq = jnp.concatenate([wq[:, :, :B_NOPE_DIM].reshape(B_Q_RANK, -1), wq_pe.reshape(B_Q_RANK, -1)],
                          axis=1).astype(BF16)
    wkv = b_w_ukv.reshape(B_KV_RANK, B_HEADS, B_NOPE_DIM + B_V_DIM)
    wukv = jnp.concatenate([wkv[:, :, :B_NOPE_DIM].reshape(B_KV_RANK, -1),
                            wkv[:, :, B_NOPE_DIM:].reshape(B_KV_RANK, -1)], axis=1).astype(BF16)
    cos, sin = _rope_tables(s)

    qa, ka, va, qb, kb, vb = _prologue(x, attn_norm[None], w_in, b_q_norm[None], wuq, b_kv_norm[None], wukv,
                                       cos, sin)
    ya = _window_attn(qa, ka, va, a_sink, _window_bias())
    yb = _mla_attn(qb, kb, vb)
    wr_hi = w_router.astype(BF16)
    wr_lo = (w_router - wr_hi.astype(F32)).astype(BF16)
    x1, xm, eidx, gates, rank, counts = _out_router(
        ya.reshape(t, A_WIDTH), yb.reshape(t, B_WIDTH), x.reshape(t, d), out_norm_a[None], out_norm_b[None],
        w_o.astype(BF16), mlp_norm[None], jnp.concatenate([wr_hi, wr_lo], axis=1), b_router[None])

    nblk = (t * TOP_K + N_EXPERTS * (MOE_BM - 1) + MOE_BM - 1) // MOE_BM
    tabs, start, pad_start, pad_len, nact = _expert_schedule(counts.reshape(N_EXPERTS), nblk, t * TOP_K)
    slot_rows = _slot_rows(start, eidx.reshape(t * TOP_K), rank.reshape(t * TOP_K))
    xs = _dispatch(pad_start, pad_len, nact, slot_rows, xm, nblk * MOE_BM)
    ys = _experts(tabs, xs, w_up, b_up[:, None, :], w_down, b_down[:, None, :])
    return x1, slot_rows, gates, ys


def kernel(x, attn_norm, w_in, a_sink, b_q_norm, b_w_uq, b_kv_norm, b_w_ukv, out_norm_a, out_norm_b, w_o,
           mlp_norm, w_router, b_router, w_up, b_up, w_down, b_down, final_norm):
    b, s, d = x.shape
    assert d == D_MODEL and s % min(TQ_MLA, s) == 0 and s % min(CK_MLA, s) == 0 and s % TM_PRO == 0 and attn_norm.shape[0] == 1
    x1, slot_rows, gates, ys = _layer(
        x, attn_norm[0], w_in[0], a_sink[0], b_q_norm[0], b_w_uq[0], b_kv_norm[0], b_w_ukv[0], out_norm_a[0],
        out_norm_b[0], w_o[0], mlp_norm[0], w_router[0], b_router[0], w_up[0], b_up[0], w_down[0], b_down[0])
    out = _combine(slot_rows, gates, x1, final_norm[None], ys)
    return out.reshape(b, s, d)
```

```python
import functools

import jax
import numpy as np
import jax.numpy as jnp
from jax import lax
from jax.experimental import pallas as pl
from jax.experimental.pallas import tpu as pltpu

D_MODEL = 2048
A_HEADS, A_KV_HEADS, A_HEAD_DIM = 16, 4, 64
A_GROUP = A_HEADS // A_KV_HEADS
WINDOW = 128
A_BLOCK = 128
B_HEADS, B_Q_RANK, B_KV_RANK = 8, 512, 256
B_NOPE_DIM, B_ROPE_DIM, B_V_DIM = 128, 64, 128
ROPE_THETA = 10000.0
A_WIDTH = A_HEADS * A_HEAD_DIM
B_WIDTH = B_HEADS * B_V_DIM
A_KV_COLS = A_KV_HEADS * A_HEAD_DIM
N_EXPERTS, TOP_K, D_FF = 32, 4, 2048
SWIGLU_ALPHA, SWIGLU_LIMIT = 1.702, 7.0
EPS = 1e-5

LANES = 128
SUBLANES = 8
B_QK_PAD = 2 * LANES
IN_COLS_PAD = A_WIDTH + 2 * A_KV_COLS + B_Q_RANK + B_KV_RANK + LANES
VMEM_LIMIT = 56 * 1024 * 1024
ROW_TILES = D_MODEL // LANES
ROW_PITCH = ROW_TILES + 1

TM_PRO = 256
TQ_MLA = 1024
CK_MLA = 1024
TM_OUT = 512
TM_DISPATCH = 512
TM_COMBINE = 128
MOE_BM = 512
MOE_SUB = 256
MOE_RM = 2048
MOE_TF = 256
WAIT_UNROLL = 16

BF16 = jnp.bfloat16
F32 = jnp.float32


def _rms(x, g):
    return x * lax.rsqrt(jnp.mean(x * x, axis=-1, keepdims=True) + EPS) * g


def _const_spec(shape):
    nd = len(shape)
    return pl.BlockSpec(shape, lambda *_: (0,) * nd, pipeline_mode=pl.Buffered(1))


def _store_token_rows(ref, first, val):
    n = val.shape[0]
    for j in range(ROW_TILES):
        ref[pl.ds(first * ROW_PITCH + j, n, stride=ROW_PITCH), :] = val[:, j * LANES:(j + 1) * LANES]
    ref[pl.ds(first * ROW_PITCH + ROW_TILES, n, stride=ROW_PITCH), :] = jnp.zeros((n, LANES), val.dtype)


def _load_token_rows(ref, n):
    return [ref[pl.ds(j, n, stride=ROW_PITCH), :] for j in range(ROW_TILES)]


def _rope(x, cos, sin):
    lane = lax.broadcasted_iota(jnp.int32, x.shape, 1)
    up = pltpu.roll(x, LANES - B_ROPE_DIM // 2, 1)
    dn = pltpu.roll(x, B_ROPE_DIM // 2, 1)
    sw = jnp.where(lane < B_ROPE_DIM // 2, -up, jnp.where(lane < B_ROPE_DIM, dn, 0.0))
    return x * cos + sw * sin


def _prologue_body(x_ref, g_ref, win_ref, qn_ref, wuq_ref, kvn_ref, wukv_ref, cos_ref, sin_ref,
                   qa_ref, ka_ref, va_ref, qb_ref, kb_ref, vb_ref, winb_ref):
    @pl.when((pl.program_id(0) == 0) & (pl.program_id(1) == 0))
    def _():
        n_in = win_ref.shape[1]
        winb_ref[:, :n_in] = win_ref[...].astype(BF16)
        winb_ref[:, n_in:] = jnp.zeros((winb_ref.shape[0], IN_COLS_PAD - n_in), BF16)

    x = x_ref[0]
    xn = _rms(x, g_ref[...]).astype(BF16)
    proj = jnp.dot(xn, winb_ref[...], preferred_element_type=F32)
    c0 = A_WIDTH
    c1 = c0 + A_KV_COLS
    c2 = c1 + A_KV_COLS
    c3 = c2 + B_Q_RANK
    c4 = c3 + B_KV_RANK
    qa_ref[0] = (proj[:, :A_WIDTH] * A_HEAD_DIM ** -0.5).astype(BF16)
    low = lax.broadcasted_iota(jnp.int32, (proj.shape[0], LANES), 1) < A_HEAD_DIM
    for src, dst in ((c0, ka_ref), (c1, va_ref)):
        for pair in range(A_KV_HEADS // 2):
            two = proj[:, src + pair * LANES:src + (pair + 1) * LANES]
            swapped = pltpu.roll(two, A_HEAD_DIM, 1)
            dst[0, 4 * pair + 0] = jnp.where(low, two, 0.0).astype(BF16)
            dst[0, 4 * pair + 1] = jnp.where(low, 0.0, swapped).astype(BF16)
            dst[0, 4 * pair + 2] = jnp.where(low, swapped, 0.0).astype(BF16)
            dst[0, 4 * pair + 3] = jnp.where(low, 0.0, two).astype(BF16)
    cq = _rms(proj[:, c2:c3], qn_ref[...]).astype(BF16)
    q = jnp.dot(cq, wuq_ref[...], preferred_element_type=F32)
    ckv = _rms(proj[:, c3:c4], kvn_ref[...]).astype(BF16)
    kv = jnp.dot(ckv, wukv_ref[...], preferred_element_type=F32)
    cos = cos_ref[...]
    sin = sin_ref[...]
    kpe = _rope(proj[:, c4:c4 + LANES], cos, sin).astype(BF16)
    b_scale = (B_NOPE_DIM + B_ROPE_DIM) ** -0.5
    hw = B_HEADS * LANES
    for h in range(B_HEADS):
        sl = slice(h * LANES, (h + 1) * LANES)
        qb_ref[0, h, :, :LANES] = (q[:, sl] * b_scale).astype(BF16)
        qpe = _rope(q[:, hw + h * LANES:hw + (h + 1) * LANES], cos, sin)
        qb_ref[0, h, :, LANES:] = (qpe * b_scale).astype(BF16)
        kb_ref[0, h, :, :LANES] = kv[:, sl].astype(BF16)
        kb_ref[0, h, :, LANES:] = kpe
        vb_ref[0, h] = kv[:, hw + h * LANES:hw + (h + 1) * LANES].astype(BF16)


def _prologue(x, g, win, qn, wuq, kvn, wukv, cos, sin):
    b, s, d = x.shape
    tm = TM_PRO
    grid = (b, s // tm)
    sds = jax.ShapeDtypeStruct
    out_shape = (
        sds((b, s, A_WIDTH), BF16),
        sds((b, 2 * A_KV_HEADS, s, LANES), BF16),
        sds((b, 2 * A_KV_HEADS, s, LANES), BF16),
        sds((b, B_HEADS, s, B_QK_PAD), BF16),
        sds((b, B_HEADS, s, B_QK_PAD), BF16),
        sds((b, B_HEADS, s, B_V_DIM), BF16),
    )

    def hspec(nh, w):
        return pl.BlockSpec((1, nh, tm, w), lambda bi, i: (bi, 0, i, 0))

    return pl.pallas_call(
        _prologue_body,
        grid=grid,
        in_specs=[
            pl.BlockSpec((1, tm, d), lambda bi, i: (bi, i, 0)),
            _const_spec(g.shape), _const_spec(win.shape), _const_spec(qn.shape), _const_spec(wuq.shape),
            _const_spec(kvn.shape), _const_spec(wukv.shape),
            pl.BlockSpec((tm, LANES), lambda bi, i: (i, 0)),
            pl.BlockSpec((tm, LANES), lambda bi, i: (i, 0)),
        ],
        out_specs=(pl.BlockSpec((1, tm, A_WIDTH), lambda bi, i: (bi, i, 0)),
                   hspec(2 * A_KV_HEADS, LANES), hspec(2 * A_KV_HEADS, LANES),
                   hspec(B_HEADS, B_QK_PAD), hspec(B_HEADS, B_QK_PAD), hspec(B_HEADS, B_V_DIM)),
        out_shape=out_shape,
        scratch_shapes=[pltpu.VMEM((d, IN_COLS_PAD), BF16)],
        compiler_params=pltpu.CompilerParams(dimension_semantics=("arbitrary", "arbitrary"),
                                             vmem_limit_bytes=VMEM_LIMIT),
        name="prologue",
    )(x, g, win, qn, wuq, kvn, wukv, cos, sin)


def _window_body(sink_ref, q_ref, kp_ref, kc_ref, kn_ref, vp_ref, vc_ref, vn_ref, bias_ref, o_ref):
    two = 2 * A_BLOCK
    low = lax.broadcasted_iota(jnp.int32, (two, LANES), 1) < A_HEAD_DIM
    for hk in range(A_KV_HEADS):
        q2 = jnp.concatenate([q_ref[0, :, (2 * hk) * LANES:(2 * hk + 1) * LANES],
                              q_ref[0, :, (2 * hk + 1) * LANES:(2 * hk + 2) * LANES]], axis=0)
        acc = jnp.zeros((two, LANES), F32)
        inv = []
        for half in range(2):
            z = 2 * hk + half
            kband = jnp.concatenate([kp_ref[0, z], kc_ref[0, z], kn_ref[0, z]], axis=0)
            vband = jnp.concatenate([vp_ref[0, z], vc_ref[0, z], vn_ref[0, z]], axis=0)
            s = lax.dot_general(q2, kband, (((1,), (1,)), ((), ())), preferred_element_type=F32)
            s = s + bias_ref[0, z]
            sink = jnp.concatenate(
                [jnp.full((A_BLOCK, 1), sink_ref[hk * A_GROUP + 2 * jj + half], F32) for jj in range(2)], axis=0)
            m = jnp.maximum(jnp.max(s, axis=-1, keepdims=True), sink)
            p = jnp.exp(s - m)
            den = jnp.sum(p, axis=-1, keepdims=True) + jnp.exp(sink - m)
            inv.append(1.0 / den)
            acc = acc + jnp.dot(p.astype(BF16), vband, preferred_element_type=F32)
        o = (acc * jnp.where(low, inv[0], inv[1])).astype(BF16)
        o_ref[0, :, (2 * hk) * LANES:(2 * hk + 1) * LANES] = o[:A_BLOCK]
        o_ref[0, :, (2 * hk + 1) * LANES:(2 * hk + 2) * LANES] = o[A_BLOCK:]


def _window_attn(qa, ka, va, sink, bias):
    b, s, _ = qa.shape
    nb = s // A_BLOCK
    kv_blk = (1, 2 * A_KV_HEADS, A_BLOCK, LANES)
    prev = pl.BlockSpec(kv_blk, lambda bi, n: (bi, 0, jnp.maximum(n - 1, 0), 0))
    cur = pl.BlockSpec(kv_blk, lambda bi, n: (bi, 0, n, 0))
    nxt = pl.BlockSpec(kv_blk, lambda bi, n: (bi, 0, jnp.minimum(n + 1, nb - 1), 0))

    def edge(bi, n):
        return ((n == 0).astype(jnp.int32) + 2 * (n == nb - 1).astype(jnp.int32), 0, 0, 0)

    return pl.pallas_call(
        _window_body,
        grid=(b, nb),
        in_specs=[
            pl.BlockSpec(memory_space=pltpu.SMEM),
            pl.BlockSpec((1, A_BLOCK, A_WIDTH), lambda bi, n: (bi, n, 0)),
            prev, cur, nxt, prev, cur, nxt,
            pl.BlockSpec((1,) + bias.shape[1:], edge),
        ],
        out_specs=pl.BlockSpec((1, A_BLOCK, A_WIDTH), lambda bi, n: (bi, n, 0)),
        out_shape=jax.ShapeDtypeStruct((b, s, A_WIDTH), BF16),
        compiler_params=pltpu.CompilerParams(dimension_semantics=("parallel", "parallel"),
                                             vmem_limit_bytes=VMEM_LIMIT),
        name="window_attn",
    )(sink, qa, ka, ka, ka, va, va, va, bias)


def _mla_body(q_ref, k_ref, v_ref, o_ref):
    q = q_ref[0, 0]
    tq = q.shape[0]
    s_len = k_ref.shape[2]
    ck = min(CK_MLA, s_len)
    m = jnp.full((tq, 1), -jnp.inf, F32)
    l = jnp.zeros((tq, 1), F32)
    acc = jnp.zeros((tq, B_V_DIM), F32)
    for c in range(s_len // ck):
        k_c = k_ref[0, 0, c * ck:(c + 1) * ck, :]
        v_c = v_ref[0, 0, c * ck:(c + 1) * ck, :]
        s = lax.dot_general(q, k_c, (((1,), (1,)), ((), ())), preferred_element_type=F32)
        m_new = jnp.maximum(m, jnp.max(s, axis=-1, keepdims=True))
        alpha = jnp.exp(m - m_new)
        p = jnp.exp(s - m_new)
        l = alpha * l + jnp.sum(p, axis=-1, keepdims=True)
        acc = alpha * acc + jnp.dot(p.astype(BF16), v_c, preferred_element_type=F32)
        m = m_new
    o_ref[0] = (acc / l).astype(BF16)


def _mla_attn(qb, kb, vb):
    b, nh, s, _ = qb.shape
    tq = min(TQ_MLA, s)
    return pl.pallas_call(
        _mla_body,
        grid=(b, nh, s // tq),
        in_specs=[
            pl.BlockSpec((1, 1, tq, B_QK_PAD), lambda bi, h, i: (bi, h, i, 0)),
            pl.BlockSpec((1, 1, s, B_QK_PAD), lambda bi, h, i: (bi, h, 0, 0)),
            pl.BlockSpec((1, 1, s, B_V_DIM), lambda bi, h, i: (bi, h, 0, 0)),
        ],
        out_specs=pl.BlockSpec((1, tq, B_V_DIM), lambda bi, h, i: (bi, i, h)),
        out_shape=jax.ShapeDtypeStruct((b, s, B_WIDTH), BF16),
        compiler_params=pltpu.CompilerParams(dimension_semantics=("parallel", "parallel", "parallel"),
                                             vmem_limit_bytes=VMEM_LIMIT),
        name="mla_attn",
    )(qb, kb, vb)


def _out_router_body(ya_ref, yb_ref, x_ref, ga_ref, gb_ref, wo_ref, gm_ref, wr_ref, br_ref,
                     x1_ref, xm_ref, eidx_ref, gate_ref, rank_ref, cnt_ref, run_ref):
    i = pl.program_id(0)

    @pl.when(i == 0)
    def _():
        run_ref[...] = jnp.zeros_like(run_ref)

    na = _rms(ya_ref[...].astype(F32), ga_ref[...]).astype(BF16)
    nb = _rms(yb_ref[...].astype(F32), gb_ref[...]).astype(BF16)
    att = jnp.dot(na, wo_ref[:A_WIDTH, :], preferred_element_type=F32)
    att = att + jnp.dot(nb, wo_ref[A_WIDTH:, :], preferred_element_type=F32)
    x1 = x_ref[...] + att
    x1_ref[...] = x1
    hn = _rms(x1, gm_ref[...])
    _store_token_rows(xm_ref, 0, hn)
    tm = hn.shape[0]
    hi = hn.astype(BF16)
    lo = (hn - hi.astype(F32)).astype(BF16)
    prod = jnp.dot(jnp.concatenate([hi, lo], axis=0), wr_ref[...], preferred_element_type=F32)
    logits = (prod[:tm, :N_EXPERTS] + prod[:tm, N_EXPERTS:] + prod[tm:, :N_EXPERTS] + prod[tm:, N_EXPERTS:]
              + br_ref[...])
    lane = lax.broadcasted_iota(jnp.int32, (tm, N_EXPERTS), 1)
    work = logits
    sel = jnp.zeros((tm, N_EXPERTS), F32)
    hots, vals, idxs = [], [], []
    for _k in range(TOP_K):
        mx = jnp.max(work, axis=-1, keepdims=True)
        idx = jnp.min(jnp.where(work == mx, lane, N_EXPERTS), axis=-1, keepdims=True)
        hot = lane == idx
        hots.append(hot)
        vals.append(mx)
        idxs.append(idx)
        sel = sel + hot.astype(F32)
        work = jnp.where(hot, -jnp.inf, work)
    exps = [jnp.exp(v - vals[0]) for v in vals]
    den = exps[0] + exps[1] + exps[2] + exps[3]
    r_i = lax.broadcasted_iota(jnp.int32, (tm, tm), 0)
    c_i = lax.broadcasted_iota(jnp.int32, (tm, tm), 1)
    tri = (c_i < r_i).astype(BF16)
    before = jnp.dot(tri, sel.astype(BF16), preferred_element_type=F32) + run_ref[...]
    lane4 = lax.broadcasted_iota(jnp.int32, (tm, TOP_K), 1)
    eidx = jnp.zeros((tm, TOP_K), jnp.int32)
    gate = jnp.zeros((tm, TOP_K), F32)
    rank = jnp.zeros((tm, TOP_K), jnp.int32)
    for k in range(TOP_K):
        rk = jnp.sum(jnp.where(hots[k], before, 0.0), axis=-1, keepdims=True).astype(jnp.int32)
        eidx = jnp.where(lane4 == k, idxs[k], eidx)
        gate = jnp.where(lane4 == k, exps[k] / den, gate)
        rank = jnp.where(lane4 == k, rk, rank)
    eidx_ref[...] = eidx
    gate_ref[...] = gate
    rank_ref[...] = rank
    run = run_ref[...] + jnp.sum(sel, axis=0, keepdims=True)
    run_ref[...] = run
    cnt_ref[...] = run.astype(jnp.int32)


def _out_router(ya, yb, x2d, ga, gb, wo, gm, wr, br):
    t, d = x2d.shape
    tm = min(TM_OUT, t)
    sds = jax.ShapeDtypeStruct

    def row(w):
        return pl.BlockSpec((tm, w), lambda i: (i, 0))

    return pl.pallas_call(
        _out_router_body,
        grid=(t // tm,),
        in_specs=[row(A_WIDTH), row(B_WIDTH), row(d), _const_spec(ga.shape), _const_spec(gb.shape),
                  _const_spec(wo.shape), _const_spec(gm.shape), _const_spec(wr.shape), _const_spec(br.shape)],
        out_specs=(row(d), pl.BlockSpec((tm * ROW_PITCH, LANES), lambda i: (i, 0)),
                   row(TOP_K), row(TOP_K), row(TOP_K),
                   pl.BlockSpec((1, N_EXPERTS), lambda i: (0, 0))),
        out_shape=(sds((t, d), F32), sds((t * ROW_PITCH, LANES), F32), sds((t, TOP_K), jnp.int32),
                   sds((t, TOP_K), F32),
                   sds((t, TOP_K), jnp.int32), sds((1, N_EXPERTS), jnp.int32)),
        scratch_shapes=[pltpu.VMEM((1, N_EXPERTS), F32)],
        compiler_params=pltpu.CompilerParams(dimension_semantics=("arbitrary",),
                                             vmem_limit_bytes=VMEM_LIMIT),
        name="out_router",
    )(ya, yb, x2d, ga, gb, wo, gm, wr, br)


def _token_copy(src_ref, src_tok, dst_ref, dst_tok, sem, ntok=1):
    return _rows_copy(src_ref, src_tok * ROW_PITCH, dst_ref, dst_tok * ROW_PITCH, sem, ntok)


def _rows_copy(src_ref, src_row, dst_ref, dst_row, sem, ntok=1):
    rows = ntok * ROW_PITCH
    return pltpu.make_async_copy(src_ref.at[pl.ds(src_row, rows)], dst_ref.at[pl.ds(dst_row, rows)], sem)


def _slot_rows_body(start_ref, eidx_ref, rank_ref, o_ref):
    e = eidx_ref[...]
    base = jnp.zeros_like(e)
    for ex in range(N_EXPERTS):
        base = jnp.where(e == ex, start_ref[ex], base)
    o_ref[...] = (base + rank_ref[...]) * ROW_PITCH


def _slot_rows(start, eidx, rank):
    n = eidx.shape[0]
    shape2 = (n // LANES, LANES)
    out = pl.pallas_call(
        _slot_rows_body,
        in_specs=[pl.BlockSpec(memory_space=pltpu.SMEM), pl.BlockSpec(shape2, lambda: (0, 0)),
                  pl.BlockSpec(shape2, lambda: (0, 0))],
        out_specs=pl.BlockSpec(shape2, lambda: (0, 0)),
        out_shape=jax.ShapeDtypeStruct(shape2, jnp.int32),
        name="slot_rows",
    )(start, eidx.reshape(shape2), rank.reshape(shape2))
    return out.reshape(n)


def _zero_fill_pads(pad_start_ref, pad_len_ref, nact_ref, xs_ref, zero_ref, zsem):
    bm = zero_ref.shape[0] // ROW_PITCH
    nblk = xs_ref.shape[0] // (bm * ROW_PITCH)
    zero_ref[...] = jnp.zeros_like(zero_ref)

    def sweep(do):
        def per_expert(e, c):
            off = pad_start_ref[e]
            n = pad_len_ref[e]
            p = bm // 2
            while p >= 1:
                take = (n & p) != 0

                @pl.when(take)
                def _(p=p, off=off):
                    do(_token_copy(zero_ref, 0, xs_ref, off, zsem, p))

                off = off + jnp.where(take, p, 0)
                p //= 2
            return c

        lax.fori_loop(0, N_EXPERTS, per_expert, 0)

        def tail(j, c):
            do(_token_copy(zero_ref, 0, xs_ref, j * bm, zsem, bm))
            return c

        lax.fori_loop(nact_ref[0], nblk, tail, 0)

    sweep(lambda cp: cp.start())
    sweep(lambda cp: cp.wait())


def _dispatch_body(pad_start_ref, pad_len_ref, nact_ref, slot_ref, xm_ref, xs_ref, zero_ref, sem, zsem):
    tm = slot_ref.shape[0] // TOP_K

    @pl.when(pl.program_id(0) == 0)
    def _():
        _zero_fill_pads(pad_start_ref, pad_len_ref, nact_ref, xs_ref, zero_ref, zsem)

    def issue(i, c):
        for k in range(TOP_K):
            _rows_copy(xm_ref, i * ROW_PITCH, xs_ref, slot_ref[i * TOP_K + k], sem).start(priority=k % 2)
        return c

    lax.fori_loop(0, tm, issue, 0, unroll=2)

    def drain(i, c):
        for _ in range(WAIT_UNROLL):
            _token_copy(xm_ref, 0, xs_ref, 0, sem).wait()
        return c

    lax.fori_loop(0, tm * TOP_K // WAIT_UNROLL, drain, 0)


def _dispatch(pad_start, pad_len, nact, slot_rows, xm, n_slots):
    t = xm.shape[0] // ROW_PITCH
    tm = min(TM_DISPATCH, t)
    grid_spec = pltpu.PrefetchScalarGridSpec(
        num_scalar_prefetch=3,
        grid=(t // tm,),
        in_specs=[pl.BlockSpec((tm * TOP_K,), lambda i, *_: (i,), memory_space=pltpu.SMEM),
                  pl.BlockSpec((tm * ROW_PITCH, LANES), lambda i, *_: (i, 0))],
        out_specs=pl.BlockSpec(memory_space=pl.ANY),
        scratch_shapes=[pltpu.VMEM((MOE_BM * ROW_PITCH, LANES), F32), pltpu.SemaphoreType.DMA(()),
                        pltpu.SemaphoreType.DMA(())],
    )
    return pl.pallas_call(
        _dispatch_body,
        grid_spec=grid_spec,
        out_shape=jax.ShapeDtypeStruct((n_slots * ROW_PITCH, LANES), F32),
        compiler_params=pltpu.CompilerParams(dimension_semantics=("arbitrary",), has_side_effects=True,
                                             vmem_limit_bytes=VMEM_LIMIT),
        name="dispatch",
    )(pad_start, pad_len, nact, slot_rows, xm)


STEP_UP, STEP_DOWN, STEP_TAIL, STEP_IDLE = 0, 1, 2, 3
PREFETCH_PER_DOWN = 2
NF = D_FF // MOE_TF
PASS_BLOCKS = MOE_RM // MOE_BM


def _swiglu(hg, hu):
    gate = jnp.minimum(hg, SWIGLU_LIMIT)
    up = jnp.clip(hu, -SWIGLU_LIMIT, SWIGLU_LIMIT)
    glu = gate / (1.0 + jnp.exp(-SWIGLU_ALPHA * gate))
    return (up + 1.0) * glu


def _experts_body(e_ref, f_ref, kind_ref, blk_ref, blk0_ref, nsub_ref, pre_ref, nxb_ref,
                  xs_ref, wg_ref, wu_ref, bg_ref, bu_ref, wd_ref, bd_ref,
                  y_ref, xb_ref, h_ref, wdb_ref, stg_ref, sem):
    t = pl.program_id(0)
    kind = kind_ref[t]
    f = f_ref[t]
    nsub = nsub_ref[t]

    def x_copy(first_blk, j, slot):
        rows = MOE_SUB * ROW_PITCH
        first = pl.multiple_of((first_blk * MOE_BM + j * MOE_SUB) * ROW_PITCH, SUBLANES)
        return pltpu.make_async_copy(xs_ref.at[pl.ds(first, rows)], stg_ref.at[slot], sem.at[slot])

    def unpack(j, slot):
        rows = pl.ds(pl.multiple_of(j * MOE_SUB, MOE_SUB), MOE_SUB)
        for jt, tile in enumerate(_load_token_rows(stg_ref.at[slot], MOE_SUB)):
            xb_ref[rows, jt * LANES:(jt + 1) * LANES] = tile.astype(BF16)

    @pl.when(kind == STEP_UP)
    def _up():
        pre = pre_ref[t]

        @pl.when((f == 0) & (pre < nsub))
        def _load_rows():
            blk0 = blk0_ref[t]
            x_copy(blk0, pre, pre % 2).start()

            def body(j, c):
                slot = j % 2

                @pl.when(j + 1 < nsub)
                def _():
                    x_copy(blk0, j + 1, 1 - slot).start()

                x_copy(blk0, j, slot).wait()
                unpack(j, slot)
                return c

            lax.fori_loop(pre, nsub, body, 0)

        wdb_ref[pl.ds(pl.multiple_of(f * MOE_TF, MOE_TF), MOE_TF), :] = wd_ref[0].astype(BF16)

        def up_rows(row0, nrows):
            x = xb_ref[pl.ds(row0, nrows), :]
            hg = jnp.dot(x, wg_ref[0].astype(BF16), preferred_element_type=F32) + bg_ref[0]
            hu = jnp.dot(x, wu_ref[0].astype(BF16), preferred_element_type=F32) + bu_ref[0]
            act = _swiglu(hg, hu).astype(BF16)
            h_ref[pl.ds(row0, nrows), pl.ds(pl.multiple_of(f * MOE_TF, MOE_TF), MOE_TF)] = act

        quad = 4 * MOE_SUB
        nquad = nsub // 4
        rem = nsub % 4

        def quad_rows(i, c):
            base = pl.multiple_of(i * quad, quad)
            up_rows(base, quad // 2)
            up_rows(base + quad // 2, quad // 2)
            return c

        lax.fori_loop(0, nquad, quad_rows, 0)

        @pl.when(rem >= 2)
        def _():
            up_rows(pl.multiple_of(nquad * quad, quad), 2 * MOE_SUB)

        @pl.when(rem % 2 == 1)
        def _():
            up_rows(pl.multiple_of(nquad * quad + (rem // 2) * 2 * MOE_SUB, MOE_SUB), MOE_SUB)

    @pl.when(kind == STEP_DOWN)
    def _down():
        jblk = blk_ref[t] - blk0_ref[t]
        row0 = pl.multiple_of(jblk * MOE_BM, MOE_BM)
        nx_nsub = pre_ref[t]
        nx_blk0 = nxb_ref[t]
        ahead = [jblk * PREFETCH_PER_DOWN + u for u in range(PREFETCH_PER_DOWN)]
        for u, j in enumerate(ahead):
            @pl.when(j < nx_nsub)
            def _(u=u, j=j):
                x_copy(nx_blk0, j, u).start()

        def down_rows(nrows):
            hrows = h_ref[pl.ds(row0, nrows), :]
            return jnp.dot(hrows, wdb_ref[...], preferred_element_type=F32) + bd_ref[0]

        @pl.when(nsub == 2)
        def _():
            _store_token_rows(y_ref, 0, down_rows(MOE_BM))

        @pl.when(nsub == 1)
        def _():
            _store_token_rows(y_ref, 0, down_rows(MOE_SUB))
            y_ref[MOE_SUB * ROW_PITCH:, :] = jnp.zeros(((MOE_BM - MOE_SUB) * ROW_PITCH, LANES), F32)

        for u, j in enumerate(ahead):
            @pl.when(j < nx_nsub)
            def _(u=u, j=j):
                x_copy(nx_blk0, j, u).wait()
                unpack(j, u)

    @pl.when(kind == STEP_TAIL)
    def _tail():
        y_ref[...] = jnp.zeros_like(y_ref)


def _schedule_body(n_blocks, cnt_ref, e_ref, f_ref, kind_ref, blk_ref, blk0_ref, nsub_ref, pre_ref, nxb_ref,
                   start_ref, pad_start_ref, pad_len_ref, nact_ref):
    n_steps = e_ref.shape[0]
    sub_per_blk = MOE_BM // MOE_SUB
    zero = jnp.int32(0)

    def put(idx, e, f, kind, blk, blk0, nsub, pre):
        e_ref[idx] = e
        f_ref[idx] = f
        kind_ref[idx] = kind
        blk_ref[idx] = blk
        blk0_ref[idx] = blk0
        nsub_ref[idx] = nsub
        pre_ref[idx] = pre
        nxb_ref[idx] = zero

    def per_expert(e, carry):
        t, blk, last_e, prev_down, prev_nb = carry
        n = cnt_ref[e]
        nblk = (n + (MOE_BM - 1)) // MOE_BM
        start_ref[e] = blk * MOE_BM
        pad_start_ref[e] = blk * MOE_BM + n
        pad_len_ref[e] = nblk * MOE_BM - n

        def per_pass(p, carry2):
            t2, pdown, pnb = carry2
            b0 = blk + p * PASS_BLOCKS
            nb = jnp.minimum(nblk - p * PASS_BLOCKS, PASS_BLOCKS)
            nsub = (jnp.minimum(n - p * MOE_RM, MOE_RM) + (MOE_SUB - 1)) // MOE_SUB
            pre = jnp.minimum(nsub, pnb * PREFETCH_PER_DOWN)
            for f in range(NF):
                put(t2 + f, e, f, STEP_UP, b0, b0, nsub, pre)

            def patch(j, c):
                pre_ref[pdown + j] = nsub
                nxb_ref[pdown + j] = b0
                return c

            lax.fori_loop(0, pnb, patch, 0)

            def per_blk(j, c):
                put(t2 + NF + j, e, NF - 1, STEP_DOWN, b0 + j, b0,
                    jnp.clip(nsub - j * sub_per_blk, 1, sub_per_blk), zero)
                return c

            lax.fori_loop(0, nb, per_blk, 0)
            return t2 + NF + nb, t2 + NF, nb

        t, prev_down, prev_nb = lax.fori_loop(0, (nblk + (PASS_BLOCKS - 1)) // PASS_BLOCKS, per_pass,
                                              (t, prev_down, prev_nb))
        return t, blk + nblk, jnp.where(n > 0, e, last_e), prev_down, prev_nb

    t, nact, last_e, _, _ = lax.fori_loop(0, N_EXPERTS, per_expert, (zero, zero, zero, zero, zero))
    nact_ref[0] = nact

    def spare(i, c):
        is_tail = i < n_blocks - nact
        blk = jnp.where(is_tail, nact + i, n_blocks - 1)
        put(t + i, last_e, NF - 1, jnp.where(is_tail, STEP_TAIL, STEP_IDLE), blk, blk, 1, zero)
        return c

    lax.fori_loop(0, n_steps - t, spare, 0)


def _expert_schedule(counts, n_blocks, n_assign):
    n_pass_max = N_EXPERTS + n_assign // MOE_RM
    n_steps = n_pass_max * NF + n_blocks
    smem = pl.BlockSpec(memory_space=pltpu.SMEM)
    i32 = jnp.int32
    out_shape = tuple(jax.ShapeDtypeStruct((n,), i32) for n in (n_steps,) * 8 + (N_EXPERTS,) * 3 + (1,))
    outs = pl.pallas_call(
        functools.partial(_schedule_body, n_blocks),
        in_specs=[smem],
        out_specs=tuple(smem for _ in out_shape),
        out_shape=out_shape,
        name="schedule",
    )(counts)
    return outs[:8], outs[8], outs[9], outs[10], outs[11]


def _experts(tabs, xs, w_up, b_up, w_down, b_down):
    d = D_MODEL
    n_steps = tabs[0].shape[0]
    grid_spec = pltpu.PrefetchScalarGridSpec(
        num_scalar_prefetch=len(tabs),
        grid=(n_steps,),
        in_specs=[
            pl.BlockSpec(memory_space=pl.ANY),
            pl.BlockSpec((1, d, MOE_TF), lambda t, e, f, *_: (e[t], 0, f[t])),
            pl.BlockSpec((1, d, MOE_TF), lambda t, e, f, *_: (e[t], 0, NF + f[t])),
            pl.BlockSpec((1, 1, MOE_TF), lambda t, e, f, *_: (e[t], 0, f[t])),
            pl.BlockSpec((1, 1, MOE_TF), lambda t, e, f, *_: (e[t], 0, NF + f[t])),
            pl.BlockSpec((1, MOE_TF, d), lambda t, e, f, *_: (e[t], f[t], 0)),
            pl.BlockSpec((1, 1, d), lambda t, e, f, *_: (e[t], 0, 0)),
        ],
        out_specs=pl.BlockSpec((MOE_BM * ROW_PITCH, LANES), lambda t, e, f, kind, blk, *_: (blk[t], 0)),
        scratch_shapes=[
            pltpu.VMEM((MOE_RM, d), BF16),
            pltpu.VMEM((MOE_RM, D_FF), BF16),
            pltpu.VMEM((D_FF, d), BF16),
            pltpu.VMEM((2, MOE_SUB * ROW_PITCH, LANES), F32),
            pltpu.SemaphoreType.DMA((2,)),
        ],
    )
    return pl.pallas_call(
        _experts_body,
        grid_spec=grid_spec,
        out_shape=jax.ShapeDtypeStruct(xs.shape, F32),
        compiler_params=pltpu.CompilerParams(dimension_semantics=("arbitrary",),
                                             vmem_limit_bytes=VMEM_LIMIT),
        name="experts",
    )(*tabs, xs, w_up, w_up, b_up, b_up, w_down, b_down)


def _combine_body(slot_ref, slot_nx_ref, gate_ref, x1_ref, gf_ref, ys_ref, o_ref, buf_ref, sem):
    step = pl.program_id(0)
    slot = step % 2
    tm = x1_ref.shape[0]

    def gather(rows_ref, s):
        def issue(i, c):
            for k in range(TOP_K):
                _rows_copy(ys_ref, rows_ref[i * TOP_K + k], buf_ref.at[s, k], i * ROW_PITCH,
                           sem.at[s]).start(priority=k % 2)
            return c

        lax.fori_loop(0, tm, issue, 0, unroll=2)

    @pl.when(step == 0)
    def _():
        gather(slot_ref, 0)

    for s in range(2):
        @pl.when((step + 1 < pl.num_programs(0)) & (slot == 1 - s))
        def _(s=s):
            gather(slot_nx_ref, s)

    def drain(i, c):
        for _ in range(WAIT_UNROLL):
            _token_copy(ys_ref, 0, buf_ref.at[slot, 0], 0, sem.at[slot]).wait()
        return c

    lax.fori_loop(0, tm * TOP_K // WAIT_UNROLL, drain, 0)
    gate = gate_ref[...]
    tiles = [x1_ref[:, j * LANES:(j + 1) * LANES] for j in range(ROW_TILES)]
    for k in range(TOP_K):
        g = gate[:, k:k + 1]
        for j, tile in enumerate(_load_token_rows(buf_ref.at[slot, k], tm)):
            tiles[j] = tiles[j] + g * tile
    o_ref[...] = _rms(jnp.concatenate(tiles, axis=1), gf_ref[...])


def _combine(slot_rows, gates, x1, gf, ys):
    t, d = x1.shape
    tm = min(TM_COMBINE, t)
    n = t // tm
    idx_blk = (tm * TOP_K,)
    return pl.pallas_call(
        _combine_body,
        grid=(n,),
        in_specs=[pl.BlockSpec(idx_blk, lambda i: (i,), memory_space=pltpu.SMEM),
                  pl.BlockSpec(idx_blk, lambda i: (jnp.minimum(i + 1, n - 1),), memory_space=pltpu.SMEM),
                  pl.BlockSpec((tm, TOP_K), lambda i: (i, 0)),
                  pl.BlockSpec((tm, d), lambda i: (i, 0)),
                  _const_spec(gf.shape),
                  pl.BlockSpec(memory_space=pl.ANY)],
        out_specs=pl.BlockSpec((tm, d), lambda i: (i, 0)),
        scratch_shapes=[pltpu.VMEM((2, TOP_K, tm * ROW_PITCH, LANES), F32), pltpu.SemaphoreType.DMA((2,))],
        out_shape=jax.ShapeDtypeStruct((t, d), F32),
        compiler_params=pltpu.CompilerParams(dimension_semantics=("arbitrary",),
                                             vmem_limit_bytes=VMEM_LIMIT),
        name="combine",
    )(slot_rows, slot_rows, gates, x1, gf, ys)


def _rope_tables(s):
    pos = np.arange(s, dtype=np.float32)
    inv_freq = np.power(np.float32(ROPE_THETA), -np.arange(0, B_ROPE_DIM, 2, dtype=np.float32) / B_ROPE_DIM)
    ang = (pos[:, None] * inv_freq[None, :]).astype(np.float32)
    cos, sin = np.cos(ang), np.sin(ang)
    pad = LANES - B_ROPE_DIM
    cos = np.concatenate([cos, cos, np.ones((s, pad), np.float32)], axis=1)
    sin = np.concatenate([sin, sin, np.zeros((s, pad), np.float32)], axis=1)
    return cos.astype(np.float32), sin.astype(np.float32)


def _window_bias():
    qi = np.arange(A_BLOCK)[:, None]
    kj = np.arange(3 * A_BLOCK)[None, :]
    dist = np.abs(qi + A_BLOCK - kj)
    slopes = np.power(np.float32(2.0), -8.0 * np.arange(1, A_HEADS + 1, dtype=np.float32) / A_HEADS)
    bias = -slopes[:, None, None] * dist.astype(np.float32)[None]
    bias = np.where((dist <= WINDOW)[None], bias, -np.inf)
    bias = bias.reshape(A_KV_HEADS, 2, 2, A_BLOCK, 3 * A_BLOCK).transpose(0, 2, 1, 3, 4)
    bias = bias.reshape(2 * A_KV_HEADS, 2 * A_BLOCK, 3 * A_BLOCK)
    no_prev = (kj < A_BLOCK)[None]
    no_next = (kj >= 2 * A_BLOCK)[None]
    ninf = -np.inf
    return np.stack([bias, np.where(no_prev, ninf, bias), np.where(no_next, ninf, bias),
                     np.where(no_prev | no_next, ninf, bias)]).astype(np.float32)


def _layer(x, attn_norm, w_in, a_sink, b_q_norm, b_w_uq, b_kv_norm, b_w_ukv, out_norm_a, out_norm_b, w_o,
           mlp_norm, w_router, b_router, w_up, b_up, w_down, b_down):
    b, s, d = x.shape
    t = b * s
    wq = b_w_uq.reshape(B_Q_RANK, B_HEADS, B_NOPE_DIM + B_ROPE_DIM)
    wq_pe = jnp.pad(wq[:, :, B_NOPE_DIM:], ((0, 0), (0, 0), (0, LANES - B_ROPE_DIM)))
    wuq = jnp.concatenate([wq[:, :, :B_NOPE_DIM].reshape(B_Q_RANK, -1), wq_pe.reshape(B_Q_RANK, -1)],
                          axis=1).astype(BF16)
    wkv = b_w_ukv.reshape(B_KV_RANK, B_HEADS, B_NOPE_DIM + B_V_DIM)
    wukv = jnp.concatenate([wkv[:, :, :B_NOPE_DIM].reshape(B_KV_RANK, -1),
                            wkv[:, :, B_NOPE_DIM:].reshape(B_KV_RANK, -1)], axis=1).astype(BF16)
    cos, sin = _rope_tables(s)

    qa, ka, va, qb, kb, vb = _prologue(x, attn_norm[None], w_in, b_q_norm[None], wuq, b_kv_norm[None], wukv,
                                       cos, sin)
    ya = _window_attn(qa, ka, va, a_sink, _window_bias())
    yb = _mla_attn(qb, kb, vb)
    wr_hi = w_router.astype(BF16)
    wr_lo = (w_router - wr_hi.astype(F32)).astype(BF16)
    x1, xm, eidx, gates, rank, counts = _out_router(
        ya.reshape(t, A_WIDTH), yb.reshape(t, B_WIDTH), x.reshape(t, d), out_norm_a[None], out_norm_b[None],
        w_o.astype(BF16), mlp_norm[None], jnp.concatenate([wr_hi, wr_lo], axis=1), b_router[None])

    nblk = (t * TOP_K + N_EXPERTS * (MOE_BM - 1) + MOE_BM - 1) // MOE_BM
    tabs, start, pad_start, pad_len, nact = _expert_schedule(counts.reshape(N_EXPERTS), nblk, t * TOP_K)
    slot_rows = _slot_rows(start, eidx.reshape(t * TOP_K), rank.reshape(t * TOP_K))
    xs = _dispatch(pad_start, pad_len, nact, slot_rows, xm, nblk * MOE_BM)
    ys = _experts(tabs, xs, w_up, b_up[:, None, :], w_down, b_down[:, None, :])
    return x1, slot_rows, gates, ys


def kernel(x, attn_norm, w_in, a_sink, b_q_norm, b_w_uq, b_kv_norm, b_w_ukv, out_norm_a, out_norm_b, w_o,
           mlp_norm, w_router, b_router, w_up, b_up, w_down, b_down, final_norm):
    b, s, d = x.shape
    assert d == D_MODEL and s % min(TQ_MLA, s) == 0 and s % min(CK_MLA, s) == 0 and s % TM_PRO == 0 and attn_norm.shape[0] == 1
    x1, slot_rows, gates, ys = _layer(
        x, attn_norm[0], w_in[0], a_sink[0], b_q_norm[0], b_w_uq[0], b_kv_norm[0], b_w_ukv[0], out_norm_a[0],
        out_norm_b[0], w_o[0], mlp_norm[0], w_router[0], b_router[0], w_up[0], b_up[0], w_down[0], b_down[0])
    out = _combine(slot_rows, gates, x1, final_norm[None], ys)
    return out.reshape(b, s, d)
```

```python
import functools

import jax
import numpy as np
import jax.numpy as jnp
from jax import lax
from jax.experimental import pallas as pl
from jax.experimental.pallas import tpu as pltpu

D_MODEL = 2048
A_HEADS, A_KV_HEADS, A_HEAD_DIM = 16, 4, 64
A_GROUP = A_HEADS // A_KV_HEADS
WINDOW = 128
A_BLOCK = 128
B_HEADS, B_Q_RANK, B_KV_RANK = 8, 512, 256
B_NOPE_DIM, B_ROPE_DIM, B_V_DIM = 128, 64, 128
ROPE_THETA = 10000.0
A_WIDTH = A_HEADS * A_HEAD_DIM
B_WIDTH = B_HEADS * B_V_DIM
A_KV_COLS = A_KV_HEADS * A_HEAD_DIM
N_EXPERTS, TOP_K, D_FF = 32, 4, 2048
SWIGLU_ALPHA, SWIGLU_LIMIT = 1.702, 7.0
EPS = 1e-5

LANES = 128
SUBLANES = 8
B_QK_PAD = 2 * LANES
IN_COLS_PAD = A_WIDTH + 2 * A_KV_COLS + B_Q_RANK + B_KV_RANK + LANES
VMEM_LIMIT = 56 * 1024 * 1024
ROW_TILES = D_MODEL // LANES
ROW_PITCH = ROW_TILES + 1

TM_PRO = 256
TQ_MLA = 1024
CK_MLA = 1024
HEADS_MLA = 2
TM_OUT = 512
TM_DISPATCH = 512
TM_COMBINE = 128
MOE_BM = 512
MOE_SUB = 256
MOE_RM = 2048
MOE_TF = 256
WAIT_UNROLL = 16

BF16 = jnp.bfloat16
F32 = jnp.float32


def _rms(x, g):
    return x * lax.rsqrt(jnp.mean(x * x, axis=-1, keepdims=True) + EPS) * g


def _const_spec(shape):
    nd = len(shape)
    return pl.BlockSpec(shape, lambda *_: (0,) * nd, pipeline_mode=pl.Buffered(1))


def _store_token_rows(ref, first, val):
    n = val.shape[0]
    for j in range(ROW_TILES):
        ref[pl.ds(first * ROW_PITCH + j, n, stride=ROW_PITCH), :] = val[:, j * LANES:(j + 1) * LANES]
    ref[pl.ds(first * ROW_PITCH + ROW_TILES, n, stride=ROW_PITCH), :] = jnp.zeros((n, LANES), val.dtype)


def _load_token_rows(ref, n):
    return [ref[pl.ds(j, n, stride=ROW_PITCH), :] for j in range(ROW_TILES)]


def _rope(x, cos, sin):
    lane = lax.broadcasted_iota(jnp.int32, x.shape, 1)
    up = pltpu.roll(x, LANES - B_ROPE_DIM // 2, 1)
    dn = pltpu.roll(x, B_ROPE_DIM // 2, 1)
    sw = jnp.where(lane < B_ROPE_DIM // 2, -up, jnp.where(lane < B_ROPE_DIM, dn, 0.0))
    return x * cos + sw * sin


def _prologue_body(x_ref, g_ref, win_ref, qn_ref, wuq_ref, kvn_ref, wukv_ref, cos_ref, sin_ref,
                   qa_ref, ka_ref, va_ref, qb_ref, kb_ref, vb_ref, winb_ref):
    @pl.when((pl.program_id(0) == 0) & (pl.program_id(1) == 0))
    def _():
        n_in = win_ref.shape[1]
        winb_ref[:, :n_in] = win_ref[...].astype(BF16)
        winb_ref[:, n_in:] = jnp.zeros((winb_ref.shape[0], IN_COLS_PAD - n_in), BF16)

    x = x_ref[0]
    xn = _rms(x, g_ref[...]).astype(BF16)
    proj = jnp.dot(xn, winb_ref[...], preferred_element_type=F32)
    c0 = A_WIDTH
    c1 = c0 + A_KV_COLS
    c2 = c1 + A_KV_COLS
    c3 = c2 + B_Q_RANK
    c4 = c3 + B_KV_RANK
    qa_ref[0] = (proj[:, :A_WIDTH] * A_HEAD_DIM ** -0.5).astype(BF16)
    low = lax.broadcasted_iota(jnp.int32, (proj.shape[0], LANES), 1) < A_HEAD_DIM
    for src, dst in ((c0, ka_ref), (c1, va_ref)):
        for pair in range(A_KV_HEADS // 2):
            two = proj[:, src + pair * LANES:src + (pair + 1) * LANES]
            swapped = pltpu.roll(two, A_HEAD_DIM, 1)
            dst[0, 4 * pair + 0] = jnp.where(low, two, 0.0).astype(BF16)
            dst[0, 4 * pair + 1] = jnp.where(low, 0.0, swapped).astype(BF16)
            dst[0, 4 * pair + 2] = jnp.where(low, swapped, 0.0).astype(BF16)
            dst[0, 4 * pair + 3] = jnp.where(low, 0.0, two).astype(BF16)
    cq = _rms(proj[:, c2:c3], qn_ref[...]).astype(BF16)
    q = jnp.dot(cq, wuq_ref[...], preferred_element_type=F32)
    ckv = _rms(proj[:, c3:c4], kvn_ref[...]).astype(BF16)
    kv = jnp.dot(ckv, wukv_ref[...], preferred_element_type=F32)
    cos = cos_ref[...]
    sin = sin_ref[...]
    kpe = _rope(proj[:, c4:c4 + LANES], cos, sin).astype(BF16)
    b_scale = (B_NOPE_DIM + B_ROPE_DIM) ** -0.5
    hw = B_HEADS * LANES
    for h in range(B_HEADS):
        sl = slice(h * LANES, (h + 1) * LANES)
        qb_ref[0, h, :, :LANES] = (q[:, sl] * b_scale).astype(BF16)
        qpe = _rope(q[:, hw + h * LANES:hw + (h + 1) * LANES], cos, sin)
        qb_ref[0, h, :, LANES:] = (qpe * b_scale).astype(BF16)
        kb_ref[0, h, :, :LANES] = kv[:, sl].astype(BF16)
        kb_ref[0, h, :, LANES:] = kpe
        vb_ref[0, h] = kv[:, hw + h * LANES:hw + (h + 1) * LANES].astype(BF16)


def _prologue(x, g, win, qn, wuq, kvn, wukv, cos, sin):
    b, s, d = x.shape
    tm = TM_PRO
    grid = (b, s // tm)
    sds = jax.ShapeDtypeStruct
    out_shape = (
        sds((b, s, A_WIDTH), BF16),
        sds((b, 2 * A_KV_HEADS, s, LANES), BF16),
        sds((b, 2 * A_KV_HEADS, s, LANES), BF16),
        sds((b, B_HEADS, s, B_QK_PAD), BF16),
        sds((b, B_HEADS, s, B_QK_PAD), BF16),
        sds((b, B_HEADS, s, B_V_DIM), BF16),
    )

    def hspec(nh, w):
        return pl.BlockSpec((1, nh, tm, w), lambda bi, i: (bi, 0, i, 0))

    return pl.pallas_call(
        _prologue_body,
        grid=grid,
        in_specs=[
            pl.BlockSpec((1, tm, d), lambda bi, i: (bi, i, 0)),
            _const_spec(g.shape), _const_spec(win.shape), _const_spec(qn.shape), _const_spec(wuq.shape),
            _const_spec(kvn.shape), _const_spec(wukv.shape),
            pl.BlockSpec((tm, LANES), lambda bi, i: (i, 0)),
            pl.BlockSpec((tm, LANES), lambda bi, i: (i, 0)),
        ],
        out_specs=(pl.BlockSpec((1, tm, A_WIDTH), lambda bi, i: (bi, i, 0)),
                   hspec(2 * A_KV_HEADS, LANES), hspec(2 * A_KV_HEADS, LANES),
                   hspec(B_HEADS, B_QK_PAD), hspec(B_HEADS, B_QK_PAD), hspec(B_HEADS, B_V_DIM)),
        out_shape=out_shape,
        scratch_shapes=[pltpu.VMEM((d, IN_COLS_PAD), BF16)],
        compiler_params=pltpu.CompilerParams(dimension_semantics=("arbitrary", "arbitrary"),
                                             vmem_limit_bytes=VMEM_LIMIT),
        name="prologue",
    )(x, g, win, qn, wuq, kvn, wukv, cos, sin)


def _window_body(sink_ref, q_ref, kp_ref, kc_ref, kn_ref, vp_ref, vc_ref, vn_ref, bias_ref, o_ref):
    two = 2 * A_BLOCK
    low = lax.broadcasted_iota(jnp.int32, (two, LANES), 1) < A_HEAD_DIM
    for hk in range(A_KV_HEADS):
        q2 = jnp.concatenate([q_ref[0, :, (2 * hk) * LANES:(2 * hk + 1) * LANES],
                              q_ref[0, :, (2 * hk + 1) * LANES:(2 * hk + 2) * LANES]], axis=0)
        acc = jnp.zeros((two, LANES), F32)
        inv = []
        for half in range(2):
            z = 2 * hk + half
            kband = jnp.concatenate([kp_ref[0, z], kc_ref[0, z], kn_ref[0, z]], axis=0)
            vband = jnp.concatenate([vp_ref[0, z], vc_ref[0, z], vn_ref[0, z]], axis=0)
            s = lax.dot_general(q2, kband, (((1,), (1,)), ((), ())), preferred_element_type=F32)
            s = s + bias_ref[0, z]
            sink = jnp.concatenate(
                [jnp.full((A_BLOCK, 1), sink_ref[hk * A_GROUP + 2 * jj + half], F32) for jj in range(2)], axis=0)
            m = jnp.maximum(jnp.max(s, axis=-1, keepdims=True), sink)
            p = jnp.exp(s - m)
            den = jnp.sum(p, axis=-1, keepdims=True) + jnp.exp(sink - m)
            inv.append(1.0 / den)
            acc = acc + jnp.dot(p.astype(BF16), vband, preferred_element_type=F32)
        o = (acc * jnp.where(low, inv[0], inv[1])).astype(BF16)
        o_ref[0, :, (2 * hk) * LANES:(2 * hk + 1) * LANES] = o[:A_BLOCK]
        o_ref[0, :, (2 * hk + 1) * LANES:(2 * hk + 2) * LANES] = o[A_BLOCK:]


def _window_attn(qa, ka, va, sink, bias):
    b, s, _ = qa.shape
    nb = s // A_BLOCK
    kv_blk = (1, 2 * A_KV_HEADS, A_BLOCK, LANES)
    prev = pl.BlockSpec(kv_blk, lambda bi, n: (bi, 0, jnp.maximum(n - 1, 0), 0))
    cur = pl.BlockSpec(kv_blk, lambda bi, n: (bi, 0, n, 0))
    nxt = pl.BlockSpec(kv_blk, lambda bi, n: (bi, 0, jnp.minimum(n + 1, nb - 1), 0))

    def edge(bi, n):
        return ((n == 0).astype(jnp.int32) + 2 * (n == nb - 1).astype(jnp.int32), 0, 0, 0)

    return pl.pallas_call(
        _window_body,
        grid=(b, nb),
        in_specs=[
            pl.BlockSpec(memory_space=pltpu.SMEM),
            pl.BlockSpec((1, A_BLOCK, A_WIDTH), lambda bi, n: (bi, n, 0)),
            prev, cur, nxt, prev, cur, nxt,
            pl.BlockSpec((1,) + bias.shape[1:], edge),
        ],
        out_specs=pl.BlockSpec((1, A_BLOCK, A_WIDTH), lambda bi, n: (bi, n, 0)),
        out_shape=jax.ShapeDtypeStruct((b, s, A_WIDTH), BF16),
        compiler_params=pltpu.CompilerParams(dimension_semantics=("parallel", "parallel"),
                                             vmem_limit_bytes=VMEM_LIMIT),
        name="window_attn",
    )(sink, qa, ka, ka, ka, va, va, va, bias)


def _mla_body(q_ref, k_ref, v_ref, o_ref):
    tq = q_ref.shape[2]
    s_len = k_ref.shape[2]
    ck = min(CK_MLA, s_len)
    for hh in range(HEADS_MLA):
        q = q_ref[0, hh]
        m = jnp.full((tq, 1), -jnp.inf, F32)
        l = jnp.zeros((tq, 1), F32)
        acc = jnp.zeros((tq, B_V_DIM), F32)
        for c in range(s_len // ck):
            k_c = k_ref[0, hh, c * ck:(c + 1) * ck, :]
            v_c = v_ref[0, hh, c * ck:(c + 1) * ck, :]
            s = lax.dot_general(q, k_c, (((1,), (1,)), ((), ())), preferred_element_type=F32)
            m_new = jnp.maximum(m, jnp.max(s, axis=-1, keepdims=True))
            alpha = jnp.exp(m - m_new)
            p = jnp.exp(s - m_new)
            l = alpha * l + jnp.sum(p, axis=-1, keepdims=True)
            acc = alpha * acc + jnp.dot(p.astype(BF16), v_c, preferred_element_type=F32)
            m = m_new
        o_ref[0, :, hh * B_V_DIM:(hh + 1) * B_V_DIM] = (acc / l).astype(BF16)


def _mla_attn(qb, kb, vb):
    b, nh, s, _ = qb.shape
    tq = min(TQ_MLA, s)
    hs = HEADS_MLA
    return pl.pallas_call(
        _mla_body,
        grid=(b, nh // hs, s // tq),
        in_specs=[
            pl.BlockSpec((1, hs, tq, B_QK_PAD), lambda bi, h, i: (bi, h, i, 0)),
            pl.BlockSpec((1, hs, s, B_QK_PAD), lambda bi, h, i: (bi, h, 0, 0)),
            pl.BlockSpec((1, hs, s, B_V_DIM), lambda bi, h, i: (bi, h, 0, 0)),
        ],
        out_specs=pl.BlockSpec((1, tq, hs * B_V_DIM), lambda bi, h, i: (bi, i, h)),
        out_shape=jax.ShapeDtypeStruct((b, s, B_WIDTH), BF16),
        compiler_params=pltpu.CompilerParams(dimension_semantics=("parallel", "parallel", "parallel"),
                                             vmem_limit_bytes=VMEM_LIMIT),
        name="mla_attn",
    )(qb, kb, vb)


def _out_router_body(ya_ref, yb_ref, x_ref, ga_ref, gb_ref, wo_ref, gm_ref, wr_ref, br_ref,
                     x1_ref, xm_ref, eidx_ref, gate_ref, rank_ref, cnt_ref, run_ref):
    i = pl.program_id(0)

    @pl.when(i == 0)
    def _():
        run_ref[...] = jnp.zeros_like(run_ref)

    na = _rms(ya_ref[...].astype(F32), ga_ref[...]).astype(BF16)
    nb = _rms(yb_ref[...].astype(F32), gb_ref[...]).astype(BF16)
    att = jnp.dot(na, wo_ref[:A_WIDTH, :], preferred_element_type=F32)
    att = att + jnp.dot(nb, wo_ref[A_WIDTH:, :], preferred_element_type=F32)
    x1 = x_ref[...] + att
    x1_ref[...] = x1
    hn = _rms(x1, gm_ref[...])
    _store_token_rows(xm_ref, 0, hn)
    tm = hn.shape[0]
    hi = hn.astype(BF16)
    lo = (hn - hi.astype(F32)).astype(BF16)
    prod = jnp.dot(jnp.concatenate([hi, lo], axis=0), wr_ref[...], preferred_element_type=F32)
    logits = (prod[:tm, :N_EXPERTS] + prod[:tm, N_EXPERTS:] + prod[tm:, :N_EXPERTS] + prod[tm:, N_EXPERTS:]
              + br_ref[...])
    lane = lax.broadcasted_iota(jnp.int32, (tm, N_EXPERTS), 1)
    work = logits
    sel = jnp.zeros((tm, N_EXPERTS), F32)
    hots, vals, idxs = [], [], []
    for _k in range(TOP_K):
        mx = jnp.max(work, axis=-1, keepdims=True)
        idx = jnp.min(jnp.where(work == mx, lane, N_EXPERTS), axis=-1, keepdims=True)
        hot = lane == idx
        hots.append(hot)
        vals.append(mx)
        idxs.append(idx)
        sel = sel + hot.astype(F32)
        work = jnp.where(hot, -jnp.inf, work)
    exps = [jnp.exp(v - vals[0]) for v in vals]
    den = exps[0] + exps[1] + exps[2] + exps[3]
    r_i = lax.broadcasted_iota(jnp.int32, (tm, tm), 0)
    c_i = lax.broadcasted_iota(jnp.int32, (tm, tm), 1)
    tri = (c_i < r_i).astype(BF16)
    before = jnp.dot(tri, sel.astype(BF16), preferred_element_type=F32) + run_ref[...]
    lane4 = lax.broadcasted_iota(jnp.int32, (tm, TOP_K), 1)
    eidx = jnp.zeros((tm, TOP_K), jnp.int32)
    gate = jnp.zeros((tm, TOP_K), F32)
    rank = jnp.zeros((tm, TOP_K), jnp.int32)
    for k in range(TOP_K):
        rk = jnp.sum(jnp.where(hots[k], before, 0.0), axis=-1, keepdims=True).astype(jnp.int32)
        eidx = jnp.where(lane4 == k, idxs[k], eidx)
        gate = jnp.where(lane4 == k, exps[k] / den, gate)
        rank = jnp.where(lane4 == k, rk, rank)
    eidx_ref[...] = eidx
    gate_ref[...] = gate
    rank_ref[...] = rank
    run = run_ref[...] + jnp.sum(sel, axis=0, keepdims=True)
    run_ref[...] = run
    cnt_ref[...] = run.astype(jnp.int32)


def _out_router(ya, yb, x2d, ga, gb, wo, gm, wr, br):
    t, d = x2d.shape
    tm = min(TM_OUT, t)
    sds = jax.ShapeDtypeStruct

    def row(w):
        return pl.BlockSpec((tm, w), lambda i: (i, 0))

    return pl.pallas_call(
        _out_router_body,
        grid=(t // tm,),
        in_specs=[row(A_WIDTH), row(B_WIDTH), row(d), _const_spec(ga.shape), _const_spec(gb.shape),
                  _const_spec(wo.shape), _const_spec(gm.shape), _const_spec(wr.shape), _const_spec(br.shape)],
        out_specs=(row(d), pl.BlockSpec((tm * ROW_PITCH, LANES), lambda i: (i, 0)),
                   row(TOP_K), row(TOP_K), row(TOP_K),
                   pl.BlockSpec((1, N_EXPERTS), lambda i: (0, 0))),
        out_shape=(sds((t, d), F32), sds((t * ROW_PITCH, LANES), F32), sds((t, TOP_K), jnp.int32),
                   sds((t, TOP_K), F32),
                   sds((t, TOP_K), jnp.int32), sds((1, N_EXPERTS), jnp.int32)),
        scratch_shapes=[pltpu.VMEM((1, N_EXPERTS), F32)],
        compiler_params=pltpu.CompilerParams(dimension_semantics=("arbitrary",),
                                             vmem_limit_bytes=VMEM_LIMIT),
        name="out_router",
    )(ya, yb, x2d, ga, gb, wo, gm, wr, br)


def _token_copy(src_ref, src_tok, dst_ref, dst_tok, sem, ntok=1):
    return _rows_copy(src_ref, src_tok * ROW_PITCH, dst_ref, dst_tok * ROW_PITCH, sem, ntok)


def _rows_copy(src_ref, src_row, dst_ref, dst_row, sem, ntok=1):
    rows = ntok * ROW_PITCH
    return pltpu.make_async_copy(src_ref.at[pl.ds(src_row, rows)], dst_ref.at[pl.ds(dst_row, rows)], sem)


def _slot_rows_body(start_ref, eidx_ref, rank_ref, o_ref):
    e = eidx_ref[...]
    base = jnp.zeros_like(e)
    for ex in range(N_EXPERTS):
        base = jnp.where(e == ex, start_ref[ex], base)
    o_ref[...] = (base + rank_ref[...]) * ROW_PITCH


def _slot_rows(start, eidx, rank):
    n = eidx.shape[0]
    shape2 = (n // LANES, LANES)
    out = pl.pallas_call(
        _slot_rows_body,
        in_specs=[pl.BlockSpec(memory_space=pltpu.SMEM), pl.BlockSpec(shape2, lambda: (0, 0)),
                  pl.BlockSpec(shape2, lambda: (0, 0))],
        out_specs=pl.BlockSpec(shape2, lambda: (0, 0)),
        out_shape=jax.ShapeDtypeStruct(shape2, jnp.int32),
        name="slot_rows",
    )(start, eidx.reshape(shape2), rank.reshape(shape2))
    return out.reshape(n)


def _zero_fill_pads(pad_start_ref, pad_len_ref, nact_ref, xs_ref, zero_ref, zsem):
    bm = zero_ref.shape[0] // ROW_PITCH
    nblk = xs_ref.shape[0] // (bm * ROW_PITCH)
    zero_ref[...] = jnp.zeros_like(zero_ref)

    def sweep(do):
        def per_expert(e, c):
            off = pad_start_ref[e]
            n = pad_len_ref[e]
            p = bm // 2
            while p >= 1:
                take = (n & p) != 0

                @pl.when(take)
                def _(p=p, off=off):
                    do(_token_copy(zero_ref, 0, xs_ref, off, zsem, p))

                off = off + jnp.where(take, p, 0)
                p //= 2
            return c

        lax.fori_loop(0, N_EXPERTS, per_expert, 0)

        def tail(j, c):
            do(_token_copy(zero_ref, 0, xs_ref, j * bm, zsem, bm))
            return c

        lax.fori_loop(nact_ref[0], nblk, tail, 0)

    sweep(lambda cp: cp.start())
    sweep(lambda cp: cp.wait())


def _dispatch_body(pad_start_ref, pad_len_ref, nact_ref, slot_ref, xm_ref, xs_ref, zero_ref, sem, zsem):
    tm = slot_ref.shape[0] // TOP_K

    @pl.when(pl.program_id(0) == 0)
    def _():
        _zero_fill_pads(pad_start_ref, pad_len_ref, nact_ref, xs_ref, zero_ref, zsem)

    def issue(i, c):
        for k in range(TOP_K):
            _rows_copy(xm_ref, i * ROW_PITCH, xs_ref, slot_ref[i * TOP_K + k], sem).start(priority=k % 2)
        return c

    lax.fori_loop(0, tm, issue, 0, unroll=2)

    def drain(i, c):
        for _ in range(WAIT_UNROLL):
            _token_copy(xm_ref, 0, xs_ref, 0, sem).wait()
        return c

    lax.fori_loop(0, tm * TOP_K // WAIT_UNROLL, drain, 0)


def _dispatch(pad_start, pad_len, nact, slot_rows, xm, n_slots):
    t = xm.shape[0] // ROW_PITCH
    tm = min(TM_DISPATCH, t)
    grid_spec = pltpu.PrefetchScalarGridSpec(
        num_scalar_prefetch=3,
        grid=(t // tm,),
        in_specs=[pl.BlockSpec((tm * TOP_K,), lambda i, *_: (i,), memory_space=pltpu.SMEM),
                  pl.BlockSpec((tm * ROW_PITCH, LANES), lambda i, *_: (i, 0))],
        out_specs=pl.BlockSpec(memory_space=pl.ANY),
        scratch_shapes=[pltpu.VMEM((MOE_BM * ROW_PITCH, LANES), F32), pltpu.SemaphoreType.DMA(()),
                        pltpu.SemaphoreType.DMA(())],
    )
    return pl.pallas_call(
        _dispatch_body,
        grid_spec=grid_spec,
        out_shape=jax.ShapeDtypeStruct((n_slots * ROW_PITCH, LANES), F32),
        compiler_params=pltpu.CompilerParams(dimension_semantics=("arbitrary",), has_side_effects=True,
                                             vmem_limit_bytes=VMEM_LIMIT),
        name="dispatch",
    )(pad_start, pad_len, nact, slot_rows, xm)


STEP_UP, STEP_DOWN, STEP_TAIL, STEP_IDLE = 0, 1, 2, 3
PREFETCH_PER_DOWN = 2
NF = D_FF // MOE_TF
PASS_BLOCKS = MOE_RM // MOE_BM


def _swiglu(hg, hu):
    gate = jnp.minimum(hg, SWIGLU_LIMIT)
    up = jnp.clip(hu, -SWIGLU_LIMIT, SWIGLU_LIMIT)
    glu = gate / (1.0 + jnp.exp(-SWIGLU_ALPHA * gate))
    return (up + 1.0) * glu


def _experts_body(e_ref, f_ref, kind_ref, blk_ref, blk0_ref, nsub_ref, pre_ref, nxb_ref,
                  xs_ref, wg_ref, wu_ref, bg_ref, bu_ref, wd_ref, bd_ref,
                  y_ref, xb_ref, h_ref, wdb_ref, stg_ref, sem):
    t = pl.program_id(0)
    kind = kind_ref[t]
    f = f_ref[t]
    nsub = nsub_ref[t]

    def x_copy(first_blk, j, slot):
        rows = MOE_SUB * ROW_PITCH
        first = pl.multiple_of((first_blk * MOE_BM + j * MOE_SUB) * ROW_PITCH, SUBLANES)
        return pltpu.make_async_copy(xs_ref.at[pl.ds(first, rows)], stg_ref.at[slot], sem.at[slot])

    def unpack(j, slot):
        rows = pl.ds(pl.multiple_of(j * MOE_SUB, MOE_SUB), MOE_SUB)
        for jt, tile in enumerate(_load_token_rows(stg_ref.at[slot], MOE_SUB)):
            xb_ref[rows, jt * LANES:(jt + 1) * LANES] = tile.astype(BF16)

    @pl.when(kind == STEP_UP)
    def _up():
        pre = pre_ref[t]

        @pl.when((f == 0) & (pre < nsub))
        def _load_rows():
            blk0 = blk0_ref[t]
            x_copy(blk0, pre, pre % 2).start()

            def body(j, c):
                slot = j % 2

                @pl.when(j + 1 < nsub)
                def _():
                    x_copy(blk0, j + 1, 1 - slot).start()

                x_copy(blk0, j, slot).wait()
                unpack(j, slot)
                return c

            lax.fori_loop(pre, nsub, body, 0)

        wdb_ref[pl.ds(pl.multiple_of(f * MOE_TF, MOE_TF), MOE_TF), :] = wd_ref[0].astype(BF16)

        def up_rows(row0, nrows):
            x = xb_ref[pl.ds(row0, nrows), :]
            hg = jnp.dot(x, wg_ref[0].astype(BF16), preferred_element_type=F32) + bg_ref[0]
            hu = jnp.dot(x, wu_ref[0].astype(BF16), preferred_element_type=F32) + bu_ref[0]
            act = _swiglu(hg, hu).astype(BF16)
            h_ref[pl.ds(row0, nrows), pl.ds(pl.multiple_of(f * MOE_TF, MOE_TF), MOE_TF)] = act

        quad = 4 * MOE_SUB
        nquad = nsub // 4
        rem = nsub % 4

        def quad_rows(i, c):
            base = pl.multiple_of(i * quad, quad)
            up_rows(base, quad // 2)
            up_rows(base + quad // 2, quad // 2)
            return c

        merged = nsub == 5

        @pl.when(merged)
        def _():
            up_rows(0, 2 * MOE_SUB)
            up_rows(2 * MOE_SUB, 3 * MOE_SUB)

        @pl.when(jnp.logical_not(merged))
        def _():
            lax.fori_loop(0, nquad, quad_rows, 0)

            @pl.when(rem >= 2)
            def _():
                up_rows(pl.multiple_of(nquad * quad, quad), 2 * MOE_SUB)

            @pl.when(rem % 2 == 1)
            def _():
                up_rows(pl.multiple_of(nquad * quad + (rem // 2) * 2 * MOE_SUB, MOE_SUB), MOE_SUB)

    @pl.when(kind == STEP_DOWN)
    def _down():
        jblk = blk_ref[t] - blk0_ref[t]
        row0 = pl.multiple_of(jblk * MOE_BM, MOE_BM)
        nx_nsub = pre_ref[t]
        nx_blk0 = nxb_ref[t]
        ahead = [jblk * PREFETCH_PER_DOWN + u for u in range(PREFETCH_PER_DOWN)]
        for u, j in enumerate(ahead):
            @pl.when(j < nx_nsub)
            def _(u=u, j=j):
                x_copy(nx_blk0, j, u).start()

        def down_rows(nrows):
            hrows = h_ref[pl.ds(row0, nrows), :]
            return jnp.dot(hrows, wdb_ref[...], preferred_element_type=F32) + bd_ref[0]

        @pl.when(nsub == 2)
        def _():
            _store_token_rows(y_ref, 0, down_rows(MOE_BM))

        @pl.when(nsub == 1)
        def _():
            _store_token_rows(y_ref, 0, down_rows(MOE_SUB))
            y_ref[MOE_SUB * ROW_PITCH:, :] = jnp.zeros(((MOE_BM - MOE_SUB) * ROW_PITCH, LANES), F32)

        for u, j in enumerate(ahead):
            @pl.when(j < nx_nsub)
            def _(u=u, j=j):
                x_copy(nx_blk0, j, u).wait()
                unpack(j, u)

    @pl.when(kind == STEP_TAIL)
    def _tail():
        y_ref[...] = jnp.zeros_like(y_ref)


def _schedule_body(n_blocks, cnt_ref, e_ref, f_ref, kind_ref, blk_ref, blk0_ref, nsub_ref, pre_ref, nxb_ref,
                   start_ref, pad_start_ref, pad_len_ref, nact_ref):
    n_steps = e_ref.shape[0]
    sub_per_blk = MOE_BM // MOE_SUB
    zero = jnp.int32(0)

    def put(idx, e, f, kind, blk, blk0, nsub, pre):
        e_ref[idx] = e
        f_ref[idx] = f
        kind_ref[idx] = kind
        blk_ref[idx] = blk
        blk0_ref[idx] = blk0
        nsub_ref[idx] = nsub
        pre_ref[idx] = pre
        nxb_ref[idx] = zero

    def per_expert(e, carry):
        t, blk, last_e, prev_down, prev_nb = carry
        n = cnt_ref[e]
        nblk = (n + (MOE_BM - 1)) // MOE_BM
        start_ref[e] = blk * MOE_BM
        pad_start_ref[e] = blk * MOE_BM + n
        pad_len_ref[e] = nblk * MOE_BM - n

        def per_pass(p, carry2):
            t2, pdown, pnb = carry2
            b0 = blk + p * PASS_BLOCKS
            nb = jnp.minimum(nblk - p * PASS_BLOCKS, PASS_BLOCKS)
            nsub = (jnp.minimum(n - p * MOE_RM, MOE_RM) + (MOE_SUB - 1)) // MOE_SUB
            pre = jnp.minimum(nsub, pnb * PREFETCH_PER_DOWN)
            for f in range(NF):
                put(t2 + f, e, f, STEP_UP, b0, b0, nsub, pre)

            def patch(j, c):
                pre_ref[pdown + j] = nsub
                nxb_ref[pdown + j] = b0
                return c

            lax.fori_loop(0, pnb, patch, 0)

            def per_blk(j, c):
                put(t2 + NF + j, e, NF - 1, STEP_DOWN, b0 + j, b0,
                    jnp.clip(nsub - j * sub_per_blk, 1, sub_per_blk), zero)
                return c

            lax.fori_loop(0, nb, per_blk, 0)
            return t2 + NF + nb, t2 + NF, nb

        t, prev_down, prev_nb = lax.fori_loop(0, (nblk + (PASS_BLOCKS - 1)) // PASS_BLOCKS, per_pass,
                                              (t, prev_down, prev_nb))
        return t, blk + nblk, jnp.where(n > 0, e, last_e), prev_down, prev_nb

    t, nact, last_e, _, _ = lax.fori_loop(0, N_EXPERTS, per_expert, (zero, zero, zero, zero, zero))
    nact_ref[0] = nact
    nact_ref[1] = t + n_blocks - nact

    def spare(i, c):
        is_tail = i < n_blocks - nact
        blk = jnp.where(is_tail, nact + i, n_blocks - 1)
        put(t + i, last_e, NF - 1, jnp.where(is_tail, STEP_TAIL, STEP_IDLE), blk, blk, 1, zero)
        return c

    lax.fori_loop(0, n_steps - t, spare, 0)


def _expert_schedule(counts, n_blocks, n_assign):
    n_pass_max = N_EXPERTS + n_assign // MOE_RM
    n_steps = n_pass_max * NF + n_blocks
    smem = pl.BlockSpec(memory_space=pltpu.SMEM)
    i32 = jnp.int32
    out_shape = tuple(jax.ShapeDtypeStruct((n,), i32) for n in (n_steps,) * 8 + (N_EXPERTS,) * 3 + (2,))
    outs = pl.pallas_call(
        functools.partial(_schedule_body, n_blocks),
        in_specs=[smem],
        out_specs=tuple(smem for _ in out_shape),
        out_shape=out_shape,
        name="schedule",
    )(counts)
    return outs[:8], outs[8], outs[9], outs[10], outs[11]


def _experts(tabs, xs, w_up, b_up, w_down, b_down, name):
    d = D_MODEL
    n_steps = tabs[0].shape[0]
    grid_spec = pltpu.PrefetchScalarGridSpec(
        num_scalar_prefetch=len(tabs),
        grid=(n_steps,),
        in_specs=[
            pl.BlockSpec(memory_space=pl.ANY),
            pl.BlockSpec((1, d, MOE_TF), lambda t, e, f, *_: (e[t], 0, f[t])),
            pl.BlockSpec((1, d, MOE_TF), lambda t, e, f, *_: (e[t], 0, NF + f[t])),
            pl.BlockSpec((1, 1, MOE_TF), lambda t, e, f, *_: (e[t], 0, f[t])),
            pl.BlockSpec((1, 1, MOE_TF), lambda t, e, f, *_: (e[t], 0, NF + f[t])),
            pl.BlockSpec((1, MOE_TF, d), lambda t, e, f, *_: (e[t], f[t], 0)),
            pl.BlockSpec((1, 1, d), lambda t, e, f, *_: (e[t], 0, 0)),
        ],
        out_specs=pl.BlockSpec((MOE_BM * ROW_PITCH, LANES), lambda t, e, f, kind, blk, *_: (blk[t], 0)),
        scratch_shapes=[
            pltpu.VMEM((MOE_RM, d), BF16),
            pltpu.VMEM((MOE_RM, D_FF), BF16),
            pltpu.VMEM((D_FF, d), BF16),
            pltpu.VMEM((2, MOE_SUB * ROW_PITCH, LANES), F32),
            pltpu.SemaphoreType.DMA((2,)),
        ],
    )
    return pl.pallas_call(
        _experts_body,
        grid_spec=grid_spec,
        out_shape=jax.ShapeDtypeStruct(xs.shape, F32),
        compiler_params=pltpu.CompilerParams(dimension_semantics=("arbitrary",),
                                             vmem_limit_bytes=VMEM_LIMIT),
        name=name,
    )(*tabs, xs, w_up, w_up, b_up, b_up, w_down, b_down)


def _combine_body(slot_ref, slot_nx_ref, gate_ref, x1_ref, gf_ref, ys_ref, o_ref, buf_ref, sem):
    step = pl.program_id(0)
    slot = step % 2
    tm = x1_ref.shape[0]

    def gather(rows_ref, s):
        def issue(i, c):
            for k in range(TOP_K):
                _rows_copy(ys_ref, rows_ref[i * TOP_K + k], buf_ref.at[s, k], i * ROW_PITCH,
                           sem.at[s]).start(priority=k % 2)
            return c

        lax.fori_loop(0, tm, issue, 0, unroll=2)

    @pl.when(step == 0)
    def _():
        gather(slot_ref, 0)

    for s in range(2):
        @pl.when((step + 1 < pl.num_programs(0)) & (slot == 1 - s))
        def _(s=s):
            gather(slot_nx_ref, s)

    def drain(i, c):
        for _ in range(WAIT_UNROLL):
            _token_copy(ys_ref, 0, buf_ref.at[slot, 0], 0, sem.at[slot]).wait()
        return c

    lax.fori_loop(0, tm * TOP_K // WAIT_UNROLL, drain, 0)
    gate = gate_ref[...]
    tiles = [x1_ref[:, j * LANES:(j + 1) * LANES] for j in range(ROW_TILES)]
    for k in range(TOP_K):
        g = gate[:, k:k + 1]
        for j, tile in enumerate(_load_token_rows(buf_ref.at[slot, k], tm)):
            tiles[j] = tiles[j] + g * tile
    o_ref[...] = _rms(jnp.concatenate(tiles, axis=1), gf_ref[...])


def _combine(slot_rows, gates, x1, gf, ys):
    t, d = x1.shape
    tm = min(TM_COMBINE, t)
    n = t // tm
    idx_blk = (tm * TOP_K,)
    return pl.pallas_call(
        _combine_body,
        grid=(n,),
        in_specs=[pl.BlockSpec(idx_blk, lambda i: (i,), memory_space=pltpu.SMEM),
                  pl.BlockSpec(idx_blk, lambda i: (jnp.minimum(i + 1, n - 1),), memory_space=pltpu.SMEM),
                  pl.BlockSpec((tm, TOP_K), lambda i: (i, 0)),
                  pl.BlockSpec((tm, d), lambda i: (i, 0)),
                  _const_spec(gf.shape),
                  pl.BlockSpec(memory_space=pl.ANY)],
        out_specs=pl.BlockSpec((tm, d), lambda i: (i, 0)),
        scratch_shapes=[pltpu.VMEM((2, TOP_K, tm * ROW_PITCH, LANES), F32), pltpu.SemaphoreType.DMA((2,))],
        out_shape=jax.ShapeDtypeStruct((t, d), F32),
        compiler_params=pltpu.CompilerParams(dimension_semantics=("arbitrary",),
                                             vmem_limit_bytes=VMEM_LIMIT),
        name="combine",
    )(slot_rows, slot_rows, gates, x1, gf, ys)


def _rope_tables(s):
    pos = np.arange(s, dtype=np.float32)
    inv_freq = np.power(np.float32(ROPE_THETA), -np.arange(0, B_ROPE_DIM, 2, dtype=np.float32) / B_ROPE_DIM)
    ang = (pos[:, None] * inv_freq[None, :]).astype(np.float32)
    cos, sin = np.cos(ang), np.sin(ang)
    pad = LANES - B_ROPE_DIM
    cos = np.concatenate([cos, cos, np.ones((s, pad), np.float32)], axis=1)
    sin = np.concatenate([sin, sin, np.zeros((s, pad), np.float32)], axis=1)
    return cos.astype(np.float32), sin.astype(np.float32)


def _window_bias():
    qi = np.arange(A_BLOCK)[:, None]
    kj = np.arange(3 * A_BLOCK)[None, :]
    dist = np.abs(qi + A_BLOCK - kj)
    slopes = np.power(np.float32(2.0), -8.0 * np.arange(1, A_HEADS + 1, dtype=np.float32) / A_HEADS)
    bias = -slopes[:, None, None] * dist.astype(np.float32)[None]
    bias = np.where((dist <= WINDOW)[None], bias, -np.inf)
    bias = bias.reshape(A_KV_HEADS, 2, 2, A_BLOCK, 3 * A_BLOCK).transpose(0, 2, 1, 3, 4)
    bias = bias.reshape(2 * A_KV_HEADS, 2 * A_BLOCK, 3 * A_BLOCK)
    no_prev = (kj < A_BLOCK)[None]
    no_next = (kj >= 2 * A_BLOCK)[None]
    ninf = -np.inf
    return np.stack([bias, np.where(no_prev, ninf, bias), np.where(no_next, ninf, bias),
                     np.where(no_prev | no_next, ninf, bias)]).astype(np.float32)


def _layer(x, attn_norm, w_in, a_sink, b_q_norm, b_w_uq, b_kv_norm, b_w_ukv, out_norm_a, out_norm_b, w_o,
           mlp_norm, w_router, b_router, w_up, b_up, w_down, b_down):
    b, s, d = x.shape
    t = b * s
    wq = b_w_uq.reshape(B_Q_RANK, B_HEADS, B_NOPE_DIM + B_ROPE_DIM)
    wq_pe = jnp.pad(wq[:, :, B_NOPE_DIM:], ((0, 0), (0, 0), (0, LANES - B_ROPE_DIM)))
    wuq = jnp.concatenate([wq[:, :, :B_NOPE_DIM].reshape(B_Q_RANK, -1), wq_pe.reshape(B_Q_RANK, -1)],
                          axis=1).astype(BF16)
    wkv = b_w_ukv.reshape(B_KV_RANK, B_HEADS, B_NOPE_DIM + B_V_DIM)
    wukv = jnp.concatenate([wkv[:, :, :B_NOPE_DIM].reshape(B_KV_RANK, -1),
                            wkv[:, :, B_NOPE_DIM:].reshape(B_KV_RANK, -1)], axis=1).astype(BF16)
    cos, sin = _rope_tables(s)

    qa, ka, va, qb, kb, vb = _prologue(x, attn_norm[None], w_in, b_q_norm[None], wuq, b_kv_norm[None], wukv,
                                       cos, sin)
    ya = _window_attn(qa, ka, va, a_sink, _window_bias())
    yb = _mla_attn(qb, kb, vb)
    wr_hi = w_router.astype(BF16)
    wr_lo = (w_router - wr_hi.astype(F32)).astype(BF16)
    x1, xm, eidx, gates, rank, counts = _out_router(
        ya.reshape(t, A_WIDTH), yb.reshape(t, B_WIDTH), x.reshape(t, d), out_norm_a[None], out_norm_b[None],
        w_o.astype(BF16), mlp_norm[None], jnp.concatenate([wr_hi, wr_lo], axis=1), b_router[None])

    nblk = (t * TOP_K + N_EXPERTS * (MOE_BM - 1) + MOE_BM - 1) // MOE_BM
    tabs, start, pad_start, pad_len, nact = _expert_schedule(counts.reshape(N_EXPERTS), nblk, t * TOP_K)
    slot_rows = _slot_rows(start, eidx.reshape(t * TOP_K), rank.reshape(t * TOP_K))
    xs = _dispatch(pad_start, pad_len, nact, slot_rows, xm, nblk * MOE_BM)
    n_short = N_EXPERTS * NF + nblk
    args = (xs, w_up, b_up[:, None, :], w_down, b_down[:, None, :])
    ys = lax.cond(nact[1] <= n_short,
                  lambda: _experts(tuple(a[:n_short] for a in tabs), *args, name="experts"),
                  lambda: _experts(tabs, *args, name="experts_long"))
    return x1, slot_rows, gates, ys


def kernel(x, attn_norm, w_in, a_sink, b_q_norm, b_w_uq, b_kv_norm, b_w_ukv, out_norm_a, out_norm_b, w_o,
           mlp_norm, w_router, b_router, w_up, b_up, w_down, b_down, final_norm):
    b, s, d = x.shape
    assert d == D_MODEL and s % min(TQ_MLA, s) == 0 and s % min(CK_MLA, s) == 0 and s % TM_PRO == 0 and attn_norm.shape[0] == 1
    x1, slot_rows, gates, ys = _layer(
        x, attn_norm[0], w_in[0], a_sink[0], b_q_norm[0], b_w_uq[0], b_kv_norm[0], b_w_ukv[0], out_norm_a[0],
        out_norm_b[0], w_o[0], mlp_norm[0], w_router[0], b_router[0], w_up[0], b_up[0], w_down[0], b_down[0])
    out = _combine(slot_rows, gates, x1, final_norm[None], ys)
    return out.reshape(b, s, d)
```

```python
import functools

import jax
import numpy as np
import jax.numpy as jnp
from jax import lax
from jax.experimental import pallas as pl
from jax.experimental.pallas import tpu as pltpu

D_MODEL = 2048
A_HEADS, A_KV_HEADS, A_HEAD_DIM = 16, 4, 64
A_GROUP = A_HEADS // A_KV_HEADS
WINDOW = 128
A_BLOCK = 128
B_HEADS, B_Q_RANK, B_KV_RANK = 8, 512, 256
B_NOPE_DIM, B_ROPE_DIM, B_V_DIM = 128, 64, 128
ROPE_THETA = 10000.0
A_WIDTH = A_HEADS * A_HEAD_DIM
B_WIDTH = B_HEADS * B_V_DIM
A_KV_COLS = A_KV_HEADS * A_HEAD_DIM
N_EXPERTS, TOP_K, D_FF = 32, 4, 2048
SWIGLU_ALPHA, SWIGLU_LIMIT = 1.702, 7.0
EPS = 1e-5

LANES = 128
SUBLANES = 8
B_QK_PAD = 2 * LANES
IN_COLS_PAD = A_WIDTH + 2 * A_KV_COLS + B_Q_RANK + B_KV_RANK + LANES
VMEM_LIMIT = 56 * 1024 * 1024
ROW_TILES = D_MODEL // LANES
ROW_PITCH = ROW_TILES + 1

TM_PRO = 256
TQ_MLA = 1024
CK_MLA = 2048
HEADS_MLA = 2
TM_OUT = 512
TM_DISPATCH = 512
TM_COMBINE = 128
MOE_BM = 512
MOE_SUB = 256
MOE_RM = 2048
MOE_TF = 256
WAIT_UNROLL = 16

BF16 = jnp.bfloat16
F32 = jnp.float32


def _rms(x, g):
    return x * lax.rsqrt(jnp.mean(x * x, axis=-1, keepdims=True) + EPS) * g


def _const_spec(shape):
    nd = len(shape)
    return pl.BlockSpec(shape, lambda *_: (0,) * nd, pipeline_mode=pl.Buffered(1))


def _store_token_rows(ref, first, val):
    n = val.shape[0]
    for j in range(ROW_TILES):
        ref[pl.ds(first * ROW_PITCH + j, n, stride=ROW_PITCH), :] = val[:, j * LANES:(j + 1) * LANES]
    ref[pl.ds(first * ROW_PITCH + ROW_TILES, n, stride=ROW_PITCH), :] = jnp.zeros((n, LANES), val.dtype)


def _load_token_rows(ref, n):
    return [ref[pl.ds(j, n, stride=ROW_PITCH), :] for j in range(ROW_TILES)]


def _rope(x, cos, sin):
    lane = lax.broadcasted_iota(jnp.int32, x.shape, 1)
    up = pltpu.roll(x, LANES - B_ROPE_DIM // 2, 1)
    dn = pltpu.roll(x, B_ROPE_DIM // 2, 1)
    sw = jnp.where(lane < B_ROPE_DIM // 2, -up, jnp.where(lane < B_ROPE_DIM, dn, 0.0))
    return x * cos + sw * sin


def _prologue_body(x_ref, g_ref, win_ref, qn_ref, wuq_ref, kvn_ref, wukv_ref, cos_ref, sin_ref,
                   qa_ref, ka_ref, va_ref, qb_ref, kb_ref, vb_ref, winb_ref):
    @pl.when((pl.program_id(0) == 0) & (pl.program_id(1) == 0))
    def _():
        n_in = win_ref.shape[1]
        winb_ref[:, :n_in] = win_ref[...].astype(BF16)
        winb_ref[:, n_in:] = jnp.zeros((winb_ref.shape[0], IN_COLS_PAD - n_in), BF16)

    x = x_ref[0]
    xn = _rms(x, g_ref[...]).astype(BF16)
    proj = jnp.dot(xn, winb_ref[...], preferred_element_type=F32)
    c0 = A_WIDTH
    c1 = c0 + A_KV_COLS
    c2 = c1 + A_KV_COLS
    c3 = c2 + B_Q_RANK
    c4 = c3 + B_KV_RANK
    qa_ref[0] = (proj[:, :A_WIDTH] * A_HEAD_DIM ** -0.5).astype(BF16)
    low = lax.broadcasted_iota(jnp.int32, (proj.shape[0], LANES), 1) < A_HEAD_DIM
    for src, dst in ((c0, ka_ref), (c1, va_ref)):
        for pair in range(A_KV_HEADS // 2):
            two = proj[:, src + pair * LANES:src + (pair + 1) * LANES]
            swapped = pltpu.roll(two, A_HEAD_DIM, 1)
            dst[0, 4 * pair + 0] = jnp.where(low, two, 0.0).astype(BF16)
            dst[0, 4 * pair + 1] = jnp.where(low, 0.0, swapped).astype(BF16)
            dst[0, 4 * pair + 2] = jnp.where(low, swapped, 0.0).astype(BF16)
            dst[0, 4 * pair + 3] = jnp.where(low, 0.0, two).astype(BF16)
    cq = _rms(proj[:, c2:c3], qn_ref[...]).astype(BF16)
    q = jnp.dot(cq, wuq_ref[...], preferred_element_type=F32)
    ckv = _rms(proj[:, c3:c4], kvn_ref[...]).astype(BF16)
    kv = jnp.dot(ckv, wukv_ref[...], preferred_element_type=F32)
    cos = cos_ref[...]
    sin = sin_ref[...]
    kpe = _rope(proj[:, c4:c4 + LANES], cos, sin).astype(BF16)
    b_scale = (B_NOPE_DIM + B_ROPE_DIM) ** -0.5
    hw = B_HEADS * LANES
    for h in range(B_HEADS):
        sl = slice(h * LANES, (h + 1) * LANES)
        qb_ref[0, h, :, :LANES] = (q[:, sl] * b_scale).astype(BF16)
        qpe = _rope(q[:, hw + h * LANES:hw + (h + 1) * LANES], cos, sin)
        qb_ref[0, h, :, LANES:] = (qpe * b_scale).astype(BF16)
        kb_ref[0, h, :, :LANES] = kv[:, sl].astype(BF16)
        kb_ref[0, h, :, LANES:] = kpe
        vb_ref[0, h] = kv[:, hw + h * LANES:hw + (h + 1) * LANES].astype(BF16)


def _prologue(x, g, win, qn, wuq, kvn, wukv, cos, sin):
    b, s, d = x.shape
    tm = TM_PRO
    grid = (b, s // tm)
    sds = jax.ShapeDtypeStruct
    out_shape = (
        sds((b, s, A_WIDTH), BF16),
        sds((b, 2 * A_KV_HEADS, s, LANES), BF16),
        sds((b, 2 * A_KV_HEADS, s, LANES), BF16),
        sds((b, B_HEADS, s, B_QK_PAD), BF16),
        sds((b, B_HEADS, s, B_QK_PAD), BF16),
        sds((b, B_HEADS, s, B_V_DIM), BF16),
    )

    def hspec(nh, w):
        return pl.BlockSpec((1, nh, tm, w), lambda bi, i: (bi, 0, i, 0))

    return pl.pallas_call(
        _prologue_body,
        grid=grid,
        in_specs=[
            pl.BlockSpec((1, tm, d), lambda bi, i: (bi, i, 0)),
            _const_spec(g.shape), _const_spec(win.shape), _const_spec(qn.shape), _const_spec(wuq.shape),
            _const_spec(kvn.shape), _const_spec(wukv.shape),
            pl.BlockSpec((tm, LANES), lambda bi, i: (i, 0)),
            pl.BlockSpec((tm, LANES), lambda bi, i: (i, 0)),
        ],
        out_specs=(pl.BlockSpec((1, tm, A_WIDTH), lambda bi, i: (bi, i, 0)),
                   hspec(2 * A_KV_HEADS, LANES), hspec(2 * A_KV_HEADS, LANES),
                   hspec(B_HEADS, B_QK_PAD), hspec(B_HEADS, B_QK_PAD), hspec(B_HEADS, B_V_DIM)),
        out_shape=out_shape,
        scratch_shapes=[pltpu.VMEM((d, IN_COLS_PAD), BF16)],
        compiler_params=pltpu.CompilerParams(dimension_semantics=("arbitrary", "arbitrary"),
                                             vmem_limit_bytes=VMEM_LIMIT),
        name="prologue",
    )(x, g, win, qn, wuq, kvn, wukv, cos, sin)


def _window_body(sink_ref, q_ref, kp_ref, kc_ref, kn_ref, vp_ref, vc_ref, vn_ref, bias_ref, o_ref):
    two = 2 * A_BLOCK
    low = lax.broadcasted_iota(jnp.int32, (two, LANES), 1) < A_HEAD_DIM
    for hk in range(A_KV_HEADS):
        q2 = jnp.concatenate([q_ref[0, :, (2 * hk) * LANES:(2 * hk + 1) * LANES],
                              q_ref[0, :, (2 * hk + 1) * LANES:(2 * hk + 2) * LANES]], axis=0)
        acc = jnp.zeros((two, LANES), F32)
        inv = []
        for half in range(2):
            z = 2 * hk + half
            kband = jnp.concatenate([kp_ref[0, z], kc_ref[0, z], kn_ref[0, z]], axis=0)
            vband = jnp.concatenate([vp_ref[0, z], vc_ref[0, z], vn_ref[0, z]], axis=0)
            s = lax.dot_general(q2, kband, (((1,), (1,)), ((), ())), preferred_element_type=F32)
            s = s + bias_ref[0, z]
            sink = jnp.concatenate(
                [jnp.full((A_BLOCK, 1), sink_ref[hk * A_GROUP + 2 * jj + half], F32) for jj in range(2)], axis=0)
            m = jnp.maximum(jnp.max(s, axis=-1, keepdims=True), sink)
            p = jnp.exp(s - m)
            den = jnp.sum(p, axis=-1, keepdims=True) + jnp.exp(sink - m)
            inv.append(1.0 / den)
            acc = acc + jnp.dot(p.astype(BF16), vband, preferred_element_type=F32)
        o = (acc * jnp.where(low, inv[0], inv[1])).astype(BF16)
        o_ref[0, :, (2 * hk) * LANES:(2 * hk + 1) * LANES] = o[:A_BLOCK]
        o_ref[0, :, (2 * hk + 1) * LANES:(2 * hk + 2) * LANES] = o[A_BLOCK:]


def _window_attn(qa, ka, va, sink, bias):
    b, s, _ = qa.shape
    nb = s // A_BLOCK
    kv_blk = (1, 2 * A_KV_HEADS, A_BLOCK, LANES)
    prev = pl.BlockSpec(kv_blk, lambda bi, n: (bi, 0, jnp.maximum(n - 1, 0), 0))
    cur = pl.BlockSpec(kv_blk, lambda bi, n: (bi, 0, n, 0))
    nxt = pl.BlockSpec(kv_blk, lambda bi, n: (bi, 0, jnp.minimum(n + 1, nb - 1), 0))

    def edge(bi, n):
        return ((n == 0).astype(jnp.int32) + 2 * (n == nb - 1).astype(jnp.int32), 0, 0, 0)

    return pl.pallas_call(
        _window_body,
        grid=(b, nb),
        in_specs=[
            pl.BlockSpec(memory_space=pltpu.SMEM),
            pl.BlockSpec((1, A_BLOCK, A_WIDTH), lambda bi, n: (bi, n, 0)),
            prev, cur, nxt, prev, cur, nxt,
            pl.BlockSpec((1,) + bias.shape[1:], edge),
        ],
        out_specs=pl.BlockSpec((1, A_BLOCK, A_WIDTH), lambda bi, n: (bi, n, 0)),
        out_shape=jax.ShapeDtypeStruct((b, s, A_WIDTH), BF16),
        compiler_params=pltpu.CompilerParams(dimension_semantics=("parallel", "parallel"),
                                             vmem_limit_bytes=VMEM_LIMIT),
        name="window_attn",
    )(sink, qa, ka, ka, ka, va, va, va, bias)


def _mla_body(q_ref, k_ref, v_ref, o_ref):
    tq = q_ref.shape[2]
    s_len = k_ref.shape[2]
    ck = min(CK_MLA, s_len)
    for hh in range(HEADS_MLA):
        q = q_ref[0, hh]
        m = jnp.full((tq, 1), -jnp.inf, F32)
        l = jnp.zeros((tq, 1), F32)
        acc = jnp.zeros((tq, B_V_DIM), F32)
        for c in range(s_len // ck):
            k_c = k_ref[0, hh, c * ck:(c + 1) * ck, :]
            v_c = v_ref[0, hh, c * ck:(c + 1) * ck, :]
            s = lax.dot_general(q, k_c, (((1,), (1,)), ((), ())), preferred_element_type=F32)
            m_new = jnp.maximum(m, jnp.max(s, axis=-1, keepdims=True))
            alpha = jnp.exp(m - m_new)
            p = jnp.exp(s - m_new)
            l = alpha * l + jnp.sum(p, axis=-1, keepdims=True)
            acc = alpha * acc + jnp.dot(p.astype(BF16), v_c, preferred_element_type=F32)
            m = m_new
        o_ref[0, :, hh * B_V_DIM:(hh + 1) * B_V_DIM] = (acc / l).astype(BF16)


def _mla_attn(qb, kb, vb):
    b, nh, s, _ = qb.shape
    tq = min(TQ_MLA, s)
    hs = HEADS_MLA
    return pl.pallas_call(
        _mla_body,
        grid=(b, nh // hs, s // tq),
        in_specs=[
            pl.BlockSpec((1, hs, tq, B_QK_PAD), lambda bi, h, i: (bi, h, i, 0)),
            pl.BlockSpec((1, hs, s, B_QK_PAD), lambda bi, h, i: (bi, h, 0, 0)),
            pl.BlockSpec((1, hs, s, B_V_DIM), lambda bi, h, i: (bi, h, 0, 0)),
        ],
        out_specs=pl.BlockSpec((1, tq, hs * B_V_DIM), lambda bi, h, i: (bi, i, h)),
        out_shape=jax.ShapeDtypeStruct((b, s, B_WIDTH), BF16),
        compiler_params=pltpu.CompilerParams(dimension_semantics=("parallel", "parallel", "parallel"),
                                             vmem_limit_bytes=VMEM_LIMIT),
        name="mla_attn",
    )(qb, kb, vb)


def _out_router_body(ya_ref, yb_ref, x_ref, ga_ref, gb_ref, wo_ref, gm_ref, wr_ref, br_ref,
                     x1_ref, xm_ref, eidx_ref, gate_ref, rank_ref, cnt_ref, run_ref):
    i = pl.program_id(0)

    @pl.when(i == 0)
    def _():
        run_ref[...] = jnp.zeros_like(run_ref)

    na = _rms(ya_ref[...].astype(F32), ga_ref[...]).astype(BF16)
    nb = _rms(yb_ref[...].astype(F32), gb_ref[...]).astype(BF16)
    att = jnp.dot(na, wo_ref[:A_WIDTH, :], preferred_element_type=F32)
    att = att + jnp.dot(nb, wo_ref[A_WIDTH:, :], preferred_element_type=F32)
    x1 = x_ref[...] + att
    x1_ref[...] = x1
    hn = _rms(x1, gm_ref[...])
    _store_token_rows(xm_ref, 0, hn)
    tm = hn.shape[0]
    hi = hn.astype(BF16)
    lo = (hn - hi.astype(F32)).astype(BF16)
    prod = jnp.dot(jnp.concatenate([hi, lo], axis=0), wr_ref[...], preferred_element_type=F32)
    logits = (prod[:tm, :N_EXPERTS] + prod[:tm, N_EXPERTS:] + prod[tm:, :N_EXPERTS] + prod[tm:, N_EXPERTS:]
              + br_ref[...])
    lane = lax.broadcasted_iota(jnp.int32, (tm, N_EXPERTS), 1)
    work = logits
    sel = jnp.zeros((tm, N_EXPERTS), F32)
    hots, vals, idxs = [], [], []
    for _k in range(TOP_K):
        mx = jnp.max(work, axis=-1, keepdims=True)
        idx = jnp.min(jnp.where(work == mx, lane, N_EXPERTS), axis=-1, keepdims=True)
        hot = lane == idx
        hots.append(hot)
        vals.append(mx)
        idxs.append(idx)
        sel = sel + hot.astype(F32)
        work = jnp.where(hot, -jnp.inf, work)
    exps = [jnp.exp(v - vals[0]) for v in vals]
    den = exps[0] + exps[1] + exps[2] + exps[3]
    r_i = lax.broadcasted_iota(jnp.int32, (tm, tm), 0)
    c_i = lax.broadcasted_iota(jnp.int32, (tm, tm), 1)
    tri = (c_i < r_i).astype(BF16)
    before = jnp.dot(tri, sel.astype(BF16), preferred_element_type=F32) + run_ref[...]
    lane4 = lax.broadcasted_iota(jnp.int32, (tm, TOP_K), 1)
    eidx = jnp.zeros((tm, TOP_K), jnp.int32)
    gate = jnp.zeros((tm, TOP_K), F32)
    rank = jnp.zeros((tm, TOP_K), jnp.int32)
    for k in range(TOP_K):
        rk = jnp.sum(jnp.where(hots[k], before, 0.0), axis=-1, keepdims=True).astype(jnp.int32)
        eidx = jnp.where(lane4 == k, idxs[k], eidx)
        gate = jnp.where(lane4 == k, exps[k] / den, gate)
        rank = jnp.where(lane4 == k, rk, rank)
    eidx_ref[...] = eidx
    gate_ref[...] = gate
    rank_ref[...] = rank
    run = run_ref[...] + jnp.sum(sel, axis=0, keepdims=True)
    run_ref[...] = run
    cnt_ref[...] = run.astype(jnp.int32)


def _out_router(ya, yb, x2d, ga, gb, wo, gm, wr, br):
    t, d = x2d.shape
    tm = min(TM_OUT, t)
    sds = jax.ShapeDtypeStruct

    def row(w):
        return pl.BlockSpec((tm, w), lambda i: (i, 0))

    return pl.pallas_call(
        _out_router_body,
        grid=(t // tm,),
        in_specs=[row(A_WIDTH), row(B_WIDTH), row(d), _const_spec(ga.shape), _const_spec(gb.shape),
                  _const_spec(wo.shape), _const_spec(gm.shape), _const_spec(wr.shape), _const_spec(br.shape)],
        out_specs=(row(d), pl.BlockSpec((tm * ROW_PITCH, LANES), lambda i: (i, 0)),
                   row(TOP_K), row(TOP_K), row(TOP_K),
                   pl.BlockSpec((1, N_EXPERTS), lambda i: (0, 0))),
        out_shape=(sds((t, d), F32), sds((t * ROW_PITCH, LANES), F32), sds((t, TOP_K), jnp.int32),
                   sds((t, TOP_K), F32),
                   sds((t, TOP_K), jnp.int32), sds((1, N_EXPERTS), jnp.int32)),
        scratch_shapes=[pltpu.VMEM((1, N_EXPERTS), F32)],
        compiler_params=pltpu.CompilerParams(dimension_semantics=("arbitrary",),
                                             vmem_limit_bytes=VMEM_LIMIT),
        name="out_router",
    )(ya, yb, x2d, ga, gb, wo, gm, wr, br)


def _token_copy(src_ref, src_tok, dst_ref, dst_tok, sem, ntok=1):
    return _rows_copy(src_ref, src_tok * ROW_PITCH, dst_ref, dst_tok * ROW_PITCH, sem, ntok)


def _rows_copy(src_ref, src_row, dst_ref, dst_row, sem, ntok=1):
    rows = ntok * ROW_PITCH
    return pltpu.make_async_copy(src_ref.at[pl.ds(src_row, rows)], dst_ref.at[pl.ds(dst_row, rows)], sem)


def _slot_rows_body(start_ref, eidx_ref, rank_ref, o_ref):
    e = eidx_ref[...]
    base = jnp.zeros_like(e)
    for ex in range(N_EXPERTS):
        base = jnp.where(e == ex, start_ref[ex], base)
    o_ref[...] = (base + rank_ref[...]) * ROW_PITCH


def _slot_rows(start, eidx, rank):
    n = eidx.shape[0]
    shape2 = (n // LANES, LANES)
    out = pl.pallas_call(
        _slot_rows_body,
        in_specs=[pl.BlockSpec(memory_space=pltpu.SMEM), pl.BlockSpec(shape2, lambda: (0, 0)),
                  pl.BlockSpec(shape2, lambda: (0, 0))],
        out_specs=pl.BlockSpec(shape2, lambda: (0, 0)),
        out_shape=jax.ShapeDtypeStruct(shape2, jnp.int32),
        name="slot_rows",
    )(start, eidx.reshape(shape2), rank.reshape(shape2))
    return out.reshape(n)


def _zero_fill_pads(pad_start_ref, pad_len_ref, nact_ref, xs_ref, zero_ref, zsem):
    bm = zero_ref.shape[0] // ROW_PITCH
    nblk = xs_ref.shape[0] // (bm * ROW_PITCH)
    zero_ref[...] = jnp.zeros_like(zero_ref)

    def sweep(do):
        def per_expert(e, c):
            off = pad_start_ref[e]
            n = pad_len_ref[e]
            p = bm // 2
            while p >= 1:
                take = (n & p) != 0

                @pl.when(take)
                def _(p=p, off=off):
                    do(_token_copy(zero_ref, 0, xs_ref, off, zsem, p))

                off = off + jnp.where(take, p, 0)
                p //= 2
            return c

        lax.fori_loop(0, N_EXPERTS, per_expert, 0)

        def tail(j, c):
            do(_token_copy(zero_ref, 0, xs_ref, j * bm, zsem, bm))
            return c

        lax.fori_loop(nact_ref[0], nblk, tail, 0)

    sweep(lambda cp: cp.start())
    sweep(lambda cp: cp.wait())


def _dispatch_body(pad_start_ref, pad_len_ref, nact_ref, slot_ref, xm_ref, xs_ref, zero_ref, sem, zsem):
    tm = slot_ref.shape[0] // TOP_K

    @pl.when(pl.program_id(0) == 0)
    def _():
        _zero_fill_pads(pad_start_ref, pad_len_ref, nact_ref, xs_ref, zero_ref, zsem)

    def issue(i, c):
        for k in range(TOP_K):
            _rows_copy(xm_ref, i * ROW_PITCH, xs_ref, slot_ref[i * TOP_K + k], sem).start(priority=k % 2)
        return c

    lax.fori_loop(0, tm, issue, 0, unroll=2)

    def drain(i, c):
        for _ in range(WAIT_UNROLL):
            _token_copy(xm_ref, 0, xs_ref, 0, sem).wait()
        return c

    lax.fori_loop(0, tm * TOP_K // WAIT_UNROLL, drain, 0)


def _dispatch(pad_start, pad_len, nact, slot_rows, xm, n_slots):
    t = xm.shape[0] // ROW_PITCH
    tm = min(TM_DISPATCH, t)
    grid_spec = pltpu.PrefetchScalarGridSpec(
        num_scalar_prefetch=3,
        grid=(t // tm,),
        in_specs=[pl.BlockSpec((tm * TOP_K,), lambda i, *_: (i,), memory_space=pltpu.SMEM),
                  pl.BlockSpec((tm * ROW_PITCH, LANES), lambda i, *_: (i, 0))],
        out_specs=pl.BlockSpec(memory_space=pl.ANY),
        scratch_shapes=[pltpu.VMEM((MOE_BM * ROW_PITCH, LANES), F32), pltpu.SemaphoreType.DMA(()),
                        pltpu.SemaphoreType.DMA(())],
    )
    return pl.pallas_call(
        _dispatch_body,
        grid_spec=grid_spec,
        out_shape=jax.ShapeDtypeStruct((n_slots * ROW_PITCH, LANES), F32),
        compiler_params=pltpu.CompilerParams(dimension_semantics=("arbitrary",), has_side_effects=True,
                                             vmem_limit_bytes=VMEM_LIMIT),
        name="dispatch",
    )(pad_start, pad_len, nact, slot_rows, xm)


STEP_UP, STEP_DOWN, STEP_TAIL, STEP_IDLE = 0, 1, 2, 3
PREFETCH_PER_DOWN = 2
NF = D_FF // MOE_TF
PASS_BLOCKS = MOE_RM // MOE_BM


def _swiglu(hg, hu):
    gate = jnp.minimum(hg, SWIGLU_LIMIT)
    up = jnp.clip(hu, -SWIGLU_LIMIT, SWIGLU_LIMIT)
    glu = gate / (1.0 + jnp.exp(-SWIGLU_ALPHA * gate))
    return (up + 1.0) * glu


def _experts_body(e_ref, f_ref, kind_ref, blk_ref, blk0_ref, nsub_ref, pre_ref, nxb_ref,
                  xs_ref, wg_ref, wu_ref, bg_ref, bu_ref, wd_ref, bd_ref,
                  y_ref, xb_ref, h_ref, wdb_ref, stg_ref, sem):
    t = pl.program_id(0)
    kind = kind_ref[t]
    f = f_ref[t]
    nsub = nsub_ref[t]

    def x_copy(first_blk, j, slot):
        rows = MOE_SUB * ROW_PITCH
        first = pl.multiple_of((first_blk * MOE_BM + j * MOE_SUB) * ROW_PITCH, SUBLANES)
        return pltpu.make_async_copy(xs_ref.at[pl.ds(first, rows)], stg_ref.at[slot], sem.at[slot])

    def unpack(j, slot):
        rows = pl.ds(pl.multiple_of(j * MOE_SUB, MOE_SUB), MOE_SUB)
        for jt, tile in enumerate(_load_token_rows(stg_ref.at[slot], MOE_SUB)):
            xb_ref[rows, jt * LANES:(jt + 1) * LANES] = tile.astype(BF16)

    @pl.when(kind == STEP_UP)
    def _up():
        pre = pre_ref[t]

        @pl.when((f == 0) & (pre < nsub))
        def _load_rows():
            blk0 = blk0_ref[t]
            x_copy(blk0, pre, pre % 2).start()

            def body(j, c):
                slot = j % 2

                @pl.when(j + 1 < nsub)
                def _():
                    x_copy(blk0, j + 1, 1 - slot).start()

                x_copy(blk0, j, slot).wait()
                unpack(j, slot)
                return c

            lax.fori_loop(pre, nsub, body, 0)

        wdb_ref[pl.ds(pl.multiple_of(f * MOE_TF, MOE_TF), MOE_TF), :] = wd_ref[0].astype(BF16)

        def up_rows(row0, nrows):
            x = xb_ref[pl.ds(row0, nrows), :]
            hg = jnp.dot(x, wg_ref[0].astype(BF16), preferred_element_type=F32) + bg_ref[0]
            hu = jnp.dot(x, wu_ref[0].astype(BF16), preferred_element_type=F32) + bu_ref[0]
            act = _swiglu(hg, hu).astype(BF16)
            h_ref[pl.ds(row0, nrows), pl.ds(pl.multiple_of(f * MOE_TF, MOE_TF), MOE_TF)] = act

        quad = 4 * MOE_SUB
        nquad = nsub // 4
        rem = nsub % 4

        def quad_rows(i, c):
            base = pl.multiple_of(i * quad, quad)
            up_rows(base, quad // 2)
            up_rows(base + quad // 2, quad // 2)
            return c

        merged = nsub == 5

        @pl.when(merged)
        def _():
            up_rows(0, 2 * MOE_SUB)
            up_rows(2 * MOE_SUB, 3 * MOE_SUB)

        @pl.when(jnp.logical_not(merged))
        def _():
            lax.fori_loop(0, nquad, quad_rows, 0)

            @pl.when(rem >= 2)
            def _():
                up_rows(pl.multiple_of(nquad * quad, quad), 2 * MOE_SUB)

            @pl.when(rem % 2 == 1)
            def _():
                up_rows(pl.multiple_of(nquad * quad + (rem // 2) * 2 * MOE_SUB, MOE_SUB), MOE_SUB)

    @pl.when(kind == STEP_DOWN)
    def _down():
        jblk = blk_ref[t] - blk0_ref[t]
        row0 = pl.multiple_of(jblk * MOE_BM, MOE_BM)
        nx_nsub = pre_ref[t]
        nx_blk0 = nxb_ref[t]
        ahead = [jblk * PREFETCH_PER_DOWN + u for u in range(PREFETCH_PER_DOWN)]
        for u, j in enumerate(ahead):
            @pl.when(j < nx_nsub)
            def _(u=u, j=j):
                x_copy(nx_blk0, j, u).start()

        def down_rows(nrows):
            hrows = h_ref[pl.ds(row0, nrows), :]
            return jnp.dot(hrows, wdb_ref[...], preferred_element_type=F32) + bd_ref[0]

        @pl.when(nsub == 2)
        def _():
            _store_token_rows(y_ref, 0, down_rows(MOE_BM))

        @pl.when(nsub == 1)
        def _():
            _store_token_rows(y_ref, 0, down_rows(MOE_SUB))
            y_ref[MOE_SUB * ROW_PITCH:, :] = jnp.zeros(((MOE_BM - MOE_SUB) * ROW_PITCH, LANES), F32)

        for u, j in enumerate(ahead):
            @pl.when(j < nx_nsub)
            def _(u=u, j=j):
                x_copy(nx_blk0, j, u).wait()
                unpack(j, u)

    @pl.when(kind == STEP_TAIL)
    def _tail():
        y_ref[...] = jnp.zeros_like(y_ref)


def _schedule_body(n_blocks, cnt_ref, e_ref, f_ref, kind_ref, blk_ref, blk0_ref, nsub_ref, pre_ref, nxb_ref,
                   start_ref, pad_start_ref, pad_len_ref, nact_ref):
    n_steps = e_ref.shape[0]
    sub_per_blk = MOE_BM // MOE_SUB
    zero = jnp.int32(0)

    def put(idx, e, f, kind, blk, blk0, nsub, pre):
        e_ref[idx] = e
        f_ref[idx] = f
        kind_ref[idx] = kind
        blk_ref[idx] = blk
        blk0_ref[idx] = blk0
        nsub_ref[idx] = nsub
        pre_ref[idx] = pre
        nxb_ref[idx] = zero

    def per_expert(e, carry):
        t, blk, last_e, prev_down, prev_nb = carry
        n = cnt_ref[e]
        nblk = (n + (MOE_BM - 1)) // MOE_BM
        start_ref[e] = blk * MOE_BM
        pad_start_ref[e] = blk * MOE_BM + n
        pad_len_ref[e] = nblk * MOE_BM - n

        def per_pass(p, carry2):
            t2, pdown, pnb = carry2
            b0 = blk + p * PASS_BLOCKS
            nb = jnp.minimum(nblk - p * PASS_BLOCKS, PASS_BLOCKS)
            nsub = (jnp.minimum(n - p * MOE_RM, MOE_RM) + (MOE_SUB - 1)) // MOE_SUB
            pre = jnp.minimum(nsub, pnb * PREFETCH_PER_DOWN)
            for f in range(NF):
                put(t2 + f, e, f, STEP_UP, b0, b0, nsub, pre)

            def patch(j, c):
                pre_ref[pdown + j] = nsub
                nxb_ref[pdown + j] = b0
                return c

            lax.fori_loop(0, pnb, patch, 0)

            def per_blk(j, c):
                put(t2 + NF + j, e, NF - 1, STEP_DOWN, b0 + j, b0,
                    jnp.clip(nsub - j * sub_per_blk, 1, sub_per_blk), zero)
                return c

            lax.fori_loop(0, nb, per_blk, 0)
            return t2 + NF + nb, t2 + NF, nb

        t, prev_down, prev_nb = lax.fori_loop(0, (nblk + (PASS_BLOCKS - 1)) // PASS_BLOCKS, per_pass,
                                              (t, prev_down, prev_nb))
        return t, blk + nblk, jnp.where(n > 0, e, last_e), prev_down, prev_nb

    t, nact, last_e, _, _ = lax.fori_loop(0, N_EXPERTS, per_expert, (zero, zero, zero, zero, zero))
    nact_ref[0] = nact
    nact_ref[1] = t + n_blocks - nact

    def spare(i, c):
        is_tail = i < n_blocks - nact
        blk = jnp.where(is_tail, nact + i, n_blocks - 1)
        put(t + i, last_e, NF - 1, jnp.where(is_tail, STEP_TAIL, STEP_IDLE), blk, blk, 1, zero)
        return c

    lax.fori_loop(0, n_steps - t, spare, 0)


def _expert_schedule(counts, n_blocks, n_assign):
    n_pass_max = N_EXPERTS + n_assign // MOE_RM
    n_steps = n_pass_max * NF + n_blocks
    smem = pl.BlockSpec(memory_space=pltpu.SMEM)
    i32 = jnp.int32
    out_shape = tuple(jax.ShapeDtypeStruct((n,), i32) for n in (n_steps,) * 8 + (N_EXPERTS,) * 3 + (2,))
    outs = pl.pallas_call(
        functools.partial(_schedule_body, n_blocks),
        in_specs=[smem],
        out_specs=tuple(smem for _ in out_shape),
        out_shape=out_shape,
        name="schedule",
    )(counts)
    return outs[:8], outs[8], outs[9], outs[10], outs[11]


def _experts(tabs, xs, w_up, b_up, w_down, b_down, name):
    d = D_MODEL
    n_steps = tabs[0].shape[0]
    grid_spec = pltpu.PrefetchScalarGridSpec(
        num_scalar_prefetch=len(tabs),
        grid=(n_steps,),
        in_specs=[
            pl.BlockSpec(memory_space=pl.ANY),
            pl.BlockSpec((1, d, MOE_TF), lambda t, e, f, *_: (e[t], 0, f[t])),
            pl.BlockSpec((1, d, MOE_TF), lambda t, e, f, *_: (e[t], 0, NF + f[t])),
            pl.BlockSpec((1, 1, MOE_TF), lambda t, e, f, *_: (e[t], 0, f[t])),
            pl.BlockSpec((1, 1, MOE_TF), lambda t, e, f, *_: (e[t], 0, NF + f[t])),
            pl.BlockSpec((1, MOE_TF, d), lambda t, e, f, *_: (e[t], f[t], 0)),
            pl.BlockSpec((1, 1, d), lambda t, e, f, *_: (e[t], 0, 0)),
        ],
        out_specs=pl.BlockSpec((MOE_BM * ROW_PITCH, LANES), lambda t, e, f, kind, blk, *_: (blk[t], 0)),
        scratch_shapes=[
            pltpu.VMEM((MOE_RM, d), BF16),
            pltpu.VMEM((MOE_RM, D_FF), BF16),
            pltpu.VMEM((D_FF, d), BF16),
            pltpu.VMEM((2, MOE_SUB * ROW_PITCH, LANES), F32),
            pltpu.SemaphoreType.DMA((2,)),
        ],
    )
    return pl.pallas_call(
        _experts_body,
        grid_spec=grid_spec,
        out_shape=jax.ShapeDtypeStruct(xs.shape, F32),
        compiler_params=pltpu.CompilerParams(dimension_semantics=("arbitrary",),
                                             vmem_limit_bytes=VMEM_LIMIT),
        name=name,
    )(*tabs, xs, w_up, w_up, b_up, b_up, w_down, b_down)


def _combine_body(slot_ref, slot_nx_ref, gate_ref, x1_ref, gf_ref, ys_ref, o_ref, buf_ref, sem):
    step = pl.program_id(0)
    slot = step % 2
    tm = x1_ref.shape[0]

    def gather(rows_ref, s):
        def issue(i, c):
            for k in range(TOP_K):
                _rows_copy(ys_ref, rows_ref[i * TOP_K + k], buf_ref.at[s, k], i * ROW_PITCH,
                           sem.at[s]).start(priority=k % 2)
            return c

        lax.fori_loop(0, tm, issue, 0, unroll=2)

    @pl.when(step == 0)
    def _():
        gather(slot_ref, 0)

    def drain(s):
        for _ in range(tm * TOP_K):
            _token_copy(ys_ref, 0, buf_ref.at[s, 0], 0, sem.at[s]).wait()

    drain(slot)
    for i in range(tm):
        for k in range(TOP_K):
            _rows_copy(ys_ref, slot_nx_ref[i * TOP_K + k], buf_ref.at[1 - slot, k], i * ROW_PITCH,
                       sem.at[1 - slot]).start(priority=k % 2)
    gate = gate_ref[...]
    tiles = [x1_ref[:, j * LANES:(j + 1) * LANES] for j in range(ROW_TILES)]
    for k in range(TOP_K):
        g = gate[:, k:k + 1]
        for j, tile in enumerate(_load_token_rows(buf_ref.at[slot, k], tm)):
            tiles[j] = tiles[j] + g * tile
    o_ref[...] = _rms(jnp.concatenate(tiles, axis=1), gf_ref[...])

    @pl.when(step == pl.num_programs(0) - 1)
    def _():
        drain(1 - slot)


def _combine(slot_rows, gates, x1, gf, ys):
    t, d = x1.shape
    tm = min(TM_COMBINE, t)
    n = t // tm
    idx_blk = (tm * TOP_K,)
    return pl.pallas_call(
        _combine_body,
        grid=(n,),
        in_specs=[pl.BlockSpec(idx_blk, lambda i: (i,), memory_space=pltpu.SMEM),
                  pl.BlockSpec(idx_blk, lambda i: (jnp.minimum(i + 1, n - 1),), memory_space=pltpu.SMEM),
                  pl.BlockSpec((tm, TOP_K), lambda i: (i, 0)),
                  pl.BlockSpec((tm, d), lambda i: (i, 0)),
                  _const_spec(gf.shape),
                  pl.BlockSpec(memory_space=pl.ANY)],
        out_specs=pl.BlockSpec((tm, d), lambda i: (i, 0)),
        scratch_shapes=[pltpu.VMEM((2, TOP_K, tm * ROW_PITCH, LANES), F32), pltpu.SemaphoreType.DMA((2,))],
        out_shape=jax.ShapeDtypeStruct((t, d), F32),
        compiler_params=pltpu.CompilerParams(dimension_semantics=("arbitrary",),
                                             vmem_limit_bytes=VMEM_LIMIT),
        name="combine",
    )(slot_rows, slot_rows, gates, x1, gf, ys)


def _rope_tables(s):
    pos = np.arange(s, dtype=np.float32)
    inv_freq = np.power(np.float32(ROPE_THETA), -np.arange(0, B_ROPE_DIM, 2, dtype=np.float32) / B_ROPE_DIM)
    ang = (pos[:, None] * inv_freq[None, :]).astype(np.float32)
    cos, sin = np.cos(ang), np.sin(ang)
    pad = LANES - B_ROPE_DIM
    cos = np.concatenate([cos, cos, np.ones((s, pad), np.float32)], axis=1)
    sin = np.concatenate([sin, sin, np.zeros((s, pad), np.float32)], axis=1)
    return cos.astype(np.float32), sin.astype(np.float32)


def _window_bias():
    qi = np.arange(A_BLOCK)[:, None]
    kj = np.arange(3 * A_BLOCK)[None, :]
    dist = np.abs(qi + A_BLOCK - kj)
    slopes = np.power(np.float32(2.0), -8.0 * np.arange(1, A_HEADS + 1, dtype=np.float32) / A_HEADS)
    bias = -slopes[:, None, None] * dist.astype(np.float32)[None]
    bias = np.where((dist <= WINDOW)[None], bias, -np.inf)
    bias = bias.reshape(A_KV_HEADS, 2, 2, A_BLOCK, 3 * A_BLOCK).transpose(0, 2, 1, 3, 4)
    bias = bias.reshape(2 * A_KV_HEADS, 2 * A_BLOCK, 3 * A_BLOCK)
    no_prev = (kj < A_BLOCK)[None]
    no_next = (kj >= 2 * A_BLOCK)[None]
    ninf = -np.inf
    return np.stack([bias, np.where(no_prev, ninf, bias), np.where(no_next, ninf, bias),
                     np.where(no_prev | no_next, ninf, bias)]).astype(np.float32)


def _layer(x, attn_norm, w_in, a_sink, b_q_norm, b_w_uq, b_kv_norm, b_w_ukv, out_norm_a, out_norm_b, w_o,
           mlp_norm, w_router, b_router, w_up, b_up, w_down, b_down):
    b, s, d = x.shape
    t = b * s
    wq = b_w_uq.reshape(B_Q_RANK, B_HEADS, B_NOPE_DIM + B_ROPE_DIM)
    wq_pe = jnp.pad(wq[:, :, B_NOPE_DIM:], ((0, 0), (0, 0), (0, LANES - B_ROPE_DIM)))
    wuq = jnp.concatenate([wq[:, :, :B_NOPE_DIM].reshape(B_Q_RANK, -1), wq_pe.reshape(B_Q_RANK, -1)],
                          axis=1).astype(BF16)
    wkv = b_w_ukv.reshape(B_KV_RANK, B_HEADS, B_NOPE_DIM + B_V_DIM)
    wukv = jnp.concatenate([wkv[:, :, :B_NOPE_DIM].reshape(B_KV_RANK, -1),
                            wkv[:, :, B_NOPE_DIM:].reshape(B_KV_RANK, -1)], axis=1).astype(BF16)
    cos, sin = _rope_tables(s)

    qa, ka, va, qb, kb, vb = _prologue(x, attn_norm[None], w_in, b_q_norm[None], wuq, b_kv_norm[None], wukv,
                                       cos, sin)
    ya = _window_attn(qa, ka, va, a_sink, _window_bias())
    yb = _mla_attn(qb, kb, vb)
    wr_hi = w_router.astype(BF16)
    wr_lo = (w_router - wr_hi.astype(F32)).astype(BF16)
    x1, xm, eidx, gates, rank, counts = _out_router(
        ya.reshape(t, A_WIDTH), yb.reshape(t, B_WIDTH), x.reshape(t, d), out_norm_a[None], out_norm_b[None],
        w_o.astype(BF16), mlp_norm[None], jnp.concatenate([wr_hi, wr_lo], axis=1), b_router[None])

    nblk = (t * TOP_K + N_EXPERTS * (MOE_BM - 1) + MOE_BM - 1) // MOE_BM
    tabs, start, pad_start, pad_len, nact = _expert_schedule(counts.reshape(N_EXPERTS), nblk, t * TOP_K)
    slot_rows = _slot_rows(start, eidx.reshape(t * TOP_K), rank.reshape(t * TOP_K))
    xs = _dispatch(pad_start, pad_len, nact, slot_rows, xm, nblk * MOE_BM)
    n_short = N_EXPERTS * NF + nblk
    args = (xs, w_up, b_up[:, None, :], w_down, b_down[:, None, :])
    ys = lax.cond(nact[1] <= n_short,
                  lambda: _experts(tuple(a[:n_short] for a in tabs), *args, name="experts"),
                  lambda: _experts(tabs, *args, name="experts_long"))
    return x1, slot_rows, gates, ys


def kernel(x, attn_norm, w_in, a_sink, b_q_norm, b_w_uq, b_kv_norm, b_w_ukv, out_norm_a, out_norm_b, w_o,
           mlp_norm, w_router, b_router, w_up, b_up, w_down, b_down, final_norm):
    b, s, d = x.shape
    assert d == D_MODEL and s % min(TQ_MLA, s) == 0 and s % min(CK_MLA, s) == 0 and s % TM_PRO == 0 and attn_norm.shape[0] == 1
    x1, slot_rows, gates, ys = _layer(
        x, attn_norm[0], w_in[0], a_sink[0], b_q_norm[0], b_w_uq[0], b_kv_norm[0], b_w_ukv[0], out_norm_a[0],
        out_norm_b[0], w_o[0], mlp_norm[0], w_router[0], b_router[0], w_up[0], b_up[0], w_down[0], b_down[0])
    out = _combine(slot_rows, gates, x1, final_norm[None], ys)
    return out.reshape(b, s, d)
```

```python
import functools

import jax
import numpy as np
import jax.numpy as jnp
from jax import lax
from jax.experimental import pallas as pl
from jax.experimental.pallas import tpu as pltpu

D_MODEL = 2048
A_HEADS, A_KV_HEADS, A_HEAD_DIM = 16, 4, 64
A_GROUP = A_HEADS // A_KV_HEADS
WINDOW = 128
A_BLOCK = 128
B_HEADS, B_Q_RANK, B_KV_RANK = 8, 512, 256
B_NOPE_DIM, B_ROPE_DIM, B_V_DIM = 128, 64, 128
ROPE_THETA = 10000.0
A_WIDTH = A_HEADS * A_HEAD_DIM
B_WIDTH = B_HEADS * B_V_DIM
A_KV_COLS = A_KV_HEADS * A_HEAD_DIM
N_EXPERTS, TOP_K, D_FF = 32, 4, 2048
SWIGLU_ALPHA, SWIGLU_LIMIT = 1.702, 7.0
EPS = 1e-5

LANES = 128
SUBLANES = 8
B_QK_PAD = 2 * LANES
IN_COLS_PAD = A_WIDTH + 2 * A_KV_COLS + B_Q_RANK + B_KV_RANK + LANES
VMEM_LIMIT = 56 * 1024 * 1024
ROW_TILES = D_MODEL // LANES
ROW_PITCH = ROW_TILES + 1

TM_PRO = 256
TQ_MLA = 1024
CK_MLA = 2048
HEADS_MLA = 2
TM_OUT = 512
TM_DISPATCH = 512
TM_COMBINE = 256
MOE_BM = 512
MOE_SUB = 256
MOE_RM = 2048
MOE_TF = 256
WAIT_UNROLL = 16

BF16 = jnp.bfloat16
F32 = jnp.float32


def _rms(x, g):
    return x * lax.rsqrt(jnp.mean(x * x, axis=-1, keepdims=True) + EPS) * g


def _const_spec(shape):
    nd = len(shape)
    return pl.BlockSpec(shape, lambda *_: (0,) * nd, pipeline_mode=pl.Buffered(1))


def _store_token_rows(ref, first, val):
    n = val.shape[0]
    for j in range(ROW_TILES):
        ref[pl.ds(first * ROW_PITCH + j, n, stride=ROW_PITCH), :] = val[:, j * LANES:(j + 1) * LANES]
    ref[pl.ds(first * ROW_PITCH + ROW_TILES, n, stride=ROW_PITCH), :] = jnp.zeros((n, LANES), val.dtype)


def _load_token_rows(ref, n):
    return [ref[pl.ds(j, n, stride=ROW_PITCH), :] for j in range(ROW_TILES)]


def _rope(x, cos, sin):
    lane = lax.broadcasted_iota(jnp.int32, x.shape, 1)
    up = pltpu.roll(x, LANES - B_ROPE_DIM // 2, 1)
    dn = pltpu.roll(x, B_ROPE_DIM // 2, 1)
    sw = jnp.where(lane < B_ROPE_DIM // 2, -up, jnp.where(lane < B_ROPE_DIM, dn, 0.0))
    return x * cos + sw * sin


def _prologue_body(x_ref, g_ref, win_ref, qn_ref, wuq_ref, kvn_ref, wukv_ref, cos_ref, sin_ref,
                   qa_ref, ka_ref, va_ref, qb_ref, kb_ref, vb_ref, winb_ref):
    @pl.when((pl.program_id(0) == 0) & (pl.program_id(1) == 0))
    def _():
        n_in = win_ref.shape[1]
        winb_ref[:, :n_in] = win_ref[...].astype(BF16)
        winb_ref[:, n_in:] = jnp.zeros((winb_ref.shape[0], IN_COLS_PAD - n_in), BF16)

    x = x_ref[0]
    xn = _rms(x, g_ref[...]).astype(BF16)
    proj = jnp.dot(xn, winb_ref[...], preferred_element_type=F32)
    c0 = A_WIDTH
    c1 = c0 + A_KV_COLS
    c2 = c1 + A_KV_COLS
    c3 = c2 + B_Q_RANK
    c4 = c3 + B_KV_RANK
    qa_ref[0] = (proj[:, :A_WIDTH] * A_HEAD_DIM ** -0.5).astype(BF16)
    low = lax.broadcasted_iota(jnp.int32, (proj.shape[0], LANES), 1) < A_HEAD_DIM
    for src, dst in ((c0, ka_ref), (c1, va_ref)):
        for pair in range(A_KV_HEADS // 2):
            two = proj[:, src + pair * LANES:src + (pair + 1) * LANES]
            swapped = pltpu.roll(two, A_HEAD_DIM, 1)
            dst[0, 4 * pair + 0] = jnp.where(low, two, 0.0).astype(BF16)
            dst[0, 4 * pair + 1] = jnp.where(low, 0.0, swapped).astype(BF16)
            dst[0, 4 * pair + 2] = jnp.where(low, swapped, 0.0).astype(BF16)
            dst[0, 4 * pair + 3] = jnp.where(low, 0.0, two).astype(BF16)
    cq = _rms(proj[:, c2:c3], qn_ref[...]).astype(BF16)
    q = jnp.dot(cq, wuq_ref[...], preferred_element_type=F32)
    ckv = _rms(proj[:, c3:c4], kvn_ref[...]).astype(BF16)
    kv = jnp.dot(ckv, wukv_ref[...], preferred_element_type=F32)
    cos = cos_ref[...]
    sin = sin_ref[...]
    kpe = _rope(proj[:, c4:c4 + LANES], cos, sin).astype(BF16)
    b_scale = (B_NOPE_DIM + B_ROPE_DIM) ** -0.5
    hw = B_HEADS * LANES
    for h in range(B_HEADS):
        sl = slice(h * LANES, (h + 1) * LANES)
        qb_ref[0, h, :, :LANES] = (q[:, sl] * b_scale).astype(BF16)
        qpe = _rope(q[:, hw + h * LANES:hw + (h + 1) * LANES], cos, sin)
        qb_ref[0, h, :, LANES:] = (qpe * b_scale).astype(BF16)
        kb_ref[0, h, :, :LANES] = kv[:, sl].astype(BF16)
        kb_ref[0, h, :, LANES:] = kpe
        vb_ref[0, h] = kv[:, hw + h * LANES:hw + (h + 1) * LANES].astype(BF16)


def _prologue(x, g, win, qn, wuq, kvn, wukv, cos, sin):
    b, s, d = x.shape
    tm = TM_PRO
    grid = (b, s // tm)
    sds = jax.ShapeDtypeStruct
    out_shape = (
        sds((b, s, A_WIDTH), BF16),
        sds((b, 2 * A_KV_HEADS, s, LANES), BF16),
        sds((b, 2 * A_KV_HEADS, s, LANES), BF16),
        sds((b, B_HEADS, s, B_QK_PAD), BF16),
        sds((b, B_HEADS, s, B_QK_PAD), BF16),
        sds((b, B_HEADS, s, B_V_DIM), BF16),
    )

    def hspec(nh, w):
        return pl.BlockSpec((1, nh, tm, w), lambda bi, i: (bi, 0, i, 0))

    return pl.pallas_call(
        _prologue_body,
        grid=grid,
        in_specs=[
            pl.BlockSpec((1, tm, d), lambda bi, i: (bi, i, 0)),
            _const_spec(g.shape), _const_spec(win.shape), _const_spec(qn.shape), _const_spec(wuq.shape),
            _const_spec(kvn.shape), _const_spec(wukv.shape),
            pl.BlockSpec((tm, LANES), lambda bi, i: (i, 0)),
            pl.BlockSpec((tm, LANES), lambda bi, i: (i, 0)),
        ],
        out_specs=(pl.BlockSpec((1, tm, A_WIDTH), lambda bi, i: (bi, i, 0)),
                   hspec(2 * A_KV_HEADS, LANES), hspec(2 * A_KV_HEADS, LANES),
                   hspec(B_HEADS, B_QK_PAD), hspec(B_HEADS, B_QK_PAD), hspec(B_HEADS, B_V_DIM)),
        out_shape=out_shape,
        scratch_shapes=[pltpu.VMEM((d, IN_COLS_PAD), BF16)],
        compiler_params=pltpu.CompilerParams(dimension_semantics=("arbitrary", "arbitrary"),
                                             vmem_limit_bytes=VMEM_LIMIT),
        name="prologue",
    )(x, g, win, qn, wuq, kvn, wukv, cos, sin)


def _window_body(sink_ref, q_ref, kp_ref, kc_ref, kn_ref, vp_ref, vc_ref, vn_ref, bias_ref, o_ref):
    two = 2 * A_BLOCK
    low = lax.broadcasted_iota(jnp.int32, (two, LANES), 1) < A_HEAD_DIM
    for hk in range(A_KV_HEADS):
        q2 = jnp.concatenate([q_ref[0, :, (2 * hk) * LANES:(2 * hk + 1) * LANES],
                              q_ref[0, :, (2 * hk + 1) * LANES:(2 * hk + 2) * LANES]], axis=0)
        acc = jnp.zeros((two, LANES), F32)
        inv = []
        for half in range(2):
            z = 2 * hk + half
            kband = jnp.concatenate([kp_ref[0, z], kc_ref[0, z], kn_ref[0, z]], axis=0)
            vband = jnp.concatenate([vp_ref[0, z], vc_ref[0, z], vn_ref[0, z]], axis=0)
            s = lax.dot_general(q2, kband, (((1,), (1,)), ((), ())), preferred_element_type=F32)
            s = s + bias_ref[0, z]
            sink = jnp.concatenate(
                [jnp.full((A_BLOCK, 1), sink_ref[hk * A_GROUP + 2 * jj + half], F32) for jj in range(2)], axis=0)
            m = jnp.maximum(jnp.max(s, axis=-1, keepdims=True), sink)
            p = jnp.exp(s - m)
            den = jnp.sum(p, axis=-1, keepdims=True) + jnp.exp(sink - m)
            inv.append(1.0 / den)
            acc = acc + jnp.dot(p.astype(BF16), vband, preferred_element_type=F32)
        o = (acc * jnp.where(low, inv[0], inv[1])).astype(BF16)
        o_ref[0, :, (2 * hk) * LANES:(2 * hk + 1) * LANES] = o[:A_BLOCK]
        o_ref[0, :, (2 * hk + 1) * LANES:(2 * hk + 2) * LANES] = o[A_BLOCK:]


def _window_attn(qa, ka, va, sink, bias):
    b, s, _ = qa.shape
    nb = s // A_BLOCK
    kv_blk = (1, 2 * A_KV_HEADS, A_BLOCK, LANES)
    prev = pl.BlockSpec(kv_blk, lambda bi, n: (bi, 0, jnp.maximum(n - 1, 0), 0))
    cur = pl.BlockSpec(kv_blk, lambda bi, n: (bi, 0, n, 0))
    nxt = pl.BlockSpec(kv_blk, lambda bi, n: (bi, 0, jnp.minimum(n + 1, nb - 1), 0))

    def edge(bi, n):
        return ((n == 0).astype(jnp.int32) + 2 * (n == nb - 1).astype(jnp.int32), 0, 0, 0)

    return pl.pallas_call(
        _window_body,
        grid=(b, nb),
        in_specs=[
            pl.BlockSpec(memory_space=pltpu.SMEM),
            pl.BlockSpec((1, A_BLOCK, A_WIDTH), lambda bi, n: (bi, n, 0)),
            prev, cur, nxt, prev, cur, nxt,
            pl.BlockSpec((1,) + bias.shape[1:], edge),
        ],
        out_specs=pl.BlockSpec((1, A_BLOCK, A_WIDTH), lambda bi, n: (bi, n, 0)),
        out_shape=jax.ShapeDtypeStruct((b, s, A_WIDTH), BF16),
        compiler_params=pltpu.CompilerParams(dimension_semantics=("parallel", "parallel"),
                                             vmem_limit_bytes=VMEM_LIMIT),
        name="window_attn",
    )(sink, qa, ka, ka, ka, va, va, va, bias)


def _mla_body(q_ref, k_ref, v_ref, o_ref):
    tq = q_ref.shape[2]
    s_len = k_ref.shape[2]
    ck = min(CK_MLA, s_len)
    for hh in range(HEADS_MLA):
        q = q_ref[0, hh]
        m = jnp.full((tq, 1), -jnp.inf, F32)
        l = jnp.zeros((tq, 1), F32)
        acc = jnp.zeros((tq, B_V_DIM), F32)
        for c in range(s_len // ck):
            k_c = k_ref[0, hh, c * ck:(c + 1) * ck, :]
            v_c = v_ref[0, hh, c * ck:(c + 1) * ck, :]
            s = lax.dot_general(q, k_c, (((1,), (1,)), ((), ())), preferred_element_type=F32)
            m_new = jnp.maximum(m, jnp.max(s, axis=-1, keepdims=True))
            alpha = jnp.exp(m - m_new)
            p = jnp.exp(s - m_new)
            l = alpha * l + jnp.sum(p, axis=-1, keepdims=True)
            acc = alpha * acc + jnp.dot(p.astype(BF16), v_c, preferred_element_type=F32)
            m = m_new
        o_ref[0, :, hh * B_V_DIM:(hh + 1) * B_V_DIM] = (acc / l).astype(BF16)


def _mla_attn(qb, kb, vb):
    b, nh, s, _ = qb.shape
    tq = min(TQ_MLA, s)
    hs = HEADS_MLA
    return pl.pallas_call(
        _mla_body,
        grid=(b, nh // hs, s // tq),
        in_specs=[
            pl.BlockSpec((1, hs, tq, B_QK_PAD), lambda bi, h, i: (bi, h, i, 0)),
            pl.BlockSpec((1, hs, s, B_QK_PAD), lambda bi, h, i: (bi, h, 0, 0)),
            pl.BlockSpec((1, hs, s, B_V_DIM), lambda bi, h, i: (bi, h, 0, 0)),
        ],
        out_specs=pl.BlockSpec((1, tq, hs * B_V_DIM), lambda bi, h, i: (bi, i, h)),
        out_shape=jax.ShapeDtypeStruct((b, s, B_WIDTH), BF16),
        compiler_params=pltpu.CompilerParams(dimension_semantics=("parallel", "parallel", "parallel"),
                                             vmem_limit_bytes=VMEM_LIMIT),
        name="mla_attn",
    )(qb, kb, vb)


def _out_router_body(ya_ref, yb_ref, x_ref, ga_ref, gb_ref, wo_ref, gm_ref, wr_ref, br_ref,
                     x1_ref, xm_ref, eidx_ref, gate_ref, rank_ref, cnt_ref, run_ref):
    i = pl.program_id(0)

    @pl.when(i == 0)
    def _():
        run_ref[...] = jnp.zeros_like(run_ref)

    na = _rms(ya_ref[...].astype(F32), ga_ref[...]).astype(BF16)
    nb = _rms(yb_ref[...].astype(F32), gb_ref[...]).astype(BF16)
    att = jnp.dot(na, wo_ref[:A_WIDTH, :], preferred_element_type=F32)
    att = att + jnp.dot(nb, wo_ref[A_WIDTH:, :], preferred_element_type=F32)
    x1 = x_ref[...] + att
    x1_ref[...] = x1
    hn = _rms(x1, gm_ref[...])
    _store_token_rows(xm_ref, 0, hn)
    tm = hn.shape[0]
    hi = hn.astype(BF16)
    lo = (hn - hi.astype(F32)).astype(BF16)
    prod = jnp.dot(jnp.concatenate([hi, lo], axis=0), wr_ref[...], preferred_element_type=F32)
    logits = (prod[:tm, :N_EXPERTS] + prod[:tm, N_EXPERTS:] + prod[tm:, :N_EXPERTS] + prod[tm:, N_EXPERTS:]
              + br_ref[...])
    lane = lax.broadcasted_iota(jnp.int32, (tm, N_EXPERTS), 1)
    work = logits
    sel = jnp.zeros((tm, N_EXPERTS), F32)
    hots, vals, idxs = [], [], []
    for _k in range(TOP_K):
        mx = jnp.max(work, axis=-1, keepdims=True)
        idx = jnp.min(jnp.where(work == mx, lane, N_EXPERTS), axis=-1, keepdims=True)
        hot = lane == idx
        hots.append(hot)
        vals.append(mx)
        idxs.append(idx)
        sel = sel + hot.astype(F32)
        work = jnp.where(hot, -jnp.inf, work)
    exps = [jnp.exp(v - vals[0]) for v in vals]
    den = exps[0] + exps[1] + exps[2] + exps[3]
    r_i = lax.broadcasted_iota(jnp.int32, (tm, tm), 0)
    c_i = lax.broadcasted_iota(jnp.int32, (tm, tm), 1)
    tri = (c_i < r_i).astype(BF16)
    before = jnp.dot(tri, sel.astype(BF16), preferred_element_type=F32) + run_ref[...]
    lane4 = lax.broadcasted_iota(jnp.int32, (tm, TOP_K), 1)
    eidx = jnp.zeros((tm, TOP_K), jnp.int32)
    gate = jnp.zeros((tm, TOP_K), F32)
    rank = jnp.zeros((tm, TOP_K), jnp.int32)
    for k in range(TOP_K):
        rk = jnp.sum(jnp.where(hots[k], before, 0.0), axis=-1, keepdims=True).astype(jnp.int32)
        eidx = jnp.where(lane4 == k, idxs[k], eidx)
        gate = jnp.where(lane4 == k, exps[k] / den, gate)
        rank = jnp.where(lane4 == k, rk, rank)
    eidx_ref[...] = eidx
    gate_ref[...] = gate
    rank_ref[...] = rank
    run = run_ref[...] + jnp.sum(sel, axis=0, keepdims=True)
    run_ref[...] = run
    cnt_ref[...] = run.astype(jnp.int32)


def _out_router(ya, yb, x2d, ga, gb, wo, gm, wr, br):
    t, d = x2d.shape
    tm = min(TM_OUT, t)
    sds = jax.ShapeDtypeStruct

    def row(w):
        return pl.BlockSpec((tm, w), lambda i: (i, 0))

    return pl.pallas_call(
        _out_router_body,
        grid=(t // tm,),
        in_specs=[row(A_WIDTH), row(B_WIDTH), row(d), _const_spec(ga.shape), _const_spec(gb.shape),
                  _const_spec(wo.shape), _const_spec(gm.shape), _const_spec(wr.shape), _const_spec(br.shape)],
        out_specs=(row(d), pl.BlockSpec((tm * ROW_PITCH, LANES), lambda i: (i, 0)),
                   row(TOP_K), row(TOP_K), row(TOP_K),
                   pl.BlockSpec((1, N_EXPERTS), lambda i: (0, 0))),
        out_shape=(sds((t, d), F32), sds((t * ROW_PITCH, LANES), F32), sds((t, TOP_K), jnp.int32),
                   sds((t, TOP_K), F32),
                   sds((t, TOP_K), jnp.int32), sds((1, N_EXPERTS), jnp.int32)),
        scratch_shapes=[pltpu.VMEM((1, N_EXPERTS), F32)],
        compiler_params=pltpu.CompilerParams(dimension_semantics=("arbitrary",),
                                             vmem_limit_bytes=VMEM_LIMIT),
        name="out_router",
    )(ya, yb, x2d, ga, gb, wo, gm, wr, br)


def _token_copy(src_ref, src_tok, dst_ref, dst_tok, sem, ntok=1):
    return _rows_copy(src_ref, src_tok * ROW_PITCH, dst_ref, dst_tok * ROW_PITCH, sem, ntok)


def _rows_copy(src_ref, src_row, dst_ref, dst_row, sem, ntok=1):
    rows = ntok * ROW_PITCH
    return pltpu.make_async_copy(src_ref.at[pl.ds(src_row, rows)], dst_ref.at[pl.ds(dst_row, rows)], sem)


def _slot_rows_body(start_ref, eidx_ref, rank_ref, o_ref):
    e = eidx_ref[...]
    base = jnp.zeros_like(e)
    for ex in range(N_EXPERTS):
        base = jnp.where(e == ex, start_ref[ex], base)
    o_ref[...] = (base + rank_ref[...]) * ROW_PITCH


def _slot_rows(start, eidx, rank):
    n = eidx.shape[0]
    shape2 = (n // LANES, LANES)
    out = pl.pallas_call(
        _slot_rows_body,
        in_specs=[pl.BlockSpec(memory_space=pltpu.SMEM), pl.BlockSpec(shape2, lambda: (0, 0)),
                  pl.BlockSpec(shape2, lambda: (0, 0))],
        out_specs=pl.BlockSpec(shape2, lambda: (0, 0)),
        out_shape=jax.ShapeDtypeStruct(shape2, jnp.int32),
        name="slot_rows",
    )(start, eidx.reshape(shape2), rank.reshape(shape2))
    return out.reshape(n)


def _zero_fill_pads(pad_start_ref, pad_len_ref, nact_ref, xs_ref, zero_ref, zsem):
    bm = zero_ref.shape[0] // ROW_PITCH
    nblk = xs_ref.shape[0] // (bm * ROW_PITCH)
    zero_ref[...] = jnp.zeros_like(zero_ref)

    def sweep(do):
        def per_expert(e, c):
            off = pad_start_ref[e]
            n = pad_len_ref[e]
            p = bm // 2
            while p >= 1:
                take = (n & p) != 0

                @pl.when(take)
                def _(p=p, off=off):
                    do(_token_copy(zero_ref, 0, xs_ref, off, zsem, p))

                off = off + jnp.where(take, p, 0)
                p //= 2
            return c

        lax.fori_loop(0, N_EXPERTS, per_expert, 0)

        def tail(j, c):
            do(_token_copy(zero_ref, 0, xs_ref, j * bm, zsem, bm))
            return c

        lax.fori_loop(nact_ref[0], nblk, tail, 0)

    sweep(lambda cp: cp.start())
    sweep(lambda cp: cp.wait())


def _dispatch_body(pad_start_ref, pad_len_ref, nact_ref, slot_ref, xm_ref, xs_ref, zero_ref, sem, zsem):
    tm = slot_ref.shape[0] // TOP_K

    @pl.when(pl.program_id(0) == 0)
    def _():
        _zero_fill_pads(pad_start_ref, pad_len_ref, nact_ref, xs_ref, zero_ref, zsem)

    def issue(i, c):
        for k in range(TOP_K):
            _rows_copy(xm_ref, i * ROW_PITCH, xs_ref, slot_ref[i * TOP_K + k], sem).start(priority=k % 2)
        return c

    lax.fori_loop(0, tm, issue, 0, unroll=2)

    def drain(i, c):
        for _ in range(WAIT_UNROLL):
            _token_copy(xm_ref, 0, xs_ref, 0, sem).wait()
        return c

    lax.fori_loop(0, tm * TOP_K // WAIT_UNROLL, drain, 0)


def _dispatch(pad_start, pad_len, nact, slot_rows, xm, n_slots):
    t = xm.shape[0] // ROW_PITCH
    tm = min(TM_DISPATCH, t)
    grid_spec = pltpu.PrefetchScalarGridSpec(
        num_scalar_prefetch=3,
        grid=(t // tm,),
        in_specs=[pl.BlockSpec((tm * TOP_K,), lambda i, *_: (i,), memory_space=pltpu.SMEM),
                  pl.BlockSpec((tm * ROW_PITCH, LANES), lambda i, *_: (i, 0))],
        out_specs=pl.BlockSpec(memory_space=pl.ANY),
        scratch_shapes=[pltpu.VMEM((MOE_BM * ROW_PITCH, LANES), F32), pltpu.SemaphoreType.DMA(()),
                        pltpu.SemaphoreType.DMA(())],
    )
    return pl.pallas_call(
        _dispatch_body,
        grid_spec=grid_spec,
        out_shape=jax.ShapeDtypeStruct((n_slots * ROW_PITCH, LANES), F32),
        compiler_params=pltpu.CompilerParams(dimension_semantics=("arbitrary",), has_side_effects=True,
                                             vmem_limit_bytes=VMEM_LIMIT),
        name="dispatch",
    )(pad_start, pad_len, nact, slot_rows, xm)


STEP_UP, STEP_DOWN, STEP_TAIL, STEP_IDLE = 0, 1, 2, 3
PREFETCH_PER_DOWN = 2
NF = D_FF // MOE_TF
PASS_BLOCKS = MOE_RM // MOE_BM


def _swiglu(hg, hu):
    gate = jnp.minimum(hg, SWIGLU_LIMIT)
    up = jnp.clip(hu, -SWIGLU_LIMIT, SWIGLU_LIMIT)
    glu = gate / (1.0 + jnp.exp(-SWIGLU_ALPHA * gate))
    return (up + 1.0) * glu


def _experts_body(e_ref, f_ref, kind_ref, blk_ref, blk0_ref, nsub_ref, pre_ref, nxb_ref,
                  xs_ref, wg_ref, wu_ref, bg_ref, bu_ref, wd_ref, bd_ref,
                  y_ref, xb_ref, h_ref, wdb_ref, stg_ref, sem):
    t = pl.program_id(0)
    kind = kind_ref[t]
    f = f_ref[t]
    nsub = nsub_ref[t]

    def x_copy(first_blk, j, slot):
        rows = MOE_SUB * ROW_PITCH
        first = pl.multiple_of((first_blk * MOE_BM + j * MOE_SUB) * ROW_PITCH, SUBLANES)
        return pltpu.make_async_copy(xs_ref.at[pl.ds(first, rows)], stg_ref.at[slot], sem.at[slot])

    def unpack(j, slot):
        rows = pl.ds(pl.multiple_of(j * MOE_SUB, MOE_SUB), MOE_SUB)
        for jt, tile in enumerate(_load_token_rows(stg_ref.at[slot], MOE_SUB)):
            xb_ref[rows, jt * LANES:(jt + 1) * LANES] = tile.astype(BF16)

    @pl.when(kind == STEP_UP)
    def _up():
        pre = pre_ref[t]

        @pl.when((f == 0) & (pre < nsub))
        def _load_rows():
            blk0 = blk0_ref[t]
            x_copy(blk0, pre, pre % 2).start()

            def body(j, c):
                slot = j % 2

                @pl.when(j + 1 < nsub)
                def _():
                    x_copy(blk0, j + 1, 1 - slot).start()

                x_copy(blk0, j, slot).wait()
                unpack(j, slot)
                return c

            lax.fori_loop(pre, nsub, body, 0)

        wdb_ref[pl.ds(pl.multiple_of(f * MOE_TF, MOE_TF), MOE_TF), :] = wd_ref[0].astype(BF16)

        def up_rows(row0, nrows):
            x = xb_ref[pl.ds(row0, nrows), :]
            hg = jnp.dot(x, wg_ref[0].astype(BF16), preferred_element_type=F32) + bg_ref[0]
            hu = jnp.dot(x, wu_ref[0].astype(BF16), preferred_element_type=F32) + bu_ref[0]
            act = _swiglu(hg, hu).astype(BF16)
            h_ref[pl.ds(row0, nrows), pl.ds(pl.multiple_of(f * MOE_TF, MOE_TF), MOE_TF)] = act

        quad = 4 * MOE_SUB
        nquad = nsub // 4
        rem = nsub % 4

        def quad_rows(i, c):
            base = pl.multiple_of(i * quad, quad)
            up_rows(base, quad // 2)
            up_rows(base + quad // 2, quad // 2)
            return c

        merged = nsub == 5

        @pl.when(merged)
        def _():
            up_rows(0, 2 * MOE_SUB)
            up_rows(2 * MOE_SUB, 3 * MOE_SUB)

        @pl.when(jnp.logical_not(merged))
        def _():
            lax.fori_loop(0, nquad, quad_rows, 0)

            @pl.when(rem >= 2)
            def _():
                up_rows(pl.multiple_of(nquad * quad, quad), 2 * MOE_SUB)

            @pl.when(rem % 2 == 1)
            def _():
                up_rows(pl.multiple_of(nquad * quad + (rem // 2) * 2 * MOE_SUB, MOE_SUB), MOE_SUB)

    @pl.when(kind == STEP_DOWN)
    def _down():
        jblk = blk_ref[t] - blk0_ref[t]
        row0 = pl.multiple_of(jblk * MOE_BM, MOE_BM)
        nx_nsub = pre_ref[t]
        nx_blk0 = nxb_ref[t]
        ahead = [jblk * PREFETCH_PER_DOWN + u for u in range(PREFETCH_PER_DOWN)]
        for u, j in enumerate(ahead):
            @pl.when(j < nx_nsub)
            def _(u=u, j=j):
                x_copy(nx_blk0, j, u).start()

        def down_rows(nrows):
            hrows = h_ref[pl.ds(row0, nrows), :]
            return jnp.dot(hrows, wdb_ref[...], preferred_element_type=F32) + bd_ref[0]

        @pl.when(nsub == 2)
        def _():
            _store_token_rows(y_ref, 0, down_rows(MOE_BM))

        @pl.when(nsub == 1)
        def _():
            _store_token_rows(y_ref, 0, down_rows(MOE_SUB))
            y_ref[MOE_SUB * ROW_PITCH:, :] = jnp.zeros(((MOE_BM - MOE_SUB) * ROW_PITCH, LANES), F32)

        for u, j in enumerate(ahead):
            @pl.when(j < nx_nsub)
            def _(u=u, j=j):
                x_copy(nx_blk0, j, u).wait()
                unpack(j, u)

    @pl.when(kind == STEP_TAIL)
    def _tail():
        y_ref[...] = jnp.zeros_like(y_ref)


def _schedule_body(n_blocks, cnt_ref, e_ref, f_ref, kind_ref, blk_ref, blk0_ref, nsub_ref, pre_ref, nxb_ref,
                   start_ref, pad_start_ref, pad_len_ref, nact_ref):
    n_steps = e_ref.shape[0]
    sub_per_blk = MOE_BM // MOE_SUB
    zero = jnp.int32(0)

    def put(idx, e, f, kind, blk, blk0, nsub, pre):
        e_ref[idx] = e
        f_ref[idx] = f
        kind_ref[idx] = kind
        blk_ref[idx] = blk
        blk0_ref[idx] = blk0
        nsub_ref[idx] = nsub
        pre_ref[idx] = pre
        nxb_ref[idx] = zero

    def per_expert(e, carry):
        t, blk, last_e, prev_down, prev_nb = carry
        n = cnt_ref[e]
        nblk = (n + (MOE_BM - 1)) // MOE_BM
        start_ref[e] = blk * MOE_BM
        pad_start_ref[e] = blk * MOE_BM + n
        pad_len_ref[e] = nblk * MOE_BM - n

        def per_pass(p, carry2):
            t2, pdown, pnb = carry2
            b0 = blk + p * PASS_BLOCKS
            nb = jnp.minimum(nblk - p * PASS_BLOCKS, PASS_BLOCKS)
            nsub = (jnp.minimum(n - p * MOE_RM, MOE_RM) + (MOE_SUB - 1)) // MOE_SUB
            pre = jnp.minimum(nsub, pnb * PREFETCH_PER_DOWN)
            for f in range(NF):
                put(t2 + f, e, f, STEP_UP, b0, b0, nsub, pre)

            def patch(j, c):
                pre_ref[pdown + j] = nsub
                nxb_ref[pdown + j] = b0
                return c

            lax.fori_loop(0, pnb, patch, 0)

            def per_blk(j, c):
                put(t2 + NF + j, e, NF - 1, STEP_DOWN, b0 + j, b0,
                    jnp.clip(nsub - j * sub_per_blk, 1, sub_per_blk), zero)
                return c

            lax.fori_loop(0, nb, per_blk, 0)
            return t2 + NF + nb, t2 + NF, nb

        t, prev_down, prev_nb = lax.fori_loop(0, (nblk + (PASS_BLOCKS - 1)) // PASS_BLOCKS, per_pass,
                                              (t, prev_down, prev_nb))
        return t, blk + nblk, jnp.where(n > 0, e, last_e), prev_down, prev_nb

    t, nact, last_e, _, _ = lax.fori_loop(0, N_EXPERTS, per_expert, (zero, zero, zero, zero, zero))
    nact_ref[0] = nact
    nact_ref[1] = t + n_blocks - nact

    def spare(i, c):
        is_tail = i < n_blocks - nact
        blk = jnp.where(is_tail, nact + i, n_blocks - 1)
        put(t + i, last_e, NF - 1, jnp.where(is_tail, STEP_TAIL, STEP_IDLE), blk, blk, 1, zero)
        return c

    lax.fori_loop(0, n_steps - t, spare, 0)


def _expert_schedule(counts, n_blocks, n_assign):
    n_pass_max = N_EXPERTS + n_assign // MOE_RM
    n_steps = n_pass_max * NF + n_blocks
    smem = pl.BlockSpec(memory_space=pltpu.SMEM)
    i32 = jnp.int32
    out_shape = tuple(jax.ShapeDtypeStruct((n,), i32) for n in (n_steps,) * 8 + (N_EXPERTS,) * 3 + (2,))
    outs = pl.pallas_call(
        functools.partial(_schedule_body, n_blocks),
        in_specs=[smem],
        out_specs=tuple(smem for _ in out_shape),
        out_shape=out_shape,
        name="schedule",
    )(counts)
    return outs[:8], outs[8], outs[9], outs[10], outs[11]


def _experts(tabs, xs, w_up, b_up, w_down, b_down, name):
    d = D_MODEL
    n_steps = tabs[0].shape[0]
    grid_spec = pltpu.PrefetchScalarGridSpec(
        num_scalar_prefetch=len(tabs),
        grid=(n_steps,),
        in_specs=[
            pl.BlockSpec(memory_space=pl.ANY),
            pl.BlockSpec((1, d, MOE_TF), lambda t, e, f, *_: (e[t], 0, f[t])),
            pl.BlockSpec((1, d, MOE_TF), lambda t, e, f, *_: (e[t], 0, NF + f[t])),
            pl.BlockSpec((1, 1, MOE_TF), lambda t, e, f, *_: (e[t], 0, f[t])),
            pl.BlockSpec((1, 1, MOE_TF), lambda t, e, f, *_: (e[t], 0, NF + f[t])),
            pl.BlockSpec((1, MOE_TF, d), lambda t, e, f, *_: (e[t], f[t], 0)),
            pl.BlockSpec((1, 1, d), lambda t, e, f, *_: (e[t], 0, 0)),
        ],
        out_specs=pl.BlockSpec((MOE_BM * ROW_PITCH, LANES), lambda t, e, f, kind, blk, *_: (blk[t], 0)),
        scratch_shapes=[
            pltpu.VMEM((MOE_RM, d), BF16),
            pltpu.VMEM((MOE_RM, D_FF), BF16),
            pltpu.VMEM((D_FF, d), BF16),
            pltpu.VMEM((2, MOE_SUB * ROW_PITCH, LANES), F32),
            pltpu.SemaphoreType.DMA((2,)),
        ],
    )
    return pl.pallas_call(
        _experts_body,
        grid_spec=grid_spec,
        out_shape=jax.ShapeDtypeStruct(xs.shape, F32),
        compiler_params=pltpu.CompilerParams(dimension_semantics=("arbitrary",),
                                             vmem_limit_bytes=VMEM_LIMIT),
        name=name,
    )(*tabs, xs, w_up, w_up, b_up, b_up, w_down, b_down)


def _combine_body(slot_ref, slot_nx_ref, gate_ref, x1_ref, gf_ref, ys_ref, o_ref, buf_ref, sem):
    step = pl.program_id(0)
    slot = step % 2
    tm = x1_ref.shape[0]

    def gather(rows_ref, s):
        def issue(i, c):
            for k in range(TOP_K):
                _rows_copy(ys_ref, rows_ref[i * TOP_K + k], buf_ref.at[s, k], i * ROW_PITCH,
                           sem.at[s]).start(priority=k % 2)
            return c

        lax.fori_loop(0, tm, issue, 0, unroll=2)

    @pl.when(step == 0)
    def _():
        gather(slot_ref, 0)

    for s in range(2):
        @pl.when((step + 1 < pl.num_programs(0)) & (slot == 1 - s))
        def _(s=s):
            gather(slot_nx_ref, s)

    def drain(i, c):
        for _ in range(WAIT_UNROLL):
            _token_copy(ys_ref, 0, buf_ref.at[slot, 0], 0, sem.at[slot]).wait()
        return c

    lax.fori_loop(0, tm * TOP_K // WAIT_UNROLL, drain, 0)
    gate = gate_ref[...]
    tiles = [x1_ref[:, j * LANES:(j + 1) * LANES] for j in range(ROW_TILES)]
    for k in range(TOP_K):
        g = gate[:, k:k + 1]
        for j, tile in enumerate(_load_token_rows(buf_ref.at[slot, k], tm)):
            tiles[j] = tiles[j] + g * tile
    o_ref[...] = _rms(jnp.concatenate(tiles, axis=1), gf_ref[...])


def _combine(slot_rows, gates, x1, gf, ys):
    t, d = x1.shape
    tm = min(TM_COMBINE, t)
    n = t // tm
    idx_blk = (tm * TOP_K,)
    return pl.pallas_call(
        _combine_body,
        grid=(n,),
        in_specs=[pl.BlockSpec(idx_blk, lambda i: (i,), memory_space=pltpu.SMEM),
                  pl.BlockSpec(idx_blk, lambda i: (jnp.minimum(i + 1, n - 1),), memory_space=pltpu.SMEM),
                  pl.BlockSpec((tm, TOP_K), lambda i: (i, 0)),
                  pl.BlockSpec((tm, d), lambda i: (i, 0)),
                  _const_spec(gf.shape),
                  pl.BlockSpec(memory_space=pl.ANY)],
        out_specs=pl.BlockSpec((tm, d), lambda i: (i, 0)),
        scratch_shapes=[pltpu.VMEM((2, TOP_K, tm * ROW_PITCH, LANES), F32), pltpu.SemaphoreType.DMA((2,))],
        out_shape=jax.ShapeDtypeStruct((t, d), F32),
        compiler_params=pltpu.CompilerParams(dimension_semantics=("arbitrary",),
                                             vmem_limit_bytes=VMEM_LIMIT),
        name="combine",
    )(slot_rows, slot_rows, gates, x1, gf, ys)


def _rope_tables(s):
    pos = np.arange(s, dtype=np.float32)
    inv_freq = np.power(np.float32(ROPE_THETA), -np.arange(0, B_ROPE_DIM, 2, dtype=np.float32) / B_ROPE_DIM)
    ang = (pos[:, None] * inv_freq[None, :]).astype(np.float32)
    cos, sin = np.cos(ang), np.sin(ang)
    pad = LANES - B_ROPE_DIM
    cos = np.concatenate([cos, cos, np.ones((s, pad), np.float32)], axis=1)
    sin = np.concatenate([sin, sin, np.zeros((s, pad), np.float32)], axis=1)
    return cos.astype(np.float32), sin.astype(np.float32)


def _window_bias():
    qi = np.arange(A_BLOCK)[:, None]
    kj = np.arange(3 * A_BLOCK)[None, :]
    dist = np.abs(qi + A_BLOCK - kj)
    slopes = np.power(np.float32(2.0), -8.0 * np.arange(1, A_HEADS + 1, dtype=np.float32) / A_HEADS)
    bias = -slopes[:, None, None] * dist.astype(np.float32)[None]
    bias = np.where((dist <= WINDOW)[None], bias, -np.inf)
    bias = bias.reshape(A_KV_HEADS, 2, 2, A_BLOCK, 3 * A_BLOCK).transpose(0, 2, 1, 3, 4)
    bias = bias.reshape(2 * A_KV_HEADS, 2 * A_BLOCK, 3 * A_BLOCK)
    no_prev = (kj < A_BLOCK)[None]
    no_next = (kj >= 2 * A_BLOCK)[None]
    ninf = -np.inf
    return np.stack([bias, np.where(no_prev, ninf, bias), np.where(no_next, ninf, bias),
                     np.where(no_prev | no_next, ninf, bias)]).astype(np.float32)


def _layer(x, attn_norm, w_in, a_sink, b_q_norm, b_w_uq, b_kv_norm, b_w_ukv, out_norm_a, out_norm_b, w_o,
           mlp_norm, w_router, b_router, w_up, b_up, w_down, b_down):
    b, s, d = x.shape
    t = b * s
    wq = b_w_uq.reshape(B_Q_RANK, B_HEADS, B_NOPE_DIM + B_ROPE_DIM)
    wq_pe = jnp.pad(wq[:, :, B_NOPE_DIM:], ((0, 0), (0, 0), (0, LANES - B_ROPE_DIM)))
    wuq = jnp.concatenate([wq[:, :, :B_NOPE_DIM].reshape(B_Q_RANK, -1), wq_pe.reshape(B_Q_RANK, -1)],
                          axis=1).astype(BF16)
    wkv = b_w_ukv.reshape(B_KV_RANK, B_HEADS, B_NOPE_DIM + B_V_DIM)
    wukv = jnp.concatenate([wkv[:, :, :B_NOPE_DIM].reshape(B_KV_RANK, -1),
                            wkv[:, :, B_NOPE_DIM:].reshape(B_KV_RANK, -1)], axis=1).astype(BF16)
    cos, sin = _rope_tables(s)

    qa, ka, va, qb, kb, vb = _prologue(x, attn_norm[None], w_in, b_q_norm[None], wuq, b_kv_norm[None], wukv,
                                       cos, sin)
    ya = _window_attn(qa, ka, va, a_sink, _window_bias())
    yb = _mla_attn(qb, kb, vb)
    wr_hi = w_router.astype(BF16)
    wr_lo = (w_router - wr_hi.astype(F32)).astype(BF16)
    x1, xm, eidx, gates, rank, counts = _out_router(
        ya.reshape(t, A_WIDTH), yb.reshape(t, B_WIDTH), x.reshape(t, d), out_norm_a[None], out_norm_b[None],
        w_o.astype(BF16), mlp_norm[None], jnp.concatenate([wr_hi, wr_lo], axis=1), b_router[None])

    nblk = (t * TOP_K + N_EXPERTS * (MOE_BM - 1) + MOE_BM - 1) // MOE_BM
    tabs, start, pad_start, pad_len, nact = _expert_schedule(counts.reshape(N_EXPERTS), nblk, t * TOP_K)
    slot_rows = _slot_rows(start, eidx.reshape(t * TOP_K), rank.reshape(t * TOP_K))
    xs = _dispatch(pad_start, pad_len, nact, slot_rows, xm, nblk * MOE_BM)
    n_short = N_EXPERTS * NF + nblk
    args = (xs, w_up, b_up[:, None, :], w_down, b_down[:, None, :])
    ys = lax.cond(nact[1] <= n_short,
                  lambda: _experts(tuple(a[:n_short] for a in tabs), *args, name="experts"),
                  lambda: _experts(tabs, *args, name="experts_long"))
    return x1, slot_rows, gates, ys


def kernel(x, attn_norm, w_in, a_sink, b_q_norm, b_w_uq, b_kv_norm, b_w_ukv, out_norm_a, out_norm_b, w_o,
           mlp_norm, w_router, b_router, w_up, b_up, w_down, b_down, final_norm):
    b, s, d = x.shape
    assert d == D_MODEL and s % min(TQ_MLA, s) == 0 and s % min(CK_MLA, s) == 0 and s % TM_PRO == 0 and attn_norm.shape[0] == 1
    x1, slot_rows, gates, ys = _layer(
        x, attn_norm[0], w_in[0], a_sink[0], b_q_norm[0], b_w_uq[0], b_kv_norm[0], b_w_ukv[0], out_norm_a[0],
        out_norm_b[0], w_o[0], mlp_norm[0], w_router[0], b_router[0], w_up[0], b_up[0], w_down[0], b_down[0])
    out = _combine(slot_rows, gates, x1, final_norm[None], ys)
    return out.reshape(b, s, d)
```

```python
import functools

import jax
import numpy as np
import jax.numpy as jnp
from jax import lax
from jax.experimental import pallas as pl
from jax.experimental.pallas import tpu as pltpu

D_MODEL = 2048
A_HEADS, A_KV_HEADS, A_HEAD_DIM = 16, 4, 64
A_GROUP = A_HEADS // A_KV_HEADS
WINDOW = 128
A_BLOCK = 128
B_HEADS, B_Q_RANK, B_KV_RANK = 8, 512, 256
B_NOPE_DIM, B_ROPE_DIM, B_V_DIM = 128, 64, 128
ROPE_THETA = 10000.0
A_WIDTH = A_HEADS * A_HEAD_DIM
B_WIDTH = B_HEADS * B_V_DIM
A_KV_COLS = A_KV_HEADS * A_HEAD_DIM
N_EXPERTS, TOP_K, D_FF = 32, 4, 2048
SWIGLU_ALPHA, SWIGLU_LIMIT = 1.702, 7.0
EPS = 1e-5

LANES = 128
SUBLANES = 8
B_QK_PAD = 2 * LANES
IN_COLS_PAD = A_WIDTH + 2 * A_KV_COLS + B_Q_RANK + B_KV_RANK + LANES
VMEM_LIMIT = 56 * 1024 * 1024
ROW_TILES = D_MODEL // LANES
ROW_PITCH = ROW_TILES + 1

TM_PRO = 256
TQ_MLA = 1024
CK_MLA = 2048
HEADS_MLA = 2
TM_OUT = 512
TM_DISPATCH = 1024
TM_COMBINE = 256
MOE_BM = 512
MOE_SUB = 256
MOE_RM = 2048
MOE_TF = 256
WAIT_UNROLL = 16

BF16 = jnp.bfloat16
F32 = jnp.float32


def _rms(x, g):
    return x * lax.rsqrt(jnp.mean(x * x, axis=-1, keepdims=True) + EPS) * g


def _const_spec(shape):
    nd = len(shape)
    return pl.BlockSpec(shape, lambda *_: (0,) * nd, pipeline_mode=pl.Buffered(1))


def _store_token_rows(ref, first, val):
    n = val.shape[0]
    for j in range(ROW_TILES):
        ref[pl.ds(first * ROW_PITCH + j, n, stride=ROW_PITCH), :] = val[:, j * LANES:(j + 1) * LANES]
    ref[pl.ds(first * ROW_PITCH + ROW_TILES, n, stride=ROW_PITCH), :] = jnp.zeros((n, LANES), val.dtype)


def _load_token_rows(ref, n):
    return [ref[pl.ds(j, n, stride=ROW_PITCH), :] for j in range(ROW_TILES)]


def _rope(x, cos, sin):
    lane = lax.broadcasted_iota(jnp.int32, x.shape, 1)
    up = pltpu.roll(x, LANES - B_ROPE_DIM // 2, 1)
    dn = pltpu.roll(x, B_ROPE_DIM // 2, 1)
    sw = jnp.where(lane < B_ROPE_DIM // 2, -up, jnp.where(lane < B_ROPE_DIM, dn, 0.0))
    return x * cos + sw * sin


def _prologue_body(x_ref, g_ref, win_ref, qn_ref, wuq_ref, kvn_ref, wukv_ref, cos_ref, sin_ref,
                   qa_ref, ka_ref, va_ref, qb_ref, kb_ref, vb_ref, winb_ref):
    @pl.when((pl.program_id(0) == 0) & (pl.program_id(1) == 0))
    def _():
        n_in = win_ref.shape[1]
        winb_ref[:, :n_in] = win_ref[...].astype(BF16)
        winb_ref[:, n_in:] = jnp.zeros((winb_ref.shape[0], IN_COLS_PAD - n_in), BF16)

    x = x_ref[0]
    xn = _rms(x, g_ref[...]).astype(BF16)
    proj = jnp.dot(xn, winb_ref[...], preferred_element_type=F32)
    c0 = A_WIDTH
    c1 = c0 + A_KV_COLS
    c2 = c1 + A_KV_COLS
    c3 = c2 + B_Q_RANK
    c4 = c3 + B_KV_RANK
    qa_ref[0] = (proj[:, :A_WIDTH] * A_HEAD_DIM ** -0.5).astype(BF16)
    low = lax.broadcasted_iota(jnp.int32, (proj.shape[0], LANES), 1) < A_HEAD_DIM
    for src, dst in ((c0, ka_ref), (c1, va_ref)):
        for pair in range(A_KV_HEADS // 2):
            two = proj[:, src + pair * LANES:src + (pair + 1) * LANES]
            swapped = pltpu.roll(two, A_HEAD_DIM, 1)
            dst[0, 4 * pair + 0] = jnp.where(low, two, 0.0).astype(BF16)
            dst[0, 4 * pair + 1] = jnp.where(low, 0.0, swapped).astype(BF16)
            dst[0, 4 * pair + 2] = jnp.where(low, swapped, 0.0).astype(BF16)
            dst[0, 4 * pair + 3] = jnp.where(low, 0.0, two).astype(BF16)
    cq = _rms(proj[:, c2:c3], qn_ref[...]).astype(BF16)
    q = jnp.dot(cq, wuq_ref[...], preferred_element_type=F32)
    ckv = _rms(proj[:, c3:c4], kvn_ref[...]).astype(BF16)
    kv = jnp.dot(ckv, wukv_ref[...], preferred_element_type=F32)
    cos = cos_ref[...]
    sin = sin_ref[...]
    kpe = _rope(proj[:, c4:c4 + LANES], cos, sin).astype(BF16)
    b_scale = (B_NOPE_DIM + B_ROPE_DIM) ** -0.5
    hw = B_HEADS * LANES
    for h in range(B_HEADS):
        sl = slice(h * LANES, (h + 1) * LANES)
        qb_ref[0, h, :, :LANES] = (q[:, sl] * b_scale).astype(BF16)
        qpe = _rope(q[:, hw + h * LANES:hw + (h + 1) * LANES], cos, sin)
        qb_ref[0, h, :, LANES:] = (qpe * b_scale).astype(BF16)
        kb_ref[0, h, :, :LANES] = kv[:, sl].astype(BF16)
        kb_ref[0, h, :, LANES:] = kpe
        vb_ref[0, h] = kv[:, hw + h * LANES:hw + (h + 1) * LANES].astype(BF16)


def _prologue(x, g, win, qn, wuq, kvn, wukv, cos, sin):
    b, s, d = x.shape
    tm = TM_PRO
    grid = (b, s // tm)
    sds = jax.ShapeDtypeStruct
    out_shape = (
        sds((b, s, A_WIDTH), BF16),
        sds((b, 2 * A_KV_HEADS, s, LANES), BF16),
        sds((b, 2 * A_KV_HEADS, s, LANES), BF16),
        sds((b, B_HEADS, s, B_QK_PAD), BF16),
        sds((b, B_HEADS, s, B_QK_PAD), BF16),
        sds((b, B_HEADS, s, B_V_DIM), BF16),
    )

    def hspec(nh, w):
        return pl.BlockSpec((1, nh, tm, w), lambda bi, i: (bi, 0, i, 0))

    return pl.pallas_call(
        _prologue_body,
        grid=grid,
        in_specs=[
            pl.BlockSpec((1, tm, d), lambda bi, i: (bi, i, 0)),
            _const_spec(g.shape), _const_spec(win.shape), _const_spec(qn.shape), _const_spec(wuq.shape),
            _const_spec(kvn.shape), _const_spec(wukv.shape),
            pl.BlockSpec((tm, LANES), lambda bi, i: (i, 0)),
            pl.BlockSpec((tm, LANES), lambda bi, i: (i, 0)),
        ],
        out_specs=(pl.BlockSpec((1, tm, A_WIDTH), lambda bi, i: (bi, i, 0)),
                   hspec(2 * A_KV_HEADS, LANES), hspec(2 * A_KV_HEADS, LANES),
                   hspec(B_HEADS, B_QK_PAD), hspec(B_HEADS, B_QK_PAD), hspec(B_HEADS, B_V_DIM)),
        out_shape=out_shape,
        scratch_shapes=[pltpu.VMEM((d, IN_COLS_PAD), BF16)],
        compiler_params=pltpu.CompilerParams(dimension_semantics=("arbitrary", "arbitrary"),
                                             vmem_limit_bytes=VMEM_LIMIT),
        name="prologue",
    )(x, g, win, qn, wuq, kvn, wukv, cos, sin)


def _window_body(sink_ref, q_ref, kp_ref, kc_ref, kn_ref, vp_ref, vc_ref, vn_ref, bias_ref, o_ref):
    two = 2 * A_BLOCK
    low = lax.broadcasted_iota(jnp.int32, (two, LANES), 1) < A_HEAD_DIM
    for hk in range(A_KV_HEADS):
        q2 = jnp.concatenate([q_ref[0, :, (2 * hk) * LANES:(2 * hk + 1) * LANES],
                              q_ref[0, :, (2 * hk + 1) * LANES:(2 * hk + 2) * LANES]], axis=0)
        acc = jnp.zeros((two, LANES), F32)
        inv = []
        for half in range(2):
            z = 2 * hk + half
            kband = jnp.concatenate([kp_ref[0, z], kc_ref[0, z], kn_ref[0, z]], axis=0)
            vband = jnp.concatenate([vp_ref[0, z], vc_ref[0, z], vn_ref[0, z]], axis=0)
            s = lax.dot_general(q2, kband, (((1,), (1,)), ((), ())), preferred_element_type=F32)
            s = s + bias_ref[0, z]
            sink = jnp.concatenate(
                [jnp.full((A_BLOCK, 1), sink_ref[hk * A_GROUP + 2 * jj + half], F32) for jj in range(2)], axis=0)
            m = jnp.maximum(jnp.max(s, axis=-1, keepdims=True), sink)
            p = jnp.exp(s - m)
            den = jnp.sum(p, axis=-1, keepdims=True) + jnp.exp(sink - m)
            inv.append(1.0 / den)
            acc = acc + jnp.dot(p.astype(BF16), vband, preferred_element_type=F32)
        o = (acc * jnp.where(low, inv[0], inv[1])).astype(BF16)
        o_ref[0, :, (2 * hk) * LANES:(2 * hk + 1) * LANES] = o[:A_BLOCK]
        o_ref[0, :, (2 * hk + 1) * LANES:(2 * hk + 2) * LANES] = o[A_BLOCK:]


def _window_attn(qa, ka, va, sink, bias):
    b, s, _ = qa.shape
    nb = s // A_BLOCK
    kv_blk = (1, 2 * A_KV_HEADS, A_BLOCK, LANES)
    prev = pl.BlockSpec(kv_blk, lambda bi, n: (bi, 0, jnp.maximum(n - 1, 0), 0))
    cur = pl.BlockSpec(kv_blk, lambda bi, n: (bi, 0, n, 0))
    nxt = pl.BlockSpec(kv_blk, lambda bi, n: (bi, 0, jnp.minimum(n + 1, nb - 1), 0))

    def edge(bi, n):
        return ((n == 0).astype(jnp.int32) + 2 * (n == nb - 1).astype(jnp.int32), 0, 0, 0)

    return pl.pallas_call(
        _window_body,
        grid=(b, nb),
        in_specs=[
            pl.BlockSpec(memory_space=pltpu.SMEM),
            pl.BlockSpec((1, A_BLOCK, A_WIDTH), lambda bi, n: (bi, n, 0)),
            prev, cur, nxt, prev, cur, nxt,
            pl.BlockSpec((1,) + bias.shape[1:], edge),
        ],
        out_specs=pl.BlockSpec((1, A_BLOCK, A_WIDTH), lambda bi, n: (bi, n, 0)),
        out_shape=jax.ShapeDtypeStruct((b, s, A_WIDTH), BF16),
        compiler_params=pltpu.CompilerParams(dimension_semantics=("parallel", "parallel"),
                                             vmem_limit_bytes=VMEM_LIMIT),
        name="window_attn",
    )(sink, qa, ka, ka, ka, va, va, va, bias)


def _mla_body(q_ref, k_ref, v_ref, o_ref):
    tq = q_ref.shape[2]
    s_len = k_ref.shape[2]
    ck = min(CK_MLA, s_len)
    for hh in range(HEADS_MLA):
        q = q_ref[0, hh]
        m = jnp.full((tq, 1), -jnp.inf, F32)
        l = jnp.zeros((tq, 1), F32)
        acc = jnp.zeros((tq, B_V_DIM), F32)
        for c in range(s_len // ck):
            k_c = k_ref[0, hh, c * ck:(c + 1) * ck, :]
            v_c = v_ref[0, hh, c * ck:(c + 1) * ck, :]
            s = lax.dot_general(q, k_c, (((1,), (1,)), ((), ())), preferred_element_type=F32)
            m_new = jnp.maximum(m, jnp.max(s, axis=-1, keepdims=True))
            alpha = jnp.exp(m - m_new)
            p = jnp.exp(s - m_new)
            l = alpha * l + jnp.sum(p, axis=-1, keepdims=True)
            acc = alpha * acc + jnp.dot(p.astype(BF16), v_c, preferred_element_type=F32)
            m = m_new
        o_ref[0, :, hh * B_V_DIM:(hh + 1) * B_V_DIM] = (acc / l).astype(BF16)


def _mla_attn(qb, kb, vb):
    b, nh, s, _ = qb.shape
    tq = min(TQ_MLA, s)
    hs = HEADS_MLA
    return pl.pallas_call(
        _mla_body,
        grid=(b, nh // hs, s // tq),
        in_specs=[
            pl.BlockSpec((1, hs, tq, B_QK_PAD), lambda bi, h, i: (bi, h, i, 0)),
            pl.BlockSpec((1, hs, s, B_QK_PAD), lambda bi, h, i: (bi, h, 0, 0)),
            pl.BlockSpec((1, hs, s, B_V_DIM), lambda bi, h, i: (bi, h, 0, 0)),
        ],
        out_specs=pl.BlockSpec((1, tq, hs * B_V_DIM), lambda bi, h, i: (bi, i, h)),
        out_shape=jax.ShapeDtypeStruct((b, s, B_WIDTH), BF16),
        compiler_params=pltpu.CompilerParams(dimension_semantics=("parallel", "parallel", "parallel"),
                                             vmem_limit_bytes=VMEM_LIMIT),
        name="mla_attn",
    )(qb, kb, vb)


def _out_router_body(ya_ref, yb_ref, x_ref, ga_ref, gb_ref, wo_ref, gm_ref, wr_ref, br_ref,
                     x1_ref, xm_ref, eidx_ref, gate_ref, rank_ref, cnt_ref, run_ref):
    i = pl.program_id(0)

    @pl.when(i == 0)
    def _():
        run_ref[...] = jnp.zeros_like(run_ref)

    na = _rms(ya_ref[...].astype(F32), ga_ref[...]).astype(BF16)
    nb = _rms(yb_ref[...].astype(F32), gb_ref[...]).astype(BF16)
    att = jnp.dot(na, wo_ref[:A_WIDTH, :], preferred_element_type=F32)
    att = att + jnp.dot(nb, wo_ref[A_WIDTH:, :], preferred_element_type=F32)
    x1 = x_ref[...] + att
    x1_ref[...] = x1
    hn = _rms(x1, gm_ref[...])
    _store_token_rows(xm_ref, 0, hn)
    tm = hn.shape[0]
    hi = hn.astype(BF16)
    lo = (hn - hi.astype(F32)).astype(BF16)
    prod = jnp.dot(jnp.concatenate([hi, lo], axis=0), wr_ref[...], preferred_element_type=F32)
    logits = (prod[:tm, :N_EXPERTS] + prod[:tm, N_EXPERTS:] + prod[tm:, :N_EXPERTS] + prod[tm:, N_EXPERTS:]
              + br_ref[...])
    lane = lax.broadcasted_iota(jnp.int32, (tm, N_EXPERTS), 1)
    work = logits
    sel = jnp.zeros((tm, N_EXPERTS), F32)
    hots, vals, idxs = [], [], []
    for _k in range(TOP_K):
        mx = jnp.max(work, axis=-1, keepdims=True)
        idx = jnp.min(jnp.where(work == mx, lane, N_EXPERTS), axis=-1, keepdims=True)
        hot = lane == idx
        hots.append(hot)
        vals.append(mx)
        idxs.append(idx)
        sel = sel + hot.astype(F32)
        work = jnp.where(hot, -jnp.inf, work)
    exps = [jnp.exp(v - vals[0]) for v in vals]
    den = exps[0] + exps[1] + exps[2] + exps[3]
    r_i = lax.broadcasted_iota(jnp.int32, (tm, tm), 0)
    c_i = lax.broadcasted_iota(jnp.int32, (tm, tm), 1)
    tri = (c_i < r_i).astype(BF16)
    before = jnp.dot(tri, sel.astype(BF16), preferred_element_type=F32) + run_ref[...]
    lane4 = lax.broadcasted_iota(jnp.int32, (tm, TOP_K), 1)
    eidx = jnp.zeros((tm, TOP_K), jnp.int32)
    gate = jnp.zeros((tm, TOP_K), F32)
    rank = jnp.zeros((tm, TOP_K), jnp.int32)
    for k in range(TOP_K):
        rk = jnp.sum(jnp.where(hots[k], before, 0.0), axis=-1, keepdims=True).astype(jnp.int32)
        eidx = jnp.where(lane4 == k, idxs[k], eidx)
        gate = jnp.where(lane4 == k, exps[k] / den, gate)
        rank = jnp.where(lane4 == k, rk, rank)
    eidx_ref[...] = eidx
    gate_ref[...] = gate
    rank_ref[...] = rank
    run = run_ref[...] + jnp.sum(sel, axis=0, keepdims=True)
    run_ref[...] = run
    cnt_ref[...] = run.astype(jnp.int32)


def _out_router(ya, yb, x2d, ga, gb, wo, gm, wr, br):
    t, d = x2d.shape
    tm = min(TM_OUT, t)
    sds = jax.ShapeDtypeStruct

    def row(w):
        return pl.BlockSpec((tm, w), lambda i: (i, 0))

    return pl.pallas_call(
        _out_router_body,
        grid=(t // tm,),
        in_specs=[row(A_WIDTH), row(B_WIDTH), row(d), _const_spec(ga.shape), _const_spec(gb.shape),
                  _const_spec(wo.shape), _const_spec(gm.shape), _const_spec(wr.shape), _const_spec(br.shape)],
        out_specs=(row(d), pl.BlockSpec((tm * ROW_PITCH, LANES), lambda i: (i, 0)),
                   row(TOP_K), row(TOP_K), row(TOP_K),
                   pl.BlockSpec((1, N_EXPERTS), lambda i: (0, 0))),
        out_shape=(sds((t, d), F32), sds((t * ROW_PITCH, LANES), F32), sds((t, TOP_K), jnp.int32),
                   sds((t, TOP_K), F32),
                   sds((t, TOP_K), jnp.int32), sds((1, N_EXPERTS), jnp.int32)),
        scratch_shapes=[pltpu.VMEM((1, N_EXPERTS), F32)],
        compiler_params=pltpu.CompilerParams(dimension_semantics=("arbitrary",),
                                             vmem_limit_bytes=VMEM_LIMIT),
        name="out_router",
    )(ya, yb, x2d, ga, gb, wo, gm, wr, br)


def _token_copy(src_ref, src_tok, dst_ref, dst_tok, sem, ntok=1):
    return _rows_copy(src_ref, src_tok * ROW_PITCH, dst_ref, dst_tok * ROW_PITCH, sem, ntok)


def _rows_copy(src_ref, src_row, dst_ref, dst_row, sem, ntok=1):
    rows = ntok * ROW_PITCH
    return pltpu.make_async_copy(src_ref.at[pl.ds(src_row, rows)], dst_ref.at[pl.ds(dst_row, rows)], sem)


def _slot_rows_body(start_ref, eidx_ref, rank_ref, o_ref):
    e = eidx_ref[...]
    base = jnp.zeros_like(e)
    for ex in range(N_EXPERTS):
        base = jnp.where(e == ex, start_ref[ex], base)
    o_ref[...] = (base + rank_ref[...]) * ROW_PITCH


def _slot_rows(start, eidx, rank):
    n = eidx.shape[0]
    shape2 = (n // LANES, LANES)
    out = pl.pallas_call(
        _slot_rows_body,
        in_specs=[pl.BlockSpec(memory_space=pltpu.SMEM), pl.BlockSpec(shape2, lambda: (0, 0)),
                  pl.BlockSpec(shape2, lambda: (0, 0))],
        out_specs=pl.BlockSpec(shape2, lambda: (0, 0)),
        out_shape=jax.ShapeDtypeStruct(shape2, jnp.int32),
        name="slot_rows",
    )(start, eidx.reshape(shape2), rank.reshape(shape2))
    return out.reshape(n)


def _zero_fill_pads(pad_start_ref, pad_len_ref, nact_ref, xs_ref, zero_ref, zsem):
    bm = zero_ref.shape[0] // ROW_PITCH
    nblk = xs_ref.shape[0] // (bm * ROW_PITCH)
    zero_ref[...] = jnp.zeros_like(zero_ref)

    def sweep(do):
        def per_expert(e, c):
            off = pad_start_ref[e]
            n = pad_len_ref[e]
            p = bm // 2
            while p >= 1:
                take = (n & p) != 0

                @pl.when(take)
                def _(p=p, off=off):
                    do(_token_copy(zero_ref, 0, xs_ref, off, zsem, p))

                off = off + jnp.where(take, p, 0)
                p //= 2
            return c

        lax.fori_loop(0, N_EXPERTS, per_expert, 0)

        def tail(j, c):
            do(_token_copy(zero_ref, 0, xs_ref, j * bm, zsem, bm))
            return c

        lax.fori_loop(nact_ref[0], nblk, tail, 0)

    sweep(lambda cp: cp.start())
    sweep(lambda cp: cp.wait())


def _dispatch_body(pad_start_ref, pad_len_ref, nact_ref, slot_ref, xm_ref, xs_ref, zero_ref, sem, zsem):
    tm = slot_ref.shape[0] // TOP_K

    @pl.when(pl.program_id(0) == 0)
    def _():
        _zero_fill_pads(pad_start_ref, pad_len_ref, nact_ref, xs_ref, zero_ref, zsem)

    def issue(i, c):
        for k in range(TOP_K):
            _rows_copy(xm_ref, i * ROW_PITCH, xs_ref, slot_ref[i * TOP_K + k], sem).start(priority=k % 2)
        return c

    lax.fori_loop(0, tm, issue, 0, unroll=2)

    def drain(i, c):
        for _ in range(WAIT_UNROLL):
            _token_copy(xm_ref, 0, xs_ref, 0, sem).wait()
        return c

    lax.fori_loop(0, tm * TOP_K // WAIT_UNROLL, drain, 0)


def _dispatch(pad_start, pad_len, nact, slot_rows, xm, n_slots):
    t = xm.shape[0] // ROW_PITCH
    tm = min(TM_DISPATCH, t)
    grid_spec = pltpu.PrefetchScalarGridSpec(
        num_scalar_prefetch=3,
        grid=(t // tm,),
        in_specs=[pl.BlockSpec((tm * TOP_K,), lambda i, *_: (i,), memory_space=pltpu.SMEM),
                  pl.BlockSpec((tm * ROW_PITCH, LANES), lambda i, *_: (i, 0))],
        out_specs=pl.BlockSpec(memory_space=pl.ANY),
        scratch_shapes=[pltpu.VMEM((MOE_BM * ROW_PITCH, LANES), F32), pltpu.SemaphoreType.DMA(()),
                        pltpu.SemaphoreType.DMA(())],
    )
    return pl.pallas_call(
        _dispatch_body,
        grid_spec=grid_spec,
        out_shape=jax.ShapeDtypeStruct((n_slots * ROW_PITCH, LANES), F32),
        compiler_params=pltpu.CompilerParams(dimension_semantics=("arbitrary",), has_side_effects=True,
                                             vmem_limit_bytes=VMEM_LIMIT),
        name="dispatch",
    )(pad_start, pad_len, nact, slot_rows, xm)


STEP_UP, STEP_DOWN, STEP_TAIL, STEP_IDLE = 0, 1, 2, 3
PREFETCH_PER_DOWN = 2
NF = D_FF // MOE_TF
PASS_BLOCKS = MOE_RM // MOE_BM


def _swiglu(hg, hu):
    gate = jnp.minimum(hg, SWIGLU_LIMIT)
    up = jnp.clip(hu, -SWIGLU_LIMIT, SWIGLU_LIMIT)
    glu = gate / (1.0 + jnp.exp(-SWIGLU_ALPHA * gate))
    return (up + 1.0) * glu


def _experts_body(e_ref, f_ref, kind_ref, blk_ref, blk0_ref, nsub_ref, pre_ref, nxb_ref,
                  xs_ref, wg_ref, wu_ref, bg_ref, bu_ref, wd_ref, bd_ref,
                  y_ref, xb_ref, h_ref, wdb_ref, stg_ref, sem):
    t = pl.program_id(0)
    kind = kind_ref[t]
    f = f_ref[t]
    nsub = nsub_ref[t]

    def x_copy(first_blk, j, slot):
        rows = MOE_SUB * ROW_PITCH
        first = pl.multiple_of((first_blk * MOE_BM + j * MOE_SUB) * ROW_PITCH, SUBLANES)
        return pltpu.make_async_copy(xs_ref.at[pl.ds(first, rows)], stg_ref.at[slot], sem.at[slot])

    def unpack(j, slot):
        rows = pl.ds(pl.multiple_of(j * MOE_SUB, MOE_SUB), MOE_SUB)
        for jt, tile in enumerate(_load_token_rows(stg_ref.at[slot], MOE_SUB)):
            xb_ref[rows, jt * LANES:(jt + 1) * LANES] = tile.astype(BF16)

    @pl.when(kind == STEP_UP)
    def _up():
        pre = pre_ref[t]

        @pl.when((f == 0) & (pre < nsub))
        def _load_rows():
            blk0 = blk0_ref[t]
            x_copy(blk0, pre, pre % 2).start()

            def body(j, c):
                slot = j % 2

                @pl.when(j + 1 < nsub)
                def _():
                    x_copy(blk0, j + 1, 1 - slot).start()

                x_copy(blk0, j, slot).wait()
                unpack(j, slot)
                return c

            lax.fori_loop(pre, nsub, body, 0)

        wdb_ref[pl.ds(pl.multiple_of(f * MOE_TF, MOE_TF), MOE_TF), :] = wd_ref[0].astype(BF16)

        def up_rows(row0, nrows):
            x = xb_ref[pl.ds(row0, nrows), :]
            hg = jnp.dot(x, wg_ref[0].astype(BF16), preferred_element_type=F32) + bg_ref[0]
            hu = jnp.dot(x, wu_ref[0].astype(BF16), preferred_element_type=F32) + bu_ref[0]
            act = _swiglu(hg, hu).astype(BF16)
            h_ref[pl.ds(row0, nrows), pl.ds(pl.multiple_of(f * MOE_TF, MOE_TF), MOE_TF)] = act

        quad = 4 * MOE_SUB
        nquad = nsub // 4
        rem = nsub % 4

        def quad_rows(i, c):
            base = pl.multiple_of(i * quad, quad)
            up_rows(base, quad // 2)
            up_rows(base + quad // 2, quad // 2)
            return c

        merged = nsub == 5

        @pl.when(merged)
        def _():
            up_rows(0, 2 * MOE_SUB)
            up_rows(2 * MOE_SUB, 3 * MOE_SUB)

        @pl.when(jnp.logical_not(merged))
        def _():
            lax.fori_loop(0, nquad, quad_rows, 0)

            @pl.when(rem >= 2)
            def _():
                up_rows(pl.multiple_of(nquad * quad, quad), 2 * MOE_SUB)

            @pl.when(rem % 2 == 1)
            def _():
                up_rows(pl.multiple_of(nquad * quad + (rem // 2) * 2 * MOE_SUB, MOE_SUB), MOE_SUB)

    @pl.when(kind == STEP_DOWN)
    def _down():
        jblk = blk_ref[t] - blk0_ref[t]
        row0 = pl.multiple_of(jblk * MOE_BM, MOE_BM)
        nx_nsub = pre_ref[t]
        nx_blk0 = nxb_ref[t]
        ahead = [jblk * PREFETCH_PER_DOWN + u for u in range(PREFETCH_PER_DOWN)]
        for u, j in enumerate(ahead):
            @pl.when(j < nx_nsub)
            def _(u=u, j=j):
                x_copy(nx_blk0, j, u).start()

        def down_rows(nrows):
            hrows = h_ref[pl.ds(row0, nrows), :]
            return jnp.dot(hrows, wdb_ref[...], preferred_element_type=F32) + bd_ref[0]

        @pl.when(nsub == 2)
        def _():
            _store_token_rows(y_ref, 0, down_rows(MOE_BM))

        @pl.when(nsub == 1)
        def _():
            _store_token_rows(y_ref, 0, down_rows(MOE_SUB))
            y_ref[MOE_SUB * ROW_PITCH:, :] = jnp.zeros(((MOE_BM - MOE_SUB) * ROW_PITCH, LANES), F32)

        for u, j in enumerate(ahead):
            @pl.when(j < nx_nsub)
            def _(u=u, j=j):
                x_copy(nx_blk0, j, u).wait()
                unpack(j, u)

    @pl.when(kind == STEP_TAIL)
    def _tail():
        y_ref[...] = jnp.zeros_like(y_ref)


def _schedule_body(n_blocks, cnt_ref, e_ref, f_ref, kind_ref, blk_ref, blk0_ref, nsub_ref, pre_ref, nxb_ref,
                   start_ref, pad_start_ref, pad_len_ref, nact_ref):
    n_steps = e_ref.shape[0]
    sub_per_blk = MOE_BM // MOE_SUB
    zero = jnp.int32(0)

    def put(idx, e, f, kind, blk, blk0, nsub, pre):
        e_ref[idx] = e
        f_ref[idx] = f
        kind_ref[idx] = kind
        blk_ref[idx] = blk
        blk0_ref[idx] = blk0
        nsub_ref[idx] = nsub
        pre_ref[idx] = pre
        nxb_ref[idx] = zero

    def per_expert(e, carry):
        t, blk, last_e, prev_down, prev_nb = carry
        n = cnt_ref[e]
        nblk = (n + (MOE_BM - 1)) // MOE_BM
        start_ref[e] = blk * MOE_BM
        pad_start_ref[e] = blk * MOE_BM + n
        pad_len_ref[e] = nblk * MOE_BM - n

        def per_pass(p, carry2):
            t2, pdown, pnb = carry2
            b0 = blk + p * PASS_BLOCKS
            nb = jnp.minimum(nblk - p * PASS_BLOCKS, PASS_BLOCKS)
            nsub = (jnp.minimum(n - p * MOE_RM, MOE_RM) + (MOE_SUB - 1)) // MOE_SUB
            pre = jnp.minimum(nsub, pnb * PREFETCH_PER_DOWN)
            for f in range(NF):
                put(t2 + f, e, f, STEP_UP, b0, b0, nsub, pre)

            def patch(j, c):
                pre_ref[pdown + j] = nsub
                nxb_ref[pdown + j] = b0
                return c

            lax.fori_loop(0, pnb, patch, 0)

            def per_blk(j, c):
                put(t2 + NF + j, e, NF - 1, STEP_DOWN, b0 + j, b0,
                    jnp.clip(nsub - j * sub_per_blk, 1, sub_per_blk), zero)
                return c

            lax.fori_loop(0, nb, per_blk, 0)
            return t2 + NF + nb, t2 + NF, nb

        t, prev_down, prev_nb = lax.fori_loop(0, (nblk + (PASS_BLOCKS - 1)) // PASS_BLOCKS, per_pass,
                                              (t, prev_down, prev_nb))
        return t, blk + nblk, jnp.where(n > 0, e, last_e), prev_down, prev_nb

    t, nact, last_e, _, _ = lax.fori_loop(0, N_EXPERTS, per_expert, (zero, zero, zero, zero, zero))
    nact_ref[0] = nact
    nact_ref[1] = t + n_blocks - nact

    def spare(i, c):
        is_tail = i < n_blocks - nact
        blk = jnp.where(is_tail, nact + i, n_blocks - 1)
        put(t + i, last_e, NF - 1, jnp.where(is_tail, STEP_TAIL, STEP_IDLE), blk, blk, 1, zero)
        return c

    lax.fori_loop(0, n_steps - t, spare, 0)


def _expert_schedule(counts, n_blocks, n_assign):
    n_pass_max = N_EXPERTS + n_assign // MOE_RM
    n_steps = n_pass_max * NF + n_blocks
    smem = pl.BlockSpec(memory_space=pltpu.SMEM)
    i32 = jnp.int32
    out_shape = tuple(jax.ShapeDtypeStruct((n,), i32) for n in (n_steps,) * 8 + (N_EXPERTS,) * 3 + (2,))
    outs = pl.pallas_call(
        functools.partial(_schedule_body, n_blocks),
        in_specs=[smem],
        out_specs=tuple(smem for _ in out_shape),
        out_shape=out_shape,
        name="schedule",
    )(counts)
    return outs[:8], outs[8], outs[9], outs[10], outs[11]


def _experts(tabs, xs, w_up, b_up, w_down, b_down, name):
    d = D_MODEL
    n_steps = tabs[0].shape[0]
    grid_spec = pltpu.PrefetchScalarGridSpec(
        num_scalar_prefetch=len(tabs),
        grid=(n_steps,),
        in_specs=[
            pl.BlockSpec(memory_space=pl.ANY),
            pl.BlockSpec((1, d, MOE_TF), lambda t, e, f, *_: (e[t], 0, f[t])),
            pl.BlockSpec((1, d, MOE_TF), lambda t, e, f, *_: (e[t], 0, NF + f[t])),
            pl.BlockSpec((1, 1, MOE_TF), lambda t, e, f, *_: (e[t], 0, f[t])),
            pl.BlockSpec((1, 1, MOE_TF), lambda t, e, f, *_: (e[t], 0, NF + f[t])),
            pl.BlockSpec((1, MOE_TF, d), lambda t, e, f, *_: (e[t], f[t], 0)),
            pl.BlockSpec((1, 1, d), lambda t, e, f, *_: (e[t], 0, 0)),
        ],
        out_specs=pl.BlockSpec((MOE_BM * ROW_PITCH, LANES), lambda t, e, f, kind, blk, *_: (blk[t], 0)),
        scratch_shapes=[
            pltpu.VMEM((MOE_RM, d), BF16),
            pltpu.VMEM((MOE_RM, D_FF), BF16),
            pltpu.VMEM((D_FF, d), BF16),
            pltpu.VMEM((2, MOE_SUB * ROW_PITCH, LANES), F32),
            pltpu.SemaphoreType.DMA((2,)),
        ],
    )
    return pl.pallas_call(
        _experts_body,
        grid_spec=grid_spec,
        out_shape=jax.ShapeDtypeStruct(xs.shape, F32),
        compiler_params=pltpu.CompilerParams(dimension_semantics=("arbitrary",),
                                             vmem_limit_bytes=VMEM_LIMIT),
        name=name,
    )(*tabs, xs, w_up, w_up, b_up, b_up, w_down, b_down)


def _combine_body(slot_ref, slot_nx_ref, gate_ref, x1_ref, gf_ref, ys_ref, o_ref, buf_ref, sem):
    step = pl.program_id(0)
    slot = step % 2
    tm = x1_ref.shape[0]

    def gather(rows_ref, s):
        def issue(i, c):
            for k in range(TOP_K):
                _rows_copy(ys_ref, rows_ref[i * TOP_K + k], buf_ref.at[s, k], i * ROW_PITCH,
                           sem.at[s]).start(priority=k % 2)
            return c

        lax.fori_loop(0, tm, issue, 0, unroll=2)

    @pl.when(step == 0)
    def _():
        gather(slot_ref, 0)

    for s in range(2):
        @pl.when((step + 1 < pl.num_programs(0)) & (slot == 1 - s))
        def _(s=s):
            gather(slot_nx_ref, s)

    def drain(i, c):
        for _ in range(WAIT_UNROLL):
            _token_copy(ys_ref, 0, buf_ref.at[slot, 0], 0, sem.at[slot]).wait()
        return c

    lax.fori_loop(0, tm * TOP_K // WAIT_UNROLL, drain, 0)
    gate = gate_ref[...]
    tiles = [x1_ref[:, j * LANES:(j + 1) * LANES] for j in range(ROW_TILES)]
    for k in range(TOP_K):
        g = gate[:, k:k + 1]
        for j, tile in enumerate(_load_token_rows(buf_ref.at[slot, k], tm)):
            tiles[j] = tiles[j] + g * tile
    o_ref[...] = _rms(jnp.concatenate(tiles, axis=1), gf_ref[...])


def _combine(slot_rows, gates, x1, gf, ys):
    t, d = x1.shape
    tm = min(TM_COMBINE, t)
    n = t // tm
    idx_blk = (tm * TOP_K,)
    return pl.pallas_call(
        _combine_body,
        grid=(n,),
        in_specs=[pl.BlockSpec(idx_blk, lambda i: (i,), memory_space=pltpu.SMEM),
                  pl.BlockSpec(idx_blk, lambda i: (jnp.minimum(i + 1, n - 1),), memory_space=pltpu.SMEM),
                  pl.BlockSpec((tm, TOP_K), lambda i: (i, 0)),
                  pl.BlockSpec((tm, d), lambda i: (i, 0)),
                  _const_spec(gf.shape),
                  pl.BlockSpec(memory_space=pl.ANY)],
        out_specs=pl.BlockSpec((tm, d), lambda i: (i, 0)),
        scratch_shapes=[pltpu.VMEM((2, TOP_K, tm * ROW_PITCH, LANES), F32), pltpu.SemaphoreType.DMA((2,))],
        out_shape=jax.ShapeDtypeStruct((t, d), F32),
        compiler_params=pltpu.CompilerParams(dimension_semantics=("arbitrary",),
                                             vmem_limit_bytes=VMEM_LIMIT),
        name="combine",
    )(slot_rows, slot_rows, gates, x1, gf, ys)


def _rope_tables(s):
    pos = np.arange(s, dtype=np.float32)
    inv_freq = np.power(np.float32(ROPE_THETA), -np.arange(0, B_ROPE_DIM, 2, dtype=np.float32) / B_ROPE_DIM)
    ang = (pos[:, None] * inv_freq[None, :]).astype(np.float32)
    cos, sin = np.cos(ang), np.sin(ang)
    pad = LANES - B_ROPE_DIM
    cos = np.concatenate([cos, cos, np.ones((s, pad), np.float32)], axis=1)
    sin = np.concatenate([sin, sin, np.zeros((s, pad), np.float32)], axis=1)
    return cos.astype(np.float32), sin.astype(np.float32)


def _window_bias():
    qi = np.arange(A_BLOCK)[:, None]
    kj = np.arange(3 * A_BLOCK)[None, :]
    dist = np.abs(qi + A_BLOCK - kj)
    slopes = np.power(np.float32(2.0), -8.0 * np.arange(1, A_HEADS + 1, dtype=np.float32) / A_HEADS)
    bias = -slopes[:, None, None] * dist.astype(np.float32)[None]
    bias = np.where((dist <= WINDOW)[None], bias, -np.inf)
    bias = bias.reshape(A_KV_HEADS, 2, 2, A_BLOCK, 3 * A_BLOCK).transpose(0, 2, 1, 3, 4)
    bias = bias.reshape(2 * A_KV_HEADS, 2 * A_BLOCK, 3 * A_BLOCK)
    no_prev = (kj < A_BLOCK)[None]
    no_next = (kj >= 2 * A_BLOCK)[None]
    ninf = -np.inf
    return np.stack([bias, np.where(no_prev, ninf, bias), np.where(no_next, ninf, bias),
                     np.where(no_prev | no_next, ninf, bias)]).astype(np.float32)


def _layer(x, attn_norm, w_in, a_sink, b_q_norm, b_w_uq, b_kv_norm, b_w_ukv, out_norm_a, out_norm_b, w_o,
           mlp_norm, w_router, b_router, w_up, b_up, w_down, b_down):
    b, s, d = x.shape
    t = b * s
    wq = b_w_uq.reshape(B_Q_RANK, B_HEADS, B_NOPE_DIM + B_ROPE_DIM)
    wq_pe = jnp.pad(wq[:, :, B_NOPE_DIM:], ((0, 0), (0, 0), (0, LANES - B_ROPE_DIM)))
    wuq = jnp.concatenate([wq[:, :, :B_NOPE_DIM].reshape(B_Q_RANK, -1), wq_pe.reshape(B_Q_RANK, -1)],
                          axis=1).astype(BF16)
    wkv = b_w_ukv.reshape(B_KV_RANK, B_HEADS, B_NOPE_DIM + B_V_DIM)
    wukv = jnp.concatenate([wkv[:, :, :B_NOPE_DIM].reshape(B_KV_RANK, -1),
                            wkv[:, :, B_NOPE_DIM:].reshape(B_KV_RANK, -1)], axis=1).astype(BF16)
    cos, sin = _rope_tables(s)

    qa, ka, va, qb, kb, vb = _prologue(x, attn_norm[None], w_in, b_q_norm[None], wuq, b_kv_norm[None], wukv,
                                       cos, sin)
    ya = _window_attn(qa, ka, va, a_sink, _window_bias())
    yb = _mla_attn(qb, kb, vb)
    wr_hi = w_router.astype(BF16)
    wr_lo = (w_router - wr_hi.astype(F32)).astype(BF16)
    x1, xm, eidx, gates, rank, counts = _out_router(
        ya.reshape(t, A_WIDTH), yb.reshape(t, B_WIDTH), x.reshape(t, d), out_norm_a[None], out_norm_b[None],
        w_o.astype(BF16), mlp_norm[None], jnp.concatenate([wr_hi, wr_lo], axis=1), b_router[None])

    nblk = (t * TOP_K + N_EXPERTS * (MOE_BM - 1) + MOE_BM - 1) // MOE_BM
    tabs, start, pad_start, pad_len, nact = _expert_schedule(counts.reshape(N_EXPERTS), nblk, t * TOP_K)
    slot_rows = _slot_rows(start, eidx.reshape(t * TOP_K), rank.reshape(t * TOP_K))
    xs = _dispatch(pad_start, pad_len, nact, slot_rows, xm, nblk * MOE_BM)
    n_short = N_EXPERTS * NF + nblk
    args = (xs, w_up, b_up[:, None, :], w_down, b_down[:, None, :])
    ys = lax.cond(nact[1] <= n_short,
                  lambda: _experts(tuple(a[:n_short] for a in tabs), *args, name="experts"),
                  lambda: _experts(tabs, *args, name="experts_long"))
    return x1, slot_rows, gates, ys


def kernel(x, attn_norm, w_in, a_sink, b_q_norm, b_w_uq, b_kv_norm, b_w_ukv, out_norm_a, out_norm_b, w_o,
           mlp_norm, w_router, b_router, w_up, b_up, w_down, b_down, final_norm):
    b, s, d = x.shape
    assert d == D_MODEL and s % min(TQ_MLA, s) == 0 and s % min(CK_MLA, s) == 0 and s % TM_PRO == 0 and attn_norm.shape[0] == 1
    x1, slot_rows, gates, ys = _layer(
        x, attn_norm[0], w_in[0], a_sink[0], b_q_norm[0], b_w_uq[0], b_kv_norm[0], b_w_ukv[0], out_norm_a[0],
        out_norm_b[0], w_o[0], mlp_norm[0], w_router[0], b_router[0], w_up[0], b_up[0], w_down[0], b_down[0])
    out = _combine(slot_rows, gates, x1, final_norm[None], ys)
    return out.reshape(b, s, d)
```

```python
import functools

import jax
import numpy as np
import jax.numpy as jnp
from jax import lax
from jax.experimental import pallas as pl
from jax.experimental.pallas import tpu as pltpu

D_MODEL = 2048
A_HEADS, A_KV_HEADS, A_HEAD_DIM = 16, 4, 64
A_GROUP = A_HEADS // A_KV_HEADS
WINDOW = 128
A_BLOCK = 128
B_HEADS, B_Q_RANK, B_KV_RANK = 8, 512, 256
B_NOPE_DIM, B_ROPE_DIM, B_V_DIM = 128, 64, 128
ROPE_THETA = 10000.0
A_WIDTH = A_HEADS * A_HEAD_DIM
B_WIDTH = B_HEADS * B_V_DIM
A_KV_COLS = A_KV_HEADS * A_HEAD_DIM
N_EXPERTS, TOP_K, D_FF = 32, 4, 2048
SWIGLU_ALPHA, SWIGLU_LIMIT = 1.702, 7.0
EPS = 1e-5

LANES = 128
SUBLANES = 8
B_QK_PAD = 2 * LANES
IN_COLS_PAD = A_WIDTH + 2 * A_KV_COLS + B_Q_RANK + B_KV_RANK + LANES
VMEM_LIMIT = 56 * 1024 * 1024
ROW_TILES = D_MODEL // LANES
ROW_PITCH = ROW_TILES + 1

TM_PRO = 256
TQ_MLA = 1024
CK_MLA = 2048
HEADS_MLA = 2
TM_OUT = 512
TM_DISPATCH = 1024
TM_COMBINE = 256
MOE_BM = 512
MOE_SUB = 256
MOE_RM = 2048
MOE_TF = 256
WAIT_UNROLL = 16

BF16 = jnp.bfloat16
F32 = jnp.float32


def _rms(x, g):
    return x * lax.rsqrt(jnp.mean(x * x, axis=-1, keepdims=True) + EPS) * g


def _const_spec(shape):
    nd = len(shape)
    return pl.BlockSpec(shape, lambda *_: (0,) * nd, pipeline_mode=pl.Buffered(1))


def _store_token_rows(ref, first, val):
    n = val.shape[0]
    for j in range(ROW_TILES):
        ref[pl.ds(first * ROW_PITCH + j, n, stride=ROW_PITCH), :] = val[:, j * LANES:(j + 1) * LANES]
    ref[pl.ds(first * ROW_PITCH + ROW_TILES, n, stride=ROW_PITCH), :] = jnp.zeros((n, LANES), val.dtype)


def _load_token_rows(ref, n):
    return [ref[pl.ds(j, n, stride=ROW_PITCH), :] for j in range(ROW_TILES)]


def _rope(x, cos, sin):
    lane = lax.broadcasted_iota(jnp.int32, x.shape, 1)
    up = pltpu.roll(x, LANES - B_ROPE_DIM // 2, 1)
    dn = pltpu.roll(x, B_ROPE_DIM // 2, 1)
    sw = jnp.where(lane < B_ROPE_DIM // 2, -up, jnp.where(lane < B_ROPE_DIM, dn, 0.0))
    return x * cos + sw * sin


def _prologue_body(x_ref, g_ref, win_ref, qn_ref, wuq_ref, kvn_ref, wukv_ref, cos_ref, sin_ref,
                   qa_ref, ka_ref, va_ref, qb_ref, kb_ref, vb_ref, winb_ref):
    @pl.when((pl.program_id(0) == 0) & (pl.program_id(1) == 0))
    def _():
        n_in = win_ref.shape[1]
        winb_ref[:, :n_in] = win_ref[...].astype(BF16)
        winb_ref[:, n_in:] = jnp.zeros((winb_ref.shape[0], IN_COLS_PAD - n_in), BF16)

    x = x_ref[0]
    xn = _rms(x, g_ref[...]).astype(BF16)
    proj = jnp.dot(xn, winb_ref[...], preferred_element_type=F32)
    c0 = A_WIDTH
    c1 = c0 + A_KV_COLS
    c2 = c1 + A_KV_COLS
    c3 = c2 + B_Q_RANK
    c4 = c3 + B_KV_RANK
    qa_ref[0] = (proj[:, :A_WIDTH] * A_HEAD_DIM ** -0.5).astype(BF16)
    low = lax.broadcasted_iota(jnp.int32, (proj.shape[0], LANES), 1) < A_HEAD_DIM
    for src, dst in ((c0, ka_ref), (c1, va_ref)):
        for pair in range(A_KV_HEADS // 2):
            two = proj[:, src + pair * LANES:src + (pair + 1) * LANES]
            swapped = pltpu.roll(two, A_HEAD_DIM, 1)
            dst[0, 4 * pair + 0] = jnp.where(low, two, 0.0).astype(BF16)
            dst[0, 4 * pair + 1] = jnp.where(low, 0.0, swapped).astype(BF16)
            dst[0, 4 * pair + 2] = jnp.where(low, swapped, 0.0).astype(BF16)
            dst[0, 4 * pair + 3] = jnp.where(low, 0.0, two).astype(BF16)
    cq = _rms(proj[:, c2:c3], qn_ref[...]).astype(BF16)
    q = jnp.dot(cq, wuq_ref[...], preferred_element_type=F32)
    ckv = _rms(proj[:, c3:c4], kvn_ref[...]).astype(BF16)
    kv = jnp.dot(ckv, wukv_ref[...], preferred_element_type=F32)
    cos = cos_ref[...]
    sin = sin_ref[...]
    kpe = _rope(proj[:, c4:c4 + LANES], cos, sin).astype(BF16)
    b_scale = (B_NOPE_DIM + B_ROPE_DIM) ** -0.5
    hw = B_HEADS * LANES
    for h in range(B_HEADS):
        sl = slice(h * LANES, (h + 1) * LANES)
        qb_ref[0, h, :, :LANES] = (q[:, sl] * b_scale).astype(BF16)
        qpe = _rope(q[:, hw + h * LANES:hw + (h + 1) * LANES], cos, sin)
        qb_ref[0, h, :, LANES:] = (qpe * b_scale).astype(BF16)
        kb_ref[0, h, :, :LANES] = kv[:, sl].astype(BF16)
        kb_ref[0, h, :, LANES:] = kpe
        vb_ref[0, h] = kv[:, hw + h * LANES:hw + (h + 1) * LANES].astype(BF16)


def _prologue(x, g, win, qn, wuq, kvn, wukv, cos, sin):
    b, s, d = x.shape
    tm = TM_PRO
    grid = (b, s // tm)
    sds = jax.ShapeDtypeStruct
    out_shape = (
        sds((b, s, A_WIDTH), BF16),
        sds((b, 2 * A_KV_HEADS, s, LANES), BF16),
        sds((b, 2 * A_KV_HEADS, s, LANES), BF16),
        sds((b, B_HEADS, s, B_QK_PAD), BF16),
        sds((b, B_HEADS, s, B_QK_PAD), BF16),
        sds((b, B_HEADS, s, B_V_DIM), BF16),
    )

    def hspec(nh, w):
        return pl.BlockSpec((1, nh, tm, w), lambda bi, i: (bi, 0, i, 0))

    return pl.pallas_call(
        _prologue_body,
        grid=grid,
        in_specs=[
            pl.BlockSpec((1, tm, d), lambda bi, i: (bi, i, 0)),
            _const_spec(g.shape), _const_spec(win.shape), _const_spec(qn.shape), _const_spec(wuq.shape),
            _const_spec(kvn.shape), _const_spec(wukv.shape),
            pl.BlockSpec((tm, LANES), lambda bi, i: (i, 0)),
            pl.BlockSpec((tm, LANES), lambda bi, i: (i, 0)),
        ],
        out_specs=(pl.BlockSpec((1, tm, A_WIDTH), lambda bi, i: (bi, i, 0)),
                   hspec(2 * A_KV_HEADS, LANES), hspec(2 * A_KV_HEADS, LANES),
                   hspec(B_HEADS, B_QK_PAD), hspec(B_HEADS, B_QK_PAD), hspec(B_HEADS, B_V_DIM)),
        out_shape=out_shape,
        scratch_shapes=[pltpu.VMEM((d, IN_COLS_PAD), BF16)],
        compiler_params=pltpu.CompilerParams(dimension_semantics=("arbitrary", "arbitrary"),
                                             vmem_limit_bytes=VMEM_LIMIT),
        name="prologue",
    )(x, g, win, qn, wuq, kvn, wukv, cos, sin)


def _window_body(sink_ref, q_ref, kp_ref, kc_ref, kn_ref, vp_ref, vc_ref, vn_ref, bias_ref, o_ref):
    two = 2 * A_BLOCK
    low = lax.broadcasted_iota(jnp.int32, (two, LANES), 1) < A_HEAD_DIM
    for hk in range(A_KV_HEADS):
        q2 = jnp.concatenate([q_ref[0, :, (2 * hk) * LANES:(2 * hk + 1) * LANES],
                              q_ref[0, :, (2 * hk + 1) * LANES:(2 * hk + 2) * LANES]], axis=0)
        acc = jnp.zeros((two, LANES), F32)
        inv = []
        for half in range(2):
            z = 2 * hk + half
            kband = jnp.concatenate([kp_ref[0, z], kc_ref[0, z], kn_ref[0, z]], axis=0)
            vband = jnp.concatenate([vp_ref[0, z], vc_ref[0, z], vn_ref[0, z]], axis=0)
            s = lax.dot_general(q2, kband, (((1,), (1,)), ((), ())), preferred_element_type=F32)
            s = s + bias_ref[0, z]
            sink = jnp.concatenate(
                [jnp.full((A_BLOCK, 1), sink_ref[hk * A_GROUP + 2 * jj + half], F32) for jj in range(2)], axis=0)
            m = jnp.maximum(jnp.max(s, axis=-1, keepdims=True), sink)
            p = jnp.exp(s - m)
            den = jnp.sum(p, axis=-1, keepdims=True) + jnp.exp(sink - m)
            inv.append(1.0 / den)
            acc = acc + jnp.dot(p.astype(BF16), vband, preferred_element_type=F32)
        o = (acc * jnp.where(low, inv[0], inv[1])).astype(BF16)
        o_ref[0, :, (2 * hk) * LANES:(2 * hk + 1) * LANES] = o[:A_BLOCK]
        o_ref[0, :, (2 * hk + 1) * LANES:(2 * hk + 2) * LANES] = o[A_BLOCK:]


def _window_attn(qa, ka, va, sink, bias):
    b, s, _ = qa.shape
    nb = s // A_BLOCK
    kv_blk = (1, 2 * A_KV_HEADS, A_BLOCK, LANES)
    prev = pl.BlockSpec(kv_blk, lambda bi, n: (bi, 0, jnp.maximum(n - 1, 0), 0))
    cur = pl.BlockSpec(kv_blk, lambda bi, n: (bi, 0, n, 0))
    nxt = pl.BlockSpec(kv_blk, lambda bi, n: (bi, 0, jnp.minimum(n + 1, nb - 1), 0))

    def edge(bi, n):
        return ((n == 0).astype(jnp.int32) + 2 * (n == nb - 1).astype(jnp.int32), 0, 0, 0)

    return pl.pallas_call(
        _window_body,
        grid=(b, nb),
        in_specs=[
            pl.BlockSpec(memory_space=pltpu.SMEM),
            pl.BlockSpec((1, A_BLOCK, A_WIDTH), lambda bi, n: (bi, n, 0)),
            prev, cur, nxt, prev, cur, nxt,
            pl.BlockSpec((1,) + bias.shape[1:], edge),
        ],
        out_specs=pl.BlockSpec((1, A_BLOCK, A_WIDTH), lambda bi, n: (bi, n, 0)),
        out_shape=jax.ShapeDtypeStruct((b, s, A_WIDTH), BF16),
        compiler_params=pltpu.CompilerParams(dimension_semantics=("parallel", "parallel"),
                                             vmem_limit_bytes=VMEM_LIMIT),
        name="window_attn",
    )(sink, qa, ka, ka, ka, va, va, va, bias)


def _mla_body(q_ref, k_ref, v_ref, o_ref):
    tq = q_ref.shape[2]
    s_len = k_ref.shape[2]
    ck = min(CK_MLA, s_len)
    for hh in range(HEADS_MLA):
        q = q_ref[0, hh]
        m = jnp.full((tq, 1), -jnp.inf, F32)
        l = jnp.zeros((tq, 1), F32)
        acc = jnp.zeros((tq, B_V_DIM), F32)
        for c in range(s_len // ck):
            k_c = k_ref[0, hh, c * ck:(c + 1) * ck, :]
            v_c = v_ref[0, hh, c * ck:(c + 1) * ck, :]
            s = lax.dot_general(q, k_c, (((1,), (1,)), ((), ())), preferred_element_type=F32)
            m_new = jnp.maximum(m, jnp.max(s, axis=-1, keepdims=True))
            alpha = jnp.exp(m - m_new)
            p = jnp.exp(s - m_new)
            l = alpha * l + jnp.sum(p, axis=-1, keepdims=True)
            acc = alpha * acc + jnp.dot(p.astype(BF16), v_c, preferred_element_type=F32)
            m = m_new
        o_ref[0, :, hh * B_V_DIM:(hh + 1) * B_V_DIM] = (acc / l).astype(BF16)


def _mla_attn(qb, kb, vb):
    b, nh, s, _ = qb.shape
    tq = min(TQ_MLA, s)
    hs = HEADS_MLA
    return pl.pallas_call(
        _mla_body,
        grid=(b, nh // hs, s // tq),
        in_specs=[
            pl.BlockSpec((1, hs, tq, B_QK_PAD), lambda bi, h, i: (bi, h, i, 0)),
            pl.BlockSpec((1, hs, s, B_QK_PAD), lambda bi, h, i: (bi, h, 0, 0)),
            pl.BlockSpec((1, hs, s, B_V_DIM), lambda bi, h, i: (bi, h, 0, 0)),
        ],
        out_specs=pl.BlockSpec((1, tq, hs * B_V_DIM), lambda bi, h, i: (bi, i, h)),
        out_shape=jax.ShapeDtypeStruct((b, s, B_WIDTH), BF16),
        compiler_params=pltpu.CompilerParams(dimension_semantics=("parallel", "parallel", "parallel"),
                                             vmem_limit_bytes=VMEM_LIMIT),
        name="mla_attn",
    )(qb, kb, vb)


def _out_router_body(ya_ref, yb_ref, x_ref, ga_ref, gb_ref, wo_ref, gm_ref, wr_ref, br_ref,
                     x1_ref, xm_ref, eidx_ref, gate_ref, rank_ref, cnt_ref, run_ref):
    i = pl.program_id(0)

    @pl.when(i == 0)
    def _():
        run_ref[...] = jnp.zeros_like(run_ref)

    na = _rms(ya_ref[...].astype(F32), ga_ref[...]).astype(BF16)
    nb = _rms(yb_ref[...].astype(F32), gb_ref[...]).astype(BF16)
    att = jnp.dot(na, wo_ref[:A_WIDTH, :], preferred_element_type=F32)
    att = att + jnp.dot(nb, wo_ref[A_WIDTH:, :], preferred_element_type=F32)
    x1 = x_ref[...] + att
    x1_ref[...] = x1
    hn = _rms(x1, gm_ref[...])
    _store_token_rows(xm_ref, 0, hn)
    tm = hn.shape[0]
    hi = hn.astype(BF16)
    lo = (hn - hi.astype(F32)).astype(BF16)
    prod = jnp.dot(jnp.concatenate([hi, lo], axis=0), wr_ref[...], preferred_element_type=F32)
    logits = (prod[:tm, :N_EXPERTS] + prod[:tm, N_EXPERTS:] + prod[tm:, :N_EXPERTS] + prod[tm:, N_EXPERTS:]
              + br_ref[...])
    lane = lax.broadcasted_iota(jnp.int32, (tm, N_EXPERTS), 1)
    work = logits
    sel = jnp.zeros((tm, N_EXPERTS), F32)
    hots, vals, idxs = [], [], []
    for _k in range(TOP_K):
        mx = jnp.max(work, axis=-1, keepdims=True)
        idx = jnp.min(jnp.where(work == mx, lane, N_EXPERTS), axis=-1, keepdims=True)
        hot = lane == idx
        hots.append(hot)
        vals.append(mx)
        idxs.append(idx)
        sel = sel + hot.astype(F32)
        work = jnp.where(hot, -jnp.inf, work)
    exps = [jnp.exp(v - vals[0]) for v in vals]
    den = exps[0] + exps[1] + exps[2] + exps[3]
    r_i = lax.broadcasted_iota(jnp.int32, (tm, tm), 0)
    c_i = lax.broadcasted_iota(jnp.int32, (tm, tm), 1)
    tri = (c_i < r_i).astype(BF16)
    before = jnp.dot(tri, sel.astype(BF16), preferred_element_type=F32) + run_ref[...]
    lane4 = lax.broadcasted_iota(jnp.int32, (tm, TOP_K), 1)
    eidx = jnp.zeros((tm, TOP_K), jnp.int32)
    gate = jnp.zeros((tm, TOP_K), F32)
    rank = jnp.zeros((tm, TOP_K), jnp.int32)
    for k in range(TOP_K):
        rk = jnp.sum(jnp.where(hots[k], before, 0.0), axis=-1, keepdims=True).astype(jnp.int32)
        eidx = jnp.where(lane4 == k, idxs[k], eidx)
        gate = jnp.where(lane4 == k, exps[k] / den, gate)
        rank = jnp.where(lane4 == k, rk, rank)
    eidx_ref[...] = eidx
    gate_ref[...] = gate
    rank_ref[...] = rank
    run = run_ref[...] + jnp.sum(sel, axis=0, keepdims=True)
    run_ref[...] = run
    cnt_ref[...] = run.astype(jnp.int32)


def _out_router(ya, yb, x2d, ga, gb, wo, gm, wr, br):
    t, d = x2d.shape
    tm = min(TM_OUT, t)
    sds = jax.ShapeDtypeStruct

    def row(w):
        return pl.BlockSpec((tm, w), lambda i: (i, 0))

    return pl.pallas_call(
        _out_router_body,
        grid=(t // tm,),
        in_specs=[row(A_WIDTH), row(B_WIDTH), row(d), _const_spec(ga.shape), _const_spec(gb.shape),
                  _const_spec(wo.shape), _const_spec(gm.shape), _const_spec(wr.shape), _const_spec(br.shape)],
        out_specs=(row(d), pl.BlockSpec((tm * ROW_PITCH, LANES), lambda i: (i, 0)),
                   row(TOP_K), row(TOP_K), row(TOP_K),
                   pl.BlockSpec((1, N_EXPERTS), lambda i: (0, 0))),
        out_shape=(sds((t, d), F32), sds((t * ROW_PITCH, LANES), F32), sds((t, TOP_K), jnp.int32),
                   sds((t, TOP_K), F32),
                   sds((t, TOP_K), jnp.int32), sds((1, N_EXPERTS), jnp.int32)),
        scratch_shapes=[pltpu.VMEM((1, N_EXPERTS), F32)],
        compiler_params=pltpu.CompilerParams(dimension_semantics=("arbitrary",),
                                             vmem_limit_bytes=VMEM_LIMIT),
        name="out_router",
    )(ya, yb, x2d, ga, gb, wo, gm, wr, br)


def _token_copy(src_ref, src_tok, dst_ref, dst_tok, sem, ntok=1):
    return _rows_copy(src_ref, src_tok * ROW_PITCH, dst_ref, dst_tok * ROW_PITCH, sem, ntok)


def _rows_copy(src_ref, src_row, dst_ref, dst_row, sem, ntok=1):
    rows = ntok * ROW_PITCH
    return pltpu.make_async_copy(src_ref.at[pl.ds(src_row, rows)], dst_ref.at[pl.ds(dst_row, rows)], sem)


def _slot_rows_body(start_ref, eidx_ref, rank_ref, o_ref):
    e = eidx_ref[...]
    base = jnp.zeros_like(e)
    for ex in range(N_EXPERTS):
        base = jnp.where(e == ex, start_ref[ex], base)
    o_ref[...] = (base + rank_ref[...]) * ROW_PITCH


def _slot_rows(start, eidx, rank):
    n = eidx.shape[0]
    shape2 = (n // LANES, LANES)
    out = pl.pallas_call(
        _slot_rows_body,
        in_specs=[pl.BlockSpec(memory_space=pltpu.SMEM), pl.BlockSpec(shape2, lambda: (0, 0)),
                  pl.BlockSpec(shape2, lambda: (0, 0))],
        out_specs=pl.BlockSpec(shape2, lambda: (0, 0)),
        out_shape=jax.ShapeDtypeStruct(shape2, jnp.int32),
        name="slot_rows",
    )(start, eidx.reshape(shape2), rank.reshape(shape2))
    return out.reshape(n)


def _zero_fill_pads(pad_start_ref, pad_len_ref, nact_ref, xs_ref, zero_ref, zsem):
    bm = zero_ref.shape[0] // ROW_PITCH
    nblk = xs_ref.shape[0] // (bm * ROW_PITCH)
    zero_ref[...] = jnp.zeros_like(zero_ref)

    def sweep(do):
        def per_expert(e, c):
            off = pad_start_ref[e]
            n = pad_len_ref[e]
            p = bm // 2
            while p >= 1:
                take = (n & p) != 0

                @pl.when(take)
                def _(p=p, off=off):
                    do(_token_copy(zero_ref, 0, xs_ref, off, zsem, p))

                off = off + jnp.where(take, p, 0)
                p //= 2
            return c

        lax.fori_loop(0, N_EXPERTS, per_expert, 0)

        def tail(j, c):
            do(_token_copy(zero_ref, 0, xs_ref, j * bm, zsem, bm))
            return c

        lax.fori_loop(nact_ref[0], nblk, tail, 0)

    sweep(lambda cp: cp.start())
    sweep(lambda cp: cp.wait())


def _dispatch_body(pad_start_ref, pad_len_ref, nact_ref, slot_ref, xm_ref, xs_ref, zero_ref, sem, zsem):
    tm = slot_ref.shape[0] // TOP_K

    @pl.when(pl.program_id(0) == 0)
    def _():
        _zero_fill_pads(pad_start_ref, pad_len_ref, nact_ref, xs_ref, zero_ref, zsem)

    def issue(i, c):
        for k in range(TOP_K):
            _rows_copy(xm_ref, i * ROW_PITCH, xs_ref, slot_ref[i * TOP_K + k], sem).start(priority=k % 2)
        return c

    lax.fori_loop(0, tm, issue, 0, unroll=2)

    def drain(i, c):
        for _ in range(WAIT_UNROLL):
            _token_copy(xm_ref, 0, xs_ref, 0, sem).wait()
        return c

    lax.fori_loop(0, tm * TOP_K // WAIT_UNROLL, drain, 0)


def _dispatch(pad_start, pad_len, nact, slot_rows, xm, n_slots):
    t = xm.shape[0] // ROW_PITCH
    tm = min(TM_DISPATCH, t)
    grid_spec = pltpu.PrefetchScalarGridSpec(
        num_scalar_prefetch=3,
        grid=(t // tm,),
        in_specs=[pl.BlockSpec((tm * TOP_K,), lambda i, *_: (i,), memory_space=pltpu.SMEM),
                  pl.BlockSpec((tm * ROW_PITCH, LANES), lambda i, *_: (i, 0))],
        out_specs=pl.BlockSpec(memory_space=pl.ANY),
        scratch_shapes=[pltpu.VMEM((MOE_BM * ROW_PITCH, LANES), F32), pltpu.SemaphoreType.DMA(()),
                        pltpu.SemaphoreType.DMA(())],
    )
    return pl.pallas_call(
        _dispatch_body,
        grid_spec=grid_spec,
        out_shape=jax.ShapeDtypeStruct((n_slots * ROW_PITCH, LANES), F32),
        compiler_params=pltpu.CompilerParams(dimension_semantics=("arbitrary",), has_side_effects=True,
                                             vmem_limit_bytes=VMEM_LIMIT),
        name="dispatch",
    )(pad_start, pad_len, nact, slot_rows, xm)


STEP_UP, STEP_DOWN, STEP_TAIL, STEP_IDLE = 0, 1, 2, 3
PREFETCH_PER_DOWN = 2
NF = D_FF // MOE_TF
PASS_BLOCKS = MOE_RM // MOE_BM


def _swiglu(hg, hu):
    gate = jnp.minimum(hg, SWIGLU_LIMIT)
    up = jnp.clip(hu, -SWIGLU_LIMIT, SWIGLU_LIMIT)
    glu = gate / (1.0 + jnp.exp(-SWIGLU_ALPHA * gate))
    return (up + 1.0) * glu


def _experts_body(e_ref, f_ref, kind_ref, blk_ref, blk0_ref, nsub_ref, pre_ref, nxb_ref,
                  xs_ref, wg_ref, wu_ref, bup_ref, wd_ref, bdn_ref,
                  y_ref, xb_ref, h_ref, wdb_ref, stg_ref, sem):
    t = pl.program_id(0)
    kind = kind_ref[t]
    f = f_ref[t]
    nsub = nsub_ref[t]
    erow = pl.ds(e_ref[t], 1)
    ftile = pl.multiple_of(f * MOE_TF, MOE_TF)

    def x_copy(first_blk, j, slot):
        rows = MOE_SUB * ROW_PITCH
        first = pl.multiple_of((first_blk * MOE_BM + j * MOE_SUB) * ROW_PITCH, SUBLANES)
        return pltpu.make_async_copy(xs_ref.at[pl.ds(first, rows)], stg_ref.at[slot], sem.at[slot])

    def unpack(j, slot):
        rows = pl.ds(pl.multiple_of(j * MOE_SUB, MOE_SUB), MOE_SUB)
        for jt, tile in enumerate(_load_token_rows(stg_ref.at[slot], MOE_SUB)):
            xb_ref[rows, jt * LANES:(jt + 1) * LANES] = tile.astype(BF16)

    @pl.when(kind == STEP_UP)
    def _up():
        pre = pre_ref[t]

        @pl.when((f == 0) & (pre < nsub))
        def _load_rows():
            blk0 = blk0_ref[t]
            x_copy(blk0, pre, pre % 2).start()

            def body(j, c):
                slot = j % 2

                @pl.when(j + 1 < nsub)
                def _():
                    x_copy(blk0, j + 1, 1 - slot).start()

                x_copy(blk0, j, slot).wait()
                unpack(j, slot)
                return c

            lax.fori_loop(pre, nsub, body, 0)

        wdb_ref[pl.ds(pl.multiple_of(f * MOE_TF, MOE_TF), MOE_TF), :] = wd_ref[0].astype(BF16)

        def up_rows(row0, nrows):
            x = xb_ref[pl.ds(row0, nrows), :]
            hg = jnp.dot(x, wg_ref[0].astype(BF16), preferred_element_type=F32) + bup_ref[erow, pl.ds(ftile, MOE_TF)]
            hu = (jnp.dot(x, wu_ref[0].astype(BF16), preferred_element_type=F32)
                  + bup_ref[erow, pl.ds(pl.multiple_of(D_FF + ftile, MOE_TF), MOE_TF)])
            act = _swiglu(hg, hu).astype(BF16)
            h_ref[pl.ds(row0, nrows), pl.ds(pl.multiple_of(f * MOE_TF, MOE_TF), MOE_TF)] = act

        quad = 4 * MOE_SUB
        nquad = nsub // 4
        rem = nsub % 4

        def quad_rows(i, c):
            base = pl.multiple_of(i * quad, quad)
            up_rows(base, quad // 2)
            up_rows(base + quad // 2, quad // 2)
            return c

        merged = nsub == 5

        @pl.when(merged)
        def _():
            up_rows(0, 2 * MOE_SUB)
            up_rows(2 * MOE_SUB, 3 * MOE_SUB)

        @pl.when(jnp.logical_not(merged))
        def _():
            lax.fori_loop(0, nquad, quad_rows, 0)

            @pl.when(rem >= 2)
            def _():
                up_rows(pl.multiple_of(nquad * quad, quad), 2 * MOE_SUB)

            @pl.when(rem % 2 == 1)
            def _():
                up_rows(pl.multiple_of(nquad * quad + (rem // 2) * 2 * MOE_SUB, MOE_SUB), MOE_SUB)

    @pl.when(kind == STEP_DOWN)
    def _down():
        jblk = blk_ref[t] - blk0_ref[t]
        row0 = pl.multiple_of(jblk * MOE_BM, MOE_BM)
        nx_nsub = pre_ref[t]
        nx_blk0 = nxb_ref[t]
        ahead = [jblk * PREFETCH_PER_DOWN + u for u in range(PREFETCH_PER_DOWN)]
        for u, j in enumerate(ahead):
            @pl.when(j < nx_nsub)
            def _(u=u, j=j):
                x_copy(nx_blk0, j, u).start()

        def down_rows(nrows):
            hrows = h_ref[pl.ds(row0, nrows), :]
            return jnp.dot(hrows, wdb_ref[...], preferred_element_type=F32) + bdn_ref[erow, :]

        @pl.when(nsub == 2)
        def _():
            _store_token_rows(y_ref, 0, down_rows(MOE_BM))

        @pl.when(nsub == 1)
        def _():
            _store_token_rows(y_ref, 0, down_rows(MOE_SUB))
            y_ref[MOE_SUB * ROW_PITCH:, :] = jnp.zeros(((MOE_BM - MOE_SUB) * ROW_PITCH, LANES), F32)

        for u, j in enumerate(ahead):
            @pl.when(j < nx_nsub)
            def _(u=u, j=j):
                x_copy(nx_blk0, j, u).wait()
                unpack(j, u)

    @pl.when(kind == STEP_TAIL)
    def _tail():
        y_ref[...] = jnp.zeros_like(y_ref)


def _schedule_body(n_blocks, cnt_ref, e_ref, f_ref, kind_ref, blk_ref, blk0_ref, nsub_ref, pre_ref, nxb_ref,
                   start_ref, pad_start_ref, pad_len_ref, nact_ref):
    n_steps = e_ref.shape[0]
    sub_per_blk = MOE_BM // MOE_SUB
    zero = jnp.int32(0)

    def put(idx, e, f, kind, blk, blk0, nsub, pre):
        e_ref[idx] = e
        f_ref[idx] = f
        kind_ref[idx] = kind
        blk_ref[idx] = blk
        blk0_ref[idx] = blk0
        nsub_ref[idx] = nsub
        pre_ref[idx] = pre
        nxb_ref[idx] = zero

    def per_expert(e, carry):
        t, blk, last_e, prev_down, prev_nb = carry
        n = cnt_ref[e]
        nblk = (n + (MOE_BM - 1)) // MOE_BM
        start_ref[e] = blk * MOE_BM
        pad_start_ref[e] = blk * MOE_BM + n
        pad_len_ref[e] = nblk * MOE_BM - n

        def per_pass(p, carry2):
            t2, pdown, pnb = carry2
            b0 = blk + p * PASS_BLOCKS
            nb = jnp.minimum(nblk - p * PASS_BLOCKS, PASS_BLOCKS)
            nsub = (jnp.minimum(n - p * MOE_RM, MOE_RM) + (MOE_SUB - 1)) // MOE_SUB
            pre = jnp.minimum(nsub, pnb * PREFETCH_PER_DOWN)
            for f in range(NF):
                put(t2 + f, e, f, STEP_UP, b0, b0, nsub, pre)

            def patch(j, c):
                pre_ref[pdown + j] = nsub
                nxb_ref[pdown + j] = b0
                return c

            lax.fori_loop(0, pnb, patch, 0)

            def per_blk(j, c):
                put(t2 + NF + j, e, NF - 1, STEP_DOWN, b0 + j, b0,
                    jnp.clip(nsub - j * sub_per_blk, 1, sub_per_blk), zero)
                return c

            lax.fori_loop(0, nb, per_blk, 0)
            return t2 + NF + nb, t2 + NF, nb

        t, prev_down, prev_nb = lax.fori_loop(0, (nblk + (PASS_BLOCKS - 1)) // PASS_BLOCKS, per_pass,
                                              (t, prev_down, prev_nb))
        return t, blk + nblk, jnp.where(n > 0, e, last_e), prev_down, prev_nb

    t, nact, last_e, _, _ = lax.fori_loop(0, N_EXPERTS, per_expert, (zero, zero, zero, zero, zero))
    nact_ref[0] = nact
    nact_ref[1] = t + n_blocks - nact

    def spare(i, c):
        is_tail = i < n_blocks - nact
        blk = jnp.where(is_tail, nact + i, n_blocks - 1)
        put(t + i, last_e, NF - 1, jnp.where(is_tail, STEP_TAIL, STEP_IDLE), blk, blk, 1, zero)
        return c

    lax.fori_loop(0, n_steps - t, spare, 0)


def _expert_schedule(counts, n_blocks, n_assign):
    n_pass_max = N_EXPERTS + n_assign // MOE_RM
    n_steps = n_pass_max * NF + n_blocks
    smem = pl.BlockSpec(memory_space=pltpu.SMEM)
    i32 = jnp.int32
    out_shape = tuple(jax.ShapeDtypeStruct((n,), i32) for n in (n_steps,) * 8 + (N_EXPERTS,) * 3 + (2,))
    outs = pl.pallas_call(
        functools.partial(_schedule_body, n_blocks),
        in_specs=[smem],
        out_specs=tuple(smem for _ in out_shape),
        out_shape=out_shape,
        name="schedule",
    )(counts)
    return outs[:8], outs[8], outs[9], outs[10], outs[11]


def _experts(tabs, xs, w_up, b_up, w_down, b_down, name):
    d = D_MODEL
    n_steps = tabs[0].shape[0]
    grid_spec = pltpu.PrefetchScalarGridSpec(
        num_scalar_prefetch=len(tabs),
        grid=(n_steps,),
        in_specs=[
            pl.BlockSpec(memory_space=pl.ANY),
            pl.BlockSpec((1, d, MOE_TF), lambda t, e, f, *_: (e[t], 0, f[t])),
            pl.BlockSpec((1, d, MOE_TF), lambda t, e, f, *_: (e[t], 0, NF + f[t])),
            _const_spec(b_up.shape),
            pl.BlockSpec((1, MOE_TF, d), lambda t, e, f, *_: (e[t], f[t], 0)),
            _const_spec(b_down.shape),
        ],
        out_specs=pl.BlockSpec((MOE_BM * ROW_PITCH, LANES), lambda t, e, f, kind, blk, *_: (blk[t], 0)),
        scratch_shapes=[
            pltpu.VMEM((MOE_RM, d), BF16),
            pltpu.VMEM((MOE_RM, D_FF), BF16),
            pltpu.VMEM((D_FF, d), BF16),
            pltpu.VMEM((2, MOE_SUB * ROW_PITCH, LANES), F32),
            pltpu.SemaphoreType.DMA((2,)),
        ],
    )
    return pl.pallas_call(
        _experts_body,
        grid_spec=grid_spec,
        out_shape=jax.ShapeDtypeStruct(xs.shape, F32),
        compiler_params=pltpu.CompilerParams(dimension_semantics=("arbitrary",),
                                             vmem_limit_bytes=VMEM_LIMIT),
        name=name,
    )(*tabs, xs, w_up, w_up, b_up, w_down, b_down)


def _combine_body(slot_ref, slot_nx_ref, gate_ref, x1_ref, gf_ref, ys_ref, o_ref, buf_ref, sem):
    step = pl.program_id(0)
    slot = step % 2
    tm = x1_ref.shape[0]

    def gather(rows_ref, s):
        def issue(i, c):
            for k in range(TOP_K):
                _rows_copy(ys_ref, rows_ref[i * TOP_K + k], buf_ref.at[s, k], i * ROW_PITCH,
                           sem.at[s]).start(priority=k % 2)
            return c

        lax.fori_loop(0, tm, issue, 0, unroll=2)

    @pl.when(step == 0)
    def _():
        gather(slot_ref, 0)

    for s in range(2):
        @pl.when((step + 1 < pl.num_programs(0)) & (slot == 1 - s))
        def _(s=s):
            gather(slot_nx_ref, s)

    def drain(i, c):
        for _ in range(WAIT_UNROLL):
            _token_copy(ys_ref, 0, buf_ref.at[slot, 0], 0, sem.at[slot]).wait()
        return c

    lax.fori_loop(0, tm * TOP_K // WAIT_UNROLL, drain, 0)
    gate = gate_ref[...]
    tiles = [x1_ref[:, j * LANES:(j + 1) * LANES] for j in range(ROW_TILES)]
    for k in range(TOP_K):
        g = gate[:, k:k + 1]
        for j, tile in enumerate(_load_token_rows(buf_ref.at[slot, k], tm)):
            tiles[j] = tiles[j] + g * tile
    o_ref[...] = _rms(jnp.concatenate(tiles, axis=1), gf_ref[...])


def _combine(slot_rows, gates, x1, gf, ys):
    t, d = x1.shape
    tm = min(TM_COMBINE, t)
    n = t // tm
    idx_blk = (tm * TOP_K,)
    return pl.pallas_call(
        _combine_body,
        grid=(n,),
        in_specs=[pl.BlockSpec(idx_blk, lambda i: (i,), memory_space=pltpu.SMEM),
                  pl.BlockSpec(idx_blk, lambda i: (jnp.minimum(i + 1, n - 1),), memory_space=pltpu.SMEM),
                  pl.BlockSpec((tm, TOP_K), lambda i: (i, 0)),
                  pl.BlockSpec((tm, d), lambda i: (i, 0)),
                  _const_spec(gf.shape),
                  pl.BlockSpec(memory_space=pl.ANY)],
        out_specs=pl.BlockSpec((tm, d), lambda i: (i, 0)),
        scratch_shapes=[pltpu.VMEM((2, TOP_K, tm * ROW_PITCH, LANES), F32), pltpu.SemaphoreType.DMA((2,))],
        out_shape=jax.ShapeDtypeStruct((t, d), F32),
        compiler_params=pltpu.CompilerParams(dimension_semantics=("arbitrary",),
                                             vmem_limit_bytes=VMEM_LIMIT),
        name="combine",
    )(slot_rows, slot_rows, gates, x1, gf, ys)


def _rope_tables(s):
    pos = np.arange(s, dtype=np.float32)
    inv_freq = np.power(np.float32(ROPE_THETA), -np.arange(0, B_ROPE_DIM, 2, dtype=np.float32) / B_ROPE_DIM)
    ang = (pos[:, None] * inv_freq[None, :]).astype(np.float32)
    cos, sin = np.cos(ang), np.sin(ang)
    pad = LANES - B_ROPE_DIM
    cos = np.concatenate([cos, cos, np.ones((s, pad), np.float32)], axis=1)
    sin = np.concatenate([sin, sin, np.zeros((s, pad), np.float32)], axis=1)
    return cos.astype(np.float32), sin.astype(np.float32)


def _window_bias():
    qi = np.arange(A_BLOCK)[:, None]
    kj = np.arange(3 * A_BLOCK)[None, :]
    dist = np.abs(qi + A_BLOCK - kj)
    slopes = np.power(np.float32(2.0), -8.0 * np.arange(1, A_HEADS + 1, dtype=np.float32) / A_HEADS)
    bias = -slopes[:, None, None] * dist.astype(np.float32)[None]
    bias = np.where((dist <= WINDOW)[None], bias, -np.inf)
    bias = bias.reshape(A_KV_HEADS, 2, 2, A_BLOCK, 3 * A_BLOCK).transpose(0, 2, 1, 3, 4)
    bias = bias.reshape(2 * A_KV_HEADS, 2 * A_BLOCK, 3 * A_BLOCK)
    no_prev = (kj < A_BLOCK)[None]
    no_next = (kj >= 2 * A_BLOCK)[None]
    ninf = -np.inf
    return np.stack([bias, np.where(no_prev, ninf, bias), np.where(no_next, ninf, bias),
                     np.where(no_prev | no_next, ninf, bias)]).astype(np.float32)


def _layer(x, attn_norm, w_in, a_sink, b_q_norm, b_w_uq, b_kv_norm, b_w_ukv, out_norm_a, out_norm_b, w_o,
           mlp_norm, w_router, b_router, w_up, b_up, w_down, b_down):
    b, s, d = x.shape
    t = b * s
    wq = b_w_uq.reshape(B_Q_RANK, B_HEADS, B_NOPE_DIM + B_ROPE_DIM)
    wq_pe = jnp.pad(wq[:, :, B_NOPE_DIM:], ((0, 0), (0, 0), (0, LANES - B_ROPE_DIM)))
    wuq = jnp.concatenate([wq[:, :, :B_NOPE_DIM].reshape(B_Q_RANK, -1), wq_pe.reshape(B_Q_RANK, -1)],
                          axis=1).astype(BF16)
    wkv = b_w_ukv.reshape(B_KV_RANK, B_HEADS, B_NOPE_DIM + B_V_DIM)
    wukv = jnp.concatenate([wkv[:, :, :B_NOPE_DIM].reshape(B_KV_RANK, -1),
                            wkv[:, :, B_NOPE_DIM:].reshape(B_KV_RANK, -1)], axis=1).astype(BF16)
    cos, sin = _rope_tables(s)

    qa, ka, va, qb, kb, vb = _prologue(x, attn_norm[None], w_in, b_q_norm[None], wuq, b_kv_norm[None], wukv,
                                       cos, sin)
    ya = _window_attn(qa, ka, va, a_sink, _window_bias())
    yb = _mla_attn(qb, kb, vb)
    wr_hi = w_router.astype(BF16)
    wr_lo = (w_router - wr_hi.astype(F32)).astype(BF16)
    x1, xm, eidx, gates, rank, counts = _out_router(
        ya.reshape(t, A_WIDTH), yb.reshape(t, B_WIDTH), x.reshape(t, d), out_norm_a[None], out_norm_b[None],
        w_o.astype(BF16), mlp_norm[None], jnp.concatenate([wr_hi, wr_lo], axis=1), b_router[None])

    nblk = (t * TOP_K + N_EXPERTS * (MOE_BM - 1) + MOE_BM - 1) // MOE_BM
    tabs, start, pad_start, pad_len, nact = _expert_schedule(counts.reshape(N_EXPERTS), nblk, t * TOP_K)
    slot_rows = _slot_rows(start, eidx.reshape(t * TOP_K), rank.reshape(t * TOP_K))
    xs = _dispatch(pad_start, pad_len, nact, slot_rows, xm, nblk * MOE_BM)
    n_short = N_EXPERTS * NF + nblk
    args = (xs, w_up, b_up, w_down, b_down)
    ys = lax.cond(nact[1] <= n_short,
                  lambda: _experts(tuple(a[:n_short] for a in tabs), *args, name="experts"),
                  lambda: _experts(tabs, *args, name="experts_long"))
    return x1, slot_rows, gates, ys


def kernel(x, attn_norm, w_in, a_sink, b_q_norm, b_w_uq, b_kv_norm, b_w_ukv, out_norm_a, out_norm_b, w_o,
           mlp_norm, w_router, b_router, w_up, b_up, w_down, b_down, final_norm):
    b, s, d = x.shape
    assert d == D_MODEL and s % min(TQ_MLA, s) == 0 and s % min(CK_MLA, s) == 0 and s % TM_PRO == 0 and attn_norm.shape[0] == 1
    x1, slot_rows, gates, ys = _layer(
        x, attn_norm[0], w_in[0], a_sink[0], b_q_norm[0], b_w_uq[0], b_kv_norm[0], b_w_ukv[0], out_norm_a[0],
        out_norm_b[0], w_o[0], mlp_norm[0], w_router[0], b_router[0], w_up[0], b_up[0], w_down[0], b_down[0])
    out = _combine(slot_rows, gates, x1, final_norm[None], ys)
    return out.reshape(b, s, d)
```

```python
import functools

import jax
import numpy as np
import jax.numpy as jnp
from jax import lax
from jax.experimental import pallas as pl
from jax.experimental.pallas import tpu as pltpu

D_MODEL = 2048
A_HEADS, A_KV_HEADS, A_HEAD_DIM = 16, 4, 64
A_GROUP = A_HEADS // A_KV_HEADS
WINDOW = 128
A_BLOCK = 128
B_HEADS, B_Q_RANK, B_KV_RANK = 8, 512, 256
B_NOPE_DIM, B_ROPE_DIM, B_V_DIM = 128, 64, 128
ROPE_THETA = 10000.0
A_WIDTH = A_HEADS * A_HEAD_DIM
B_WIDTH = B_HEADS * B_V_DIM
A_KV_COLS = A_KV_HEADS * A_HEAD_DIM
N_EXPERTS, TOP_K, D_FF = 32, 4, 2048
SWIGLU_ALPHA, SWIGLU_LIMIT = 1.702, 7.0
EPS = 1e-5

LANES = 128
SUBLANES = 8
B_QK_PAD = 2 * LANES
IN_COLS_PAD = A_WIDTH + 2 * A_KV_COLS + B_Q_RANK + B_KV_RANK + LANES
VMEM_LIMIT = 56 * 1024 * 1024
ROW_TILES = D_MODEL // LANES
ROW_PITCH = ROW_TILES + 1

TM_PRO = 256
TQ_MLA = 1024
CK_MLA = 2048
HEADS_MLA = 2
TM_OUT = 512
TM_DISPATCH = 1024
TM_COMBINE = 256
MOE_BM = 512
MOE_SUB = 256
MOE_RM = 2048
MOE_TF = 256
WAIT_UNROLL = 16

BF16 = jnp.bfloat16
F32 = jnp.float32


def _rms(x, g):
    return x * lax.rsqrt(jnp.mean(x * x, axis=-1, keepdims=True) + EPS) * g


def _const_spec(shape):
    nd = len(shape)
    return pl.BlockSpec(shape, lambda *_: (0,) * nd, pipeline_mode=pl.Buffered(1))


def _store_token_rows(ref, first, val):
    n = val.shape[0]
    for j in range(ROW_TILES):
        ref[pl.ds(first * ROW_PITCH + j, n, stride=ROW_PITCH), :] = val[:, j * LANES:(j + 1) * LANES]
    ref[pl.ds(first * ROW_PITCH + ROW_TILES, n, stride=ROW_PITCH), :] = jnp.zeros((n, LANES), val.dtype)


def _load_token_rows(ref, n):
    return [ref[pl.ds(j, n, stride=ROW_PITCH), :] for j in range(ROW_TILES)]


def _rope(x, cos, sin):
    lane = lax.broadcasted_iota(jnp.int32, x.shape, 1)
    up = pltpu.roll(x, LANES - B_ROPE_DIM // 2, 1)
    dn = pltpu.roll(x, B_ROPE_DIM // 2, 1)
    sw = jnp.where(lane < B_ROPE_DIM // 2, -up, jnp.where(lane < B_ROPE_DIM, dn, 0.0))
    return x * cos + sw * sin


def _prologue_body(x_ref, g_ref, win_ref, qn_ref, wuq_ref, kvn_ref, wukv_ref, cos_ref, sin_ref,
                   qa_ref, ka_ref, va_ref, qb_ref, kb_ref, vb_ref, winb_ref):
    @pl.when((pl.program_id(0) == 0) & (pl.program_id(1) == 0))
    def _():
        n_in = win_ref.shape[1]
        winb_ref[:, :n_in] = win_ref[...].astype(BF16)
        winb_ref[:, n_in:] = jnp.zeros((winb_ref.shape[0], IN_COLS_PAD - n_in), BF16)

    x = x_ref[0]
    xn = _rms(x, g_ref[...]).astype(BF16)
    proj = jnp.dot(xn, winb_ref[...], preferred_element_type=F32)
    c0 = A_WIDTH
    c1 = c0 + A_KV_COLS
    c2 = c1 + A_KV_COLS
    c3 = c2 + B_Q_RANK
    c4 = c3 + B_KV_RANK
    qa_ref[0] = (proj[:, :A_WIDTH] * A_HEAD_DIM ** -0.5).astype(BF16)
    low = lax.broadcasted_iota(jnp.int32, (proj.shape[0], LANES), 1) < A_HEAD_DIM
    for src, dst in ((c0, ka_ref), (c1, va_ref)):
        for pair in range(A_KV_HEADS // 2):
            two = proj[:, src + pair * LANES:src + (pair + 1) * LANES]
            swapped = pltpu.roll(two, A_HEAD_DIM, 1)
            dst[0, 4 * pair + 0] = jnp.where(low, two, 0.0).astype(BF16)
            dst[0, 4 * pair + 1] = jnp.where(low, 0.0, swapped).astype(BF16)
            dst[0, 4 * pair + 2] = jnp.where(low, swapped, 0.0).astype(BF16)
            dst[0, 4 * pair + 3] = jnp.where(low, 0.0, two).astype(BF16)
    cq = _rms(proj[:, c2:c3], qn_ref[...]).astype(BF16)
    q = jnp.dot(cq, wuq_ref[...], preferred_element_type=F32)
    ckv = _rms(proj[:, c3:c4], kvn_ref[...]).astype(BF16)
    kv = jnp.dot(ckv, wukv_ref[...], preferred_element_type=F32)
    cos = cos_ref[...]
    sin = sin_ref[...]
    kpe = _rope(proj[:, c4:c4 + LANES], cos, sin).astype(BF16)
    b_scale = (B_NOPE_DIM + B_ROPE_DIM) ** -0.5
    hw = B_HEADS * LANES
    for h in range(B_HEADS):
        sl = slice(h * LANES, (h + 1) * LANES)
        qb_ref[0, h, :, :LANES] = (q[:, sl] * b_scale).astype(BF16)
        qpe = _rope(q[:, hw + h * LANES:hw + (h + 1) * LANES], cos, sin)
        qb_ref[0, h, :, LANES:] = (qpe * b_scale).astype(BF16)
        kb_ref[0, h, :, :LANES] = kv[:, sl].astype(BF16)
        kb_ref[0, h, :, LANES:] = kpe
        vb_ref[0, h] = kv[:, hw + h * LANES:hw + (h + 1) * LANES].astype(BF16)


def _prologue(x, g, win, qn, wuq, kvn, wukv, cos, sin):
    b, s, d = x.shape
    tm = TM_PRO
    grid = (b, s // tm)
    sds = jax.ShapeDtypeStruct
    out_shape = (
        sds((b, s, A_WIDTH), BF16),
        sds((b, 2 * A_KV_HEADS, s, LANES), BF16),
        sds((b, 2 * A_KV_HEADS, s, LANES), BF16),
        sds((b, B_HEADS, s, B_QK_PAD), BF16),
        sds((b, B_HEADS, s, B_QK_PAD), BF16),
        sds((b, B_HEADS, s, B_V_DIM), BF16),
    )

    def hspec(nh, w):
        return pl.BlockSpec((1, nh, tm, w), lambda bi, i: (bi, 0, i, 0))

    return pl.pallas_call(
        _prologue_body,
        grid=grid,
        in_specs=[
            pl.BlockSpec((1, tm, d), lambda bi, i: (bi, i, 0)),
            _const_spec(g.shape), _const_spec(win.shape), _const_spec(qn.shape), _const_spec(wuq.shape),
            _const_spec(kvn.shape), _const_spec(wukv.shape),
            pl.BlockSpec((tm, LANES), lambda bi, i: (i, 0)),
            pl.BlockSpec((tm, LANES), lambda bi, i: (i, 0)),
        ],
        out_specs=(pl.BlockSpec((1, tm, A_WIDTH), lambda bi, i: (bi, i, 0)),
                   hspec(2 * A_KV_HEADS, LANES), hspec(2 * A_KV_HEADS, LANES),
                   hspec(B_HEADS, B_QK_PAD), hspec(B_HEADS, B_QK_PAD), hspec(B_HEADS, B_V_DIM)),
        out_shape=out_shape,
        scratch_shapes=[pltpu.VMEM((d, IN_COLS_PAD), BF16)],
        compiler_params=pltpu.CompilerParams(dimension_semantics=("arbitrary", "arbitrary"),
                                             vmem_limit_bytes=VMEM_LIMIT),
        name="prologue",
    )(x, g, win, qn, wuq, kvn, wukv, cos, sin)


def _window_body(sink_ref, q_ref, kp_ref, kc_ref, kn_ref, vp_ref, vc_ref, vn_ref, bias_ref, o_ref):
    two = 2 * A_BLOCK
    low = lax.broadcasted_iota(jnp.int32, (two, LANES), 1) < A_HEAD_DIM
    for hk in range(A_KV_HEADS):
        q2 = jnp.concatenate([q_ref[0, :, (2 * hk) * LANES:(2 * hk + 1) * LANES],
                              q_ref[0, :, (2 * hk + 1) * LANES:(2 * hk + 2) * LANES]], axis=0)
        acc = jnp.zeros((two, LANES), F32)
        inv = []
        for half in range(2):
            z = 2 * hk + half
            kband = jnp.concatenate([kp_ref[0, z], kc_ref[0, z], kn_ref[0, z]], axis=0)
            vband = jnp.concatenate([vp_ref[0, z], vc_ref[0, z], vn_ref[0, z]], axis=0)
            s = lax.dot_general(q2, kband, (((1,), (1,)), ((), ())), preferred_element_type=F32)
            s = s + bias_ref[0, z]
            sink = jnp.concatenate(
                [jnp.full((A_BLOCK, 1), sink_ref[hk * A_GROUP + 2 * jj + half], F32) for jj in range(2)], axis=0)
            m = jnp.maximum(jnp.max(s, axis=-1, keepdims=True), sink)
            p = jnp.exp(s - m)
            den = jnp.sum(p, axis=-1, keepdims=True) + jnp.exp(sink - m)
            inv.append(1.0 / den)
            acc = acc + jnp.dot(p.astype(BF16), vband, preferred_element_type=F32)
        o = (acc * jnp.where(low, inv[0], inv[1])).astype(BF16)
        o_ref[0, :, (2 * hk) * LANES:(2 * hk + 1) * LANES] = o[:A_BLOCK]
        o_ref[0, :, (2 * hk + 1) * LANES:(2 * hk + 2) * LANES] = o[A_BLOCK:]


def _window_attn(qa, ka, va, sink, bias):
    b, s, _ = qa.shape
    nb = s // A_BLOCK
    kv_blk = (1, 2 * A_KV_HEADS, A_BLOCK, LANES)
    prev = pl.BlockSpec(kv_blk, lambda bi, n: (bi, 0, jnp.maximum(n - 1, 0), 0))
    cur = pl.BlockSpec(kv_blk, lambda bi, n: (bi, 0, n, 0))
    nxt = pl.BlockSpec(kv_blk, lambda bi, n: (bi, 0, jnp.minimum(n + 1, nb - 1), 0))

    def edge(bi, n):
        return ((n == 0).astype(jnp.int32) + 2 * (n == nb - 1).astype(jnp.int32), 0, 0, 0)

    return pl.pallas_call(
        _window_body,
        grid=(b, nb),
        in_specs=[
            pl.BlockSpec(memory_space=pltpu.SMEM),
            pl.BlockSpec((1, A_BLOCK, A_WIDTH), lambda bi, n: (bi, n, 0)),
            prev, cur, nxt, prev, cur, nxt,
            pl.BlockSpec((1,) + bias.shape[1:], edge),
        ],
        out_specs=pl.BlockSpec((1, A_BLOCK, A_WIDTH), lambda bi, n: (bi, n, 0)),
        out_shape=jax.ShapeDtypeStruct((b, s, A_WIDTH), BF16),
        compiler_params=pltpu.CompilerParams(dimension_semantics=("parallel", "parallel"),
                                             vmem_limit_bytes=VMEM_LIMIT),
        name="window_attn",
    )(sink, qa, ka, ka, ka, va, va, va, bias)


def _mla_body(q_ref, k_ref, v_ref, o_ref):
    tq = q_ref.shape[2]
    s_len = k_ref.shape[2]
    ck = min(CK_MLA, s_len)
    for hh in range(HEADS_MLA):
        q = q_ref[0, hh]
        m = jnp.full((tq, 1), -jnp.inf, F32)
        l = jnp.zeros((tq, 1), F32)
        acc = jnp.zeros((tq, B_V_DIM), F32)
        for c in range(s_len // ck):
            k_c = k_ref[0, hh, c * ck:(c + 1) * ck, :]
            v_c = v_ref[0, hh, c * ck:(c + 1) * ck, :]
            s = lax.dot_general(q, k_c, (((1,), (1,)), ((), ())), preferred_element_type=F32)
            m_new = jnp.maximum(m, jnp.max(s, axis=-1, keepdims=True))
            alpha = jnp.exp(m - m_new)
            p = jnp.exp(s - m_new)
            l = alpha * l + jnp.sum(p, axis=-1, keepdims=True)
            acc = alpha * acc + jnp.dot(p.astype(BF16), v_c, preferred_element_type=F32)
            m = m_new
        o_ref[0, :, hh * B_V_DIM:(hh + 1) * B_V_DIM] = (acc / l).astype(BF16)


def _mla_attn(qb, kb, vb):
    b, nh, s, _ = qb.shape
    tq = min(TQ_MLA, s)
    hs = HEADS_MLA
    return pl.pallas_call(
        _mla_body,
        grid=(b, nh // hs, s // tq),
        in_specs=[
            pl.BlockSpec((1, hs, tq, B_QK_PAD), lambda bi, h, i: (bi, h, i, 0)),
            pl.BlockSpec((1, hs, s, B_QK_PAD), lambda bi, h, i: (bi, h, 0, 0)),
            pl.BlockSpec((1, hs, s, B_V_DIM), lambda bi, h, i: (bi, h, 0, 0)),
        ],
        out_specs=pl.BlockSpec((1, tq, hs * B_V_DIM), lambda bi, h, i: (bi, i, h)),
        out_shape=jax.ShapeDtypeStruct((b, s, B_WIDTH), BF16),
        compiler_params=pltpu.CompilerParams(dimension_semantics=("parallel", "parallel", "parallel"),
                                             vmem_limit_bytes=VMEM_LIMIT),
        name="mla_attn",
    )(qb, kb, vb)


def _out_router_body(ya_ref, yb_ref, x_ref, ga_ref, gb_ref, wo_ref, gm_ref, wr_ref, br_ref,
                     x1_ref, xm_ref, eidx_ref, gate_ref, rank_ref, cnt_ref, run_ref):
    i = pl.program_id(0)

    @pl.when(i == 0)
    def _():
        run_ref[...] = jnp.zeros_like(run_ref)

    na = _rms(ya_ref[...].astype(F32), ga_ref[...]).astype(BF16)
    nb = _rms(yb_ref[...].astype(F32), gb_ref[...]).astype(BF16)
    att = jnp.dot(na, wo_ref[:A_WIDTH, :], preferred_element_type=F32)
    att = att + jnp.dot(nb, wo_ref[A_WIDTH:, :], preferred_element_type=F32)
    x1 = x_ref[...] + att
    x1_ref[...] = x1
    hn = _rms(x1, gm_ref[...])
    _store_token_rows(xm_ref, 0, hn)
    tm = hn.shape[0]
    hi = hn.astype(BF16)
    lo = (hn - hi.astype(F32)).astype(BF16)
    prod = jnp.dot(jnp.concatenate([hi, lo], axis=0), wr_ref[...], preferred_element_type=F32)
    logits = (prod[:tm, :N_EXPERTS] + prod[:tm, N_EXPERTS:] + prod[tm:, :N_EXPERTS] + prod[tm:, N_EXPERTS:]
              + br_ref[...])
    lane = lax.broadcasted_iota(jnp.int32, (tm, N_EXPERTS), 1)
    work = logits
    sel = jnp.zeros((tm, N_EXPERTS), F32)
    hots, vals, idxs = [], [], []
    for _k in range(TOP_K):
        mx = jnp.max(work, axis=-1, keepdims=True)
        idx = jnp.min(jnp.where(work == mx, lane, N_EXPERTS), axis=-1, keepdims=True)
        hot = lane == idx
        hots.append(hot)
        vals.append(mx)
        idxs.append(idx)
        sel = sel + hot.astype(F32)
        work = jnp.where(hot, -jnp.inf, work)
    exps = [jnp.exp(v - vals[0]) for v in vals]
    den = exps[0] + exps[1] + exps[2] + exps[3]
    r_i = lax.broadcasted_iota(jnp.int32, (tm, tm), 0)
    c_i = lax.broadcasted_iota(jnp.int32, (tm, tm), 1)
    tri = (c_i < r_i).astype(BF16)
    before = jnp.dot(tri, sel.astype(BF16), preferred_element_type=F32) + run_ref[...]
    lane4 = lax.broadcasted_iota(jnp.int32, (tm, TOP_K), 1)
    eidx = jnp.zeros((tm, TOP_K), jnp.int32)
    gate = jnp.zeros((tm, TOP_K), F32)
    rank = jnp.zeros((tm, TOP_K), jnp.int32)
    for k in range(TOP_K):
        rk = jnp.sum(jnp.where(hots[k], before, 0.0), axis=-1, keepdims=True).astype(jnp.int32)
        eidx = jnp.where(lane4 == k, idxs[k], eidx)
        gate = jnp.where(lane4 == k, exps[k] / den, gate)
        rank = jnp.where(lane4 == k, rk, rank)
    eidx_ref[...] = eidx
    gate_ref[...] = gate
    rank_ref[...] = rank
    run = run_ref[...] + jnp.sum(sel, axis=0, keepdims=True)
    run_ref[...] = run
    cnt_ref[...] = run.astype(jnp.int32)


def _out_router(ya, yb, x2d, ga, gb, wo, gm, wr, br):
    t, d = x2d.shape
    tm = min(TM_OUT, t)
    sds = jax.ShapeDtypeStruct

    def row(w):
        return pl.BlockSpec((tm, w), lambda i: (i, 0))

    return pl.pallas_call(
        _out_router_body,
        grid=(t // tm,),
        in_specs=[row(A_WIDTH), row(B_WIDTH), row(d), _const_spec(ga.shape), _const_spec(gb.shape),
                  _const_spec(wo.shape), _const_spec(gm.shape), _const_spec(wr.shape), _const_spec(br.shape)],
        out_specs=(row(d), pl.BlockSpec((tm * ROW_PITCH, LANES), lambda i: (i, 0)),
                   row(TOP_K), row(TOP_K), row(TOP_K),
                   pl.BlockSpec((1, N_EXPERTS), lambda i: (0, 0))),
        out_shape=(sds((t, d), F32), sds((t * ROW_PITCH, LANES), F32), sds((t, TOP_K), jnp.int32),
                   sds((t, TOP_K), F32),
                   sds((t, TOP_K), jnp.int32), sds((1, N_EXPERTS), jnp.int32)),
        scratch_shapes=[pltpu.VMEM((1, N_EXPERTS), F32)],
        compiler_params=pltpu.CompilerParams(dimension_semantics=("arbitrary",),
                                             vmem_limit_bytes=VMEM_LIMIT),
        name="out_router",
    )(ya, yb, x2d, ga, gb, wo, gm, wr, br)


def _token_copy(src_ref, src_tok, dst_ref, dst_tok, sem, ntok=1):
    return _rows_copy(src_ref, src_tok * ROW_PITCH, dst_ref, dst_tok * ROW_PITCH, sem, ntok)


def _rows_copy(src_ref, src_row, dst_ref, dst_row, sem, ntok=1):
    rows = ntok * ROW_PITCH
    return pltpu.make_async_copy(src_ref.at[pl.ds(src_row, rows)], dst_ref.at[pl.ds(dst_row, rows)], sem)


def _slot_rows_body(start_ref, eidx_ref, rank_ref, o_ref):
    e = eidx_ref[...]
    base = jnp.zeros_like(e)
    for ex in range(N_EXPERTS):
        base = jnp.where(e == ex, start_ref[ex], base)
    o_ref[...] = (base + rank_ref[...]) * ROW_PITCH


def _slot_rows(start, eidx, rank):
    n = eidx.shape[0]
    shape2 = (n // LANES, LANES)
    out = pl.pallas_call(
        _slot_rows_body,
        in_specs=[pl.BlockSpec(memory_space=pltpu.SMEM), pl.BlockSpec(shape2, lambda: (0, 0)),
                  pl.BlockSpec(shape2, lambda: (0, 0))],
        out_specs=pl.BlockSpec(shape2, lambda: (0, 0)),
        out_shape=jax.ShapeDtypeStruct(shape2, jnp.int32),
        name="slot_rows",
    )(start, eidx.reshape(shape2), rank.reshape(shape2))
    return out.reshape(n)


def _zero_fill_pads(pad_start_ref, pad_len_ref, nact_ref, xs_ref, zero_ref, zsem):
    bm = zero_ref.shape[0] // ROW_PITCH
    nblk = xs_ref.shape[0] // (bm * ROW_PITCH)
    zero_ref[...] = jnp.zeros_like(zero_ref)

    def sweep(do):
        def per_expert(e, c):
            off = pad_start_ref[e]
            n = pad_len_ref[e]
            p = bm // 2
            while p >= 1:
                take = (n & p) != 0

                @pl.when(take)
                def _(p=p, off=off):
                    do(_token_copy(zero_ref, 0, xs_ref, off, zsem, p))

                off = off + jnp.where(take, p, 0)
                p //= 2
            return c

        lax.fori_loop(0, N_EXPERTS, per_expert, 0)

        def tail(j, c):
            do(_token_copy(zero_ref, 0, xs_ref, j * bm, zsem, bm))
            return c

        lax.fori_loop(nact_ref[0], nblk, tail, 0)

    sweep(lambda cp: cp.start())
    sweep(lambda cp: cp.wait())


def _dispatch_body(pad_start_ref, pad_len_ref, nact_ref, slot_ref, xm_ref, xs_ref, zero_ref, sem, zsem):
    tm = slot_ref.shape[0] // TOP_K

    @pl.when(pl.program_id(0) == 0)
    def _():
        _zero_fill_pads(pad_start_ref, pad_len_ref, nact_ref, xs_ref, zero_ref, zsem)

    def issue(i, c):
        for k in range(TOP_K):
            _rows_copy(xm_ref, i * ROW_PITCH, xs_ref, slot_ref[i * TOP_K + k], sem).start(priority=k % 2)
        return c

    lax.fori_loop(0, tm, issue, 0, unroll=2)

    def drain(i, c):
        for _ in range(WAIT_UNROLL):
            _token_copy(xm_ref, 0, xs_ref, 0, sem).wait()
        return c

    lax.fori_loop(0, tm * TOP_K // WAIT_UNROLL, drain, 0)


def _dispatch(pad_start, pad_len, nact, slot_rows, xm, n_slots):
    t = xm.shape[0] // ROW_PITCH
    tm = min(TM_DISPATCH, t)
    grid_spec = pltpu.PrefetchScalarGridSpec(
        num_scalar_prefetch=3,
        grid=(t // tm,),
        in_specs=[pl.BlockSpec((tm * TOP_K,), lambda i, *_: (i,), memory_space=pltpu.SMEM),
                  pl.BlockSpec((tm * ROW_PITCH, LANES), lambda i, *_: (i, 0))],
        out_specs=pl.BlockSpec(memory_space=pl.ANY),
        scratch_shapes=[pltpu.VMEM((MOE_BM * ROW_PITCH, LANES), F32), pltpu.SemaphoreType.DMA(()),
                        pltpu.SemaphoreType.DMA(())],
    )
    return pl.pallas_call(
        _dispatch_body,
        grid_spec=grid_spec,
        out_shape=jax.ShapeDtypeStruct((n_slots * ROW_PITCH, LANES), F32),
        compiler_params=pltpu.CompilerParams(dimension_semantics=("arbitrary",), has_side_effects=True,
                                             vmem_limit_bytes=VMEM_LIMIT),
        name="dispatch",
    )(pad_start, pad_len, nact, slot_rows, xm)


STEP_UP, STEP_DOWN, STEP_TAIL, STEP_IDLE = 0, 1, 2, 3
PREFETCH_PER_DOWN = 2
NF = D_FF // MOE_TF
PASS_BLOCKS = MOE_RM // MOE_BM


def _swiglu(hg, hu):
    gate = jnp.minimum(hg, SWIGLU_LIMIT)
    up = jnp.clip(hu, -SWIGLU_LIMIT, SWIGLU_LIMIT)
    glu = gate / (1.0 + jnp.exp(-SWIGLU_ALPHA * gate))
    return (up + 1.0) * glu


def _experts_body(e_ref, f_ref, kind_ref, blk_ref, blk0_ref, nsub_ref, pre_ref, nxb_ref,
                  xs_ref, wg_ref, wu_ref, bup_ref, wd_ref, bdn_ref,
                  y_ref, xb_ref, h_ref, wdb_ref, stg_ref, sem):
    t = pl.program_id(0)
    kind = kind_ref[t]
    f = f_ref[t]
    nsub = nsub_ref[t]
    erow = pl.ds(e_ref[t], 1)
    ftile = pl.multiple_of(f * MOE_TF, MOE_TF)

    def x_copy(first_blk, j, slot):
        rows = MOE_SUB * ROW_PITCH
        first = pl.multiple_of((first_blk * MOE_BM + j * MOE_SUB) * ROW_PITCH, SUBLANES)
        return pltpu.make_async_copy(xs_ref.at[pl.ds(first, rows)], stg_ref.at[slot], sem.at[slot])

    def unpack(j, slot):
        rows = pl.ds(pl.multiple_of(j * MOE_SUB, MOE_SUB), MOE_SUB)
        for jt, tile in enumerate(_load_token_rows(stg_ref.at[slot], MOE_SUB)):
            xb_ref[rows, jt * LANES:(jt + 1) * LANES] = tile.astype(BF16)

    @pl.when(kind == STEP_UP)
    def _up():
        pre = pre_ref[t]

        @pl.when((f == 0) & (pre < nsub))
        def _load_rows():
            blk0 = blk0_ref[t]
            x_copy(blk0, pre, pre % 2).start()

            def body(j, c):
                slot = j % 2

                @pl.when(j + 1 < nsub)
                def _():
                    x_copy(blk0, j + 1, 1 - slot).start()

                x_copy(blk0, j, slot).wait()
                unpack(j, slot)
                return c

            lax.fori_loop(pre, nsub, body, 0)

        wdb_ref[pl.ds(pl.multiple_of(f * MOE_TF, MOE_TF), MOE_TF), :] = wd_ref[0].astype(BF16)

        def up_rows(row0, nrows):
            x = xb_ref[pl.ds(row0, nrows), :]
            hg = jnp.dot(x, wg_ref[0].astype(BF16), preferred_element_type=F32) + bup_ref[erow, pl.ds(ftile, MOE_TF)]
            hu = (jnp.dot(x, wu_ref[0].astype(BF16), preferred_element_type=F32)
                  + bup_ref[erow, pl.ds(pl.multiple_of(D_FF + ftile, MOE_TF), MOE_TF)])
            act = _swiglu(hg, hu).astype(BF16)
            h_ref[pl.ds(row0, nrows), pl.ds(pl.multiple_of(f * MOE_TF, MOE_TF), MOE_TF)] = act

        quad = 4 * MOE_SUB
        nquad = nsub // 4
        rem = nsub % 4

        def quad_rows(i, c):
            base = pl.multiple_of(i * quad, quad)
            up_rows(base, quad // 2)
            up_rows(base + quad // 2, quad // 2)
            return c

        merged = nsub == 5
        short_tail = nxb_ref[t] == 1

        @pl.when(merged & jnp.logical_not(short_tail))
        def _():
            up_rows(0, 2 * MOE_SUB)
            up_rows(2 * MOE_SUB, 3 * MOE_SUB)

        @pl.when(merged & short_tail)
        def _():
            up_rows(0, 2 * MOE_SUB)
            up_rows(2 * MOE_SUB, 2 * MOE_SUB + MOE_SUB // 2)
            h_ref[4 * MOE_SUB + MOE_SUB // 2:5 * MOE_SUB, pl.ds(ftile, MOE_TF)] = jnp.zeros(
                (MOE_SUB // 2, MOE_TF), BF16)

        @pl.when(jnp.logical_not(merged))
        def _():
            lax.fori_loop(0, nquad, quad_rows, 0)

            @pl.when(rem >= 2)
            def _():
                up_rows(pl.multiple_of(nquad * quad, quad), 2 * MOE_SUB)

            @pl.when(rem % 2 == 1)
            def _():
                up_rows(pl.multiple_of(nquad * quad + (rem // 2) * 2 * MOE_SUB, MOE_SUB), MOE_SUB)

    @pl.when(kind == STEP_DOWN)
    def _down():
        jblk = blk_ref[t] - blk0_ref[t]
        row0 = pl.multiple_of(jblk * MOE_BM, MOE_BM)
        nx_nsub = pre_ref[t]
        nx_blk0 = nxb_ref[t]
        ahead = [jblk * PREFETCH_PER_DOWN + u for u in range(PREFETCH_PER_DOWN)]
        for u, j in enumerate(ahead):
            @pl.when(j < nx_nsub)
            def _(u=u, j=j):
                x_copy(nx_blk0, j, u).start()

        def down_rows(nrows):
            hrows = h_ref[pl.ds(row0, nrows), :]
            return jnp.dot(hrows, wdb_ref[...], preferred_element_type=F32) + bdn_ref[erow, :]

        @pl.when(nsub == 2)
        def _():
            _store_token_rows(y_ref, 0, down_rows(MOE_BM))

        @pl.when(nsub == 1)
        def _():
            _store_token_rows(y_ref, 0, down_rows(MOE_SUB))
            y_ref[MOE_SUB * ROW_PITCH:, :] = jnp.zeros(((MOE_BM - MOE_SUB) * ROW_PITCH, LANES), F32)

        for u, j in enumerate(ahead):
            @pl.when(j < nx_nsub)
            def _(u=u, j=j):
                x_copy(nx_blk0, j, u).wait()
                unpack(j, u)

    @pl.when(kind == STEP_TAIL)
    def _tail():
        y_ref[...] = jnp.zeros_like(y_ref)


def _schedule_body(n_blocks, cnt_ref, e_ref, f_ref, kind_ref, blk_ref, blk0_ref, nsub_ref, pre_ref, nxb_ref,
                   start_ref, pad_start_ref, pad_len_ref, nact_ref):
    n_steps = e_ref.shape[0]
    sub_per_blk = MOE_BM // MOE_SUB
    zero = jnp.int32(0)

    def put(idx, e, f, kind, blk, blk0, nsub, pre, nxb=zero):
        e_ref[idx] = e
        f_ref[idx] = f
        kind_ref[idx] = kind
        blk_ref[idx] = blk
        blk0_ref[idx] = blk0
        nsub_ref[idx] = nsub
        pre_ref[idx] = pre
        nxb_ref[idx] = nxb

    def per_expert(e, carry):
        t, blk, last_e, prev_down, prev_nb = carry
        n = cnt_ref[e]
        nblk = (n + (MOE_BM - 1)) // MOE_BM
        start_ref[e] = blk * MOE_BM
        pad_start_ref[e] = blk * MOE_BM + n
        pad_len_ref[e] = nblk * MOE_BM - n

        def per_pass(p, carry2):
            t2, pdown, pnb = carry2
            b0 = blk + p * PASS_BLOCKS
            nb = jnp.minimum(nblk - p * PASS_BLOCKS, PASS_BLOCKS)
            rows = jnp.minimum(n - p * MOE_RM, MOE_RM)
            nsub = (rows + (MOE_SUB - 1)) // MOE_SUB
            pre = jnp.minimum(nsub, pnb * PREFETCH_PER_DOWN)
            short_tail = ((nsub == 5) & (rows <= 4 * MOE_SUB + MOE_SUB // 2)).astype(jnp.int32)
            for f in range(NF):
                put(t2 + f, e, f, STEP_UP, b0, b0, nsub, pre, short_tail)

            def patch(j, c):
                pre_ref[pdown + j] = nsub
                nxb_ref[pdown + j] = b0
                return c

            lax.fori_loop(0, pnb, patch, 0)

            def per_blk(j, c):
                put(t2 + NF + j, e, NF - 1, STEP_DOWN, b0 + j, b0,
                    jnp.clip(nsub - j * sub_per_blk, 1, sub_per_blk), zero)
                return c

            lax.fori_loop(0, nb, per_blk, 0)
            return t2 + NF + nb, t2 + NF, nb

        t, prev_down, prev_nb = lax.fori_loop(0, (nblk + (PASS_BLOCKS - 1)) // PASS_BLOCKS, per_pass,
                                              (t, prev_down, prev_nb))
        return t, blk + nblk, jnp.where(n > 0, e, last_e), prev_down, prev_nb

    t, nact, last_e, _, _ = lax.fori_loop(0, N_EXPERTS, per_expert, (zero, zero, zero, zero, zero))
    nact_ref[0] = nact
    nact_ref[1] = t + n_blocks - nact

    def spare(i, c):
        is_tail = i < n_blocks - nact
        blk = jnp.where(is_tail, nact + i, n_blocks - 1)
        put(t + i, last_e, NF - 1, jnp.where(is_tail, STEP_TAIL, STEP_IDLE), blk, blk, 1, zero)
        return c

    lax.fori_loop(0, n_steps - t, spare, 0)


def _expert_schedule(counts, n_blocks, n_assign):
    n_pass_max = N_EXPERTS + n_assign // MOE_RM
    n_steps = n_pass_max * NF + n_blocks
    smem = pl.BlockSpec(memory_space=pltpu.SMEM)
    i32 = jnp.int32
    out_shape = tuple(jax.ShapeDtypeStruct((n,), i32) for n in (n_steps,) * 8 + (N_EXPERTS,) * 3 + (2,))
    outs = pl.pallas_call(
        functools.partial(_schedule_body, n_blocks),
        in_specs=[smem],
        out_specs=tuple(smem for _ in out_shape),
        out_shape=out_shape,
        name="schedule",
    )(counts)
    return outs[:8], outs[8], outs[9], outs[10], outs[11]


def _experts(tabs, xs, w_up, b_up, w_down, b_down, name):
    d = D_MODEL
    n_steps = tabs[0].shape[0]
    grid_spec = pltpu.PrefetchScalarGridSpec(
        num_scalar_prefetch=len(tabs),
        grid=(n_steps,),
        in_specs=[
            pl.BlockSpec(memory_space=pl.ANY),
            pl.BlockSpec((1, d, MOE_TF), lambda t, e, f, *_: (e[t], 0, f[t])),
            pl.BlockSpec((1, d, MOE_TF), lambda t, e, f, *_: (e[t], 0, NF + f[t])),
            _const_spec(b_up.shape),
            pl.BlockSpec((1, MOE_TF, d), lambda t, e, f, *_: (e[t], f[t], 0)),
            _const_spec(b_down.shape),
        ],
        out_specs=pl.BlockSpec((MOE_BM * ROW_PITCH, LANES), lambda t, e, f, kind, blk, *_: (blk[t], 0)),
        scratch_shapes=[
            pltpu.VMEM((MOE_RM, d), BF16),
            pltpu.VMEM((MOE_RM, D_FF), BF16),
            pltpu.VMEM((D_FF, d), BF16),
            pltpu.VMEM((2, MOE_SUB * ROW_PITCH, LANES), F32),
            pltpu.SemaphoreType.DMA((2,)),
        ],
    )
    return pl.pallas_call(
        _experts_body,
        grid_spec=grid_spec,
        out_shape=jax.ShapeDtypeStruct(xs.shape, F32),
        compiler_params=pltpu.CompilerParams(dimension_semantics=("arbitrary",),
                                             vmem_limit_bytes=VMEM_LIMIT),
        name=name,
    )(*tabs, xs, w_up, w_up, b_up, w_down, b_down)


def _combine_body(slot_ref, slot_nx_ref, gate_ref, x1_ref, gf_ref, ys_ref, o_ref, buf_ref, sem):
    step = pl.program_id(0)
    slot = step % 2
    tm = x1_ref.shape[0]

    def gather(rows_ref, s):
        def issue(i, c):
            for k in range(TOP_K):
                _rows_copy(ys_ref, rows_ref[i * TOP_K + k], buf_ref.at[s, k], i * ROW_PITCH,
                           sem.at[s]).start(priority=k % 2)
            return c

        lax.fori_loop(0, tm, issue, 0, unroll=2)

    @pl.when(step == 0)
    def _():
        gather(slot_ref, 0)

    for s in range(2):
        @pl.when((step + 1 < pl.num_programs(0)) & (slot == 1 - s))
        def _(s=s):
            gather(slot_nx_ref, s)

    def drain(i, c):
        for _ in range(WAIT_UNROLL):
            _token_copy(ys_ref, 0, buf_ref.at[slot, 0], 0, sem.at[slot]).wait()
        return c

    lax.fori_loop(0, tm * TOP_K // WAIT_UNROLL, drain, 0)
    gate = gate_ref[...]
    tiles = [x1_ref[:, j * LANES:(j + 1) * LANES] for j in range(ROW_TILES)]
    for k in range(TOP_K):
        g = gate[:, k:k + 1]
        for j, tile in enumerate(_load_token_rows(buf_ref.at[slot, k], tm)):
            tiles[j] = tiles[j] + g * tile
    o_ref[...] = _rms(jnp.concatenate(tiles, axis=1), gf_ref[...])


def _combine(slot_rows, gates, x1, gf, ys):
    t, d = x1.shape
    tm = min(TM_COMBINE, t)
    n = t // tm
    idx_blk = (tm * TOP_K,)
    return pl.pallas_call(
        _combine_body,
        grid=(n,),
        in_specs=[pl.BlockSpec(idx_blk, lambda i: (i,), memory_space=pltpu.SMEM),
                  pl.BlockSpec(idx_blk, lambda i: (jnp.minimum(i + 1, n - 1),), memory_space=pltpu.SMEM),
                  pl.BlockSpec((tm, TOP_K), lambda i: (i, 0)),
                  pl.BlockSpec((tm, d), lambda i: (i, 0)),
                  _const_spec(gf.shape),
                  pl.BlockSpec(memory_space=pl.ANY)],
        out_specs=pl.BlockSpec((tm, d), lambda i: (i, 0)),
        scratch_shapes=[pltpu.VMEM((2, TOP_K, tm * ROW_PITCH, LANES), F32), pltpu.SemaphoreType.DMA((2,))],
        out_shape=jax.ShapeDtypeStruct((t, d), F32),
        compiler_params=pltpu.CompilerParams(dimension_semantics=("arbitrary",),
                                             vmem_limit_bytes=VMEM_LIMIT),
        name="combine",
    )(slot_rows, slot_rows, gates, x1, gf, ys)


def _rope_tables(s):
    pos = np.arange(s, dtype=np.float32)
    inv_freq = np.power(np.float32(ROPE_THETA), -np.arange(0, B_ROPE_DIM, 2, dtype=np.float32) / B_ROPE_DIM)
    ang = (pos[:, None] * inv_freq[None, :]).astype(np.float32)
    cos, sin = np.cos(ang), np.sin(ang)
    pad = LANES - B_ROPE_DIM
    cos = np.concatenate([cos, cos, np.ones((s, pad), np.float32)], axis=1)
    sin = np.concatenate([sin, sin, np.zeros((s, pad), np.float32)], axis=1)
    return cos.astype(np.float32), sin.astype(np.float32)


def _window_bias():
    qi = np.arange(A_BLOCK)[:, None]
    kj = np.arange(3 * A_BLOCK)[None, :]
    dist = np.abs(qi + A_BLOCK - kj)
    slopes = np.power(np.float32(2.0), -8.0 * np.arange(1, A_HEADS + 1, dtype=np.float32) / A_HEADS)
    bias = -slopes[:, None, None] * dist.astype(np.float32)[None]
    bias = np.where((dist <= WINDOW)[None], bias, -np.inf)
    bias = bias.reshape(A_KV_HEADS, 2, 2, A_BLOCK, 3 * A_BLOCK).transpose(0, 2, 1, 3, 4)
    bias = bias.reshape(2 * A_KV_HEADS, 2 * A_BLOCK, 3 * A_BLOCK)
    no_prev = (kj < A_BLOCK)[None]
    no_next = (kj >= 2 * A_BLOCK)[None]
    ninf = -np.inf
    return np.stack([bias, np.where(no_prev, ninf, bias), np.where(no_next, ninf, bias),
                     np.where(no_prev | no_next, ninf, bias)]).astype(np.float32)


def _layer(x, attn_norm, w_in, a_sink, b_q_norm, b_w_uq, b_kv_norm, b_w_ukv, out_norm_a, out_norm_b, w_o,
           mlp_norm, w_router, b_router, w_up, b_up, w_down, b_down):
    b, s, d = x.shape
    t = b * s
    wq = b_w_uq.reshape(B_Q_RANK, B_HEADS, B_NOPE_DIM + B_ROPE_DIM)
    wq_pe = jnp.pad(wq[:, :, B_NOPE_DIM:], ((0, 0), (0, 0), (0, LANES - B_ROPE_DIM)))
    wuq = jnp.concatenate([wq[:, :, :B_NOPE_DIM].reshape(B_Q_RANK, -1), wq_pe.reshape(B_Q_RANK, -1)],
                          axis=1).astype(BF16)
    wkv = b_w_ukv.reshape(B_KV_RANK, B_HEADS, B_NOPE_DIM + B_V_DIM)
    wukv = jnp.concatenate([wkv[:, :, :B_NOPE_DIM].reshape(B_KV_RANK, -1),
                            wkv[:, :, B_NOPE_DIM:].reshape(B_KV_RANK, -1)], axis=1).astype(BF16)
    cos, sin = _rope_tables(s)

    qa, ka, va, qb, kb, vb = _prologue(x, attn_norm[None], w_in, b_q_norm[None], wuq, b_kv_norm[None], wukv,
                                       cos, sin)
    ya = _window_attn(qa, ka, va, a_sink, _window_bias())
    yb = _mla_attn(qb, kb, vb)
    wr_hi = w_router.astype(BF16)
    wr_lo = (w_router - wr_hi.astype(F32)).astype(BF16)
    x1, xm, eidx, gates, rank, counts = _out_router(
        ya.reshape(t, A_WIDTH), yb.reshape(t, B_WIDTH), x.reshape(t, d), out_norm_a[None], out_norm_b[None],
        w_o.astype(BF16), mlp_norm[None], jnp.concatenate([wr_hi, wr_lo], axis=1), b_router[None])

    nblk = (t * TOP_K + N_EXPERTS * (MOE_BM - 1) + MOE_BM - 1) // MOE_BM
    tabs, start, pad_start, pad_len, nact = _expert_schedule(counts.reshape(N_EXPERTS), nblk, t * TOP_K)
    slot_rows = _slot_rows(start, eidx.reshape(t * TOP_K), rank.reshape(t * TOP_K))
    xs = _dispatch(pad_start, pad_len, nact, slot_rows, xm, nblk * MOE_BM)
    n_short = N_EXPERTS * NF + nblk
    args = (xs, w_up, b_up, w_down, b_down)
    ys = lax.cond(nact[1] <= n_short,
                  lambda: _experts(tuple(a[:n_short] for a in tabs), *args, name="experts"),
                  lambda: _experts(tabs, *args, name="experts_long"))
    return x1, slot_rows, gates, ys


def kernel(x, attn_norm, w_in, a_sink, b_q_norm, b_w_uq, b_kv_norm, b_w_ukv, out_norm_a, out_norm_b, w_o,
           mlp_norm, w_router, b_router, w_up, b_up, w_down, b_down, final_norm):
    b, s, d = x.shape
    assert d == D_MODEL and s % min(TQ_MLA, s) == 0 and s % min(CK_MLA, s) == 0 and s % TM_PRO == 0 and attn_norm.shape[0] == 1
    x1, slot_rows, gates, ys = _layer(
        x, attn_norm[0], w_in[0], a_sink[0], b_q_norm[0], b_w_uq[0], b_kv_norm[0], b_w_ukv[0], out_norm_a[0],
        out_norm_b[0], w_o[0], mlp_norm[0], w_router[0], b_router[0], w_up[0], b_up[0], w_down[0], b_down[0])
    out = _combine(slot_rows, gates, x1, final_norm[None], ys)
    return out.reshape(b, s, d)
```

```python
import functools

import jax
import numpy as np
import jax.numpy as jnp
from jax import lax
from jax.experimental import pallas as pl
from jax.experimental.pallas import tpu as pltpu

D_MODEL = 2048
A_HEADS, A_KV_HEADS, A_HEAD_DIM = 16, 4, 64
A_GROUP = A_HEADS // A_KV_HEADS
WINDOW = 128
A_BLOCK = 128
B_HEADS, B_Q_RANK, B_KV_RANK = 8, 512, 256
B_NOPE_DIM, B_ROPE_DIM, B_V_DIM = 128, 64, 128
ROPE_THETA = 10000.0
A_WIDTH = A_HEADS * A_HEAD_DIM
B_WIDTH = B_HEADS * B_V_DIM
A_KV_COLS = A_KV_HEADS * A_HEAD_DIM
N_EXPERTS, TOP_K, D_FF = 32, 4, 2048
SWIGLU_ALPHA, SWIGLU_LIMIT = 1.702, 7.0
EPS = 1e-5

LANES = 128
SUBLANES = 8
B_QK_PAD = 2 * LANES
IN_COLS_PAD = A_WIDTH + 2 * A_KV_COLS + B_Q_RANK + B_KV_RANK + LANES
VMEM_LIMIT = 56 * 1024 * 1024
ROW_TILES = D_MODEL // LANES
ROW_PITCH = ROW_TILES + 1

TM_PRO = 256
TQ_MLA = 1024
CK_MLA = 2048
HEADS_MLA = 2
TM_OUT = 512
TM_DISPATCH = 1024
TM_COMBINE = 256
MOE_BM = 512
MOE_SUB = 256
MOE_RM = 2048
MOE_TF = 256
WAIT_UNROLL = 16

BF16 = jnp.bfloat16
F32 = jnp.float32


def _rms(x, g):
    return x * lax.rsqrt(jnp.mean(x * x, axis=-1, keepdims=True) + EPS) * g


def _const_spec(shape):
    nd = len(shape)
    return pl.BlockSpec(shape, lambda *_: (0,) * nd, pipeline_mode=pl.Buffered(1))


def _store_token_rows(ref, first, val):
    n = val.shape[0]
    for j in range(ROW_TILES):
        ref[pl.ds(first * ROW_PITCH + j, n, stride=ROW_PITCH), :] = val[:, j * LANES:(j + 1) * LANES]
    ref[pl.ds(first * ROW_PITCH + ROW_TILES, n, stride=ROW_PITCH), :] = jnp.zeros((n, LANES), val.dtype)


def _load_token_rows(ref, n):
    return [ref[pl.ds(j, n, stride=ROW_PITCH), :] for j in range(ROW_TILES)]


def _rope(x, cos, sin):
    lane = lax.broadcasted_iota(jnp.int32, x.shape, 1)
    up = pltpu.roll(x, LANES - B_ROPE_DIM // 2, 1)
    dn = pltpu.roll(x, B_ROPE_DIM // 2, 1)
    sw = jnp.where(lane < B_ROPE_DIM // 2, -up, jnp.where(lane < B_ROPE_DIM, dn, 0.0))
    return x * cos + sw * sin


def _prologue_body(x_ref, g_ref, win_ref, qn_ref, wuq_ref, kvn_ref, wukv_ref, cos_ref, sin_ref,
                   qa_ref, ka_ref, va_ref, qb_ref, kb_ref, vb_ref, winb_ref):
    @pl.when((pl.program_id(0) == 0) & (pl.program_id(1) == 0))
    def _():
        n_in = win_ref.shape[1]
        winb_ref[:, :n_in] = win_ref[...].astype(BF16)
        winb_ref[:, n_in:] = jnp.zeros((winb_ref.shape[0], IN_COLS_PAD - n_in), BF16)

    x = x_ref[0]
    xn = _rms(x, g_ref[...]).astype(BF16)
    proj = jnp.dot(xn, winb_ref[...], preferred_element_type=F32)
    c0 = A_WIDTH
    c1 = c0 + A_KV_COLS
    c2 = c1 + A_KV_COLS
    c3 = c2 + B_Q_RANK
    c4 = c3 + B_KV_RANK
    qa_ref[0] = (proj[:, :A_WIDTH] * A_HEAD_DIM ** -0.5).astype(BF16)
    low = lax.broadcasted_iota(jnp.int32, (proj.shape[0], LANES), 1) < A_HEAD_DIM
    for src, dst in ((c0, ka_ref), (c1, va_ref)):
        for pair in range(A_KV_HEADS // 2):
            two = proj[:, src + pair * LANES:src + (pair + 1) * LANES]
            swapped = pltpu.roll(two, A_HEAD_DIM, 1)
            dst[0, 4 * pair + 0] = jnp.where(low, two, 0.0).astype(BF16)
            dst[0, 4 * pair + 1] = jnp.where(low, 0.0, swapped).astype(BF16)
            dst[0, 4 * pair + 2] = jnp.where(low, swapped, 0.0).astype(BF16)
            dst[0, 4 * pair + 3] = jnp.where(low, 0.0, two).astype(BF16)
    cq = _rms(proj[:, c2:c3], qn_ref[...]).astype(BF16)
    q = jnp.dot(cq, wuq_ref[...], preferred_element_type=F32)
    ckv = _rms(proj[:, c3:c4], kvn_ref[...]).astype(BF16)
    kv = jnp.dot(ckv, wukv_ref[...], preferred_element_type=F32)
    cos = cos_ref[...]
    sin = sin_ref[...]
    kpe = _rope(proj[:, c4:c4 + LANES], cos, sin).astype(BF16)
    b_scale = (B_NOPE_DIM + B_ROPE_DIM) ** -0.5
    hw = B_HEADS * LANES
    for h in range(B_HEADS):
        sl = slice(h * LANES, (h + 1) * LANES)
        qb_ref[0, h, :, :LANES] = (q[:, sl] * b_scale).astype(BF16)
        qpe = _rope(q[:, hw + h * LANES:hw + (h + 1) * LANES], cos, sin)
        qb_ref[0, h, :, LANES:] = (qpe * b_scale).astype(BF16)
        kb_ref[0, h, :, :LANES] = kv[:, sl].astype(BF16)
        kb_ref[0, h, :, LANES:] = kpe
        vb_ref[0, h] = kv[:, hw + h * LANES:hw + (h + 1) * LANES].astype(BF16)


def _prologue(x, g, win, qn, wuq, kvn, wukv, cos, sin):
    b, s, d = x.shape
    tm = TM_PRO
    grid = (b, s // tm)
    sds = jax.ShapeDtypeStruct
    out_shape = (
        sds((b, s, A_WIDTH), BF16),
        sds((b, 2 * A_KV_HEADS, s, LANES), BF16),
        sds((b, 2 * A_KV_HEADS, s, LANES), BF16),
        sds((b, B_HEADS, s, B_QK_PAD), BF16),
        sds((b, B_HEADS, s, B_QK_PAD), BF16),
        sds((b, B_HEADS, s, B_V_DIM), BF16),
    )

    def hspec(nh, w):
        return pl.BlockSpec((1, nh, tm, w), lambda bi, i: (bi, 0, i, 0))

    return pl.pallas_call(
        _prologue_body,
        grid=grid,
        in_specs=[
            pl.BlockSpec((1, tm, d), lambda bi, i: (bi, i, 0)),
            _const_spec(g.shape), _const_spec(win.shape), _const_spec(qn.shape), _const_spec(wuq.shape),
            _const_spec(kvn.shape), _const_spec(wukv.shape),
            pl.BlockSpec((tm, LANES), lambda bi, i: (i, 0)),
            pl.BlockSpec((tm, LANES), lambda bi, i: (i, 0)),
        ],
        out_specs=(pl.BlockSpec((1, tm, A_WIDTH), lambda bi, i: (bi, i, 0)),
                   hspec(2 * A_KV_HEADS, LANES), hspec(2 * A_KV_HEADS, LANES),
                   hspec(B_HEADS, B_QK_PAD), hspec(B_HEADS, B_QK_PAD), hspec(B_HEADS, B_V_DIM)),
        out_shape=out_shape,
        scratch_shapes=[pltpu.VMEM((d, IN_COLS_PAD), BF16)],
        compiler_params=pltpu.CompilerParams(dimension_semantics=("arbitrary", "arbitrary"),
                                             vmem_limit_bytes=VMEM_LIMIT),
        name="prologue",
    )(x, g, win, qn, wuq, kvn, wukv, cos, sin)


def _window_body(sink_ref, q_ref, kp_ref, kc_ref, kn_ref, vp_ref, vc_ref, vn_ref, bias_ref, o_ref):
    two = 2 * A_BLOCK
    low = lax.broadcasted_iota(jnp.int32, (two, LANES), 1) < A_HEAD_DIM
    for hk in range(A_KV_HEADS):
        q2 = jnp.concatenate([q_ref[0, :, (2 * hk) * LANES:(2 * hk + 1) * LANES],
                              q_ref[0, :, (2 * hk + 1) * LANES:(2 * hk + 2) * LANES]], axis=0)
        acc = jnp.zeros((two, LANES), F32)
        inv = []
        for half in range(2):
            z = 2 * hk + half
            kband = jnp.concatenate([kp_ref[0, z], kc_ref[0, z], kn_ref[0, z]], axis=0)
            vband = jnp.concatenate([vp_ref[0, z], vc_ref[0, z], vn_ref[0, z]], axis=0)
            s = lax.dot_general(q2, kband, (((1,), (1,)), ((), ())), preferred_element_type=F32)
            s = s + bias_ref[0, z]
            sink = jnp.concatenate(
                [jnp.full((A_BLOCK, 1), sink_ref[hk * A_GROUP + 2 * jj + half], F32) for jj in range(2)], axis=0)
            m = jnp.maximum(jnp.max(s, axis=-1, keepdims=True), sink)
            p = jnp.exp(s - m)
            den = jnp.sum(p, axis=-1, keepdims=True) + jnp.exp(sink - m)
            inv.append(1.0 / den)
            acc = acc + jnp.dot(p.astype(BF16), vband, preferred_element_type=F32)
        o = (acc * jnp.where(low, inv[0], inv[1])).astype(BF16)
        o_ref[0, :, (2 * hk) * LANES:(2 * hk + 1) * LANES] = o[:A_BLOCK]
        o_ref[0, :, (2 * hk + 1) * LANES:(2 * hk + 2) * LANES] = o[A_BLOCK:]


def _window_attn(qa, ka, va, sink, bias):
    b, s, _ = qa.shape
    nb = s // A_BLOCK
    kv_blk = (1, 2 * A_KV_HEADS, A_BLOCK, LANES)
    prev = pl.BlockSpec(kv_blk, lambda bi, n: (bi, 0, jnp.maximum(n - 1, 0), 0))
    cur = pl.BlockSpec(kv_blk, lambda bi, n: (bi, 0, n, 0))
    nxt = pl.BlockSpec(kv_blk, lambda bi, n: (bi, 0, jnp.minimum(n + 1, nb - 1), 0))

    def edge(bi, n):
        return ((n == 0).astype(jnp.int32) + 2 * (n == nb - 1).astype(jnp.int32), 0, 0, 0)

    return pl.pallas_call(
        _window_body,
        grid=(b, nb),
        in_specs=[
            pl.BlockSpec(memory_space=pltpu.SMEM),
            pl.BlockSpec((1, A_BLOCK, A_WIDTH), lambda bi, n: (bi, n, 0)),
            prev, cur, nxt, prev, cur, nxt,
            pl.BlockSpec((1,) + bias.shape[1:], edge),
        ],
        out_specs=pl.BlockSpec((1, A_BLOCK, A_WIDTH), lambda bi, n: (bi, n, 0)),
        out_shape=jax.ShapeDtypeStruct((b, s, A_WIDTH), BF16),
        compiler_params=pltpu.CompilerParams(dimension_semantics=("parallel", "parallel"),
                                             vmem_limit_bytes=VMEM_LIMIT),
        name="window_attn",
    )(sink, qa, ka, ka, ka, va, va, va, bias)


def _mla_body(q_ref, k_ref, v_ref, o_ref):
    tq = q_ref.shape[2]
    s_len = k_ref.shape[2]
    ck = min(CK_MLA, s_len)
    for hh in range(HEADS_MLA):
        q = q_ref[0, hh]
        m = jnp.full((tq, 1), -jnp.inf, F32)
        l = jnp.zeros((tq, 1), F32)
        acc = jnp.zeros((tq, B_V_DIM), F32)
        for c in range(s_len // ck):
            k_c = k_ref[0, hh, c * ck:(c + 1) * ck, :]
            v_c = v_ref[0, hh, c * ck:(c + 1) * ck, :]
            s = lax.dot_general(q, k_c, (((1,), (1,)), ((), ())), preferred_element_type=F32)
            m_new = jnp.maximum(m, jnp.max(s, axis=-1, keepdims=True))
            alpha = jnp.exp(m - m_new)
            p = jnp.exp(s - m_new)
            l = alpha * l + jnp.sum(p, axis=-1, keepdims=True)
            acc = alpha * acc + jnp.dot(p.astype(BF16), v_c, preferred_element_type=F32)
            m = m_new
        o_ref[0, :, hh * B_V_DIM:(hh + 1) * B_V_DIM] = (acc / l).astype(BF16)


def _mla_attn(qb, kb, vb):
    b, nh, s, _ = qb.shape
    tq = min(TQ_MLA, s)
    hs = HEADS_MLA
    return pl.pallas_call(
        _mla_body,
        grid=(b, nh // hs, s // tq),
        in_specs=[
            pl.BlockSpec((1, hs, tq, B_QK_PAD), lambda bi, h, i: (bi, h, i, 0)),
            pl.BlockSpec((1, hs, s, B_QK_PAD), lambda bi, h, i: (bi, h, 0, 0)),
            pl.BlockSpec((1, hs, s, B_V_DIM), lambda bi, h, i: (bi, h, 0, 0)),
        ],
        out_specs=pl.BlockSpec((1, tq, hs * B_V_DIM), lambda bi, h, i: (bi, i, h)),
        out_shape=jax.ShapeDtypeStruct((b, s, B_WIDTH), BF16),
        compiler_params=pltpu.CompilerParams(dimension_semantics=("parallel", "parallel", "parallel"),
                                             vmem_limit_bytes=VMEM_LIMIT),
        name="mla_attn",
    )(qb, kb, vb)


def _out_router_body(ya_ref, yb_ref, x_ref, ga_ref, gb_ref, wo_ref, gm_ref, wr_ref, br_ref,
                     x1_ref, xm_ref, eidx_ref, gate_ref, rank_ref, cnt_ref, run_ref):
    i = pl.program_id(0)

    @pl.when(i == 0)
    def _():
        run_ref[...] = jnp.zeros_like(run_ref)

    na = _rms(ya_ref[...].astype(F32), ga_ref[...]).astype(BF16)
    nb = _rms(yb_ref[...].astype(F32), gb_ref[...]).astype(BF16)
    att = jnp.dot(na, wo_ref[:A_WIDTH, :], preferred_element_type=F32)
    att = att + jnp.dot(nb, wo_ref[A_WIDTH:, :], preferred_element_type=F32)
    x1 = x_ref[...] + att
    x1_ref[...] = x1
    hn = _rms(x1, gm_ref[...])
    _store_token_rows(xm_ref, 0, hn)
    tm = hn.shape[0]
    hi = hn.astype(BF16)
    lo = (hn - hi.astype(F32)).astype(BF16)
    prod = jnp.dot(jnp.concatenate([hi, lo], axis=0), wr_ref[...], preferred_element_type=F32)
    logits = (prod[:tm, :N_EXPERTS] + prod[:tm, N_EXPERTS:] + prod[tm:, :N_EXPERTS] + prod[tm:, N_EXPERTS:]
              + br_ref[...])
    lane = lax.broadcasted_iota(jnp.int32, (tm, N_EXPERTS), 1)
    work = logits
    sel = jnp.zeros((tm, N_EXPERTS), F32)
    hots, vals, idxs = [], [], []
    for _k in range(TOP_K):
        mx = jnp.max(work, axis=-1, keepdims=True)
        idx = jnp.min(jnp.where(work == mx, lane, N_EXPERTS), axis=-1, keepdims=True)
        hot = lane == idx
        hots.append(hot)
        vals.append(mx)
        idxs.append(idx)
        sel = sel + hot.astype(F32)
        work = jnp.where(hot, -jnp.inf, work)
    exps = [jnp.exp(v - vals[0]) for v in vals]
    den = exps[0] + exps[1] + exps[2] + exps[3]
    r_i = lax.broadcasted_iota(jnp.int32, (tm, tm), 0)
    c_i = lax.broadcasted_iota(jnp.int32, (tm, tm), 1)
    tri = (c_i < r_i).astype(BF16)
    before = jnp.dot(tri, sel.astype(BF16), preferred_element_type=F32) + run_ref[...]
    lane4 = lax.broadcasted_iota(jnp.int32, (tm, TOP_K), 1)
    eidx = jnp.zeros((tm, TOP_K), jnp.int32)
    gate = jnp.zeros((tm, TOP_K), F32)
    rank = jnp.zeros((tm, TOP_K), jnp.int32)
    for k in range(TOP_K):
        rk = jnp.sum(jnp.where(hots[k], before, 0.0), axis=-1, keepdims=True).astype(jnp.int32)
        eidx = jnp.where(lane4 == k, idxs[k], eidx)
        gate = jnp.where(lane4 == k, exps[k] / den, gate)
        rank = jnp.where(lane4 == k, rk, rank)
    eidx_ref[...] = eidx
    gate_ref[...] = gate
    rank_ref[...] = rank
    run = run_ref[...] + jnp.sum(sel, axis=0, keepdims=True)
    run_ref[...] = run
    cnt_ref[...] = run.astype(jnp.int32)


def _out_router(ya, yb, x2d, ga, gb, wo, gm, wr, br):
    t, d = x2d.shape
    tm = min(TM_OUT, t)
    sds = jax.ShapeDtypeStruct

    def row(w):
        return pl.BlockSpec((tm, w), lambda i: (i, 0))

    return pl.pallas_call(
        _out_router_body,
        grid=(t // tm,),
        in_specs=[row(A_WIDTH), row(B_WIDTH), row(d), _const_spec(ga.shape), _const_spec(gb.shape),
                  _const_spec(wo.shape), _const_spec(gm.shape), _const_spec(wr.shape), _const_spec(br.shape)],
        out_specs=(row(d), pl.BlockSpec((tm * ROW_PITCH, LANES), lambda i: (i, 0)),
                   row(TOP_K), row(TOP_K), row(TOP_K),
                   pl.BlockSpec((1, N_EXPERTS), lambda i: (0, 0))),
        out_shape=(sds((t, d), F32), sds((t * ROW_PITCH, LANES), F32), sds((t, TOP_K), jnp.int32),
                   sds((t, TOP_K), F32),
                   sds((t, TOP_K), jnp.int32), sds((1, N_EXPERTS), jnp.int32)),
        scratch_shapes=[pltpu.VMEM((1, N_EXPERTS), F32)],
        compiler_params=pltpu.CompilerParams(dimension_semantics=("arbitrary",),
                                             vmem_limit_bytes=VMEM_LIMIT),
        name="out_router",
    )(ya, yb, x2d, ga, gb, wo, gm, wr, br)


def _token_copy(src_ref, src_tok, dst_ref, dst_tok, sem, ntok=1):
    return _rows_copy(src_ref, src_tok * ROW_PITCH, dst_ref, dst_tok * ROW_PITCH, sem, ntok)


def _rows_copy(src_ref, src_row, dst_ref, dst_row, sem, ntok=1):
    rows = ntok * ROW_PITCH
    return pltpu.make_async_copy(src_ref.at[pl.ds(src_row, rows)], dst_ref.at[pl.ds(dst_row, rows)], sem)


def _slot_rows_body(start_ref, eidx_ref, rank_ref, o_ref):
    e = eidx_ref[...]
    base = jnp.zeros_like(e)
    for ex in range(N_EXPERTS):
        base = jnp.where(e == ex, start_ref[ex], base)
    o_ref[...] = (base + rank_ref[...]) * ROW_PITCH


def _slot_rows(start, eidx, rank):
    n = eidx.shape[0]
    shape2 = (n // LANES, LANES)
    out = pl.pallas_call(
        _slot_rows_body,
        in_specs=[pl.BlockSpec(memory_space=pltpu.SMEM), pl.BlockSpec(shape2, lambda: (0, 0)),
                  pl.BlockSpec(shape2, lambda: (0, 0))],
        out_specs=pl.BlockSpec(shape2, lambda: (0, 0)),
        out_shape=jax.ShapeDtypeStruct(shape2, jnp.int32),
        name="slot_rows",
    )(start, eidx.reshape(shape2), rank.reshape(shape2))
    return out.reshape(n)


def _zero_fill_pads(pad_start_ref, pad_len_ref, nact_ref, xs_ref, zero_ref, zsem):
    bm = zero_ref.shape[0] // ROW_PITCH
    nblk = xs_ref.shape[0] // (bm * ROW_PITCH)
    zero_ref[...] = jnp.zeros_like(zero_ref)

    def sweep(do):
        def per_expert(e, c):
            off = pad_start_ref[e]
            n = pad_len_ref[e]
            p = bm // 2
            while p >= 1:
                take = (n & p) != 0

                @pl.when(take)
                def _(p=p, off=off):
                    do(_token_copy(zero_ref, 0, xs_ref, off, zsem, p))

                off = off + jnp.where(take, p, 0)
                p //= 2
            return c

        lax.fori_loop(0, N_EXPERTS, per_expert, 0)

        def tail(j, c):
            do(_token_copy(zero_ref, 0, xs_ref, j * bm, zsem, bm))
            return c

        lax.fori_loop(nact_ref[0], nblk, tail, 0)

    sweep(lambda cp: cp.start())
    sweep(lambda cp: cp.wait())


def _dispatch_body(pad_start_ref, pad_len_ref, nact_ref, slot_ref, xm_ref, xs_ref, zero_ref, sem, zsem):
    tm = slot_ref.shape[0] // TOP_K

    @pl.when(pl.program_id(0) == 0)
    def _():
        _zero_fill_pads(pad_start_ref, pad_len_ref, nact_ref, xs_ref, zero_ref, zsem)

    def issue(i, c):
        for k in range(TOP_K):
            _rows_copy(xm_ref, i * ROW_PITCH, xs_ref, slot_ref[i * TOP_K + k], sem).start(priority=k % 2)
        return c

    lax.fori_loop(0, tm, issue, 0, unroll=2)

    def drain(i, c):
        for _ in range(WAIT_UNROLL):
            _token_copy(xm_ref, 0, xs_ref, 0, sem).wait()
        return c

    lax.fori_loop(0, tm * TOP_K // WAIT_UNROLL, drain, 0)


def _dispatch(pad_start, pad_len, nact, slot_rows, xm, n_slots):
    t = xm.shape[0] // ROW_PITCH
    tm = min(TM_DISPATCH, t)
    grid_spec = pltpu.PrefetchScalarGridSpec(
        num_scalar_prefetch=3,
        grid=(t // tm,),
        in_specs=[pl.BlockSpec((tm * TOP_K,), lambda i, *_: (i,), memory_space=pltpu.SMEM),
                  pl.BlockSpec((tm * ROW_PITCH, LANES), lambda i, *_: (i, 0))],
        out_specs=pl.BlockSpec(memory_space=pl.ANY),
        scratch_shapes=[pltpu.VMEM((MOE_BM * ROW_PITCH, LANES), F32), pltpu.SemaphoreType.DMA(()),
                        pltpu.SemaphoreType.DMA(())],
    )
    return pl.pallas_call(
        _dispatch_body,
        grid_spec=grid_spec,
        out_shape=jax.ShapeDtypeStruct((n_slots * ROW_PITCH, LANES), F32),
        compiler_params=pltpu.CompilerParams(dimension_semantics=("arbitrary",), has_side_effects=True,
                                             vmem_limit_bytes=VMEM_LIMIT),
        name="dispatch",
    )(pad_start, pad_len, nact, slot_rows, xm)


STEP_UP, STEP_DOWN, STEP_TAIL, STEP_IDLE = 0, 1, 2, 3
PREFETCH_PER_DOWN = 2
NF = D_FF // MOE_TF
PASS_BLOCKS = MOE_RM // MOE_BM


def _swiglu(hg, hu):
    gate = jnp.minimum(hg, SWIGLU_LIMIT)
    up = jnp.clip(hu, -SWIGLU_LIMIT, SWIGLU_LIMIT)
    glu = gate / (1.0 + jnp.exp(-SWIGLU_ALPHA * gate))
    return (up + 1.0) * glu


def _experts_body(e_ref, f_ref, kind_ref, blk_ref, blk0_ref, nsub_ref, pre_ref, nxb_ref,
                  xs_ref, wg_ref, wu_ref, bup_ref, wd_ref, bdn_ref,
                  y_ref, xb_ref, h_ref, wdb_ref, stg_ref, sem):
    t = pl.program_id(0)
    kind = kind_ref[t]
    f = f_ref[t]
    nsub = nsub_ref[t]
    erow = pl.ds(e_ref[t], 1)
    ftile = pl.multiple_of(f * MOE_TF, MOE_TF)

    def x_copy(first_blk, j, slot):
        rows = MOE_SUB * ROW_PITCH
        first = pl.multiple_of((first_blk * MOE_BM + j * MOE_SUB) * ROW_PITCH, SUBLANES)
        return pltpu.make_async_copy(xs_ref.at[pl.ds(first, rows)], stg_ref.at[slot], sem.at[slot])

    def unpack(j, slot):
        rows = pl.ds(pl.multiple_of(j * MOE_SUB, MOE_SUB), MOE_SUB)
        for jt, tile in enumerate(_load_token_rows(stg_ref.at[slot], MOE_SUB)):
            xb_ref[rows, jt * LANES:(jt + 1) * LANES] = tile.astype(BF16)

    @pl.when(kind == STEP_UP)
    def _up():
        pre = pre_ref[t]

        @pl.when((f == 0) & (pre < nsub))
        def _load_rows():
            blk0 = blk0_ref[t]
            x_copy(blk0, pre, pre % 2).start()

            def body(j, c):
                slot = j % 2

                @pl.when(j + 1 < nsub)
                def _():
                    x_copy(blk0, j + 1, 1 - slot).start()

                x_copy(blk0, j, slot).wait()
                unpack(j, slot)
                return c

            lax.fori_loop(pre, nsub, body, 0)

        wdb_ref[pl.ds(pl.multiple_of(f * MOE_TF, MOE_TF), MOE_TF), :] = wd_ref[0].astype(BF16)

        def up_rows(row0, nrows):
            x = xb_ref[pl.ds(row0, nrows), :]
            hg = jnp.dot(x, wg_ref[0].astype(BF16), preferred_element_type=F32) + bup_ref[erow, pl.ds(ftile, MOE_TF)]
            hu = (jnp.dot(x, wu_ref[0].astype(BF16), preferred_element_type=F32)
                  + bup_ref[erow, pl.ds(pl.multiple_of(D_FF + ftile, MOE_TF), MOE_TF)])
            act = _swiglu(hg, hu).astype(BF16)
            h_ref[pl.ds(row0, nrows), pl.ds(pl.multiple_of(f * MOE_TF, MOE_TF), MOE_TF)] = act

        quad = 4 * MOE_SUB
        nquad = nsub // 4
        rem = nsub % 4

        def quad_rows(i, c):
            base = pl.multiple_of(i * quad, quad)
            up_rows(base, quad // 2)
            up_rows(base + quad // 2, quad // 2)
            return c

        merged = nsub == 5
        tail_class = nxb_ref[t]

        @pl.when(merged & (tail_class == 0))
        def _():
            up_rows(0, 2 * MOE_SUB)
            up_rows(2 * MOE_SUB, 3 * MOE_SUB)

        for cls, tail in ((1, MOE_SUB // 2), (2, MOE_SUB // 4)):
            @pl.when(merged & (tail_class == cls))
            def _(tail=tail):
                up_rows(0, 2 * MOE_SUB)
                up_rows(2 * MOE_SUB, 2 * MOE_SUB + tail)
                h_ref[4 * MOE_SUB + tail:5 * MOE_SUB, pl.ds(ftile, MOE_TF)] = jnp.zeros(
                    (MOE_SUB - tail, MOE_TF), BF16)

        @pl.when(jnp.logical_not(merged))
        def _():
            lax.fori_loop(0, nquad, quad_rows, 0)

            @pl.when(rem >= 2)
            def _():
                up_rows(pl.multiple_of(nquad * quad, quad), 2 * MOE_SUB)

            @pl.when(rem % 2 == 1)
            def _():
                up_rows(pl.multiple_of(nquad * quad + (rem // 2) * 2 * MOE_SUB, MOE_SUB), MOE_SUB)

    @pl.when(kind == STEP_DOWN)
    def _down():
        jblk = blk_ref[t] - blk0_ref[t]
        row0 = pl.multiple_of(jblk * MOE_BM, MOE_BM)
        nx_nsub = pre_ref[t]
        nx_blk0 = nxb_ref[t]
        ahead = [jblk * PREFETCH_PER_DOWN + u for u in range(PREFETCH_PER_DOWN)]
        for u, j in enumerate(ahead):
            @pl.when(j < nx_nsub)
            def _(u=u, j=j):
                x_copy(nx_blk0, j, u).start()

        def down_rows(nrows):
            hrows = h_ref[pl.ds(row0, nrows), :]
            return jnp.dot(hrows, wdb_ref[...], preferred_element_type=F32) + bdn_ref[erow, :]

        @pl.when(nsub == 2)
        def _():
            _store_token_rows(y_ref, 0, down_rows(MOE_BM))

        @pl.when(nsub == 1)
        def _():
            _store_token_rows(y_ref, 0, down_rows(MOE_SUB))
            y_ref[MOE_SUB * ROW_PITCH:, :] = jnp.zeros(((MOE_BM - MOE_SUB) * ROW_PITCH, LANES), F32)

        for u, j in enumerate(ahead):
            @pl.when(j < nx_nsub)
            def _(u=u, j=j):
                x_copy(nx_blk0, j, u).wait()
                unpack(j, u)

    @pl.when(kind == STEP_TAIL)
    def _tail():
        y_ref[...] = jnp.zeros_like(y_ref)


def _schedule_body(n_blocks, cnt_ref, e_ref, f_ref, kind_ref, blk_ref, blk0_ref, nsub_ref, pre_ref, nxb_ref,
                   start_ref, pad_start_ref, pad_len_ref, nact_ref):
    n_steps = e_ref.shape[0]
    sub_per_blk = MOE_BM // MOE_SUB
    zero = jnp.int32(0)

    def put(idx, e, f, kind, blk, blk0, nsub, pre, nxb=zero):
        e_ref[idx] = e
        f_ref[idx] = f
        kind_ref[idx] = kind
        blk_ref[idx] = blk
        blk0_ref[idx] = blk0
        nsub_ref[idx] = nsub
        pre_ref[idx] = pre
        nxb_ref[idx] = nxb

    def per_expert(e, carry):
        t, blk, last_e, prev_down, prev_nb = carry
        n = cnt_ref[e]
        nblk = (n + (MOE_BM - 1)) // MOE_BM
        start_ref[e] = blk * MOE_BM
        pad_start_ref[e] = blk * MOE_BM + n
        pad_len_ref[e] = nblk * MOE_BM - n

        def per_pass(p, carry2):
            t2, pdown, pnb = carry2
            b0 = blk + p * PASS_BLOCKS
            nb = jnp.minimum(nblk - p * PASS_BLOCKS, PASS_BLOCKS)
            rows = jnp.minimum(n - p * MOE_RM, MOE_RM)
            nsub = (rows + (MOE_SUB - 1)) // MOE_SUB
            pre = jnp.minimum(nsub, pnb * PREFETCH_PER_DOWN)
            tail_rows = rows - 4 * MOE_SUB
            tail_class = jnp.where(nsub != 5, 0, (tail_rows <= MOE_SUB // 2).astype(jnp.int32)
                                   + (tail_rows <= MOE_SUB // 4).astype(jnp.int32))
            for f in range(NF):
                put(t2 + f, e, f, STEP_UP, b0, b0, nsub, pre, tail_class)

            def patch(j, c):
                pre_ref[pdown + j] = nsub
                nxb_ref[pdown + j] = b0
                return c

            lax.fori_loop(0, pnb, patch, 0)

            def per_blk(j, c):
                put(t2 + NF + j, e, NF - 1, STEP_DOWN, b0 + j, b0,
                    jnp.clip(nsub - j * sub_per_blk, 1, sub_per_blk), zero)
                return c

            lax.fori_loop(0, nb, per_blk, 0)
            return t2 + NF + nb, t2 + NF, nb

        t, prev_down, prev_nb = lax.fori_loop(0, (nblk + (PASS_BLOCKS - 1)) // PASS_BLOCKS, per_pass,
                                              (t, prev_down, prev_nb))
        return t, blk + nblk, jnp.where(n > 0, e, last_e), prev_down, prev_nb

    t, nact, last_e, _, _ = lax.fori_loop(0, N_EXPERTS, per_expert, (zero, zero, zero, zero, zero))
    nact_ref[0] = nact
    nact_ref[1] = t + n_blocks - nact

    def spare(i, c):
        is_tail = i < n_blocks - nact
        blk = jnp.where(is_tail, nact + i, n_blocks - 1)
        put(t + i, last_e, NF - 1, jnp.where(is_tail, STEP_TAIL, STEP_IDLE), blk, blk, 1, zero)
        return c

    lax.fori_loop(0, n_steps - t, spare, 0)


def _expert_schedule(counts, n_blocks, n_assign):
    n_pass_max = N_EXPERTS + n_assign // MOE_RM
    n_steps = n_pass_max * NF + n_blocks
    smem = pl.BlockSpec(memory_space=pltpu.SMEM)
    i32 = jnp.int32
    out_shape = tuple(jax.ShapeDtypeStruct((n,), i32) for n in (n_steps,) * 8 + (N_EXPERTS,) * 3 + (2,))
    outs = pl.pallas_call(
        functools.partial(_schedule_body, n_blocks),
        in_specs=[smem],
        out_specs=tuple(smem for _ in out_shape),
        out_shape=out_shape,
        name="schedule",
    )(counts)
    return outs[:8], outs[8], outs[9], outs[10], outs[11]


def _experts(tabs, xs, w_up, b_up, w_down, b_down, n_steps, name):
    d = D_MODEL
    grid_spec = pltpu.PrefetchScalarGridSpec(
        num_scalar_prefetch=len(tabs),
        grid=(n_steps,),
        in_specs=[
            pl.BlockSpec(memory_space=pl.ANY),
            pl.BlockSpec((1, d, MOE_TF), lambda t, e, f, *_: (e[t], 0, f[t])),
            pl.BlockSpec((1, d, MOE_TF), lambda t, e, f, *_: (e[t], 0, NF + f[t])),
            _const_spec(b_up.shape),
            pl.BlockSpec((1, MOE_TF, d), lambda t, e, f, *_: (e[t], f[t], 0)),
            _const_spec(b_down.shape),
        ],
        out_specs=pl.BlockSpec((MOE_BM * ROW_PITCH, LANES), lambda t, e, f, kind, blk, *_: (blk[t], 0)),
        scratch_shapes=[
            pltpu.VMEM((MOE_RM, d), BF16),
            pltpu.VMEM((MOE_RM, D_FF), BF16),
            pltpu.VMEM((D_FF, d), BF16),
            pltpu.VMEM((2, MOE_SUB * ROW_PITCH, LANES), F32),
            pltpu.SemaphoreType.DMA((2,)),
        ],
    )
    return pl.pallas_call(
        _experts_body,
        grid_spec=grid_spec,
        out_shape=jax.ShapeDtypeStruct(xs.shape, F32),
        compiler_params=pltpu.CompilerParams(dimension_semantics=("arbitrary",),
                                             vmem_limit_bytes=VMEM_LIMIT),
        name=name,
    )(*tabs, xs, w_up, w_up, b_up, w_down, b_down)


def _combine_body(slot_ref, slot_nx_ref, gate_ref, x1_ref, gf_ref, ys_ref, o_ref, buf_ref, sem):
    step = pl.program_id(0)
    slot = step % 2
    tm = x1_ref.shape[0]

    def gather(rows_ref, s):
        def issue(i, c):
            for k in range(TOP_K):
                _rows_copy(ys_ref, rows_ref[i * TOP_K + k], buf_ref.at[s, k], i * ROW_PITCH,
                           sem.at[s]).start(priority=k % 2)
            return c

        lax.fori_loop(0, tm, issue, 0, unroll=2)

    @pl.when(step == 0)
    def _():
        gather(slot_ref, 0)

    for s in range(2):
        @pl.when((step + 1 < pl.num_programs(0)) & (slot == 1 - s))
        def _(s=s):
            gather(slot_nx_ref, s)

    def drain(i, c):
        for _ in range(WAIT_UNROLL):
            _token_copy(ys_ref, 0, buf_ref.at[slot, 0], 0, sem.at[slot]).wait()
        return c

    lax.fori_loop(0, tm * TOP_K // WAIT_UNROLL, drain, 0)
    gate = gate_ref[...]
    tiles = [x1_ref[:, j * LANES:(j + 1) * LANES] for j in range(ROW_TILES)]
    for k in range(TOP_K):
        g = gate[:, k:k + 1]
        for j, tile in enumerate(_load_token_rows(buf_ref.at[slot, k], tm)):
            tiles[j] = tiles[j] + g * tile
    o_ref[...] = _rms(jnp.concatenate(tiles, axis=1), gf_ref[...])


def _combine(slot_rows, gates, x1, gf, ys):
    t, d = x1.shape
    tm = min(TM_COMBINE, t)
    n = t // tm
    idx_blk = (tm * TOP_K,)
    return pl.pallas_call(
        _combine_body,
        grid=(n,),
        in_specs=[pl.BlockSpec(idx_blk, lambda i: (i,), memory_space=pltpu.SMEM),
                  pl.BlockSpec(idx_blk, lambda i: (jnp.minimum(i + 1, n - 1),), memory_space=pltpu.SMEM),
                  pl.BlockSpec((tm, TOP_K), lambda i: (i, 0)),
                  pl.BlockSpec((tm, d), lambda i: (i, 0)),
                  _const_spec(gf.shape),
                  pl.BlockSpec(memory_space=pl.ANY)],
        out_specs=pl.BlockSpec((tm, d), lambda i: (i, 0)),
        scratch_shapes=[pltpu.VMEM((2, TOP_K, tm * ROW_PITCH, LANES), F32), pltpu.SemaphoreType.DMA((2,))],
        out_shape=jax.ShapeDtypeStruct((t, d), F32),
        compiler_params=pltpu.CompilerParams(dimension_semantics=("arbitrary",),
                                             vmem_limit_bytes=VMEM_LIMIT),
        name="combine",
    )(slot_rows, slot_rows, gates, x1, gf, ys)


def _rope_tables(s):
    pos = np.arange(s, dtype=np.float32)
    inv_freq = np.power(np.float32(ROPE_THETA), -np.arange(0, B_ROPE_DIM, 2, dtype=np.float32) / B_ROPE_DIM)
    ang = (pos[:, None] * inv_freq[None, :]).astype(np.float32)
    cos, sin = np.cos(ang), np.sin(ang)
    pad = LANES - B_ROPE_DIM
    cos = np.concatenate([cos, cos, np.ones((s, pad), np.float32)], axis=1)
    sin = np.concatenate([sin, sin, np.zeros((s, pad), np.float32)], axis=1)
    return cos.astype(np.float32), sin.astype(np.float32)


def _window_bias():
    qi = np.arange(A_BLOCK)[:, None]
    kj = np.arange(3 * A_BLOCK)[None, :]
    dist = np.abs(qi + A_BLOCK - kj)
    slopes = np.power(np.float32(2.0), -8.0 * np.arange(1, A_HEADS + 1, dtype=np.float32) / A_HEADS)
    bias = -slopes[:, None, None] * dist.astype(np.float32)[None]
    bias = np.where((dist <= WINDOW)[None], bias, -np.inf)
    bias = bias.reshape(A_KV_HEADS, 2, 2, A_BLOCK, 3 * A_BLOCK).transpose(0, 2, 1, 3, 4)
    bias = bias.reshape(2 * A_KV_HEADS, 2 * A_BLOCK, 3 * A_BLOCK)
    no_prev = (kj < A_BLOCK)[None]
    no_next = (kj >= 2 * A_BLOCK)[None]
    ninf = -np.inf
    return np.stack([bias, np.where(no_prev, ninf, bias), np.where(no_next, ninf, bias),
                     np.where(no_prev | no_next, ninf, bias)]).astype(np.float32)


def _layer(x, attn_norm, w_in, a_sink, b_q_norm, b_w_uq, b_kv_norm, b_w_ukv, out_norm_a, out_norm_b, w_o,
           mlp_norm, w_router, b_router, w_up, b_up, w_down, b_down):
    b, s, d = x.shape
    t = b * s
    wq = b_w_uq.reshape(B_Q_RANK, B_HEADS, B_NOPE_DIM + B_ROPE_DIM)
    wq_pe = jnp.pad(wq[:, :, B_NOPE_DIM:], ((0, 0), (0, 0), (0, LANES - B_ROPE_DIM)))
    wuq = jnp.concatenate([wq[:, :, :B_NOPE_DIM].reshape(B_Q_RANK, -1), wq_pe.reshape(B_Q_RANK, -1)],
                          axis=1).astype(BF16)
    wkv = b_w_ukv.reshape(B_KV_RANK, B_HEADS, B_NOPE_DIM + B_V_DIM)
    wukv = jnp.concatenate([wkv[:, :, :B_NOPE_DIM].reshape(B_KV_RANK, -1),
                            wkv[:, :, B_NOPE_DIM:].reshape(B_KV_RANK, -1)], axis=1).astype(BF16)
    cos, sin = _rope_tables(s)

    qa, ka, va, qb, kb, vb = _prologue(x, attn_norm[None], w_in, b_q_norm[None], wuq, b_kv_norm[None], wukv,
                                       cos, sin)
    ya = _window_attn(qa, ka, va, a_sink, _window_bias())
    yb = _mla_attn(qb, kb, vb)
    wr_hi = w_router.astype(BF16)
    wr_lo = (w_router - wr_hi.astype(F32)).astype(BF16)
    x1, xm, eidx, gates, rank, counts = _out_router(
        ya.reshape(t, A_WIDTH), yb.reshape(t, B_WIDTH), x.reshape(t, d), out_norm_a[None], out_norm_b[None],
        w_o.astype(BF16), mlp_norm[None], jnp.concatenate([wr_hi, wr_lo], axis=1), b_router[None])

    nblk = (t * TOP_K + N_EXPERTS * (MOE_BM - 1) + MOE_BM - 1) // MOE_BM
    tabs, start, pad_start, pad_len, nact = _expert_schedule(counts.reshape(N_EXPERTS), nblk, t * TOP_K)
    slot_rows = _slot_rows(start, eidx.reshape(t * TOP_K), rank.reshape(t * TOP_K))
    xs = _dispatch(pad_start, pad_len, nact, slot_rows, xm, nblk * MOE_BM)
    n_short = N_EXPERTS * NF + nblk
    args = (xs, w_up, b_up, w_down, b_down)
    ys = lax.cond(nact[1] <= n_short,
                  lambda: _experts(tabs, *args, n_steps=n_short, name="experts"),
                  lambda: _experts(tabs, *args, n_steps=tabs[0].shape[0], name="experts_long"))
    return x1, slot_rows, gates, ys


def kernel(x, attn_norm, w_in, a_sink, b_q_norm, b_w_uq, b_kv_norm, b_w_ukv, out_norm_a, out_norm_b, w_o,
           mlp_norm, w_router, b_router, w_up, b_up, w_down, b_down, final_norm):
    b, s, d = x.shape
    assert d == D_MODEL and s % min(TQ_MLA, s) == 0 and s % min(CK_MLA, s) == 0 and s % TM_PRO == 0 and attn_norm.shape[0] == 1
    x1, slot_rows, gates, ys = _layer(
        x, attn_norm[0], w_in[0], a_sink[0], b_q_norm[0], b_w_uq[0], b_kv_norm[0], b_w_ukv[0], out_norm_a[0],
        out_norm_b[0], w_o[0], mlp_norm[0], w_router[0], b_router[0], w_up[0], b_up[0], w_down[0], b_down[0])
    out = _combine(slot_rows, gates, x1, final_norm[None], ys)
    return out.reshape(b, s, d)
```

```python
import functools

import jax
import numpy as np
import jax.numpy as jnp
from jax import lax
from jax.experimental import pallas as pl
from jax.experimental.pallas import tpu as pltpu

D_MODEL = 2048
A_HEADS, A_KV_HEADS, A_HEAD_DIM = 16, 4, 64
A_GROUP = A_HEADS // A_KV_HEADS
WINDOW = 128
A_BLOCK = 128
B_HEADS, B_Q_RANK, B_KV_RANK = 8, 512, 256
B_NOPE_DIM, B_ROPE_DIM, B_V_DIM = 128, 64, 128
ROPE_THETA = 10000.0
A_WIDTH = A_HEADS * A_HEAD_DIM
B_WIDTH = B_HEADS * B_V_DIM
A_KV_COLS = A_KV_HEADS * A_HEAD_DIM
N_EXPERTS, TOP_K, D_FF = 32, 4, 2048
SWIGLU_ALPHA, SWIGLU_LIMIT = 1.702, 7.0
EPS = 1e-5

LANES = 128
SUBLANES = 8
B_QK_PAD = 2 * LANES
IN_COLS_PAD = A_WIDTH + 2 * A_KV_COLS + B_Q_RANK + B_KV_RANK + LANES
VMEM_LIMIT = 56 * 1024 * 1024
ROW_TILES = D_MODEL // LANES
ROW_PITCH = ROW_TILES + 1

TM_PRO = 256
TQ_MLA = 1024
CK_MLA = 2048
HEADS_MLA = 2
TM_OUT = 512
TM_DISPATCH = 1024
TM_COMBINE = 256
MOE_BM = 512
MOE_SUB = 256
MOE_RM = 2048
MOE_TF = 256
WAIT_UNROLL = 16

BF16 = jnp.bfloat16
F32 = jnp.float32


def _rms(x, g):
    return x * lax.rsqrt(jnp.mean(x * x, axis=-1, keepdims=True) + EPS) * g


def _const_spec(shape):
    nd = len(shape)
    return pl.BlockSpec(shape, lambda *_: (0,) * nd, pipeline_mode=pl.Buffered(1))


def _store_token_rows(ref, first, val):
    n = val.shape[0]
    for j in range(ROW_TILES):
        ref[pl.ds(first * ROW_PITCH + j, n, stride=ROW_PITCH), :] = val[:, j * LANES:(j + 1) * LANES]
    ref[pl.ds(first * ROW_PITCH + ROW_TILES, n, stride=ROW_PITCH), :] = jnp.zeros((n, LANES), val.dtype)


def _load_token_rows(ref, n):
    return [ref[pl.ds(j, n, stride=ROW_PITCH), :] for j in range(ROW_TILES)]


def _rope(x, cos, sin):
    lane = lax.broadcasted_iota(jnp.int32, x.shape, 1)
    up = pltpu.roll(x, LANES - B_ROPE_DIM // 2, 1)
    dn = pltpu.roll(x, B_ROPE_DIM // 2, 1)
    sw = jnp.where(lane < B_ROPE_DIM // 2, -up, jnp.where(lane < B_ROPE_DIM, dn, 0.0))
    return x * cos + sw * sin


def _prologue_body(x_ref, g_ref, win_ref, qn_ref, wuq_ref, kvn_ref, wukv_ref, cos_ref, sin_ref,
                   qa_ref, ka_ref, va_ref, qb_ref, kb_ref, vb_ref, winb_ref):
    @pl.when((pl.program_id(0) == 0) & (pl.program_id(1) == 0))
    def _():
        n_in = win_ref.shape[1]
        winb_ref[:, :n_in] = win_ref[...].astype(BF16)
        winb_ref[:, n_in:] = jnp.zeros((winb_ref.shape[0], IN_COLS_PAD - n_in), BF16)

    x = x_ref[0]
    xn = _rms(x, g_ref[...]).astype(BF16)
    proj = jnp.dot(xn, winb_ref[...], preferred_element_type=F32)
    c0 = A_WIDTH
    c1 = c0 + A_KV_COLS
    c2 = c1 + A_KV_COLS
    c3 = c2 + B_Q_RANK
    c4 = c3 + B_KV_RANK
    qa_ref[0] = (proj[:, :A_WIDTH] * A_HEAD_DIM ** -0.5).astype(BF16)
    low = lax.broadcasted_iota(jnp.int32, (proj.shape[0], LANES), 1) < A_HEAD_DIM
    for src, dst in ((c0, ka_ref), (c1, va_ref)):
        for pair in range(A_KV_HEADS // 2):
            two = proj[:, src + pair * LANES:src + (pair + 1) * LANES]
            swapped = pltpu.roll(two, A_HEAD_DIM, 1)
            dst[0, 4 * pair + 0] = jnp.where(low, two, 0.0).astype(BF16)
            dst[0, 4 * pair + 1] = jnp.where(low, 0.0, swapped).astype(BF16)
            dst[0, 4 * pair + 2] = jnp.where(low, swapped, 0.0).astype(BF16)
            dst[0, 4 * pair + 3] = jnp.where(low, 0.0, two).astype(BF16)
    cq = _rms(proj[:, c2:c3], qn_ref[...]).astype(BF16)
    q = jnp.dot(cq, wuq_ref[...], preferred_element_type=F32)
    ckv = _rms(proj[:, c3:c4], kvn_ref[...]).astype(BF16)
    kv = jnp.dot(ckv, wukv_ref[...], preferred_element_type=F32)
    cos = cos_ref[...]
    sin = sin_ref[...]
    kpe = _rope(proj[:, c4:c4 + LANES], cos, sin).astype(BF16)
    b_scale = (B_NOPE_DIM + B_ROPE_DIM) ** -0.5
    hw = B_HEADS * LANES
    for h in range(B_HEADS):
        sl = slice(h * LANES, (h + 1) * LANES)
        qb_ref[0, h, :, :LANES] = (q[:, sl] * b_scale).astype(BF16)
        qpe = _rope(q[:, hw + h * LANES:hw + (h + 1) * LANES], cos, sin)
        qb_ref[0, h, :, LANES:] = (qpe * b_scale).astype(BF16)
        kb_ref[0, h, :, :LANES] = kv[:, sl].astype(BF16)
        kb_ref[0, h, :, LANES:] = kpe
        vb_ref[0, h] = kv[:, hw + h * LANES:hw + (h + 1) * LANES].astype(BF16)


def _prologue(x, g, win, qn, wuq, kvn, wukv, cos, sin):
    b, s, d = x.shape
    tm = TM_PRO
    grid = (b, s // tm)
    sds = jax.ShapeDtypeStruct
    out_shape = (
        sds((b, s, A_WIDTH), BF16),
        sds((b, 2 * A_KV_HEADS, s, LANES), BF16),
        sds((b, 2 * A_KV_HEADS, s, LANES), BF16),
        sds((b, B_HEADS, s, B_QK_PAD), BF16),
        sds((b, B_HEADS, s, B_QK_PAD), BF16),
        sds((b, B_HEADS, s, B_V_DIM), BF16),
    )

    def hspec(nh, w):
        return pl.BlockSpec((1, nh, tm, w), lambda bi, i: (bi, 0, i, 0))

    return pl.pallas_call(
        _prologue_body,
        grid=grid,
        in_specs=[
            pl.BlockSpec((1, tm, d), lambda bi, i: (bi, i, 0)),
            _const_spec(g.shape), _const_spec(win.shape), _const_spec(qn.shape), _const_spec(wuq.shape),
            _const_spec(kvn.shape), _const_spec(wukv.shape),
            pl.BlockSpec((tm, LANES), lambda bi, i: (i, 0)),
            pl.BlockSpec((tm, LANES), lambda bi, i: (i, 0)),
        ],
        out_specs=(pl.BlockSpec((1, tm, A_WIDTH), lambda bi, i: (bi, i, 0)),
                   hspec(2 * A_KV_HEADS, LANES), hspec(2 * A_KV_HEADS, LANES),
                   hspec(B_HEADS, B_QK_PAD), hspec(B_HEADS, B_QK_PAD), hspec(B_HEADS, B_V_DIM)),
        out_shape=out_shape,
        scratch_shapes=[pltpu.VMEM((d, IN_COLS_PAD), BF16)],
        compiler_params=pltpu.CompilerParams(dimension_semantics=("arbitrary", "arbitrary"),
                                             vmem_limit_bytes=VMEM_LIMIT),
        name="prologue",
    )(x, g, win, qn, wuq, kvn, wukv, cos, sin)


def _window_body(sink_ref, q_ref, kp_ref, kc_ref, kn_ref, vp_ref, vc_ref, vn_ref, bias_ref, o_ref):
    two = 2 * A_BLOCK
    low = lax.broadcasted_iota(jnp.int32, (two, LANES), 1) < A_HEAD_DIM
    for hk in range(A_KV_HEADS):
        q2 = jnp.concatenate([q_ref[0, :, (2 * hk) * LANES:(2 * hk + 1) * LANES],
                              q_ref[0, :, (2 * hk + 1) * LANES:(2 * hk + 2) * LANES]], axis=0)
        acc = jnp.zeros((two, LANES), F32)
        inv = []
        for half in range(2):
            z = 2 * hk + half
            kband = jnp.concatenate([kp_ref[0, z], kc_ref[0, z], kn_ref[0, z]], axis=0)
            vband = jnp.concatenate([vp_ref[0, z], vc_ref[0, z], vn_ref[0, z]], axis=0)
            s = lax.dot_general(q2, kband, (((1,), (1,)), ((), ())), preferred_element_type=F32)
            s = s + bias_ref[0, z]
            sink = jnp.concatenate(
                [jnp.full((A_BLOCK, 1), sink_ref[hk * A_GROUP + 2 * jj + half], F32) for jj in range(2)], axis=0)
            m = jnp.maximum(jnp.max(s, axis=-1, keepdims=True), sink)
            p = jnp.exp(s - m)
            den = jnp.sum(p, axis=-1, keepdims=True) + jnp.exp(sink - m)
            inv.append(1.0 / den)
            acc = acc + jnp.dot(p.astype(BF16), vband, preferred_element_type=F32)
        o = (acc * jnp.where(low, inv[0], inv[1])).astype(BF16)
        o_ref[0, :, (2 * hk) * LANES:(2 * hk + 1) * LANES] = o[:A_BLOCK]
        o_ref[0, :, (2 * hk + 1) * LANES:(2 * hk + 2) * LANES] = o[A_BLOCK:]


def _window_attn(qa, ka, va, sink, bias):
    b, s, _ = qa.shape
    nb = s // A_BLOCK
    kv_blk = (1, 2 * A_KV_HEADS, A_BLOCK, LANES)
    prev = pl.BlockSpec(kv_blk, lambda bi, n: (bi, 0, jnp.maximum(n - 1, 0), 0))
    cur = pl.BlockSpec(kv_blk, lambda bi, n: (bi, 0, n, 0))
    nxt = pl.BlockSpec(kv_blk, lambda bi, n: (bi, 0, jnp.minimum(n + 1, nb - 1), 0))

    def edge(bi, n):
        return ((n == 0).astype(jnp.int32) + 2 * (n == nb - 1).astype(jnp.int32), 0, 0, 0)

    return pl.pallas_call(
        _window_body,
        grid=(b, nb),
        in_specs=[
            pl.BlockSpec(memory_space=pltpu.SMEM),
            pl.BlockSpec((1, A_BLOCK, A_WIDTH), lambda bi, n: (bi, n, 0)),
            prev, cur, nxt, prev, cur, nxt,
            pl.BlockSpec((1,) + bias.shape[1:], edge),
        ],
        out_specs=pl.BlockSpec((1, A_BLOCK, A_WIDTH), lambda bi, n: (bi, n, 0)),
        out_shape=jax.ShapeDtypeStruct((b, s, A_WIDTH), BF16),
        compiler_params=pltpu.CompilerParams(dimension_semantics=("parallel", "parallel"),
                                             vmem_limit_bytes=VMEM_LIMIT),
        name="window_attn",
    )(sink, qa, ka, ka, ka, va, va, va, bias)


def _mla_body(q_ref, k_ref, v_ref, o_ref):
    tq = q_ref.shape[2]
    s_len = k_ref.shape[2]
    ck = min(CK_MLA, s_len)
    for hh in range(HEADS_MLA):
        q = q_ref[0, hh]
        m = jnp.full((tq, 1), -jnp.inf, F32)
        l = jnp.zeros((tq, 1), F32)
        acc = jnp.zeros((tq, B_V_DIM), F32)
        for c in range(s_len // ck):
            k_c = k_ref[0, hh, c * ck:(c + 1) * ck, :]
            v_c = v_ref[0, hh, c * ck:(c + 1) * ck, :]
            s = lax.dot_general(q, k_c, (((1,), (1,)), ((), ())), preferred_element_type=F32)
            m_new = jnp.maximum(m, jnp.max(s, axis=-1, keepdims=True))
            alpha = jnp.exp(m - m_new)
            p = jnp.exp(s - m_new)
            l = alpha * l + jnp.sum(p, axis=-1, keepdims=True)
            acc = alpha * acc + jnp.dot(p.astype(BF16), v_c, preferred_element_type=F32)
            m = m_new
        o_ref[0, :, hh * B_V_DIM:(hh + 1) * B_V_DIM] = (acc / l).astype(BF16)


def _mla_attn(qb, kb, vb):
    b, nh, s, _ = qb.shape
    tq = min(TQ_MLA, s)
    hs = HEADS_MLA
    return pl.pallas_call(
        _mla_body,
        grid=(b, nh // hs, s // tq),
        in_specs=[
            pl.BlockSpec((1, hs, tq, B_QK_PAD), lambda bi, h, i: (bi, h, i, 0)),
            pl.BlockSpec((1, hs, s, B_QK_PAD), lambda bi, h, i: (bi, h, 0, 0)),
            pl.BlockSpec((1, hs, s, B_V_DIM), lambda bi, h, i: (bi, h, 0, 0)),
        ],
        out_specs=pl.BlockSpec((1, tq, hs * B_V_DIM), lambda bi, h, i: (bi, i, h)),
        out_shape=jax.ShapeDtypeStruct((b, s, B_WIDTH), BF16),
        compiler_params=pltpu.CompilerParams(dimension_semantics=("parallel", "parallel", "parallel"),
                                             vmem_limit_bytes=VMEM_LIMIT),
        name="mla_attn",
    )(qb, kb, vb)


def _out_router_body(ya_ref, yb_ref, x_ref, ga_ref, gb_ref, wo_ref, gm_ref, wr_ref, br_ref,
                     x1_ref, xm_ref, eidx_ref, gate_ref, rank_ref, cnt_ref, run_ref):
    i = pl.program_id(0)

    @pl.when(i == 0)
    def _():
        run_ref[...] = jnp.zeros_like(run_ref)

    na = _rms(ya_ref[...].astype(F32), ga_ref[...]).astype(BF16)
    nb = _rms(yb_ref[...].astype(F32), gb_ref[...]).astype(BF16)
    att = jnp.dot(na, wo_ref[:A_WIDTH, :], preferred_element_type=F32)
    att = att + jnp.dot(nb, wo_ref[A_WIDTH:, :], preferred_element_type=F32)
    x1 = x_ref[...] + att
    x1_ref[...] = x1
    hn = _rms(x1, gm_ref[...])
    _store_token_rows(xm_ref, 0, hn)
    tm = hn.shape[0]
    hi = hn.astype(BF16)
    lo = (hn - hi.astype(F32)).astype(BF16)
    prod = jnp.dot(jnp.concatenate([hi, lo], axis=0), wr_ref[...], preferred_element_type=F32)
    logits = (prod[:tm, :N_EXPERTS] + prod[:tm, N_EXPERTS:] + prod[tm:, :N_EXPERTS] + prod[tm:, N_EXPERTS:]
              + br_ref[...])
    lane = lax.broadcasted_iota(jnp.int32, (tm, N_EXPERTS), 1)
    work = logits
    sel = jnp.zeros((tm, N_EXPERTS), F32)
    hots, vals, idxs = [], [], []
    for _k in range(TOP_K):
        mx = jnp.max(work, axis=-1, keepdims=True)
        idx = jnp.min(jnp.where(work == mx, lane, N_EXPERTS), axis=-1, keepdims=True)
        hot = lane == idx
        hots.append(hot)
        vals.append(mx)
        idxs.append(idx)
        sel = sel + hot.astype(F32)
        work = jnp.where(hot, -jnp.inf, work)
    exps = [jnp.exp(v - vals[0]) for v in vals]
    den = exps[0] + exps[1] + exps[2] + exps[3]
    r_i = lax.broadcasted_iota(jnp.int32, (tm, tm), 0)
    c_i = lax.broadcasted_iota(jnp.int32, (tm, tm), 1)
    tri = (c_i < r_i).astype(BF16)
    before = jnp.dot(tri, sel.astype(BF16), preferred_element_type=F32) + run_ref[...]
    lane4 = lax.broadcasted_iota(jnp.int32, (tm, TOP_K), 1)
    eidx = jnp.zeros((tm, TOP_K), jnp.int32)
    gate = jnp.zeros((tm, TOP_K), F32)
    rank = jnp.zeros((tm, TOP_K), jnp.int32)
    for k in range(TOP_K):
        rk = jnp.sum(jnp.where(hots[k], before, 0.0), axis=-1, keepdims=True).astype(jnp.int32)
        eidx = jnp.where(lane4 == k, idxs[k], eidx)
        gate = jnp.where(lane4 == k, exps[k] / den, gate)
        rank = jnp.where(lane4 == k, rk, rank)
    eidx_ref[...] = eidx
    gate_ref[...] = gate
    rank_ref[...] = rank
    run = run_ref[...] + jnp.sum(sel, axis=0, keepdims=True)
    run_ref[...] = run
    cnt_ref[...] = run.astype(jnp.int32)


def _out_router(ya, yb, x2d, ga, gb, wo, gm, wr, br):
    t, d = x2d.shape
    tm = min(TM_OUT, t)
    sds = jax.ShapeDtypeStruct

    def row(w):
        return pl.BlockSpec((tm, w), lambda i: (i, 0))

    return pl.pallas_call(
        _out_router_body,
        grid=(t // tm,),
        in_specs=[row(A_WIDTH), row(B_WIDTH), row(d), _const_spec(ga.shape), _const_spec(gb.shape),
                  _const_spec(wo.shape), _const_spec(gm.shape), _const_spec(wr.shape), _const_spec(br.shape)],
        out_specs=(row(d), pl.BlockSpec((tm * ROW_PITCH, LANES), lambda i: (i, 0)),
                   row(TOP_K), row(TOP_K), row(TOP_K),
                   pl.BlockSpec((1, N_EXPERTS), lambda i: (0, 0))),
        out_shape=(sds((t, d), F32), sds((t * ROW_PITCH, LANES), F32), sds((t, TOP_K), jnp.int32),
                   sds((t, TOP_K), F32),
                   sds((t, TOP_K), jnp.int32), sds((1, N_EXPERTS), jnp.int32)),
        scratch_shapes=[pltpu.VMEM((1, N_EXPERTS), F32)],
        compiler_params=pltpu.CompilerParams(dimension_semantics=("arbitrary",),
                                             vmem_limit_bytes=VMEM_LIMIT),
        name="out_router",
    )(ya, yb, x2d, ga, gb, wo, gm, wr, br)


def _token_copy(src_ref, src_tok, dst_ref, dst_tok, sem, ntok=1):
    return _rows_copy(src_ref, src_tok * ROW_PITCH, dst_ref, dst_tok * ROW_PITCH, sem, ntok)


def _rows_copy(src_ref, src_row, dst_ref, dst_row, sem, ntok=1):
    rows = ntok * ROW_PITCH
    return pltpu.make_async_copy(src_ref.at[pl.ds(src_row, rows)], dst_ref.at[pl.ds(dst_row, rows)], sem)


def _slot_rows_body(start_ref, eidx_ref, rank_ref, o_ref):
    e = eidx_ref[...]
    base = jnp.zeros_like(e)
    for ex in range(N_EXPERTS):
        base = jnp.where(e == ex, start_ref[ex], base)
    o_ref[...] = (base + rank_ref[...]) * ROW_PITCH


def _slot_rows(start, eidx, rank):
    n = eidx.shape[0]
    shape2 = (n // LANES, LANES)
    out = pl.pallas_call(
        _slot_rows_body,
        in_specs=[pl.BlockSpec(memory_space=pltpu.SMEM), pl.BlockSpec(shape2, lambda: (0, 0)),
                  pl.BlockSpec(shape2, lambda: (0, 0))],
        out_specs=pl.BlockSpec(shape2, lambda: (0, 0)),
        out_shape=jax.ShapeDtypeStruct(shape2, jnp.int32),
        name="slot_rows",
    )(start, eidx.reshape(shape2), rank.reshape(shape2))
    return out.reshape(n)


def _zero_fill_pads(pad_start_ref, pad_len_ref, nact_ref, xs_ref, zero_ref, zsem, start):
    bm = zero_ref.shape[0] // ROW_PITCH
    nblk = xs_ref.shape[0] // (bm * ROW_PITCH)
    if start:
        zero_ref[...] = jnp.zeros_like(zero_ref)

    def sweep(do):
        def per_expert(e, c):
            off = pad_start_ref[e]
            n = pad_len_ref[e]
            p = bm // 2
            while p >= 1:
                take = (n & p) != 0

                @pl.when(take)
                def _(p=p, off=off):
                    do(_token_copy(zero_ref, 0, xs_ref, off, zsem, p))

                off = off + jnp.where(take, p, 0)
                p //= 2
            return c

        lax.fori_loop(0, N_EXPERTS, per_expert, 0)

        def tail(j, c):
            do(_token_copy(zero_ref, 0, xs_ref, j * bm, zsem, bm))
            return c

        lax.fori_loop(nact_ref[0], nblk, tail, 0)

    sweep((lambda cp: cp.start()) if start else (lambda cp: cp.wait()))


def _dispatch_body(pad_start_ref, pad_len_ref, nact_ref, slot_ref, xm_ref, xs_ref, zero_ref, sem, zsem):
    tm = slot_ref.shape[0] // TOP_K

    @pl.when(pl.program_id(0) == 0)
    def _():
        _zero_fill_pads(pad_start_ref, pad_len_ref, nact_ref, xs_ref, zero_ref, zsem, start=True)

    def issue(i, c):
        for k in range(TOP_K):
            _rows_copy(xm_ref, i * ROW_PITCH, xs_ref, slot_ref[i * TOP_K + k], sem).start(priority=k % 2)
        return c

    lax.fori_loop(0, tm, issue, 0, unroll=2)

    def drain(i, c):
        for _ in range(WAIT_UNROLL):
            _token_copy(xm_ref, 0, xs_ref, 0, sem).wait()
        return c

    lax.fori_loop(0, tm * TOP_K // WAIT_UNROLL, drain, 0)

    @pl.when(pl.program_id(0) == pl.num_programs(0) - 1)
    def _():
        _zero_fill_pads(pad_start_ref, pad_len_ref, nact_ref, xs_ref, zero_ref, zsem, start=False)


def _dispatch(pad_start, pad_len, nact, slot_rows, xm, n_slots):
    t = xm.shape[0] // ROW_PITCH
    tm = min(TM_DISPATCH, t)
    grid_spec = pltpu.PrefetchScalarGridSpec(
        num_scalar_prefetch=3,
        grid=(t // tm,),
        in_specs=[pl.BlockSpec((tm * TOP_K,), lambda i, *_: (i,), memory_space=pltpu.SMEM),
                  pl.BlockSpec((tm * ROW_PITCH, LANES), lambda i, *_: (i, 0))],
        out_specs=pl.BlockSpec(memory_space=pl.ANY),
        scratch_shapes=[pltpu.VMEM((MOE_BM * ROW_PITCH, LANES), F32), pltpu.SemaphoreType.DMA(()),
                        pltpu.SemaphoreType.DMA(())],
    )
    return pl.pallas_call(
        _dispatch_body,
        grid_spec=grid_spec,
        out_shape=jax.ShapeDtypeStruct((n_slots * ROW_PITCH, LANES), F32),
        compiler_params=pltpu.CompilerParams(dimension_semantics=("arbitrary",), has_side_effects=True,
                                             vmem_limit_bytes=VMEM_LIMIT),
        name="dispatch",
    )(pad_start, pad_len, nact, slot_rows, xm)


STEP_UP, STEP_DOWN, STEP_TAIL, STEP_IDLE = 0, 1, 2, 3
PREFETCH_PER_DOWN = 2
NF = D_FF // MOE_TF
PASS_BLOCKS = MOE_RM // MOE_BM


def _swiglu(hg, hu):
    gate = jnp.minimum(hg, SWIGLU_LIMIT)
    up = jnp.clip(hu, -SWIGLU_LIMIT, SWIGLU_LIMIT)
    glu = gate / (1.0 + jnp.exp(-SWIGLU_ALPHA * gate))
    return (up + 1.0) * glu


def _experts_body(e_ref, f_ref, kind_ref, blk_ref, blk0_ref, nsub_ref, pre_ref, nxb_ref,
                  xs_ref, wg_ref, wu_ref, bup_ref, wd_ref, bdn_ref,
                  y_ref, xb_ref, h_ref, wdb_ref, stg_ref, sem):
    t = pl.program_id(0)
    kind = kind_ref[t]
    f = f_ref[t]
    nsub = nsub_ref[t]
    erow = pl.ds(e_ref[t], 1)
    ftile = pl.multiple_of(f * MOE_TF, MOE_TF)

    def x_copy(first_blk, j, slot):
        rows = MOE_SUB * ROW_PITCH
        first = pl.multiple_of((first_blk * MOE_BM + j * MOE_SUB) * ROW_PITCH, SUBLANES)
        return pltpu.make_async_copy(xs_ref.at[pl.ds(first, rows)], stg_ref.at[slot], sem.at[slot])

    def unpack(j, slot):
        rows = pl.ds(pl.multiple_of(j * MOE_SUB, MOE_SUB), MOE_SUB)
        for jt, tile in enumerate(_load_token_rows(stg_ref.at[slot], MOE_SUB)):
            xb_ref[rows, jt * LANES:(jt + 1) * LANES] = tile.astype(BF16)

    @pl.when(kind == STEP_UP)
    def _up():
        pre = pre_ref[t]

        @pl.when((f == 0) & (pre < nsub))
        def _load_rows():
            blk0 = blk0_ref[t]
            x_copy(blk0, pre, pre % 2).start()

            def body(j, c):
                slot = j % 2

                @pl.when(j + 1 < nsub)
                def _():
                    x_copy(blk0, j + 1, 1 - slot).start()

                x_copy(blk0, j, slot).wait()
                unpack(j, slot)
                return c

            lax.fori_loop(pre, nsub, body, 0)

        wdb_ref[pl.ds(pl.multiple_of(f * MOE_TF, MOE_TF), MOE_TF), :] = wd_ref[0].astype(BF16)

        def up_rows(row0, nrows):
            x = xb_ref[pl.ds(row0, nrows), :]
            hg = jnp.dot(x, wg_ref[0].astype(BF16), preferred_element_type=F32) + bup_ref[erow, pl.ds(ftile, MOE_TF)]
            hu = (jnp.dot(x, wu_ref[0].astype(BF16), preferred_element_type=F32)
                  + bup_ref[erow, pl.ds(pl.multiple_of(D_FF + ftile, MOE_TF), MOE_TF)])
            act = _swiglu(hg, hu).astype(BF16)
            h_ref[pl.ds(row0, nrows), pl.ds(pl.multiple_of(f * MOE_TF, MOE_TF), MOE_TF)] = act

        quad = 4 * MOE_SUB
        nquad = nsub // 4
        rem = nsub % 4

        def quad_rows(i, c):
            base = pl.multiple_of(i * quad, quad)
            up_rows(base, quad // 2)
            up_rows(base + quad // 2, quad // 2)
            return c

        merged = nsub == 5
        short_tail = nxb_ref[t] == 1

        @pl.when(merged & jnp.logical_not(short_tail))
        def _():
            up_rows(0, 2 * MOE_SUB)
            up_rows(2 * MOE_SUB, 3 * MOE_SUB)

        @pl.when(merged & short_tail)
        def _():
            up_rows(0, 2 * MOE_SUB)
            up_rows(2 * MOE_SUB, 2 * MOE_SUB + MOE_SUB // 2)
            h_ref[4 * MOE_SUB + MOE_SUB // 2:5 * MOE_SUB, pl.ds(ftile, MOE_TF)] = jnp.zeros(
                (MOE_SUB // 2, MOE_TF), BF16)

        @pl.when(jnp.logical_not(merged))
        def _():
            lax.fori_loop(0, nquad, quad_rows, 0)

            @pl.when(rem >= 2)
            def _():
                up_rows(pl.multiple_of(nquad * quad, quad), 2 * MOE_SUB)

            @pl.when(rem % 2 == 1)
            def _():
                up_rows(pl.multiple_of(nquad * quad + (rem // 2) * 2 * MOE_SUB, MOE_SUB), MOE_SUB)

    @pl.when(kind == STEP_DOWN)
    def _down():
        jblk = blk_ref[t] - blk0_ref[t]
        row0 = pl.multiple_of(jblk * MOE_BM, MOE_BM)
        nx_nsub = pre_ref[t]
        nx_blk0 = nxb_ref[t]
        ahead = [jblk * PREFETCH_PER_DOWN + u for u in range(PREFETCH_PER_DOWN)]
        for u, j in enumerate(ahead):
            @pl.when(j < nx_nsub)
            def _(u=u, j=j):
                x_copy(nx_blk0, j, u).start()

        def down_rows(nrows):
            hrows = h_ref[pl.ds(row0, nrows), :]
            return jnp.dot(hrows, wdb_ref[...], preferred_element_type=F32) + bdn_ref[erow, :]

        @pl.when(nsub == 2)
        def _():
            _store_token_rows(y_ref, 0, down_rows(MOE_BM))

        @pl.when(nsub == 1)
        def _():
            _store_token_rows(y_ref, 0, down_rows(MOE_SUB))
            y_ref[MOE_SUB * ROW_PITCH:, :] = jnp.zeros(((MOE_BM - MOE_SUB) * ROW_PITCH, LANES), F32)

        for u, j in enumerate(ahead):
            @pl.when(j < nx_nsub)
            def _(u=u, j=j):
                x_copy(nx_blk0, j, u).wait()
                unpack(j, u)

    @pl.when(kind == STEP_TAIL)
    def _tail():
        y_ref[...] = jnp.zeros_like(y_ref)


def _schedule_body(n_blocks, cnt_ref, e_ref, f_ref, kind_ref, blk_ref, blk0_ref, nsub_ref, pre_ref, nxb_ref,
                   start_ref, pad_start_ref, pad_len_ref, nact_ref):
    n_steps = e_ref.shape[0]
    sub_per_blk = MOE_BM // MOE_SUB
    zero = jnp.int32(0)

    def put(idx, e, f, kind, blk, blk0, nsub, pre, nxb=zero):
        e_ref[idx] = e
        f_ref[idx] = f
        kind_ref[idx] = kind
        blk_ref[idx] = blk
        blk0_ref[idx] = blk0
        nsub_ref[idx] = nsub
        pre_ref[idx] = pre
        nxb_ref[idx] = nxb

    def per_expert(e, carry):
        t, blk, last_e, prev_down, prev_nb = carry
        n = cnt_ref[e]
        nblk = (n + (MOE_BM - 1)) // MOE_BM
        start_ref[e] = blk * MOE_BM
        pad_start_ref[e] = blk * MOE_BM + n
        pad_len_ref[e] = nblk * MOE_BM - n

        def per_pass(p, carry2):
            t2, pdown, pnb = carry2
            b0 = blk + p * PASS_BLOCKS
            nb = jnp.minimum(nblk - p * PASS_BLOCKS, PASS_BLOCKS)
            rows = jnp.minimum(n - p * MOE_RM, MOE_RM)
            nsub = (rows + (MOE_SUB - 1)) // MOE_SUB
            pre = jnp.minimum(nsub, pnb * PREFETCH_PER_DOWN)
            short_tail = ((nsub == 5) & (rows <= 4 * MOE_SUB + MOE_SUB // 2)).astype(jnp.int32)
            for f in range(NF):
                put(t2 + f, e, f, STEP_UP, b0, b0, nsub, pre, short_tail)

            def patch(j, c):
                pre_ref[pdown + j] = nsub
                nxb_ref[pdown + j] = b0
                return c

            lax.fori_loop(0, pnb, patch, 0)

            def per_blk(j, c):
                put(t2 + NF + j, e, NF - 1, STEP_DOWN, b0 + j, b0,
                    jnp.clip(nsub - j * sub_per_blk, 1, sub_per_blk), zero)
                return c

            lax.fori_loop(0, nb, per_blk, 0)
            return t2 + NF + nb, t2 + NF, nb

        t, prev_down, prev_nb = lax.fori_loop(0, (nblk + (PASS_BLOCKS - 1)) // PASS_BLOCKS, per_pass,
                                              (t, prev_down, prev_nb))
        return t, blk + nblk, jnp.where(n > 0, e, last_e), prev_down, prev_nb

    t, nact, last_e, _, _ = lax.fori_loop(0, N_EXPERTS, per_expert, (zero, zero, zero, zero, zero))
    nact_ref[0] = nact
    nact_ref[1] = t + n_blocks - nact

    def spare(i, c):
        is_tail = i < n_blocks - nact
        blk = jnp.where(is_tail, nact + i, n_blocks - 1)
        put(t + i, last_e, NF - 1, jnp.where(is_tail, STEP_TAIL, STEP_IDLE), blk, blk, 1, zero)
        return c

    lax.fori_loop(0, n_steps - t, spare, 0)


def _expert_schedule(counts, n_blocks, n_assign):
    n_pass_max = N_EXPERTS + n_assign // MOE_RM
    n_steps = n_pass_max * NF + n_blocks
    smem = pl.BlockSpec(memory_space=pltpu.SMEM)
    i32 = jnp.int32
    out_shape = tuple(jax.ShapeDtypeStruct((n,), i32) for n in (n_steps,) * 8 + (N_EXPERTS,) * 3 + (2,))
    outs = pl.pallas_call(
        functools.partial(_schedule_body, n_blocks),
        in_specs=[smem],
        out_specs=tuple(smem for _ in out_shape),
        out_shape=out_shape,
        name="schedule",
    )(counts)
    return outs[:8], outs[8], outs[9], outs[10], outs[11]


def _experts(tabs, xs, w_up, b_up, w_down, b_down, name):
    d = D_MODEL
    n_steps = tabs[0].shape[0]
    grid_spec = pltpu.PrefetchScalarGridSpec(
        num_scalar_prefetch=len(tabs),
        grid=(n_steps,),
        in_specs=[
            pl.BlockSpec(memory_space=pl.ANY),
            pl.BlockSpec((1, d, MOE_TF), lambda t, e, f, *_: (e[t], 0, f[t])),
            pl.BlockSpec((1, d, MOE_TF), lambda t, e, f, *_: (e[t], 0, NF + f[t])),
            _const_spec(b_up.shape),
            pl.BlockSpec((1, MOE_TF, d), lambda t, e, f, *_: (e[t], f[t], 0)),
            _const_spec(b_down.shape),
        ],
        out_specs=pl.BlockSpec((MOE_BM * ROW_PITCH, LANES), lambda t, e, f, kind, blk, *_: (blk[t], 0)),
        scratch_shapes=[
            pltpu.VMEM((MOE_RM, d), BF16),
            pltpu.VMEM((MOE_RM, D_FF), BF16),
            pltpu.VMEM((D_FF, d), BF16),
            pltpu.VMEM((2, MOE_SUB * ROW_PITCH, LANES), F32),
            pltpu.SemaphoreType.DMA((2,)),
        ],
    )
    return pl.pallas_call(
        _experts_body,
        grid_spec=grid_spec,
        out_shape=jax.ShapeDtypeStruct(xs.shape, F32),
        compiler_params=pltpu.CompilerParams(dimension_semantics=("arbitrary",),
                                             vmem_limit_bytes=VMEM_LIMIT),
        name=name,
    )(*tabs, xs, w_up, w_up, b_up, w_down, b_down)


def _combine_body(slot_ref, slot_nx_ref, gate_ref, x1_ref, gf_ref, ys_ref, o_ref, buf_ref, sem):
    step = pl.program_id(0)
    slot = step % 2
    tm = x1_ref.shape[0]

    def gather(rows_ref, s):
        def issue(i, c):
            for k in range(TOP_K):
                _rows_copy(ys_ref, rows_ref[i * TOP_K + k], buf_ref.at[s, k], i * ROW_PITCH,
                           sem.at[s]).start(priority=k % 2)
            return c

        lax.fori_loop(0, tm, issue, 0, unroll=2)

    @pl.when(step == 0)
    def _():
        gather(slot_ref, 0)

    for s in range(2):
        @pl.when((step + 1 < pl.num_programs(0)) & (slot == 1 - s))
        def _(s=s):
            gather(slot_nx_ref, s)

    def drain(i, c):
        for _ in range(WAIT_UNROLL):
            _token_copy(ys_ref, 0, buf_ref.at[slot, 0], 0, sem.at[slot]).wait()
        return c

    lax.fori_loop(0, tm * TOP_K // WAIT_UNROLL, drain, 0)
    gate = gate_ref[...]
    tiles = [x1_ref[:, j * LANES:(j + 1) * LANES] for j in range(ROW_TILES)]
    for k in range(TOP_K):
        g = gate[:, k:k + 1]
        for j, tile in enumerate(_load_token_rows(buf_ref.at[slot, k], tm)):
            tiles[j] = tiles[j] + g * tile
    o_ref[...] = _rms(jnp.concatenate(tiles, axis=1), gf_ref[...])


def _combine(slot_rows, gates, x1, gf, ys):
    t, d = x1.shape
    tm = min(TM_COMBINE, t)
    n = t // tm
    idx_blk = (tm * TOP_K,)
    return pl.pallas_call(
        _combine_body,
        grid=(n,),
        in_specs=[pl.BlockSpec(idx_blk, lambda i: (i,), memory_space=pltpu.SMEM),
                  pl.BlockSpec(idx_blk, lambda i: (jnp.minimum(i + 1, n - 1),), memory_space=pltpu.SMEM),
                  pl.BlockSpec((tm, TOP_K), lambda i: (i, 0)),
                  pl.BlockSpec((tm, d), lambda i: (i, 0)),
                  _const_spec(gf.shape),
                  pl.BlockSpec(memory_space=pl.ANY)],
        out_specs=pl.BlockSpec((tm, d), lambda i: (i, 0)),
        scratch_shapes=[pltpu.VMEM((2, TOP_K, tm * ROW_PITCH, LANES), F32), pltpu.SemaphoreType.DMA((2,))],
        out_shape=jax.ShapeDtypeStruct((t, d), F32),
        compiler_params=pltpu.CompilerParams(dimension_semantics=("arbitrary",),
                                             vmem_limit_bytes=VMEM_LIMIT),
        name="combine",
    )(slot_rows, slot_rows, gates, x1, gf, ys)


def _rope_tables(s):
    pos = np.arange(s, dtype=np.float32)
    inv_freq = np.power(np.float32(ROPE_THETA), -np.arange(0, B_ROPE_DIM, 2, dtype=np.float32) / B_ROPE_DIM)
    ang = (pos[:, None] * inv_freq[None, :]).astype(np.float32)
    cos, sin = np.cos(ang), np.sin(ang)
    pad = LANES - B_ROPE_DIM
    cos = np.concatenate([cos, cos, np.ones((s, pad), np.float32)], axis=1)
    sin = np.concatenate([sin, sin, np.zeros((s, pad), np.float32)], axis=1)
    return cos.astype(np.float32), sin.astype(np.float32)


def _window_bias():
    qi = np.arange(A_BLOCK)[:, None]
    kj = np.arange(3 * A_BLOCK)[None, :]
    dist = np.abs(qi + A_BLOCK - kj)
    slopes = np.power(np.float32(2.0), -8.0 * np.arange(1, A_HEADS + 1, dtype=np.float32) / A_HEADS)
    bias = -slopes[:, None, None] * dist.astype(np.float32)[None]
    bias = np.where((dist <= WINDOW)[None], bias, -np.inf)
    bias = bias.reshape(A_KV_HEADS, 2, 2, A_BLOCK, 3 * A_BLOCK).transpose(0, 2, 1, 3, 4)
    bias = bias.reshape(2 * A_KV_HEADS, 2 * A_BLOCK, 3 * A_BLOCK)
    no_prev = (kj < A_BLOCK)[None]
    no_next = (kj >= 2 * A_BLOCK)[None]
    ninf = -np.inf
    return np.stack([bias, np.where(no_prev, ninf, bias), np.where(no_next, ninf, bias),
                     np.where(no_prev | no_next, ninf, bias)]).astype(np.float32)


def _layer(x, attn_norm, w_in, a_sink, b_q_norm, b_w_uq, b_kv_norm, b_w_ukv, out_norm_a, out_norm_b, w_o,
           mlp_norm, w_router, b_router, w_up, b_up, w_down, b_down):
    b, s, d = x.shape
    t = b * s
    wq = b_w_uq.reshape(B_Q_RANK, B_HEADS, B_NOPE_DIM + B_ROPE_DIM)
    wq_pe = jnp.pad(wq[:, :, B_NOPE_DIM:], ((0, 0), (0, 0), (0, LANES - B_ROPE_DIM)))
    wuq = jnp.concatenate([wq[:, :, :B_NOPE_DIM].reshape(B_Q_RANK, -1), wq_pe.reshape(B_Q_RANK, -1)],
                          axis=1).astype(BF16)
    wkv = b_w_ukv.reshape(B_KV_RANK, B_HEADS, B_NOPE_DIM + B_V_DIM)
    wukv = jnp.concatenate([wkv[:, :, :B_NOPE_DIM].reshape(B_KV_RANK, -1),
                            wkv[:, :, B_NOPE_DIM:].reshape(B_KV_RANK, -1)], axis=1).astype(BF16)
    cos, sin = _rope_tables(s)

    qa, ka, va, qb, kb, vb = _prologue(x, attn_norm[None], w_in, b_q_norm[None], wuq, b_kv_norm[None], wukv,
                                       cos, sin)
    ya = _window_attn(qa, ka, va, a_sink, _window_bias())
    yb = _mla_attn(qb, kb, vb)
    wr_hi = w_router.astype(BF16)
    wr_lo = (w_router - wr_hi.astype(F32)).astype(BF16)
    x1, xm, eidx, gates, rank, counts = _out_router(
        ya.reshape(t, A_WIDTH), yb.reshape(t, B_WIDTH), x.reshape(t, d), out_norm_a[None], out_norm_b[None],
        w_o.astype(BF16), mlp_norm[None], jnp.concatenate([wr_hi, wr_lo], axis=1), b_router[None])

    nblk = (t * TOP_K + N_EXPERTS * (MOE_BM - 1) + MOE_BM - 1) // MOE_BM
    tabs, start, pad_start, pad_len, nact = _expert_schedule(counts.reshape(N_EXPERTS), nblk, t * TOP_K)
    slot_rows = _slot_rows(start, eidx.reshape(t * TOP_K), rank.reshape(t * TOP_K))
    xs = _dispatch(pad_start, pad_len, nact, slot_rows, xm, nblk * MOE_BM)
    n_short = N_EXPERTS * NF + nblk
    args = (xs, w_up, b_up, w_down, b_down)
    ys = lax.cond(nact[1] <= n_short,
                  lambda: _experts(tuple(a[:n_short] for a in tabs), *args, name="experts"),
                  lambda: _experts(tabs, *args, name="experts_long"))
    return x1, slot_rows, gates, ys


def kernel(x, attn_norm, w_in, a_sink, b_q_norm, b_w_uq, b_kv_norm, b_w_ukv, out_norm_a, out_norm_b, w_o,
           mlp_norm, w_router, b_router, w_up, b_up, w_down, b_down, final_norm):
    b, s, d = x.shape
    assert d == D_MODEL and s % min(TQ_MLA, s) == 0 and s % min(CK_MLA, s) == 0 and s % TM_PRO == 0 and attn_norm.shape[0] == 1
    x1, slot_rows, gates, ys = _layer(
        x, attn_norm[0], w_in[0], a_sink[0], b_q_norm[0], b_w_uq[0], b_kv_norm[0], b_w_ukv[0], out_norm_a[0],
        out_norm_b[0], w_o[0], mlp_norm[0], w_router[0], b_router[0], w_up[0], b_up[0], w_down[0], b_down[0])
    out = _combine(slot_rows, gates, x1, final_norm[None], ys)
    return out.reshape(b, s, d)
```

```python
import functools

import jax
import numpy as np
import jax.numpy as jnp
from jax import lax
from jax.experimental import pallas as pl
from jax.experimental.pallas import tpu as pltpu

D_MODEL = 2048
A_HEADS, A_KV_HEADS, A_HEAD_DIM = 16, 4, 64
A_GROUP = A_HEADS // A_KV_HEADS
WINDOW = 128
A_BLOCK = 128
B_HEADS, B_Q_RANK, B_KV_RANK = 8, 512, 256
B_NOPE_DIM, B_ROPE_DIM, B_V_DIM = 128, 64, 128
ROPE_THETA = 10000.0
A_WIDTH = A_HEADS * A_HEAD_DIM
B_WIDTH = B_HEADS * B_V_DIM
A_KV_COLS = A_KV_HEADS * A_HEAD_DIM
N_EXPERTS, TOP_K, D_FF = 32, 4, 2048
SWIGLU_ALPHA, SWIGLU_LIMIT = 1.702, 7.0
EPS = 1e-5

LANES = 128
SUBLANES = 8
B_QK_PAD = 2 * LANES
IN_COLS_PAD = A_WIDTH + 2 * A_KV_COLS + B_Q_RANK + B_KV_RANK + LANES
VMEM_LIMIT = 56 * 1024 * 1024
ROW_TILES = D_MODEL // LANES
ROW_PITCH = ROW_TILES + 1

TM_PRO = 256
TQ_MLA = 1024
CK_MLA = 2048
HEADS_MLA = 2
TM_OUT = 512
TM_DISPATCH = 1024
TM_COMBINE = 256
MOE_BM = 512
MOE_SUB = 256
MOE_RM = 2048
MOE_TF = 256
WAIT_UNROLL = 16

BF16 = jnp.bfloat16
F32 = jnp.float32


def _rms(x, g):
    return x * lax.rsqrt(jnp.mean(x * x, axis=-1, keepdims=True) + EPS) * g


def _const_spec(shape):
    nd = len(shape)
    return pl.BlockSpec(shape, lambda *_: (0,) * nd, pipeline_mode=pl.Buffered(1))


def _store_token_rows(ref, first, val):
    n = val.shape[0]
    for j in range(ROW_TILES):
        ref[pl.ds(first * ROW_PITCH + j, n, stride=ROW_PITCH), :] = val[:, j * LANES:(j + 1) * LANES]
    ref[pl.ds(first * ROW_PITCH + ROW_TILES, n, stride=ROW_PITCH), :] = jnp.zeros((n, LANES), val.dtype)


def _load_token_rows(ref, n):
    return [ref[pl.ds(j, n, stride=ROW_PITCH), :] for j in range(ROW_TILES)]


def _rope(x, cos, sin):
    lane = lax.broadcasted_iota(jnp.int32, x.shape, 1)
    up = pltpu.roll(x, LANES - B_ROPE_DIM // 2, 1)
    dn = pltpu.roll(x, B_ROPE_DIM // 2, 1)
    sw = jnp.where(lane < B_ROPE_DIM // 2, -up, jnp.where(lane < B_ROPE_DIM, dn, 0.0))
    return x * cos + sw * sin


def _prologue_body(x_ref, g_ref, win_ref, qn_ref, wuq_ref, kvn_ref, wukv_ref, cos_ref, sin_ref,
                   qa_ref, ka_ref, va_ref, qb_ref, kb_ref, vb_ref, winb_ref):
    @pl.when((pl.program_id(0) == 0) & (pl.program_id(1) == 0))
    def _():
        n_in = win_ref.shape[1]
        winb_ref[:, :n_in] = win_ref[...].astype(BF16)
        winb_ref[:, n_in:] = jnp.zeros((winb_ref.shape[0], IN_COLS_PAD - n_in), BF16)

    x = x_ref[0]
    xn = _rms(x, g_ref[...]).astype(BF16)
    proj = jnp.dot(xn, winb_ref[...], preferred_element_type=F32)
    c0 = A_WIDTH
    c1 = c0 + A_KV_COLS
    c2 = c1 + A_KV_COLS
    c3 = c2 + B_Q_RANK
    c4 = c3 + B_KV_RANK
    qa_ref[0] = (proj[:, :A_WIDTH] * A_HEAD_DIM ** -0.5).astype(BF16)
    low = lax.broadcasted_iota(jnp.int32, (proj.shape[0], LANES), 1) < A_HEAD_DIM
    for src, dst in ((c0, ka_ref), (c1, va_ref)):
        for pair in range(A_KV_HEADS // 2):
            two = proj[:, src + pair * LANES:src + (pair + 1) * LANES]
            swapped = pltpu.roll(two, A_HEAD_DIM, 1)
            dst[0, 4 * pair + 0] = jnp.where(low, two, 0.0).astype(BF16)
            dst[0, 4 * pair + 1] = jnp.where(low, 0.0, swapped).astype(BF16)
            dst[0, 4 * pair + 2] = jnp.where(low, swapped, 0.0).astype(BF16)
            dst[0, 4 * pair + 3] = jnp.where(low, 0.0, two).astype(BF16)
    cq = _rms(proj[:, c2:c3], qn_ref[...]).astype(BF16)
    q = jnp.dot(cq, wuq_ref[...], preferred_element_type=F32)
    ckv = _rms(proj[:, c3:c4], kvn_ref[...]).astype(BF16)
    kv = jnp.dot(ckv, wukv_ref[...], preferred_element_type=F32)
    cos = cos_ref[...]
    sin = sin_ref[...]
    kpe = _rope(proj[:, c4:c4 + LANES], cos, sin).astype(BF16)
    b_scale = (B_NOPE_DIM + B_ROPE_DIM) ** -0.5
    hw = B_HEADS * LANES
    for h in range(B_HEADS):
        sl = slice(h * LANES, (h + 1) * LANES)
        qb_ref[0, h, :, :LANES] = (q[:, sl] * b_scale).astype(BF16)
        qpe = _rope(q[:, hw + h * LANES:hw + (h + 1) * LANES], cos, sin)
        qb_ref[0, h, :, LANES:] = (qpe * b_scale).astype(BF16)
        kb_ref[0, h, :, :LANES] = kv[:, sl].astype(BF16)
        kb_ref[0, h, :, LANES:] = kpe
        vb_ref[0, h] = kv[:, hw + h * LANES:hw + (h + 1) * LANES].astype(BF16)


def _prologue(x, g, win, qn, wuq, kvn, wukv, cos, sin):
    b, s, d = x.shape
    tm = TM_PRO
    grid = (b, s // tm)
    sds = jax.ShapeDtypeStruct
    out_shape = (
        sds((b, s, A_WIDTH), BF16),
        sds((b, 2 * A_KV_HEADS, s, LANES), BF16),
        sds((b, 2 * A_KV_HEADS, s, LANES), BF16),
        sds((b, B_HEADS, s, B_QK_PAD), BF16),
        sds((b, B_HEADS, s, B_QK_PAD), BF16),
        sds((b, B_HEADS, s, B_V_DIM), BF16),
    )

    def hspec(nh, w):
        return pl.BlockSpec((1, nh, tm, w), lambda bi, i: (bi, 0, i, 0))

    return pl.pallas_call(
        _prologue_body,
        grid=grid,
        in_specs=[
            pl.BlockSpec((1, tm, d), lambda bi, i: (bi, i, 0)),
            _const_spec(g.shape), _const_spec(win.shape), _const_spec(qn.shape), _const_spec(wuq.shape),
            _const_spec(kvn.shape), _const_spec(wukv.shape),
            pl.BlockSpec((tm, LANES), lambda bi, i: (i, 0)),
            pl.BlockSpec((tm, LANES), lambda bi, i: (i, 0)),
        ],
        out_specs=(pl.BlockSpec((1, tm, A_WIDTH), lambda bi, i: (bi, i, 0)),
                   hspec(2 * A_KV_HEADS, LANES), hspec(2 * A_KV_HEADS, LANES),
                   hspec(B_HEADS, B_QK_PAD), hspec(B_HEADS, B_QK_PAD), hspec(B_HEADS, B_V_DIM)),
        out_shape=out_shape,
        scratch_shapes=[pltpu.VMEM((d, IN_COLS_PAD), BF16)],
        compiler_params=pltpu.CompilerParams(dimension_semantics=("arbitrary", "arbitrary"),
                                             vmem_limit_bytes=VMEM_LIMIT),
        name="prologue",
    )(x, g, win, qn, wuq, kvn, wukv, cos, sin)


def _window_body(sink_ref, q_ref, kp_ref, kc_ref, kn_ref, vp_ref, vc_ref, vn_ref, bias_ref, o_ref):
    two = 2 * A_BLOCK
    low = lax.broadcasted_iota(jnp.int32, (two, LANES), 1) < A_HEAD_DIM
    for hk in range(A_KV_HEADS):
        q2 = jnp.concatenate([q_ref[0, :, (2 * hk) * LANES:(2 * hk + 1) * LANES],
                              q_ref[0, :, (2 * hk + 1) * LANES:(2 * hk + 2) * LANES]], axis=0)
        acc = jnp.zeros((two, LANES), F32)
        inv = []
        for half in range(2):
            z = 2 * hk + half
            kband = jnp.concatenate([kp_ref[0, z], kc_ref[0, z], kn_ref[0, z]], axis=0)
            vband = jnp.concatenate([vp_ref[0, z], vc_ref[0, z], vn_ref[0, z]], axis=0)
            s = lax.dot_general(q2, kband, (((1,), (1,)), ((), ())), preferred_element_type=F32)
            s = s + bias_ref[0, z]
            sink = jnp.concatenate(
                [jnp.full((A_BLOCK, 1), sink_ref[hk * A_GROUP + 2 * jj + half], F32) for jj in range(2)], axis=0)
            m = jnp.maximum(jnp.max(s, axis=-1, keepdims=True), sink)
            p = jnp.exp(s - m)
            den = jnp.sum(p, axis=-1, keepdims=True) + jnp.exp(sink - m)
            inv.append(1.0 / den)
            acc = acc + jnp.dot(p.astype(BF16), vband, preferred_element_type=F32)
        o = (acc * jnp.where(low, inv[0], inv[1])).astype(BF16)
        o_ref[0, :, (2 * hk) * LANES:(2 * hk + 1) * LANES] = o[:A_BLOCK]
        o_ref[0, :, (2 * hk + 1) * LANES:(2 * hk + 2) * LANES] = o[A_BLOCK:]


def _window_attn(qa, ka, va, sink, bias):
    b, s, _ = qa.shape
    nb = s // A_BLOCK
    kv_blk = (1, 2 * A_KV_HEADS, A_BLOCK, LANES)
    prev = pl.BlockSpec(kv_blk, lambda bi, n: (bi, 0, jnp.maximum(n - 1, 0), 0))
    cur = pl.BlockSpec(kv_blk, lambda bi, n: (bi, 0, n, 0))
    nxt = pl.BlockSpec(kv_blk, lambda bi, n: (bi, 0, jnp.minimum(n + 1, nb - 1), 0))

    def edge(bi, n):
        return ((n == 0).astype(jnp.int32) + 2 * (n == nb - 1).astype(jnp.int32), 0, 0, 0)

    return pl.pallas_call(
        _window_body,
        grid=(b, nb),
        in_specs=[
            pl.BlockSpec(memory_space=pltpu.SMEM),
            pl.BlockSpec((1, A_BLOCK, A_WIDTH), lambda bi, n: (bi, n, 0)),
            prev, cur, nxt, prev, cur, nxt,
            pl.BlockSpec((1,) + bias.shape[1:], edge),
        ],
        out_specs=pl.BlockSpec((1, A_BLOCK, A_WIDTH), lambda bi, n: (bi, n, 0)),
        out_shape=jax.ShapeDtypeStruct((b, s, A_WIDTH), BF16),
        compiler_params=pltpu.CompilerParams(dimension_semantics=("parallel", "parallel"),
                                             vmem_limit_bytes=VMEM_LIMIT),
        name="window_attn",
    )(sink, qa, ka, ka, ka, va, va, va, bias)


def _mla_body(q_ref, k_ref, v_ref, o_ref):
    tq = q_ref.shape[2]
    s_len = k_ref.shape[2]
    ck = min(CK_MLA, s_len)
    for hh in range(HEADS_MLA):
        q = q_ref[0, hh]
        m = jnp.full((tq, 1), -jnp.inf, F32)
        l = jnp.zeros((tq, 1), F32)
        acc = jnp.zeros((tq, B_V_DIM), F32)
        for c in range(s_len // ck):
            k_c = k_ref[0, hh, c * ck:(c + 1) * ck, :]
            v_c = v_ref[0, hh, c * ck:(c + 1) * ck, :]
            s = lax.dot_general(q, k_c, (((1,), (1,)), ((), ())), preferred_element_type=F32)
            m_new = jnp.maximum(m, jnp.max(s, axis=-1, keepdims=True))
            alpha = jnp.exp(m - m_new)
            p = jnp.exp(s - m_new)
            l = alpha * l + jnp.sum(p, axis=-1, keepdims=True)
            acc = alpha * acc + jnp.dot(p.astype(BF16), v_c, preferred_element_type=F32)
            m = m_new
        o_ref[0, :, hh * B_V_DIM:(hh + 1) * B_V_DIM] = (acc / l).astype(BF16)


def _mla_attn(qb, kb, vb):
    b, nh, s, _ = qb.shape
    tq = min(TQ_MLA, s)
    hs = HEADS_MLA
    return pl.pallas_call(
        _mla_body,
        grid=(b, nh // hs, s // tq),
        in_specs=[
            pl.BlockSpec((1, hs, tq, B_QK_PAD), lambda bi, h, i: (bi, h, i, 0)),
            pl.BlockSpec((1, hs, s, B_QK_PAD), lambda bi, h, i: (bi, h, 0, 0)),
            pl.BlockSpec((1, hs, s, B_V_DIM), lambda bi, h, i: (bi, h, 0, 0)),
        ],
        out_specs=pl.BlockSpec((1, tq, hs * B_V_DIM), lambda bi, h, i: (bi, i, h)),
        out_shape=jax.ShapeDtypeStruct((b, s, B_WIDTH), BF16),
        compiler_params=pltpu.CompilerParams(dimension_semantics=("parallel", "parallel", "parallel"),
                                             vmem_limit_bytes=VMEM_LIMIT),
        name="mla_attn",
    )(qb, kb, vb)


def _out_router_body(ya_ref, yb_ref, x_ref, ga_ref, gb_ref, wo_ref, gm_ref, wr_ref, br_ref,
                     x1_ref, xm_ref, eidx_ref, gate_ref, rank_ref, cnt_ref, run_ref):
    i = pl.program_id(0)

    @pl.when(i == 0)
    def _():
        run_ref[...] = jnp.zeros_like(run_ref)

    na = _rms(ya_ref[...].astype(F32), ga_ref[...]).astype(BF16)
    nb = _rms(yb_ref[...].astype(F32), gb_ref[...]).astype(BF16)
    att = jnp.dot(na, wo_ref[:A_WIDTH, :], preferred_element_type=F32)
    att = att + jnp.dot(nb, wo_ref[A_WIDTH:, :], preferred_element_type=F32)
    x1 = x_ref[...] + att
    x1_ref[...] = x1
    hn = _rms(x1, gm_ref[...])
    _store_token_rows(xm_ref, 0, hn)
    tm = hn.shape[0]
    hi = hn.astype(BF16)
    lo = (hn - hi.astype(F32)).astype(BF16)
    prod = jnp.dot(jnp.concatenate([hi, lo], axis=0), wr_ref[...], preferred_element_type=F32)
    logits = (prod[:tm, :N_EXPERTS] + prod[:tm, N_EXPERTS:] + prod[tm:, :N_EXPERTS] + prod[tm:, N_EXPERTS:]
              + br_ref[...])
    lane = lax.broadcasted_iota(jnp.int32, (tm, N_EXPERTS), 1)
    work = logits
    sel = jnp.zeros((tm, N_EXPERTS), F32)
    hots, vals, idxs = [], [], []
    for _k in range(TOP_K):
        mx = jnp.max(work, axis=-1, keepdims=True)
        idx = jnp.min(jnp.where(work == mx, lane, N_EXPERTS), axis=-1, keepdims=True)
        hot = lane == idx
        hots.append(hot)
        vals.append(mx)
        idxs.append(idx)
        sel = sel + hot.astype(F32)
        work = jnp.where(hot, -jnp.inf, work)
    exps = [jnp.exp(v - vals[0]) for v in vals]
    den = exps[0] + exps[1] + exps[2] + exps[3]
    r_i = lax.broadcasted_iota(jnp.int32, (tm, tm), 0)
    c_i = lax.broadcasted_iota(jnp.int32, (tm, tm), 1)
    tri = (c_i < r_i).astype(BF16)
    before = jnp.dot(tri, sel.astype(BF16), preferred_element_type=F32) + run_ref[...]
    lane4 = lax.broadcasted_iota(jnp.int32, (tm, TOP_K), 1)
    eidx = jnp.zeros((tm, TOP_K), jnp.int32)
    gate = jnp.zeros((tm, TOP_K), F32)
    rank = jnp.zeros((tm, TOP_K), jnp.int32)
    for k in range(TOP_K):
        rk = jnp.sum(jnp.where(hots[k], before, 0.0), axis=-1, keepdims=True).astype(jnp.int32)
        eidx = jnp.where(lane4 == k, idxs[k], eidx)
        gate = jnp.where(lane4 == k, exps[k] / den, gate)
        rank = jnp.where(lane4 == k, rk, rank)
    eidx_ref[...] = eidx
    gate_ref[...] = gate
    rank_ref[...] = rank
    run = run_ref[...] + jnp.sum(sel, axis=0, keepdims=True)
    run_ref[...] = run
    cnt_ref[...] = run.astype(jnp.int32)


def _out_router(ya, yb, x2d, ga, gb, wo, gm, wr, br):
    t, d = x2d.shape
    tm = min(TM_OUT, t)
    sds = jax.ShapeDtypeStruct

    def row(w):
        return pl.BlockSpec((tm, w), lambda i: (i, 0))

    return pl.pallas_call(
        _out_router_body,
        grid=(t // tm,),
        in_specs=[row(A_WIDTH), row(B_WIDTH), row(d), _const_spec(ga.shape), _const_spec(gb.shape),
                  _const_spec(wo.shape), _const_spec(gm.shape), _const_spec(wr.shape), _const_spec(br.shape)],
        out_specs=(row(d), pl.BlockSpec((tm * ROW_PITCH, LANES), lambda i: (i, 0)),
                   row(TOP_K), row(TOP_K), row(TOP_K),
                   pl.BlockSpec((1, N_EXPERTS), lambda i: (0, 0))),
        out_shape=(sds((t, d), F32), sds((t * ROW_PITCH, LANES), F32), sds((t, TOP_K), jnp.int32),
                   sds((t, TOP_K), F32),
                   sds((t, TOP_K), jnp.int32), sds((1, N_EXPERTS), jnp.int32)),
        scratch_shapes=[pltpu.VMEM((1, N_EXPERTS), F32)],
        compiler_params=pltpu.CompilerParams(dimension_semantics=("arbitrary",),
                                             vmem_limit_bytes=VMEM_LIMIT),
        name="out_router",
    )(ya, yb, x2d, ga, gb, wo, gm, wr, br)


def _token_copy(src_ref, src_tok, dst_ref, dst_tok, sem, ntok=1):
    return _rows_copy(src_ref, src_tok * ROW_PITCH, dst_ref, dst_tok * ROW_PITCH, sem, ntok)


def _rows_copy(src_ref, src_row, dst_ref, dst_row, sem, ntok=1):
    rows = ntok * ROW_PITCH
    return pltpu.make_async_copy(src_ref.at[pl.ds(src_row, rows)], dst_ref.at[pl.ds(dst_row, rows)], sem)


def _slot_rows_body(start_ref, eidx_ref, rank_ref, o_ref):
    e = eidx_ref[...]
    base = jnp.zeros_like(e)
    for ex in range(N_EXPERTS):
        base = jnp.where(e == ex, start_ref[ex], base)
    o_ref[...] = (base + rank_ref[...]) * ROW_PITCH


def _slot_rows(start, eidx, rank):
    n = eidx.shape[0]
    shape2 = (n // LANES, LANES)
    out = pl.pallas_call(
        _slot_rows_body,
        in_specs=[pl.BlockSpec(memory_space=pltpu.SMEM), pl.BlockSpec(shape2, lambda: (0, 0)),
                  pl.BlockSpec(shape2, lambda: (0, 0))],
        out_specs=pl.BlockSpec(shape2, lambda: (0, 0)),
        out_shape=jax.ShapeDtypeStruct(shape2, jnp.int32),
        name="slot_rows",
    )(start, eidx.reshape(shape2), rank.reshape(shape2))
    return out.reshape(n)


def _zero_fill_pads(pad_start_ref, pad_len_ref, nact_ref, xs_ref, zero_ref, zsem):
    bm = zero_ref.shape[0] // ROW_PITCH
    nblk = xs_ref.shape[0] // (bm * ROW_PITCH)
    zero_ref[...] = jnp.zeros_like(zero_ref)

    def sweep(do):
        def per_expert(e, c):
            off = pad_start_ref[e]
            n = pad_len_ref[e]
            p = bm // 2
            while p >= 1:
                take = (n & p) != 0

                @pl.when(take)
                def _(p=p, off=off):
                    do(_token_copy(zero_ref, 0, xs_ref, off, zsem, p))

                off = off + jnp.where(take, p, 0)
                p //= 2
            return c

        lax.fori_loop(0, N_EXPERTS, per_expert, 0)

        def tail(j, c):
            do(_token_copy(zero_ref, 0, xs_ref, j * bm, zsem, bm))
            return c

        lax.fori_loop(nact_ref[0], nblk, tail, 0)

    sweep(lambda cp: cp.start())
    sweep(lambda cp: cp.wait())


def _dispatch_body(pad_start_ref, pad_len_ref, nact_ref, slot_ref, xm_ref, xs_ref, zero_ref, sem, zsem):
    tm = slot_ref.shape[0] // TOP_K

    @pl.when(pl.program_id(0) == 0)
    def _():
        _zero_fill_pads(pad_start_ref, pad_len_ref, nact_ref, xs_ref, zero_ref, zsem)

    def issue(i, c):
        for k in range(TOP_K):
            _rows_copy(xm_ref, i * ROW_PITCH, xs_ref, slot_ref[i * TOP_K + k], sem).start(priority=k % 2)
        return c

    lax.fori_loop(0, tm, issue, 0, unroll=2)

    def drain(i, c):
        for _ in range(WAIT_UNROLL):
            _token_copy(xm_ref, 0, xs_ref, 0, sem).wait()
        return c

    lax.fori_loop(0, tm * TOP_K // WAIT_UNROLL, drain, 0)


def _dispatch(pad_start, pad_len, nact, slot_rows, xm, n_slots):
    t = xm.shape[0] // ROW_PITCH
    tm = min(TM_DISPATCH, t)
    grid_spec = pltpu.PrefetchScalarGridSpec(
        num_scalar_prefetch=3,
        grid=(t // tm,),
        in_specs=[pl.BlockSpec((tm * TOP_K,), lambda i, *_: (i,), memory_space=pltpu.SMEM),
                  pl.BlockSpec((tm * ROW_PITCH, LANES), lambda i, *_: (i, 0))],
        out_specs=pl.BlockSpec(memory_space=pl.ANY),
        scratch_shapes=[pltpu.VMEM((MOE_BM * ROW_PITCH, LANES), F32), pltpu.SemaphoreType.DMA(()),
                        pltpu.SemaphoreType.DMA(())],
    )
    return pl.pallas_call(
        _dispatch_body,
        grid_spec=grid_spec,
        out_shape=jax.ShapeDtypeStruct((n_slots * ROW_PITCH, LANES), F32),
        compiler_params=pltpu.CompilerParams(dimension_semantics=("arbitrary",), has_side_effects=True,
                                             vmem_limit_bytes=VMEM_LIMIT),
        name="dispatch",
    )(pad_start, pad_len, nact, slot_rows, xm)


STEP_UP, STEP_DOWN, STEP_TAIL, STEP_IDLE = 0, 1, 2, 3
PREFETCH_PER_DOWN = 2
NF = D_FF // MOE_TF
PASS_BLOCKS = MOE_RM // MOE_BM


def _swiglu(hg, hu):
    gate = jnp.minimum(hg, SWIGLU_LIMIT)
    up = jnp.clip(hu, -SWIGLU_LIMIT, SWIGLU_LIMIT)
    glu = gate / (1.0 + jnp.exp(-SWIGLU_ALPHA * gate))
    return (up + 1.0) * glu


def _experts_body(e_ref, f_ref, kind_ref, blk_ref, blk0_ref, nsub_ref, pre_ref, nxb_ref,
                  xs_ref, wg_ref, wu_ref, bup_ref, wd_ref, bdn_ref,
                  y_ref, xb_ref, h_ref, wdb_ref, stg_ref, sem):
    t = pl.program_id(0)
    kind = kind_ref[t]
    f = f_ref[t]
    nsub = nsub_ref[t]
    erow = pl.ds(e_ref[t], 1)
    ftile = pl.multiple_of(f * MOE_TF, MOE_TF)

    def x_copy(first_blk, j, slot):
        rows = MOE_SUB * ROW_PITCH
        first = pl.multiple_of((first_blk * MOE_BM + j * MOE_SUB) * ROW_PITCH, SUBLANES)
        return pltpu.make_async_copy(xs_ref.at[pl.ds(first, rows)], stg_ref.at[slot], sem.at[slot])

    def unpack(j, slot):
        rows = pl.ds(pl.multiple_of(j * MOE_SUB, MOE_SUB), MOE_SUB)
        for jt, tile in enumerate(_load_token_rows(stg_ref.at[slot], MOE_SUB)):
            xb_ref[rows, jt * LANES:(jt + 1) * LANES] = tile.astype(BF16)

    @pl.when(kind == STEP_UP)
    def _up():
        pre = pre_ref[t]

        @pl.when((f == 0) & (pre < nsub))
        def _load_rows():
            blk0 = blk0_ref[t]
            x_copy(blk0, pre, pre % 2).start()

            def body(j, c):
                slot = j % 2

                @pl.when(j + 1 < nsub)
                def _():
                    x_copy(blk0, j + 1, 1 - slot).start()

                x_copy(blk0, j, slot).wait()
                unpack(j, slot)
                return c

            lax.fori_loop(pre, nsub, body, 0)

        wdb_ref[pl.ds(pl.multiple_of(f * MOE_TF, MOE_TF), MOE_TF), :] = wd_ref[0].astype(BF16)

        def up_rows(row0, nrows):
            x = xb_ref[pl.ds(row0, nrows), :]
            hg = jnp.dot(x, wg_ref[0].astype(BF16), preferred_element_type=F32) + bup_ref[erow, pl.ds(ftile, MOE_TF)]
            hu = (jnp.dot(x, wu_ref[0].astype(BF16), preferred_element_type=F32)
                  + bup_ref[erow, pl.ds(pl.multiple_of(D_FF + ftile, MOE_TF), MOE_TF)])
            act = _swiglu(hg, hu).astype(BF16)
            h_ref[pl.ds(row0, nrows), pl.ds(pl.multiple_of(f * MOE_TF, MOE_TF), MOE_TF)] = act

        quad = 4 * MOE_SUB
        nquad = nsub // 4
        rem = nsub % 4

        def quad_rows(i, c):
            base = pl.multiple_of(i * quad, quad)
            up_rows(base, quad // 2)
            up_rows(base + quad // 2, quad // 2)
            return c

        merged = nsub == 5
        short_tail = nxb_ref[t] == 1

        @pl.when(merged & jnp.logical_not(short_tail))
        def _():
            up_rows(0, 2 * MOE_SUB)
            up_rows(2 * MOE_SUB, 3 * MOE_SUB)

        @pl.when(merged & short_tail)
        def _():
            up_rows(0, 2 * MOE_SUB)
            up_rows(2 * MOE_SUB, 2 * MOE_SUB + MOE_SUB // 2)
            h_ref[4 * MOE_SUB + MOE_SUB // 2:5 * MOE_SUB, pl.ds(ftile, MOE_TF)] = jnp.zeros(
                (MOE_SUB // 2, MOE_TF), BF16)

        @pl.when(jnp.logical_not(merged))
        def _():
            lax.fori_loop(0, nquad, quad_rows, 0)

            @pl.when(rem >= 2)
            def _():
                up_rows(pl.multiple_of(nquad * quad, quad), 2 * MOE_SUB)

            @pl.when(rem % 2 == 1)
            def _():
                up_rows(pl.multiple_of(nquad * quad + (rem // 2) * 2 * MOE_SUB, MOE_SUB), MOE_SUB)

    @pl.when(kind == STEP_DOWN)
    def _down():
        jblk = blk_ref[t] - blk0_ref[t]
        row0 = pl.multiple_of(jblk * MOE_BM, MOE_BM)
        nx_nsub = pre_ref[t]
        nx_blk0 = nxb_ref[t]
        ahead = [jblk * PREFETCH_PER_DOWN + u for u in range(PREFETCH_PER_DOWN)]
        for u, j in enumerate(ahead):
            @pl.when(j < nx_nsub)
            def _(u=u, j=j):
                x_copy(nx_blk0, j, u).start()

        def down_rows(nrows):
            hrows = h_ref[pl.ds(row0, nrows), :]
            return jnp.dot(hrows, wdb_ref[...], preferred_element_type=F32) + bdn_ref[erow, :]

        @pl.when(nsub == 2)
        def _():
            _store_token_rows(y_ref, 0, down_rows(MOE_BM))

        for code, nrows in ((1, MOE_SUB), (0, MOE_SUB // 2)):
            @pl.when(nsub == code)
            def _(nrows=nrows):
                _store_token_rows(y_ref, 0, down_rows(nrows))
                y_ref[nrows * ROW_PITCH:, :] = jnp.zeros(((MOE_BM - nrows) * ROW_PITCH, LANES), F32)

        for u, j in enumerate(ahead):
            @pl.when(j < nx_nsub)
            def _(u=u, j=j):
                x_copy(nx_blk0, j, u).wait()
                unpack(j, u)

    @pl.when(kind == STEP_TAIL)
    def _tail():
        y_ref[...] = jnp.zeros_like(y_ref)


def _schedule_body(n_blocks, cnt_ref, e_ref, f_ref, kind_ref, blk_ref, blk0_ref, nsub_ref, pre_ref, nxb_ref,
                   start_ref, pad_start_ref, pad_len_ref, nact_ref):
    n_steps = e_ref.shape[0]
    sub_per_blk = MOE_BM // MOE_SUB
    zero = jnp.int32(0)

    def put(idx, e, f, kind, blk, blk0, nsub, pre, nxb=zero):
        e_ref[idx] = e
        f_ref[idx] = f
        kind_ref[idx] = kind
        blk_ref[idx] = blk
        blk0_ref[idx] = blk0
        nsub_ref[idx] = nsub
        pre_ref[idx] = pre
        nxb_ref[idx] = nxb

    def per_expert(e, carry):
        t, blk, last_e, prev_down, prev_nb = carry
        n = cnt_ref[e]
        nblk = (n + (MOE_BM - 1)) // MOE_BM
        start_ref[e] = blk * MOE_BM
        pad_start_ref[e] = blk * MOE_BM + n
        pad_len_ref[e] = nblk * MOE_BM - n

        def per_pass(p, carry2):
            t2, pdown, pnb = carry2
            b0 = blk + p * PASS_BLOCKS
            nb = jnp.minimum(nblk - p * PASS_BLOCKS, PASS_BLOCKS)
            rows = jnp.minimum(n - p * MOE_RM, MOE_RM)
            nsub = (rows + (MOE_SUB - 1)) // MOE_SUB
            pre = jnp.minimum(nsub, pnb * PREFETCH_PER_DOWN)
            short_tail = ((nsub == 5) & (rows <= 4 * MOE_SUB + MOE_SUB // 2)).astype(jnp.int32)
            for f in range(NF):
                put(t2 + f, e, f, STEP_UP, b0, b0, nsub, pre, short_tail)

            def patch(j, c):
                pre_ref[pdown + j] = nsub
                nxb_ref[pdown + j] = b0
                return c

            lax.fori_loop(0, pnb, patch, 0)

            def per_blk(j, c):
                put(t2 + NF + j, e, NF - 1, STEP_DOWN, b0 + j, b0,
                    jnp.where(rows - j * MOE_BM <= MOE_SUB // 2, 0,
                              jnp.clip(nsub - j * sub_per_blk, 1, sub_per_blk)), zero)
                return c

            lax.fori_loop(0, nb, per_blk, 0)
            return t2 + NF + nb, t2 + NF, nb

        t, prev_down, prev_nb = lax.fori_loop(0, (nblk + (PASS_BLOCKS - 1)) // PASS_BLOCKS, per_pass,
                                              (t, prev_down, prev_nb))
        return t, blk + nblk, jnp.where(n > 0, e, last_e), prev_down, prev_nb

    t, nact, last_e, _, _ = lax.fori_loop(0, N_EXPERTS, per_expert, (zero, zero, zero, zero, zero))
    nact_ref[0] = nact
    nact_ref[1] = t + n_blocks - nact

    def spare(i, c):
        is_tail = i < n_blocks - nact
        blk = jnp.where(is_tail, nact + i, n_blocks - 1)
        put(t + i, last_e, NF - 1, jnp.where(is_tail, STEP_TAIL, STEP_IDLE), blk, blk, 1, zero)
        return c

    lax.fori_loop(0, n_steps - t, spare, 0)


def _expert_schedule(counts, n_blocks, n_assign):
    n_pass_max = N_EXPERTS + n_assign // MOE_RM
    n_steps = n_pass_max * NF + n_blocks
    smem = pl.BlockSpec(memory_space=pltpu.SMEM)
    i32 = jnp.int32
    out_shape = tuple(jax.ShapeDtypeStruct((n,), i32) for n in (n_steps,) * 8 + (N_EXPERTS,) * 3 + (2,))
    outs = pl.pallas_call(
        functools.partial(_schedule_body, n_blocks),
        in_specs=[smem],
        out_specs=tuple(smem for _ in out_shape),
        out_shape=out_shape,
        name="schedule",
    )(counts)
    return outs[:8], outs[8], outs[9], outs[10], outs[11]


def _experts(tabs, xs, w_up, b_up, w_down, b_down, name):
    d = D_MODEL
    n_steps = tabs[0].shape[0]
    grid_spec = pltpu.PrefetchScalarGridSpec(
        num_scalar_prefetch=len(tabs),
        grid=(n_steps,),
        in_specs=[
            pl.BlockSpec(memory_space=pl.ANY),
            pl.BlockSpec((1, d, MOE_TF), lambda t, e, f, *_: (e[t], 0, f[t])),
            pl.BlockSpec((1, d, MOE_TF), lambda t, e, f, *_: (e[t], 0, NF + f[t])),
            _const_spec(b_up.shape),
            pl.BlockSpec((1, MOE_TF, d), lambda t, e, f, *_: (e[t], f[t], 0)),
            _const_spec(b_down.shape),
        ],
        out_specs=pl.BlockSpec((MOE_BM * ROW_PITCH, LANES), lambda t, e, f, kind, blk, *_: (blk[t], 0)),
        scratch_shapes=[
            pltpu.VMEM((MOE_RM, d), BF16),
            pltpu.VMEM((MOE_RM, D_FF), BF16),
            pltpu.VMEM((D_FF, d), BF16),
            pltpu.VMEM((2, MOE_SUB * ROW_PITCH, LANES), F32),
            pltpu.SemaphoreType.DMA((2,)),
        ],
    )
    return pl.pallas_call(
        _experts_body,
        grid_spec=grid_spec,
        out_shape=jax.ShapeDtypeStruct(xs.shape, F32),
        compiler_params=pltpu.CompilerParams(dimension_semantics=("arbitrary",),
                                             vmem_limit_bytes=VMEM_LIMIT),
        name=name,
    )(*tabs, xs, w_up, w_up, b_up, w_down, b_down)


def _combine_body(slot_ref, slot_nx_ref, gate_ref, x1_ref, gf_ref, ys_ref, o_ref, buf_ref, sem):
    step = pl.program_id(0)
    slot = step % 2
    tm = x1_ref.shape[0]

    def gather(rows_ref, s):
        def issue(i, c):
            for k in range(TOP_K):
                _rows_copy(ys_ref, rows_ref[i * TOP_K + k], buf_ref.at[s, k], i * ROW_PITCH,
                           sem.at[s]).start(priority=k % 2)
            return c

        lax.fori_loop(0, tm, issue, 0, unroll=2)

    @pl.when(step == 0)
    def _():
        gather(slot_ref, 0)

    for s in range(2):
        @pl.when((step + 1 < pl.num_programs(0)) & (slot == 1 - s))
        def _(s=s):
            gather(slot_nx_ref, s)

    def drain(i, c):
        for _ in range(WAIT_UNROLL):
            _token_copy(ys_ref, 0, buf_ref.at[slot, 0], 0, sem.at[slot]).wait()
        return c

    lax.fori_loop(0, tm * TOP_K // WAIT_UNROLL, drain, 0)
    gate = gate_ref[...]
    tiles = [x1_ref[:, j * LANES:(j + 1) * LANES] for j in range(ROW_TILES)]
    for k in range(TOP_K):
        g = gate[:, k:k + 1]
        for j, tile in enumerate(_load_token_rows(buf_ref.at[slot, k], tm)):
            tiles[j] = tiles[j] + g * tile
    o_ref[...] = _rms(jnp.concatenate(tiles, axis=1), gf_ref[...])


def _combine(slot_rows, gates, x1, gf, ys):
    t, d = x1.shape
    tm = min(TM_COMBINE, t)
    n = t // tm
    idx_blk = (tm * TOP_K,)
    return pl.pallas_call(
        _combine_body,
        grid=(n,),
        in_specs=[pl.BlockSpec(idx_blk, lambda i: (i,), memory_space=pltpu.SMEM),
                  pl.BlockSpec(idx_blk, lambda i: (jnp.minimum(i + 1, n - 1),), memory_space=pltpu.SMEM),
                  pl.BlockSpec((tm, TOP_K), lambda i: (i, 0)),
                  pl.BlockSpec((tm, d), lambda i: (i, 0)),
                  _const_spec(gf.shape),
                  pl.BlockSpec(memory_space=pl.ANY)],
        out_specs=pl.BlockSpec((tm, d), lambda i: (i, 0)),
        scratch_shapes=[pltpu.VMEM((2, TOP_K, tm * ROW_PITCH, LANES), F32), pltpu.SemaphoreType.DMA((2,))],
        out_shape=jax.ShapeDtypeStruct((t, d), F32),
        compiler_params=pltpu.CompilerParams(dimension_semantics=("arbitrary",),
                                             vmem_limit_bytes=VMEM_LIMIT),
        name="combine",
    )(slot_rows, slot_rows, gates, x1, gf, ys)


def _rope_tables(s):
    pos = np.arange(s, dtype=np.float32)
    inv_freq = np.power(np.float32(ROPE_THETA), -np.arange(0, B_ROPE_DIM, 2, dtype=np.float32) / B_ROPE_DIM)
    ang = (pos[:, None] * inv_freq[None, :]).astype(np.float32)
    cos, sin = np.cos(ang), np.sin(ang)
    pad = LANES - B_ROPE_DIM
    cos = np.concatenate([cos, cos, np.ones((s, pad), np.float32)], axis=1)
    sin = np.concatenate([sin, sin, np.zeros((s, pad), np.float32)], axis=1)
    return cos.astype(np.float32), sin.astype(np.float32)


def _window_bias():
    qi = np.arange(A_BLOCK)[:, None]
    kj = np.arange(3 * A_BLOCK)[None, :]
    dist = np.abs(qi + A_BLOCK - kj)
    slopes = np.power(np.float32(2.0), -8.0 * np.arange(1, A_HEADS + 1, dtype=np.float32) / A_HEADS)
    bias = -slopes[:, None, None] * dist.astype(np.float32)[None]
    bias = np.where((dist <= WINDOW)[None], bias, -np.inf)
    bias = bias.reshape(A_KV_HEADS, 2, 2, A_BLOCK, 3 * A_BLOCK).transpose(0, 2, 1, 3, 4)
    bias = bias.reshape(2 * A_KV_HEADS, 2 * A_BLOCK, 3 * A_BLOCK)
    no_prev = (kj < A_BLOCK)[None]
    no_next = (kj >= 2 * A_BLOCK)[None]
    ninf = -np.inf
    return np.stack([bias, np.where(no_prev, ninf, bias), np.where(no_next, ninf, bias),
                     np.where(no_prev | no_next, ninf, bias)]).astype(np.float32)


def _layer(x, attn_norm, w_in, a_sink, b_q_norm, b_w_uq, b_kv_norm, b_w_ukv, out_norm_a, out_norm_b, w_o,
           mlp_norm, w_router, b_router, w_up, b_up, w_down, b_down):
    b, s, d = x.shape
    t = b * s
    wq = b_w_uq.reshape(B_Q_RANK, B_HEADS, B_NOPE_DIM + B_ROPE_DIM)
    wq_pe = jnp.pad(wq[:, :, B_NOPE_DIM:], ((0, 0), (0, 0), (0, LANES - B_ROPE_DIM)))
    wuq = jnp.concatenate([wq[:, :, :B_NOPE_DIM].reshape(B_Q_RANK, -1), wq_pe.reshape(B_Q_RANK, -1)],
                          axis=1).astype(BF16)
    wkv = b_w_ukv.reshape(B_KV_RANK, B_HEADS, B_NOPE_DIM + B_V_DIM)
    wukv = jnp.concatenate([wkv[:, :, :B_NOPE_DIM].reshape(B_KV_RANK, -1),
                            wkv[:, :, B_NOPE_DIM:].reshape(B_KV_RANK, -1)], axis=1).astype(BF16)
    cos, sin = _rope_tables(s)

    qa, ka, va, qb, kb, vb = _prologue(x, attn_norm[None], w_in, b_q_norm[None], wuq, b_kv_norm[None], wukv,
                                       cos, sin)
    ya = _window_attn(qa, ka, va, a_sink, _window_bias())
    yb = _mla_attn(qb, kb, vb)
    wr_hi = w_router.astype(BF16)
    wr_lo = (w_router - wr_hi.astype(F32)).astype(BF16)
    x1, xm, eidx, gates, rank, counts = _out_router(
        ya.reshape(t, A_WIDTH), yb.reshape(t, B_WIDTH), x.reshape(t, d), out_norm_a[None], out_norm_b[None],
        w_o.astype(BF16), mlp_norm[None], jnp.concatenate([wr_hi, wr_lo], axis=1), b_router[None])

    nblk = (t * TOP_K + N_EXPERTS * (MOE_BM - 1) + MOE_BM - 1) // MOE_BM
    tabs, start, pad_start, pad_len, nact = _expert_schedule(counts.reshape(N_EXPERTS), nblk, t * TOP_K)
    slot_rows = _slot_rows(start, eidx.reshape(t * TOP_K), rank.reshape(t * TOP_K))
    xs = _dispatch(pad_start, pad_len, nact, slot_rows, xm, nblk * MOE_BM)
    n_short = N_EXPERTS * NF + nblk
    args = (xs, w_up, b_up, w_down, b_down)
    ys = lax.cond(nact[1] <= n_short,
                  lambda: _experts(tuple(a[:n_short] for a in tabs), *args, name="experts"),
                  lambda: _experts(tabs, *args, name="experts_long"))
    return x1, slot_rows, gates, ys


def kernel(x, attn_norm, w_in, a_sink, b_q_norm, b_w_uq, b_kv_norm, b_w_ukv, out_norm_a, out_norm_b, w_o,
           mlp_norm, w_router, b_router, w_up, b_up, w_down, b_down, final_norm):
    b, s, d = x.shape
    assert d == D_MODEL and s % min(TQ_MLA, s) == 0 and s % min(CK_MLA, s) == 0 and s % TM_PRO == 0 and attn_norm.shape[0] == 1
    x1, slot_rows, gates, ys = _layer(
        x, attn_norm[0], w_in[0], a_sink[0], b_q_norm[0], b_w_uq[0], b_kv_norm[0], b_w_ukv[0], out_norm_a[0],
        out_norm_b[0], w_o[0], mlp_norm[0], w_router[0], b_router[0], w_up[0], b_up[0], w_down[0], b_down[0])
    out = _combine(slot_rows, gates, x1, final_norm[None], ys)
    return out.reshape(b, s, d)
```
